```python
import math
import jax, jax.numpy as jnp
from jax import lax
import numpy as np

D_MODEL = 2048
BATCH = 4
SEQ = 2048
DEPTH = 1
DEC_BATCH = 128
DEC_SEQ = 1
PAST_LEN = 16384
PAGE_SIZE = 128

N_HEADS = D_MODEL // 128
N_KV_HEADS = N_HEADS // 4
HEAD_DIM = 64
Q_GROUP = N_HEADS // N_KV_HEADS
WINDOW = 128
BAND = WINDOW
WIN_BUF = min(WINDOW, PAST_LEN)
N_BUCKETS = 32
MAX_EXACT = N_BUCKETS // 2
MAX_DISTANCE = 128
D_SSM = D_MODEL // 2
SSM_GROUP = 16
N_SSM_GROUPS = D_SSM // SSM_GROUP
SSM_STATE = 64
DT_MIN = 0.001
DT_MAX = 0.1
N_EXPERTS = 32
TOP_K = 4
D_FF = D_MODEL
SWIGLU_LIMIT = 7.0
SWIGLU_ALPHA = 1.702
EXPERT_BLOCK = 128
NORM_EPS = 1e-6
NEG_INF = -1e30
Q_W = N_HEADS * HEAD_DIM
KV_W = N_KV_HEADS * HEAD_DIM
IN_W = Q_W + 2 * KV_W + D_SSM + 2 * D_MODEL

kernel_name = "hybrid_swa_s5_moe_decode_step"


def rms_norm(x, g):
    xf = x.astype(jnp.float32)
    y = xf * lax.rsqrt(jnp.mean(xf * xf, axis=-1, keepdims=True) + NORM_EPS)
    return (y * g.astype(jnp.float32)).astype(x.dtype)


def t5_bucket(dist):
    n = jnp.maximum(dist, 0)
    nf = jnp.maximum(n, 1).astype(jnp.float32)
    large = MAX_EXACT + (jnp.log(nf / MAX_EXACT) / math.log(MAX_DISTANCE / MAX_EXACT)
                         * (N_BUCKETS - MAX_EXACT)).astype(jnp.int32)
    large = jnp.minimum(large, N_BUCKETS - 1)
    return jnp.where(n < MAX_EXACT, n, large)


def sink_attention(q, k, v, dist, valid, rel_bias, sinks):
    B, N, Tq = q.shape[:3]
    Tk = k.shape[2]
    s = jnp.einsum('bnqhgd,bnkhd->bnhgqk', q, k).astype(jnp.float32) * (HEAD_DIM ** -0.5)
    bias = rel_bias[t5_bucket(dist)].astype(jnp.float32)
    bias = bias.reshape(1, Tq, Tk, N_KV_HEADS, Q_GROUP).transpose(0, 3, 4, 1, 2)
    s = jnp.where(valid[None, :, None, None], s + bias[None], NEG_INF)
    sink = sinks.astype(jnp.float32).reshape(N_KV_HEADS, Q_GROUP)[:, :, None, None]
    m = jnp.maximum(s.max(axis=-1, keepdims=True), sink)
    p = jnp.exp(s - m)
    p = p / (p.sum(axis=-1, keepdims=True) + jnp.exp(sink - m))
    o = jnp.einsum('bnhgqk,bnkhd->bnqhgd', p, v.astype(jnp.float32))
    return o.reshape(B, N * Tq, Q_W)


def attend_prompt(q, k, v, rel_bias, sinks):
    B, S = q.shape[:2]
    nb = S // BAND
    qb = q.reshape(B, nb, BAND, N_KV_HEADS, Q_GROUP, HEAD_DIM)

    def band(t):
        tb = t.reshape(B, nb, BAND, N_KV_HEADS, HEAD_DIM)
        prev = jnp.concatenate([jnp.zeros_like(tb[:, :1]), tb[:, :-1]], axis=1)
        return jnp.concatenate([prev, tb], axis=2)

    ql = jnp.arange(BAND)
    kl = jnp.arange(2 * BAND)
    dist = (ql[:, None] + BAND - kl[None, :])[None]
    k_abs = jnp.arange(nb)[:, None, None] * BAND - BAND + kl[None, None, :]
    valid = (dist >= 0) & (dist < WINDOW) & (k_abs >= 0)
    o = sink_attention(qb, band(k), band(v), dist, valid, rel_bias, sinks)
    return o, k[:, S - WIN_BUF:], v[:, S - WIN_BUF:]


def attend_sample(q, k, v, rel_bias, sinks, cache_k, cache_v):
    B, T = q.shape[:2]
    wb = cache_k.shape[1]
    k_all = jnp.concatenate([cache_k.astype(k.dtype), k], axis=1)
    v_all = jnp.concatenate([cache_v.astype(v.dtype), v], axis=1)
    q_pos = PAST_LEN + jnp.arange(T)
    k_pos = jnp.concatenate([PAST_LEN - wb + jnp.arange(wb), PAST_LEN + jnp.arange(T)])
    dist = (q_pos[:, None] - k_pos[None, :])[None]
    valid = (dist >= 0) & (dist < WINDOW)
    qb = q.reshape(B, 1, T, N_KV_HEADS, Q_GROUP, HEAD_DIM)
    o = sink_attention(qb, k_all[:, None], v_all[:, None], dist, valid, rel_bias, sinks)
    return o, k_all[:, k_all.shape[1] - wb:], v_all[:, v_all.shape[1] - wb:]


def ssm_branch(u, h0_re, h0_im, a_re, a_im, log_dt, b_re, b_im, c_re, c_im, d):
    B, T = u.shape[:2]
    ug = u.astype(jnp.float32).reshape(B, T, N_SSM_GROUPS, SSM_GROUP)
    a_re = a_re.astype(jnp.float32)
    a_im = a_im.astype(jnp.float32)
    dt = jnp.exp(log_dt.astype(jnp.float32))[:, None]
    lam_re, lam_im = a_re * dt, a_im * dt
    mag = jnp.exp(lam_re)
    lb_re, lb_im = mag * jnp.cos(lam_im), mag * jnp.sin(lam_im)
    den = a_re * a_re + a_im * a_im
    nr, ni = lb_re - 1.0, lb_im
    coef_re = (nr * a_re + ni * a_im) / den
    coef_im = (ni * a_re - nr * a_im) / den
    b_re = b_re.astype(jnp.float32)
    b_im = b_im.astype(jnp.float32)
    bb_re = coef_re[..., None] * b_re - coef_im[..., None] * b_im
    bb_im = coef_re[..., None] * b_im + coef_im[..., None] * b_re
    bu_re = jnp.einsum('btgj,gpj->btgp', ug, bb_re)
    bu_im = jnp.einsum('btgj,gpj->btgp', ug, bb_im)
    ar = jnp.broadcast_to(lb_re, bu_re.shape)
    ai = jnp.broadcast_to(lb_im, bu_re.shape)

    def combine(e1, e2):
        a1r, a1i, b1r, b1i = e1
        a2r, a2i, b2r, b2i = e2
        return (a2r * a1r - a2i * a1i, a2r * a1i + a2i * a1r,
                a2r * b1r - a2i * b1i + b2r, a2r * b1i + a2i * b1r + b2i)

    _, _, hr, hi = lax.associative_scan(combine, (ar, ai, bu_re, bu_im), axis=1)
    t = jnp.arange(1, T + 1, dtype=jnp.float32)[:, None, None]
    pm = jnp.exp(lam_re * t)
    pr, pi = pm * jnp.cos(lam_im * t), pm * jnp.sin(lam_im * t)
    h0r = h0_re.astype(jnp.float32)[:, None]
    h0i = h0_im.astype(jnp.float32)[:, None]
    hr = hr + pr * h0r - pi * h0i
    hi = hi + pr * h0i + pi * h0r
    y = (jnp.einsum('btgp,gjp->btgj', hr, c_re.astype(jnp.float32))
         - jnp.einsum('btgp,gjp->btgj', hi, c_im.astype(jnp.float32))
         + d.astype(jnp.float32).reshape(N_SSM_GROUPS, SSM_GROUP) * ug)
    return y.reshape(B, T, D_SSM).astype(u.dtype), hr[:, -1], hi[:, -1]


def moe(x, w_router, b_router, w_gate_up, b_gate_up, w_down, b_down):
    T = x.shape[0]
    logits = (x @ w_router + b_router).astype(jnp.float32)
    top_val, top_idx = lax.top_k(logits, TOP_K)
    gates = jax.nn.softmax(top_val, axis=-1)
    n_assign = T * TOP_K
    flat_e = top_idx.reshape(-1)
    flat_tok = jnp.arange(n_assign, dtype=jnp.int32) // TOP_K
    flat_g = gates.reshape(-1)
    order = jnp.argsort(flat_e)
    se = flat_e[order]
    counts = jnp.bincount(flat_e, length=N_EXPERTS)
    start = jnp.cumsum(counts) - counts
    pcounts = ((counts + EXPERT_BLOCK - 1) // EXPERT_BLOCK) * EXPERT_BLOCK
    pend = jnp.cumsum(pcounts)
    pstart = pend - pcounts
    dest = pstart[se] + (jnp.arange(n_assign) - start[se])
    n_rows = (-(-n_assign // EXPERT_BLOCK)) * EXPERT_BLOCK + N_EXPERTS * EXPERT_BLOCK
    n_blocks = n_rows // EXPERT_BLOCK
    tok_pad = jnp.full((n_rows,), T, jnp.int32).at[dest].set(flat_tok[order])
    gate_pad = jnp.zeros((n_rows,), jnp.float32).at[dest].set(flat_g[order])
    blk_e = jnp.clip(jnp.searchsorted(pend, jnp.arange(n_blocks) * EXPERT_BLOCK, side='right'),
                     0, N_EXPERTS - 1)
    x_pad = jnp.concatenate([x, jnp.zeros((1, x.shape[1]), x.dtype)], axis=0)
    xb = x_pad[tok_pad].reshape(n_blocks, EXPERT_BLOCK, x.shape[1])

    def expert_block(args):
        xe, e = args
        h = xe @ w_gate_up[e] + b_gate_up[e]
        x_glu = jnp.minimum(h[:, :D_FF], SWIGLU_LIMIT)
        x_lin = jnp.clip(h[:, D_FF:], -SWIGLU_LIMIT, SWIGLU_LIMIT)
        act = x_glu * jax.nn.sigmoid(SWIGLU_ALPHA * x_glu) * (x_lin + 1.0)
        return act @ w_down[e] + b_down[e]

    out = lax.map(expert_block, (xb, blk_e)).reshape(n_rows, x.shape[1])
    y = jax.ops.segment_sum(out.astype(jnp.float32) * gate_pad[:, None], tok_pad,
                            num_segments=T + 1)[:T]
    return y.astype(x.dtype)


def hybrid_layer(x, c, attend, h0_re, h0_im, rel_bias, lw):
    B, T, _ = x.shape
    mod = (jax.nn.silu(c) @ lw['w_ada'] + lw['b_ada'])[:, None, :]
    sh1, sc1, g1, sh2, sc2, g2 = jnp.split(mod, 6, axis=-1)
    h = rms_norm(x, lw['g_pre_mix']) * (1.0 + sc1) + sh1
    proj = h @ lw['w_in']
    o1 = Q_W
    o2 = o1 + KV_W
    o3 = o2 + KV_W
    o4 = o3 + D_SSM
    o5 = o4 + D_MODEL
    q = proj[..., :o1].reshape(B, T, N_HEADS, HEAD_DIM)
    k = proj[..., o1:o2].reshape(B, T, N_KV_HEADS, HEAD_DIM)
    v = proj[..., o2:o3].reshape(B, T, N_KV_HEADS, HEAD_DIM)
    u = proj[..., o3:o4]
    gate_a = jax.nn.sigmoid(proj[..., o4:o5])
    gate_s = jax.nn.sigmoid(proj[..., o5:])
    o_attn, new_k, new_v = attend(q, k, v, rel_bias, lw['attn_sinks'])
    y_ssm, h_re, h_im = ssm_branch(u, h0_re, h0_im, lw['ssm_a_re'], lw['ssm_a_im'], lw['ssm_log_dt'],
                                   lw['ssm_b_re'], lw['ssm_b_im'], lw['ssm_c_re'], lw['ssm_c_im'],
                                   lw['ssm_d'])
    y_ssm = jax.nn.gelu(y_ssm)
    y_ssm = y_ssm * jax.nn.sigmoid(y_ssm @ lw['w_glu'] + lw['b_glu'])
    merged = (gate_a * (o_attn.astype(x.dtype) @ lw['w_br_attn'])
              + gate_s * (y_ssm @ lw['w_br_ssm']))
    mix = merged @ lw['w_out']
    x = x + g1 * rms_norm(mix, lw['g_post_mix'])
    h = rms_norm(x, lw['g_pre_ffn']) * (1.0 + sc2) + sh2
    f = moe(h.reshape(B * T, D_MODEL), lw['w_router'], lw['b_router'], lw['w_gate_up'],
            lw['b_gate_up'], lw['w_down'], lw['b_down']).reshape(B, T, D_MODEL)
    x = x + g2 * rms_norm(f, lw['g_post_ffn'])
    return x, new_k, new_v, h_re, h_im


def setup_inputs(seed: int = 0) -> dict:
    key = jax.random.key(seed)
    ks = jax.random.split(key, 40)
    f32 = jnp.float32
    nrm = lambda k, shape, s: jax.random.normal(k, shape, f32) * s
    L, D, E, G, P, J = DEPTH, D_MODEL, N_EXPERTS, N_SSM_GROUPS, SSM_STATE, SSM_GROUP
    n_idx = jnp.arange(P, dtype=f32)
    a_re = -0.5 + nrm(ks[9], (L, G, P), 0.01)
    a_im = math.pi * n_idx + nrm(ks[10], (L, G, P), 0.01)
    log_dt = jax.random.uniform(ks[11], (L, G), f32, math.log(DT_MIN), math.log(DT_MAX))
    return {
        'x_prompt': nrm(ks[0], (BATCH, SEQ, D), 1.0),
        'x_sample': nrm(ks[1], (DEC_BATCH, DEC_SEQ, D), 1.0),
        'c_prompt': nrm(ks[2], (BATCH, D), 1.0),
        'c_sample': nrm(ks[3], (DEC_BATCH, D), 1.0),
        'cache_win_k': nrm(ks[4], (L, DEC_BATCH, WIN_BUF, N_KV_HEADS, HEAD_DIM), 1.0),
        'cache_win_v': nrm(ks[5], (L, DEC_BATCH, WIN_BUF, N_KV_HEADS, HEAD_DIM), 1.0),
        'state_ssm_re': nrm(ks[6], (L, DEC_BATCH, G, P), 0.5),
        'state_ssm_im': nrm(ks[7], (L, DEC_BATCH, G, P), 0.5),
        'w_ada': nrm(ks[8], (L, D, 6 * D), 0.5 * D ** -0.5),
        'b_ada': nrm(ks[12], (L, 6 * D), 0.02),
        'g_pre_mix': 1.0 + nrm(ks[13], (L, D), 0.05),
        'g_post_mix': 1.0 + nrm(ks[14], (L, D), 0.05),
        'g_pre_ffn': 1.0 + nrm(ks[15], (L, D), 0.05),
        'g_post_ffn': 1.0 + nrm(ks[16], (L, D), 0.05),
        'w_in': nrm(ks[17], (L, D, IN_W), D ** -0.5),
        'attn_sinks': nrm(ks[18], (L, N_HEADS), 0.5),
        'rel_bias': nrm(ks[19], (N_BUCKETS, N_HEADS), 0.5),
        'ssm_a_re': a_re,
        'ssm_a_im': a_im,
        'ssm_log_dt': log_dt,
        'ssm_b_re': nrm(ks[20], (L, G, P, J), J ** -0.5),
        'ssm_b_im': nrm(ks[21], (L, G, P, J), J ** -0.5),
        'ssm_c_re': nrm(ks[22], (L, G, J, P), P ** -0.5),
        'ssm_c_im': nrm(ks[23], (L, G, J, P), P ** -0.5),
        'ssm_d': nrm(ks[24], (L, D_SSM), 1.0),
        'w_glu': nrm(ks[25], (L, D_SSM, D_SSM), D_SSM ** -0.5),
        'b_glu': nrm(ks[26], (L, D_SSM), 0.02),
        'w_br_attn': nrm(ks[27], (L, Q_W, D), Q_W ** -0.5),
        'w_br_ssm': nrm(ks[28], (L, D_SSM, D), D_SSM ** -0.5),
        'w_out': nrm(ks[29], (L, D, D), D ** -0.5),
        'w_router': nrm(ks[30], (L, D, E), D ** -0.5),
        'b_router': nrm(ks[31], (L, E), 0.01),
        'w_gate_up': nrm(ks[32], (L, E, D, 2 * D_FF), D ** -0.5),
        'b_gate_up': nrm(ks[33], (L, E, 2 * D_FF), 0.02),
        'w_down': nrm(ks[34], (L, E, D_FF, D), D_FF ** -0.5),
        'b_down': nrm(ks[35], (L, E, D), 0.02),
    }


def reference(x_prompt, x_sample, c_prompt, c_sample, cache_win_k, cache_win_v, state_ssm_re,
              state_ssm_im, w_ada, b_ada, g_pre_mix, g_post_mix, g_pre_ffn, g_post_ffn, w_in,
              attn_sinks, rel_bias, ssm_a_re, ssm_a_im, ssm_log_dt, ssm_b_re, ssm_b_im, ssm_c_re,
              ssm_c_im, ssm_d, w_glu, b_glu, w_br_attn, w_br_ssm, w_out, w_router, b_router,
              w_gate_up, b_gate_up, w_down, b_down):
    xp, xs = x_prompt, x_sample
    kp_l, vp_l, hrp_l, hip_l = [], [], [], []
    ks_l, vs_l, hrs_l, his_l = [], [], [], []
    zeros_state = jnp.zeros((xp.shape[0], N_SSM_GROUPS, SSM_STATE), jnp.float32)
    for l in range(DEPTH):
        lw = dict(w_ada=w_ada[l], b_ada=b_ada[l], g_pre_mix=g_pre_mix[l], g_post_mix=g_post_mix[l],
                  g_pre_ffn=g_pre_ffn[l], g_post_ffn=g_post_ffn[l], w_in=w_in[l],
                  attn_sinks=attn_sinks[l], ssm_a_re=ssm_a_re[l], ssm_a_im=ssm_a_im[l],
                  ssm_log_dt=ssm_log_dt[l], ssm_b_re=ssm_b_re[l], ssm_b_im=ssm_b_im[l],
                  ssm_c_re=ssm_c_re[l], ssm_c_im=ssm_c_im[l], ssm_d=ssm_d[l], w_glu=w_glu[l],
                  b_glu=b_glu[l], w_br_attn=w_br_attn[l], w_br_ssm=w_br_ssm[l], w_out=w_out[l],
                  w_router=w_router[l], b_router=b_router[l], w_gate_up=w_gate_up[l],
                  b_gate_up=b_gate_up[l], w_down=w_down[l], b_down=b_down[l])
        xp, kp, vp, hrp, hip = hybrid_layer(xp, c_prompt, attend_prompt, zeros_state, zeros_state,
                                            rel_bias, lw)
        ck, cv = cache_win_k[l], cache_win_v[l]
        attend_s = lambda q, k, v, rb, sk, ck=ck, cv=cv: attend_sample(q, k, v, rb, sk, ck, cv)
        xs, kss, vss, hrs, his = hybrid_layer(xs, c_sample, attend_s, state_ssm_re[l],
                                              state_ssm_im[l], rel_bias, lw)
        kp_l.append(kp); vp_l.append(vp); hrp_l.append(hrp); hip_l.append(hip)
        ks_l.append(kss); vs_l.append(vss); hrs_l.append(hrs); his_l.append(his)
    return (xp, xs, jnp.stack(kp_l), jnp.stack(vp_l), jnp.stack(hrp_l), jnp.stack(hip_l),
            jnp.stack(ks_l), jnp.stack(vs_l), jnp.stack(hrs_l), jnp.stack(his_l))
```

```python
import functools
import math

import numpy as np
import jax
import jax.numpy as jnp
from jax import lax
from jax.experimental import pallas as pl
from jax.experimental.pallas import tpu as pltpu

F32 = jnp.float32
BF16 = jnp.bfloat16
I32 = jnp.int32

D_MODEL = 2048
N_HEADS = 16
N_KV_HEADS = 4
HEAD_DIM = 64
Q_GROUP = N_HEADS // N_KV_HEADS
WINDOW = 128
N_BUCKETS = 32
MAX_EXACT = N_BUCKETS // 2
MAX_DISTANCE = 128
D_SSM = 1024
SSM_GROUP = 16
N_SSM_GROUPS = 64
SSM_STATE = 64
N_EXPERTS = 32
TOP_K = 4
D_FF = 2048
SWIGLU_LIMIT = 7.0
SWIGLU_ALPHA = 1.702
NORM_EPS = 1e-6
NEG_INF = -1e30
Q_W = N_HEADS * HEAD_DIM
KV_W = N_KV_HEADS * HEAD_DIM
IN_W = Q_W + 2 * KV_W + D_SSM + 2 * D_MODEL
SSM_W = N_SSM_GROUPS * SSM_STATE

LANE = 128
SUBLANE = 8
VMEM_LIMIT = 56 * 1024 * 1024

PROJ_TM = 1024
PROJ_TN = 512
MERGE_TM = 512
MERGE_TK = 512
SSM_TC = 128
SSM_LB = 512
SSM_TILES = D_SSM // LANE
TOK_BLK = 128
EXP_TM = 1280
EXP_SUB = 256
EXP_TF = 256
EXP_TD = 512
EXP_NF = D_FF // EXP_TF
EXP_ND = D_MODEL // EXP_TD


def _cparams(sem):
    return pltpu.CompilerParams(dimension_semantics=sem, vmem_limit_bytes=VMEM_LIMIT)


def _sigmoid(x):
    return 1.0 / (1.0 + jnp.exp(-x))


def _rms(x, g):
    return x * lax.rsqrt(jnp.mean(x * x, axis=-1, keepdims=True) + NORM_EPS) * g


def _dot(a, b):
    return jnp.dot(a, b, preferred_element_type=F32)


def _dot_nt(a, b):
    return lax.dot_general(a, b, (((1,), (1,)), ((), ())), preferred_element_type=F32)


def _ada_kernel(c_ref, w_ref, b_ref, o_ref):
    c = c_ref[...]
    s = (c * _sigmoid(c)).astype(BF16)
    o_ref[...] = _dot(s, w_ref[...].astype(BF16)) + b_ref[...]


def _ada(c_all, w_ada, b_ada):
    rows = c_all.shape[0]
    tn = 1024
    n = w_ada.shape[1]
    return pl.pallas_call(
        _ada_kernel,
        grid=(n // tn,),
        in_specs=[pl.BlockSpec((rows, D_MODEL), lambda j: (0, 0)),
                  pl.BlockSpec((D_MODEL, tn), lambda j: (0, j)),
                  pl.BlockSpec((1, tn), lambda j: (0, j))],
        out_specs=pl.BlockSpec((rows, tn), lambda j: (0, j)),
        out_shape=jax.ShapeDtypeStruct((rows, n), F32),
        compiler_params=_cparams(("arbitrary",)),
        name="ada",
    )(c_all, w_ada, b_ada.reshape(1, n))


def _t5_bucket_np(dist):
    n = np.maximum(dist, 0)
    nf = np.maximum(n, 1).astype(np.float64)
    large = MAX_EXACT + (np.log(nf / MAX_EXACT) / math.log(MAX_DISTANCE / MAX_EXACT)
                         * (N_BUCKETS - MAX_EXACT)).astype(np.int32)
    large = np.minimum(large, N_BUCKETS - 1)
    return np.where(n < MAX_EXACT, n, large).astype(np.int32)


def _bias_kernel(bucket_ref, rb_ref, o_ref):
    h = pl.program_id(0)
    bucket = bucket_ref[...]
    acc = jnp.full(bucket.shape, NEG_INF, F32)
    for b in range(N_BUCKETS):
        acc = jnp.where(bucket == b, rb_ref[b, h], acc)
    o_ref[...] = acc


def _bias_table(bucket_np, rel_bias):
    r, c = bucket_np.shape
    return pl.pallas_call(
        _bias_kernel,
        grid=(N_HEADS,),
        in_specs=[pl.BlockSpec((r, c), lambda h: (0, 0)),
                  pl.BlockSpec(memory_space=pltpu.SMEM)],
        out_specs=pl.BlockSpec((None, r, c), lambda h: (h, 0, 0)),
        out_shape=jax.ShapeDtypeStruct((N_HEADS, r, c), F32),
        compiler_params=_cparams(("arbitrary",)),
        name="bias",
    )(jnp.asarray(bucket_np), rel_bias)


def _proj_kernel(x_ref, sc_ref, sh_ref, g_ref, w_ref, o_ref, h_scr):
    @pl.when(pl.program_id(1) == 0)
    def _():
        h = _rms(x_ref[...], g_ref[...]) * (1.0 + sc_ref[...]) + sh_ref[...]
        h_scr[...] = h.astype(BF16)

    o_ref[...] = _dot(h_scr[...], w_ref[...].astype(BF16))


def _mod_spec(per_row, tm, rows_per_seq, col, nargs):
    if per_row:
        if nargs == 1:
            return pl.BlockSpec((tm, D_MODEL), lambda i: (i, col))
        return pl.BlockSpec((tm, D_MODEL), lambda i, j: (i, col))
    tiles_per_seq = rows_per_seq // tm
    if nargs == 1:
        return pl.BlockSpec((None, 1, D_MODEL), lambda i: (i // tiles_per_seq, 0, col))
    return pl.BlockSpec((None, 1, D_MODEL), lambda i, j: (i // tiles_per_seq, 0, col))


def _proj(x, mod, per_row, rows_per_seq, g_pre, w_in):
    rows = x.shape[0]
    tm = min(PROJ_TM, rows)
    return pl.pallas_call(
        _proj_kernel,
        grid=(rows // tm, IN_W // PROJ_TN),
        in_specs=[pl.BlockSpec((tm, D_MODEL), lambda i, j: (i, 0)),
                  _mod_spec(per_row, tm, rows_per_seq, 1, 2),
                  _mod_spec(per_row, tm, rows_per_seq, 0, 2),
                  pl.BlockSpec((1, D_MODEL), lambda i, j: (0, 0)),
                  pl.BlockSpec((D_MODEL, PROJ_TN), lambda i, j: (0, j))],
        out_specs=pl.BlockSpec((tm, PROJ_TN), lambda i, j: (i, j)),
        out_shape=jax.ShapeDtypeStruct((rows, IN_W), F32),
        scratch_shapes=[pltpu.VMEM((tm, D_MODEL), BF16)],
        compiler_params=_cparams(("arbitrary", "arbitrary")),
        name="proj",
    )(x, mod, mod, g_pre.reshape(1, D_MODEL), w_in)


def _attn_prompt_kernel(q_ref, kc_ref, kp_ref, vc_ref, vp_ref, bias_ref, sink_ref, o_ref):
    has_prev = pl.program_id(1) > 0
    q = q_ref[...]
    k = jnp.concatenate([kp_ref[...], kc_ref[...]], axis=0)
    v = jnp.concatenate([vp_ref[...], vc_ref[...]], axis=0)
    col = lax.broadcasted_iota(I32, (WINDOW, 2 * WINDOW), 1)
    key_ok = (col >= WINDOW) | has_prev
    outs = []
    for g in range(N_KV_HEADS):
        kg = k[:, g * HEAD_DIM:(g + 1) * HEAD_DIM].astype(BF16)
        vg = v[:, g * HEAD_DIM:(g + 1) * HEAD_DIM].astype(BF16)
        for hh in range(Q_GROUP):
            h = g * Q_GROUP + hh
            qh = q[:, h * HEAD_DIM:(h + 1) * HEAD_DIM].astype(BF16)
            s = _dot_nt(qh, kg) * (HEAD_DIM ** -0.5) + bias_ref[h]
            s = jnp.where(key_ok, s, NEG_INF)
            sink = sink_ref[h]
            m = jnp.maximum(jnp.max(s, axis=-1, keepdims=True), sink)
            p = jnp.exp(s - m)
            den = jnp.sum(p, axis=-1, keepdims=True) + jnp.exp(sink - m)
            outs.append(_dot(p.astype(BF16), vg) / den)
    o_ref[...] = jnp.concatenate(outs, axis=-1)


def _attn_prompt(proj, batch, seq, bias, sinks):
    nb = seq // WINDOW
    kcol = Q_W // KV_W
    vcol = kcol + 1
    cur = lambda c: (lambda b, n: (b * nb + n, c))
    prev = lambda c: (lambda b, n: (b * nb + jnp.maximum(n - 1, 0), c))
    return pl.pallas_call(
        _attn_prompt_kernel,
        grid=(batch, nb),
        in_specs=[pl.BlockSpec((WINDOW, Q_W), cur(0)),
                  pl.BlockSpec((WINDOW, KV_W), cur(kcol)),
                  pl.BlockSpec((WINDOW, KV_W), prev(kcol)),
                  pl.BlockSpec((WINDOW, KV_W), cur(vcol)),
                  pl.BlockSpec((WINDOW, KV_W), prev(vcol)),
                  pl.BlockSpec((N_HEADS, WINDOW, 2 * WINDOW), lambda b, n: (0, 0, 0)),
                  pl.BlockSpec(memory_space=pltpu.SMEM)],
        out_specs=pl.BlockSpec((WINDOW, Q_W), lambda b, n: (b * nb + n, 0)),
        out_shape=jax.ShapeDtypeStruct((batch * seq, Q_W), F32),
        compiler_params=_cparams(("arbitrary", "arbitrary")),
        name="attn_prompt",
    )(proj, proj, proj, proj, proj, bias, sinks)


def _attn_sample_kernel(q_ref, kn_ref, vn_ref, ck_ref, cv_ref, bias_ref, sink_ref,
                        o_ref, nk_ref, nv_ref):
    tb = q_ref.shape[0]
    row = lax.broadcasted_iota(I32, (tb, WINDOW, KV_W), 1)
    last = row == WINDOW - 1
    nk = jnp.where(last, kn_ref[...], pltpu.roll(ck_ref[...], WINDOW - 1, 1))
    nv = jnp.where(last, vn_ref[...], pltpu.roll(cv_ref[...], WINDOW - 1, 1))
    nk_ref[...] = nk
    nv_ref[...] = nv
    lane_grp = lax.broadcasted_iota(I32, (N_HEADS, KV_W), 1) // HEAD_DIM
    head_grp = lax.broadcasted_iota(I32, (N_HEADS, KV_W), 0) // Q_GROUP
    gmask = (lane_grp == head_grp).astype(F32)
    q = q_ref[...]
    qrow = jnp.concatenate([q] * N_KV_HEADS, axis=-1) * gmask
    s = jnp.einsum('bhc,brc->bhr', qrow.astype(BF16), nk.astype(BF16),
                   preferred_element_type=F32) * (HEAD_DIM ** -0.5)
    s = s + bias_ref[...]
    sink = sink_ref[...]
    m = jnp.maximum(jnp.max(s, axis=-1, keepdims=True), sink)
    p = jnp.exp(s - m)
    den = jnp.sum(p, axis=-1, keepdims=True) + jnp.exp(sink - m)
    o = jnp.einsum('bhr,brc->bhc', p.astype(BF16), nv.astype(BF16),
                   preferred_element_type=F32) * gmask
    o64 = o[..., 0:HEAD_DIM]
    for g in range(1, N_KV_HEADS):
        o64 = o64 + o[..., g * HEAD_DIM:(g + 1) * HEAD_DIM]
    o_ref[...] = o64 / den


def _attn_sample(q3, kn, vn, cache_k, cache_v, bias, sinks):
    nseq = q3.shape[0]
    tb = 16
    seq3 = lambda w: pl.BlockSpec((tb, WINDOW, w), lambda i: (i, 0, 0))
    return pl.pallas_call(
        _attn_sample_kernel,
        grid=(nseq // tb,),
        in_specs=[pl.BlockSpec((tb, N_HEADS, HEAD_DIM), lambda i: (i, 0, 0)),
                  pl.BlockSpec((tb, 1, KV_W), lambda i: (i, 0, 0)),
                  pl.BlockSpec((tb, 1, KV_W), lambda i: (i, 0, 0)),
                  seq3(KV_W), seq3(KV_W),
                  pl.BlockSpec((N_HEADS, WINDOW), lambda i: (0, 0)),
                  pl.BlockSpec((N_HEADS, 1), lambda i: (0, 0))],
        out_specs=[pl.BlockSpec((tb, N_HEADS, HEAD_DIM), lambda i: (i, 0, 0)),
                   seq3(KV_W), seq3(KV_W)],
        out_shape=[jax.ShapeDtypeStruct((nseq, N_HEADS, HEAD_DIM), F32),
                   jax.ShapeDtypeStruct((nseq, WINDOW, KV_W), F32),
                   jax.ShapeDtypeStruct((nseq, WINDOW, KV_W), F32)],
        compiler_params=_cparams(("arbitrary",)),
        name="attn_sample",
    )(q3, kn, vn, cache_k, cache_v, bias, sinks.reshape(N_HEADS, 1))


def _ssm_disc_kernel(are_ref, aim_ref, ldt_ref, bre_ref, bim_ref,
                     lbr_ref, lbi_ref, bbr_ref, bbi_ref):
    a_re = are_ref[...]
    a_im = aim_ref[...]
    dt = jnp.exp(ldt_ref[...])
    lam_re = a_re * dt
    lam_im = a_im * dt
    mag = jnp.exp(lam_re)
    lb_re = mag * jnp.cos(lam_im)
    lb_im = mag * jnp.sin(lam_im)
    den = a_re * a_re + a_im * a_im
    nr = lb_re - 1.0
    ni = lb_im
    coef_re = (nr * a_re + ni * a_im) / den
    coef_im = (ni * a_re - nr * a_im) / den
    b_re = bre_ref[...]
    b_im = bim_ref[...]
    lbr_ref[...] = lb_re
    lbi_ref[...] = lb_im
    bbr_ref[...] = coef_re * b_re - coef_im * b_im
    bbi_ref[...] = coef_re * b_im + coef_im * b_re


def _ssm_disc(a_re, a_im, log_dt, b_re, b_im):
    g, p, j = N_SSM_GROUPS, SSM_STATE, SSM_GROUP
    vec = jax.ShapeDtypeStruct((g, 1, p), F32)
    mat = jax.ShapeDtypeStruct((g, j, p), F32)
    return pl.pallas_call(
        _ssm_disc_kernel,
        out_shape=[vec, vec, mat, mat],
        name="ssm_disc",
    )(a_re.reshape(g, 1, p), a_im.reshape(g, 1, p), log_dt.reshape(g, 1, 1),
      b_re.transpose(0, 2, 1), b_im.transpose(0, 2, 1))


def _block_diag_tiles(x):
    a, b = x.shape[1], x.shape[2]
    eye = jnp.eye(SUBLANE, dtype=x.dtype)
    y = jnp.einsum('kgab,gh->kgahb', x.reshape(SSM_TILES, SUBLANE, a, b), eye)
    return y.reshape(SSM_TILES, SUBLANE * a, SUBLANE * b)


def _gelu_tanh(x):
    return 0.5 * x * (1.0 + jnp.tanh(math.sqrt(2.0 / math.pi) * (x + 0.044715 * (x * x * x))))


def _ssm_kernel(u_ref, h0r_ref, h0i_ref, lbr_ref, lbi_ref, bb_ref, cc_ref, d_ref, wglu_ref, bglu_ref,
                y_ref, hTr_ref, hTi_ref, hre, him, st_r, st_i, *, nseq, tc):
    paired = nseq == 4

    @pl.when(pl.program_id(0) == 0)
    def _():
        if paired:
            st_r[...] = jnp.concatenate([h0r_ref[...], h0r_ref[...]], axis=0)
            st_i[...] = jnp.concatenate([h0i_ref[...], h0i_ref[...]], axis=0)
        else:
            st_r[...] = h0r_ref[...]
            st_i[...] = h0i_ref[...]

    u = u_ref[...]
    ub = u.astype(BF16)
    half = SSM_W // SSM_TILES
    for k in range(SSM_TILES):
        bu = _dot(ub[:, k * LANE:(k + 1) * LANE], bb_ref[k])
        hre[:, k * half:(k + 1) * half] = bu[:, :half]
        him[:, k * half:(k + 1) * half] = bu[:, half:]

    for blk in range(SSM_W // SSM_LB):
        sl = slice(blk * SSM_LB, (blk + 1) * SSM_LB)
        ar = lbr_ref[:, sl]
        ai = lbi_ref[:, sl]
        if paired:
            lower = lax.broadcasted_iota(I32, (SUBLANE, SSM_LB), 0) < nseq

            def body(m, carry):
                sr, si = carry
                r0 = pl.multiple_of(m * SUBLANE, SUBLANE)
                br = hre[pl.ds(r0, SUBLANE), sl]
                bi = him[pl.ds(r0, SUBLANE), sl]
                xr = pltpu.roll(sr, nseq, 0)
                xi = pltpu.roll(si, nseq, 0)
                h1r = ar * xr - ai * xi + br
                h1i = ar * xi + ai * xr + bi
                yr = pltpu.roll(h1r, nseq, 0)
                yi = pltpu.roll(h1i, nseq, 0)
                h2r = ar * yr - ai * yi + br
                h2i = ar * yi + ai * yr + bi
                hre[pl.ds(r0, SUBLANE), sl] = jnp.where(lower, h1r, h2r)
                him[pl.ds(r0, SUBLANE), sl] = jnp.where(lower, h1i, h2i)
                return h2r, h2i

            sr, si = lax.fori_loop(0, tc * nseq // SUBLANE, body, (st_r[:, sl], st_i[:, sl]))
        else:
            def body(t, carry):
                sr, si = carry
                r0 = pl.multiple_of(t * nseq, SUBLANE)
                br = hre[pl.ds(r0, nseq), sl]
                bi = him[pl.ds(r0, nseq), sl]
                nr = ar * sr - ai * si + br
                ni = ar * si + ai * sr + bi
                hre[pl.ds(r0, nseq), sl] = nr
                him[pl.ds(r0, nseq), sl] = ni
                return nr, ni

            sr, si = lax.fori_loop(0, tc, body, (st_r[:, sl], st_i[:, sl]))
        st_r[:, sl] = sr
        st_i[:, sl] = si

    ys = []
    for k in range(SSM_TILES):
        hr = hre[:, k * half:(k + 1) * half].astype(BF16)
        hi = him[:, k * half:(k + 1) * half].astype(BF16)
        yk = _dot(hr, cc_ref[k, :half, :]) + _dot(hi, cc_ref[k, half:, :])
        ys.append(yk + d_ref[:, k * LANE:(k + 1) * LANE] * u[:, k * LANE:(k + 1) * LANE])
    y = _gelu_tanh(jnp.concatenate(ys, axis=-1))
    z = _dot(y.astype(BF16), wglu_ref[...].astype(BF16)) + bglu_ref[...]
    y_ref[...] = y * _sigmoid(z)

    if paired:
        hTr_ref[...] = st_r[nseq:, :]
        hTi_ref[...] = st_i[nseq:, :]
    else:
        hTr_ref[...] = st_r[...]
        hTi_ref[...] = st_i[...]


def _ssm(u_rows, h0_re, h0_im, nseq, tc, lbr, lbi, bb, cc, d, w_glu, b_glu):
    rows = u_rows.shape[0]
    r = nseq * tc
    st_rows = max(nseq, SUBLANE)
    const2 = lambda shape: pl.BlockSpec(shape, lambda c: (0, 0))
    const3 = lambda shape: pl.BlockSpec(shape, lambda c: (0, 0, 0))
    return pl.pallas_call(
        functools.partial(_ssm_kernel, nseq=nseq, tc=tc),
        grid=(rows // r,),
        in_specs=[pl.BlockSpec((r, D_SSM), lambda c: (c, 0)),
                  const2((nseq, SSM_W)), const2((nseq, SSM_W)),
                  const2((1, SSM_W)), const2((1, SSM_W)),
                  const3((SSM_TILES, LANE, 2 * SSM_W // SSM_TILES)),
                  const3((SSM_TILES, 2 * SSM_W // SSM_TILES, LANE)),
                  const2((1, D_SSM)), const2((D_SSM, D_SSM)), const2((1, D_SSM))],
        out_specs=[pl.BlockSpec((r, D_SSM), lambda c: (c, 0)),
                   const2((nseq, SSM_W)), const2((nseq, SSM_W))],
        out_shape=[jax.ShapeDtypeStruct((rows, D_SSM), F32),
                   jax.ShapeDtypeStruct((nseq, SSM_W), F32),
                   jax.ShapeDtypeStruct((nseq, SSM_W), F32)],
        scratch_shapes=[pltpu.VMEM((r, SSM_W), F32), pltpu.VMEM((r, SSM_W), F32),
                        pltpu.VMEM((st_rows, SSM_W), F32), pltpu.VMEM((st_rows, SSM_W), F32)],
        compiler_params=_cparams(("arbitrary",)),
        name="ssm",
    )(u_rows, h0_re, h0_im, lbr, lbi, bb, cc, d.reshape(1, D_SSM), w_glu, b_glu.reshape(1, D_SSM))


def _split_bf16(x):
    hi = x.astype(BF16)
    lo = (x - hi.astype(F32)).astype(BF16)
    return hi, lo


def _merge_kernel(o_ref, y_ref, ga_ref, gs_ref, wa_ref, ws_ref, wo_ref, mix_ref):
    j = pl.program_id(1)
    a = _dot(o_ref[...].astype(BF16), wa_ref[...].astype(BF16))
    s = _dot(y_ref[...].astype(BF16), ws_ref[...].astype(BF16))
    merged = _sigmoid(ga_ref[...]) * a + _sigmoid(gs_ref[...]) * s
    contrib = _dot(merged.astype(BF16), wo_ref[...].astype(BF16))

    @pl.when(j == 0)
    def _():
        mix_ref[...] = contrib

    @pl.when(j > 0)
    def _():
        mix_ref[...] += contrib


def _merge(o_attn, y_ssm, proj, w_br_attn, w_br_ssm, w_out):
    rows = o_attn.shape[0]
    tm = min(MERGE_TM, rows)
    nk = D_MODEL // MERGE_TK
    ga0 = (Q_W + 2 * KV_W + D_SSM) // MERGE_TK
    gs0 = ga0 + nk
    row2 = lambda w: pl.BlockSpec((tm, w), lambda i, j: (i, 0))
    return pl.pallas_call(
        _merge_kernel,
        grid=(rows // tm, nk),
        in_specs=[row2(Q_W), row2(D_SSM),
                  pl.BlockSpec((tm, MERGE_TK), lambda i, j: (i, ga0 + j)),
                  pl.BlockSpec((tm, MERGE_TK), lambda i, j: (i, gs0 + j)),
                  pl.BlockSpec((Q_W, MERGE_TK), lambda i, j: (0, j)),
                  pl.BlockSpec((D_SSM, MERGE_TK), lambda i, j: (0, j)),
                  pl.BlockSpec((MERGE_TK, D_MODEL), lambda i, j: (j, 0))],
        out_specs=row2(D_MODEL),
        out_shape=jax.ShapeDtypeStruct((rows, D_MODEL), F32),
        compiler_params=_cparams(("arbitrary", "arbitrary")),
        name="merge",
    )(o_attn, y_ssm, proj, proj, w_br_attn, w_br_ssm, w_out)


def _post_kernel(x_ref, mix_ref, gpm_ref, g1_ref, sc2_ref, sh2_ref, gpf_ref, wrt_ref, br_ref,
                 x1_ref, h2_ref, lg_ref):
    x1 = x_ref[...] + g1_ref[...] * _rms(mix_ref[...], gpm_ref[...])
    x1_ref[...] = x1
    h2 = _rms(x1, gpf_ref[...]) * (1.0 + sc2_ref[...]) + sh2_ref[...]
    h2_ref[...] = h2
    h_hi, h_lo = _split_bf16(h2)
    w_hi, w_lo = _split_bf16(wrt_ref[...])
    lg_ref[...] = (_dot_nt(w_hi, h_hi) + _dot_nt(w_hi, h_lo) + _dot_nt(w_lo, h_hi)) + br_ref[...]


def _post(x, mix, mod, per_row, rows_per_seq, g_post_mix, g_pre_ffn, w_router_t, b_router):
    rows = x.shape[0]
    tm = min(MERGE_TM, rows)
    row = pl.BlockSpec((tm, D_MODEL), lambda i: (i, 0))
    vec = pl.BlockSpec((1, D_MODEL), lambda i: (0, 0))
    return pl.pallas_call(
        _post_kernel,
        grid=(rows // tm,),
        in_specs=[row, row, vec,
                  _mod_spec(per_row, tm, rows_per_seq, 2, 1),
                  _mod_spec(per_row, tm, rows_per_seq, 4, 1),
                  _mod_spec(per_row, tm, rows_per_seq, 3, 1),
                  vec,
                  pl.BlockSpec((N_EXPERTS, D_MODEL), lambda i: (0, 0)),
                  pl.BlockSpec((N_EXPERTS, 1), lambda i: (0, 0))],
        out_specs=[row, row, pl.BlockSpec((N_EXPERTS, tm), lambda i: (0, i))],
        out_shape=[jax.ShapeDtypeStruct((rows, D_MODEL), F32),
                   jax.ShapeDtypeStruct((rows, D_MODEL), F32),
                   jax.ShapeDtypeStruct((N_EXPERTS, rows), F32)],
        compiler_params=_cparams(("arbitrary",)),
        name="post",
    )(x, mix, g_post_mix.reshape(1, D_MODEL), mod, mod, mod, g_pre_ffn.reshape(1, D_MODEL),
      w_router_t, b_router.reshape(N_EXPERTS, 1))


def _bucket_tables():
    ql = np.arange(WINDOW)[:, None]
    kl = np.arange(2 * WINDOW)[None, :]
    dist = ql + WINDOW - kl
    prompt = np.where((dist >= 0) & (dist < WINDOW), _t5_bucket_np(dist), -1).astype(np.int32)
    d_s = (WINDOW - 1 - np.arange(WINDOW))[None, :]
    sample = np.broadcast_to(_t5_bucket_np(d_s), (SUBLANE, WINDOW)).astype(np.int32)
    return prompt, sample


def _front(p):
    l = 0
    batch, seq, _ = p['x_prompt'].shape
    nseq = p['x_sample'].shape[0]
    xp = p['x_prompt'].reshape(batch * seq, D_MODEL)
    xs = p['x_sample'].reshape(nseq, D_MODEL)

    c_all = jnp.concatenate([p['c_prompt'], p['c_sample'],
                             jnp.zeros((SUBLANE - (batch + nseq) % SUBLANE, D_MODEL), F32)], axis=0)
    mod = _ada(c_all, p['w_ada'][l], p['b_ada'][l])
    mod_p = mod[:batch].reshape(batch, 1, 6 * D_MODEL)
    mod_s = mod[batch:batch + nseq]

    bucket_p, bucket_s = _bucket_tables()
    bias_p = _bias_table(bucket_p, p['rel_bias'])
    bias_s = _bias_table(bucket_s, p['rel_bias'])[:, 0, :]
    sinks = p['attn_sinks'][l]

    proj_p = _proj(xp, mod_p, False, seq, p['g_pre_mix'][l], p['w_in'][l])
    proj_s = _proj(xs, mod_s, True, 1, p['g_pre_mix'][l], p['w_in'][l])

    o_p = _attn_prompt(proj_p, batch, seq, bias_p, sinks)
    kv_p = proj_p.reshape(batch, seq, IN_W)[:, seq - WINDOW:, Q_W:Q_W + 2 * KV_W]
    new_k_p = kv_p[..., :KV_W].reshape(1, batch, WINDOW, N_KV_HEADS, HEAD_DIM)
    new_v_p = kv_p[..., KV_W:].reshape(1, batch, WINDOW, N_KV_HEADS, HEAD_DIM)
    o_s3, new_k_s, new_v_s = _attn_sample(
        proj_s[:, :Q_W].reshape(nseq, N_HEADS, HEAD_DIM),
        proj_s[:, Q_W:Q_W + KV_W].reshape(nseq, 1, KV_W),
        proj_s[:, Q_W + KV_W:Q_W + 2 * KV_W].reshape(nseq, 1, KV_W),
        p['cache_win_k'][l].reshape(nseq, WINDOW, KV_W),
        p['cache_win_v'][l].reshape(nseq, WINDOW, KV_W), bias_s, sinks)
    o_s = o_s3.reshape(nseq, Q_W)

    lbr, lbi, bbr, bbi = _ssm_disc(p['ssm_a_re'][l], p['ssm_a_im'][l], p['ssm_log_dt'][l],
                                   p['ssm_b_re'][l], p['ssm_b_im'][l])
    lbr = lbr.reshape(1, SSM_W)
    lbi = lbi.reshape(1, SSM_W)
    bb = jnp.concatenate([_block_diag_tiles(bbr), _block_diag_tiles(bbi)], axis=-1).astype(BF16)
    c_re_t = p['ssm_c_re'][l].transpose(0, 2, 1)
    c_im_t = p['ssm_c_im'][l].transpose(0, 2, 1)
    cc = jnp.concatenate([_block_diag_tiles(c_re_t), -_block_diag_tiles(c_im_t)], axis=1).astype(BF16)
    u0 = Q_W + 2 * KV_W
    u_tb = proj_p[:, u0:u0 + D_SSM].reshape(batch, seq, D_SSM).transpose(1, 0, 2).reshape(seq * batch, D_SSM)
    zeros = jnp.zeros((batch, SSM_W), F32)
    y_tb, hr_p, hi_p = _ssm(u_tb, zeros, zeros, batch, SSM_TC, lbr, lbi, bb, cc,
                            p['ssm_d'][l], p['w_glu'][l], p['b_glu'][l])
    y_p = y_tb.reshape(seq, batch, D_SSM).transpose(1, 0, 2).reshape(batch * seq, D_SSM)
    y_s, hr_s, hi_s = _ssm(proj_s[:, u0:u0 + D_SSM], p['state_ssm_re'][l].reshape(nseq, SSM_W),
                           p['state_ssm_im'][l].reshape(nseq, SSM_W), nseq, 1, lbr, lbi, bb, cc,
                           p['ssm_d'][l], p['w_glu'][l], p['b_glu'][l])

    mix_p = _merge(o_p, y_p, proj_p, p['w_br_attn'][l], p['w_br_ssm'][l], p['w_out'][l])
    mix_s = _merge(o_s, y_s, proj_s, p['w_br_attn'][l], p['w_br_ssm'][l], p['w_out'][l])
    post = functools.partial(_post, g_post_mix=p['g_post_mix'][l], g_pre_ffn=p['g_pre_ffn'][l],
                             w_router_t=p['w_router'][l].T, b_router=p['b_router'][l])
    x1_p, h2_p, lg_p = post(xp, mix_p, mod_p, False, seq)
    x1_s, h2_s, lg_s = post(xs, mix_s, mod_s, True, 1)

    st = lambda h, n: h.reshape(1, n, N_SSM_GROUPS, SSM_STATE)
    return dict(
        mod_p=mod_p, mod_s=mod_s,
        p=dict(proj=proj_p, o_attn=o_p, new_k=new_k_p, new_v=new_v_p, y_ssm=y_p, h_re=st(hr_p, batch),
               h_im=st(hi_p, batch), x1=x1_p, h2=h2_p, logits_t=lg_p),
        s=dict(proj=proj_s, o_attn=o_s, new_k=new_k_s.reshape(1, nseq, WINDOW, N_KV_HEADS, HEAD_DIM),
               new_v=new_v_s.reshape(1, nseq, WINDOW, N_KV_HEADS, HEAD_DIM), y_ssm=y_s,
               h_re=st(hr_s, nseq), h_im=st(hi_s, nseq), x1=x1_s, h2=h2_s, logits_t=lg_s))


def _count_steps(c, step, n_max):
    out = jnp.zeros_like(c)
    for q in range(-(-n_max // step)):
        out = out + jnp.where(c > float(q * step), 1.0, 0.0)
    return out


def _route_kernel(lg_ref, pos_ref, gate_ref, texp_ref, trows_ref, ntiles_ref, pstart_ref, plen_ref):
    lg = lg_ref[...]
    e, tn = lg.shape
    erow = lax.broadcasted_iota(I32, (e, tn), 0).astype(F32)
    work = lg
    vals, hits = [], []
    for _ in range(TOP_K):
        m = jnp.max(work, axis=0, keepdims=True)
        idx = jnp.min(jnp.where(work == m, erow, float(e)), axis=0, keepdims=True)
        hit = erow == idx
        vals.append(m)
        hits.append(hit)
        work = jnp.where(hit, -jnp.inf, work)
    ex = [jnp.exp(v - vals[0]) for v in vals]
    den = ex[0] + ex[1] + ex[2] + ex[3]
    gate_ref[...] = jnp.concatenate([x / den for x in ex], axis=0)

    chosen = jnp.zeros((e, tn), F32)
    for hit in hits:
        chosen = chosen + jnp.where(hit, 1.0, 0.0)
    chosen_b = chosen.astype(BF16)
    tri = (lax.broadcasted_iota(I32, (LANE, LANE), 0) <= lax.broadcasted_iota(I32, (LANE, LANE), 1))
    tri = jnp.where(tri, 1.0, 0.0).astype(BF16)
    carry = jnp.zeros((e, 1), F32)
    ranks = []
    for b in range(tn // LANE):
        blk = chosen[:, b * LANE:(b + 1) * LANE]
        inc = _dot(chosen_b[:, b * LANE:(b + 1) * LANE], tri) + carry
        ranks.append(inc - blk)
        carry = inc[:, LANE - 1:LANE]
    rank = jnp.concatenate(ranks, axis=1)
    cnt_col = carry
    cnt_row = _dot_nt(jnp.ones((SUBLANE, tn), BF16), chosen_b)[0:1, :]

    tiles_col = _count_steps(cnt_col, EXP_TM, tn)
    tiles_row = _count_steps(cnt_row, EXP_TM, tn)
    ee_r = lax.broadcasted_iota(I32, (e, e), 0)
    ee_c = lax.broadcasted_iota(I32, (e, e), 1)
    tstart_col = jnp.sum(jnp.where(ee_c < ee_r, tiles_row, 0.0), axis=1, keepdims=True)
    ntiles = jnp.sum(tiles_row, axis=1, keepdims=True)
    rstart_col = tstart_col * float(EXP_TM)
    pos = [jnp.sum(jnp.where(hit, rstart_col + rank, 0.0), axis=0, keepdims=True) for hit in hits]
    pos_ref[...] = jnp.concatenate(pos, axis=0).astype(I32)

    mm = lax.broadcasted_iota(I32, (e, LANE), 1).astype(F32)
    e_col = lax.broadcasted_iota(I32, (e, LANE), 0).astype(F32)
    own = (mm >= tstart_col) & (mm < tstart_col + tiles_col)
    texp = jnp.sum(jnp.where(own, e_col, 0.0), axis=0, keepdims=True)
    rows_here = jnp.minimum(float(EXP_TM), cnt_col - (mm - tstart_col) * float(EXP_TM))
    trows = jnp.sum(jnp.where(own, rows_here, 0.0), axis=0, keepdims=True)
    last_e = jnp.max(jnp.where(tiles_col > 0.0, e_col, 0.0), axis=0, keepdims=True)
    texp = jnp.where(mm[0:1, :] < ntiles, texp, last_e)
    texp_ref[...] = texp.astype(I32)
    trows_ref[...] = trows.astype(I32)
    ntiles_ref[...] = jnp.broadcast_to(ntiles, (1, LANE)).astype(I32)
    nsub_col = _count_steps(cnt_col, EXP_SUB, tn)
    pstart_ref[...] = jnp.broadcast_to(rstart_col + cnt_col, (e, LANE)).astype(I32)
    plen_ref[...] = jnp.broadcast_to(nsub_col * float(EXP_SUB) - cnt_col, (e, LANE)).astype(I32)


def _route(lg_t):
    e, tn = lg_t.shape
    i32 = lambda shape: jax.ShapeDtypeStruct(shape, I32)
    return pl.pallas_call(
        _route_kernel,
        out_shape=[i32((TOP_K, tn)), jax.ShapeDtypeStruct((TOP_K, tn), F32),
                   i32((1, LANE)), i32((1, LANE)), i32((1, LANE)), i32((e, LANE)), i32((e, LANE))],
        compiler_params=pltpu.CompilerParams(vmem_limit_bytes=VMEM_LIMIT),
        name="route",
    )(lg_t)


def _max_tiles(n_tok):
    return (n_tok * TOP_K) // EXP_TM + N_EXPERTS


def _dispatch_kernel(pos_ref, pstart_ref, plen_ref, h2_ref, zero_ref, xs_ref, sem):
    i = pl.program_id(0)
    base = i * TOK_BLK

    def row_copy(src, s, d):
        return pltpu.make_async_copy(src.at[pl.ds(s, 1)], xs_ref.at[pl.ds(d, 1)], sem)

    def issue(t, c):
        for k in range(TOP_K):
            row_copy(h2_ref, base + t, pos_ref[k, t]).start()
        return c

    def drain(t, c):
        row_copy(h2_ref, 0, 0).wait()
        return c

    lax.fori_loop(0, TOK_BLK, issue, 0)
    lax.fori_loop(0, TOK_BLK * TOP_K, drain, 0)

    @pl.when(i == 0)
    def _():
        def per_expert(ex, c):
            n = plen_ref[ex]
            s = pstart_ref[ex]

            def zissue(r, cc):
                row_copy(zero_ref, 0, s + r).start()
                return cc

            def zdrain(r, cc):
                row_copy(zero_ref, 0, 0).wait()
                return cc

            lax.fori_loop(0, n, zissue, 0)
            lax.fori_loop(0, n, zdrain, 0)
            return c

        lax.fori_loop(0, N_EXPERTS, per_expert, 0)


def _dispatch(pos3, pstart, plen, h2_all):
    n_tok = h2_all.shape[0]
    n_rows = _max_tiles(n_tok) * EXP_TM
    smem = pl.BlockSpec(memory_space=pltpu.SMEM)
    hbm = pl.BlockSpec(memory_space=pl.ANY)
    return pl.pallas_call(
        _dispatch_kernel,
        grid=(n_tok // TOK_BLK,),
        in_specs=[pl.BlockSpec((None, TOP_K, TOK_BLK), lambda i: (i, 0, 0), memory_space=pltpu.SMEM),
                  smem, smem, hbm, hbm],
        out_specs=hbm,
        out_shape=jax.ShapeDtypeStruct((n_rows, D_MODEL), F32),
        scratch_shapes=[pltpu.SemaphoreType.DMA(())],
        compiler_params=_cparams(("arbitrary",)),
        name="dispatch",
    )(pos3, pstart, plen, h2_all, jnp.zeros((SUBLANE, D_MODEL), F32))


def _expert_kernel(texp_ref, trows_ref, nt_ref, x_ref, wg_ref, wl_ref, bg_ref, bl_ref, wd_ref, bd_ref,
                   o_ref, act_scr, wg_scr, wl_scr, wd_scr):
    m = pl.program_id(0)
    s = pl.program_id(1)
    valid = m < nt_ref[0]
    nsub = lax.shift_right_logical(trows_ref[m] + (EXP_SUB - 1), int(math.log2(EXP_SUB)))

    @pl.when(valid & (s < EXP_NF))
    def _():
        wg_scr[...] = wg_ref[...].astype(BF16)
        wl_scr[...] = wl_ref[...].astype(BF16)

        def body(r, c):
            r0 = pl.multiple_of(r * EXP_SUB, EXP_SUB)
            xb = x_ref[pl.ds(r0, EXP_SUB), :].astype(BF16)
            hg = _dot(xb, wg_scr[...]) + bg_ref[...]
            hl = _dot(xb, wl_scr[...]) + bl_ref[...]
            x_glu = jnp.minimum(hg, SWIGLU_LIMIT)
            x_lin = jnp.clip(hl, -SWIGLU_LIMIT, SWIGLU_LIMIT)
            act = x_glu * _sigmoid(SWIGLU_ALPHA * x_glu) * (x_lin + 1.0)
            act_scr[s, pl.ds(r0, EXP_SUB), :] = act.astype(BF16)
            return c

        lax.fori_loop(0, nsub, body, 0)

    @pl.when(valid & (s >= EXP_NF))
    def _():
        wd_scr[...] = wd_ref[...].astype(BF16)

        def body(r, c):
            r0 = pl.multiple_of(r * EXP_SUB, EXP_SUB)
            acc = jnp.broadcast_to(bd_ref[...], (EXP_SUB, EXP_TD))
            for f in range(EXP_NF):
                acc = acc + _dot(act_scr[f, pl.ds(r0, EXP_SUB), :], wd_scr[f * EXP_TF:(f + 1) * EXP_TF, :])
            o_ref[pl.ds(r0, EXP_SUB), :] = acc
            return c

        def zero(r, c):
            r0 = pl.multiple_of(r * EXP_SUB, EXP_SUB)
            o_ref[pl.ds(r0, EXP_SUB), :] = jnp.zeros((EXP_SUB, EXP_TD), F32)
            return c

        lax.fori_loop(0, nsub, body, 0)
        lax.fori_loop(nsub, EXP_TM // EXP_SUB, zero, 0)


def _experts(texp, trows, ntiles, xs, w_gate_up, b_gate_up, w_down, b_down):
    n_tiles = xs.shape[0] // EXP_TM
    nsteps = EXP_NF + EXP_ND

    def tile(m, nt):
        return jnp.minimum(m, nt[0] - 1)

    def fa(m, s, nt):
        return jnp.where(m < nt[0], jnp.minimum(s, EXP_NF - 1), EXP_NF - 1)

    def fb(m, s, nt):
        return jnp.where(m < nt[0], jnp.maximum(s - EXP_NF, 0), EXP_ND - 1)

    grid_spec = pltpu.PrefetchScalarGridSpec(
        num_scalar_prefetch=3,
        grid=(n_tiles, nsteps),
        in_specs=[
            pl.BlockSpec((EXP_TM, D_MODEL), lambda m, s, te, tr, nt: (tile(m, nt), 0)),
            pl.BlockSpec((None, D_MODEL, EXP_TF), lambda m, s, te, tr, nt: (te[m], 0, fa(m, s, nt))),
            pl.BlockSpec((None, D_MODEL, EXP_TF), lambda m, s, te, tr, nt: (te[m], 0, EXP_NF + fa(m, s, nt))),
            pl.BlockSpec((None, 1, EXP_TF), lambda m, s, te, tr, nt: (te[m], 0, fa(m, s, nt))),
            pl.BlockSpec((None, 1, EXP_TF), lambda m, s, te, tr, nt: (te[m], 0, EXP_NF + fa(m, s, nt))),
            pl.BlockSpec((None, D_FF, EXP_TD), lambda m, s, te, tr, nt: (te[m], 0, fb(m, s, nt))),
            pl.BlockSpec((None, 1, EXP_TD), lambda m, s, te, tr, nt: (te[m], 0, fb(m, s, nt))),
        ],
        out_specs=pl.BlockSpec((EXP_TM, EXP_TD), lambda m, s, te, tr, nt: (tile(m, nt), fb(m, s, nt))),
        scratch_shapes=[pltpu.VMEM((EXP_NF, EXP_TM, EXP_TF), BF16),
                        pltpu.VMEM((D_MODEL, EXP_TF), BF16), pltpu.VMEM((D_MODEL, EXP_TF), BF16),
                        pltpu.VMEM((D_FF, EXP_TD), BF16)],
    )
    return pl.pallas_call(
        _expert_kernel,
        grid_spec=grid_spec,
        out_shape=jax.ShapeDtypeStruct((xs.shape[0], D_MODEL), F32),
        compiler_params=_cparams(("arbitrary", "arbitrary")),
        name="experts",
    )(texp, trows, ntiles, xs, w_gate_up, w_gate_up,
      b_gate_up.reshape(N_EXPERTS, 1, 2 * D_FF), b_gate_up.reshape(N_EXPERTS, 1, 2 * D_FF),
      w_down, b_down.reshape(N_EXPERTS, 1, D_MODEL))


def _combine_kernel(pos_ref, gate_ref, ys_ref, x1_ref, g2_ref, gpf_ref, o_ref, buf, sem):
    def row_copy(s, k, t):
        return pltpu.make_async_copy(ys_ref.at[pl.ds(s, 1)], buf.at[k, pl.ds(t, 1)], sem)

    def issue(t, c):
        for k in range(TOP_K):
            row_copy(pos_ref[k, t], k, t).start()
        return c

    def drain(t, c):
        row_copy(0, 0, 0).wait()
        return c

    lax.fori_loop(0, TOK_BLK, issue, 0)
    lax.fori_loop(0, TOK_BLK * TOP_K, drain, 0)
    g = gate_ref[...]
    f = g[:, 0:1] * buf[0]
    for k in range(1, TOP_K):
        f = f + g[:, k:k + 1] * buf[k]
    o_ref[...] = x1_ref[...] + g2_ref[...] * _rms(f, gpf_ref[...])


def _combine(pos3, gates_t, ys, x1, mod, per_row, rows_per_seq, g_post_ffn, blk0):
    rows = x1.shape[0]
    return pl.pallas_call(
        _combine_kernel,
        grid=(rows // TOK_BLK,),
        in_specs=[pl.BlockSpec((None, TOP_K, TOK_BLK), lambda i: (blk0 + i, 0, 0), memory_space=pltpu.SMEM),
                  pl.BlockSpec((TOK_BLK, TOP_K), lambda i: (blk0 + i, 0)),
                  pl.BlockSpec(memory_space=pl.ANY),
                  pl.BlockSpec((TOK_BLK, D_MODEL), lambda i: (i, 0)),
                  _mod_spec(per_row, TOK_BLK, rows_per_seq, 5, 1),
                  pl.BlockSpec((1, D_MODEL), lambda i: (0, 0))],
        out_specs=pl.BlockSpec((TOK_BLK, D_MODEL), lambda i: (i, 0)),
        out_shape=jax.ShapeDtypeStruct((rows, D_MODEL), F32),
        scratch_shapes=[pltpu.VMEM((TOP_K, TOK_BLK, D_MODEL), F32), pltpu.SemaphoreType.DMA(())],
        compiler_params=_cparams(("arbitrary",)),
        name="combine",
    )(pos3, gates_t, ys, x1, mod, g_post_ffn.reshape(1, D_MODEL))


def _moe(fr, p, batch, seq, nseq):
    l = 0
    h2_all = jnp.concatenate([fr['p']['h2'], fr['s']['h2']], axis=0)
    lg_all = jnp.concatenate([fr['p']['logits_t'], fr['s']['logits_t']], axis=1)
    n_tok = h2_all.shape[0]
    pos, gates, texp, trows, ntiles, pstart, plen = _route(lg_all)
    pos3 = pos.reshape(TOP_K, n_tok // TOK_BLK, TOK_BLK).transpose(1, 0, 2)
    xs = _dispatch(pos3, pstart[:, 0], plen[:, 0], h2_all)
    ys = _experts(texp[0], trows[0], ntiles[0, :1], xs, p['w_gate_up'][l], p['b_gate_up'][l],
                  p['w_down'][l], p['b_down'][l])
    gates_t = gates.T
    y_p = _combine(pos3, gates_t, ys, fr['p']['x1'], fr['mod_p'], False, seq, p['g_post_ffn'][l], 0)
    y_s = _combine(pos3, gates_t, ys, fr['s']['x1'], fr['mod_s'], True, 1, p['g_post_ffn'][l],
                   batch * seq // TOK_BLK)
    return y_p, y_s


def kernel(x_prompt, x_sample, c_prompt, c_sample, cache_win_k, cache_win_v, state_ssm_re, state_ssm_im, w_ada, b_ada, g_pre_mix, g_post_mix, g_pre_ffn, g_post_ffn, w_in, attn_sinks, rel_bias, ssm_a_re, ssm_a_im, ssm_log_dt, ssm_b_re, ssm_b_im, ssm_c_re, ssm_c_im, ssm_d, w_glu, b_glu, w_br_attn, w_br_ssm, w_out, w_router, b_router, w_gate_up, b_gate_up, w_down, b_down):
    p = dict(locals())
    batch, seq, _ = x_prompt.shape
    nseq = x_sample.shape[0]
    fr = _front(p)
    y_p, y_s = _moe(fr, p, batch, seq, nseq)
    fp, fs = fr['p'], fr['s']
    return (y_p.reshape(batch, seq, D_MODEL), y_s.reshape(nseq, 1, D_MODEL),
            fp['new_k'], fp['new_v'], fp['h_re'], fp['h_im'],
            fs['new_k'], fs['new_v'], fs['h_re'], fs['h_im'])
```

```python
import functools
import math

import numpy as np
import jax
import jax.numpy as jnp
from jax import lax
from jax.experimental import pallas as pl
from jax.experimental.pallas import tpu as pltpu

F32 = jnp.float32
BF16 = jnp.bfloat16
I32 = jnp.int32

D_MODEL = 2048
N_HEADS = 16
N_KV_HEADS = 4
HEAD_DIM = 64
Q_GROUP = N_HEADS // N_KV_HEADS
WINDOW = 128
N_BUCKETS = 32
MAX_EXACT = N_BUCKETS // 2
MAX_DISTANCE = 128
D_SSM = 1024
SSM_GROUP = 16
N_SSM_GROUPS = 64
SSM_STATE = 64
N_EXPERTS = 32
TOP_K = 4
D_FF = 2048
SWIGLU_LIMIT = 7.0
SWIGLU_ALPHA = 1.702
NORM_EPS = 1e-6
NEG_INF = -1e30
Q_W = N_HEADS * HEAD_DIM
KV_W = N_KV_HEADS * HEAD_DIM
IN_W = Q_W + 2 * KV_W + D_SSM + 2 * D_MODEL
SSM_W = N_SSM_GROUPS * SSM_STATE

LANE = 128
SUBLANE = 8
VMEM_LIMIT = 56 * 1024 * 1024

PROJ_TM = 1024
PROJ_TN = 512
MERGE_TM = 512
MERGE_TK = 512
SSM_TC = 128
SSM_LB = 512
SSM_TILES = D_SSM // LANE
TOK_BLK = 128
EXP_TM = 1280
EXP_BIG = 512
EXP_SUB = 128
EXP_TF = 256
EXP_TD = 512
EXP_NF = D_FF // EXP_TF
EXP_ND = D_MODEL // EXP_TD


def _cparams(sem):
    return pltpu.CompilerParams(dimension_semantics=sem, vmem_limit_bytes=VMEM_LIMIT)


def _sigmoid(x):
    return 1.0 / (1.0 + jnp.exp(-x))


def _rms(x, g):
    return x * lax.rsqrt(jnp.mean(x * x, axis=-1, keepdims=True) + NORM_EPS) * g


def _dot(a, b):
    return jnp.dot(a, b, preferred_element_type=F32)


def _dot_nt(a, b):
    return lax.dot_general(a, b, (((1,), (1,)), ((), ())), preferred_element_type=F32)


def _ada_kernel(c_ref, w_ref, b_ref, o_ref):
    c = c_ref[...]
    s = (c * _sigmoid(c)).astype(BF16)
    o_ref[...] = _dot(s, w_ref[...].astype(BF16)) + b_ref[...]


def _ada(c_all, w_ada, b_ada):
    rows = c_all.shape[0]
    tn = 1024
    n = w_ada.shape[1]
    return pl.pallas_call(
        _ada_kernel,
        grid=(n // tn,),
        in_specs=[pl.BlockSpec((rows, D_MODEL), lambda j: (0, 0)),
                  pl.BlockSpec((D_MODEL, tn), lambda j: (0, j)),
                  pl.BlockSpec((1, tn), lambda j: (0, j))],
        out_specs=pl.BlockSpec((rows, tn), lambda j: (0, j)),
        out_shape=jax.ShapeDtypeStruct((rows, n), F32),
        compiler_params=_cparams(("arbitrary",)),
        name="ada",
    )(c_all, w_ada, b_ada.reshape(1, n))


def _t5_bucket_np(dist):
    n = np.maximum(dist, 0)
    nf = np.maximum(n, 1).astype(np.float64)
    large = MAX_EXACT + (np.log(nf / MAX_EXACT) / math.log(MAX_DISTANCE / MAX_EXACT)
                         * (N_BUCKETS - MAX_EXACT)).astype(np.int32)
    large = np.minimum(large, N_BUCKETS - 1)
    return np.where(n < MAX_EXACT, n, large).astype(np.int32)


def _bias_kernel(bucket_ref, rb_ref, o_ref):
    h = pl.program_id(0)
    bucket = bucket_ref[...]
    acc = jnp.full(bucket.shape, NEG_INF, F32)
    for b in range(N_BUCKETS):
        acc = jnp.where(bucket == b, rb_ref[b, h], acc)
    o_ref[...] = acc


def _bias_table(bucket_np, rel_bias):
    r, c = bucket_np.shape
    return pl.pallas_call(
        _bias_kernel,
        grid=(N_HEADS,),
        in_specs=[pl.BlockSpec((r, c), lambda h: (0, 0)),
                  pl.BlockSpec(memory_space=pltpu.SMEM)],
        out_specs=pl.BlockSpec((None, r, c), lambda h: (h, 0, 0)),
        out_shape=jax.ShapeDtypeStruct((N_HEADS, r, c), F32),
        compiler_params=_cparams(("arbitrary",)),
        name="bias",
    )(jnp.asarray(bucket_np), rel_bias)


def _proj_kernel(x_ref, sc_ref, sh_ref, g_ref, w_ref, o_ref, h_scr):
    @pl.when(pl.program_id(1) == 0)
    def _():
        h = _rms(x_ref[...], g_ref[...]) * (1.0 + sc_ref[...]) + sh_ref[...]
        h_scr[...] = h.astype(BF16)

    o_ref[...] = _dot(h_scr[...], w_ref[...].astype(BF16))


def _mod_spec(per_row, tm, rows_per_seq, col, nargs):
    if per_row:
        if nargs == 1:
            return pl.BlockSpec((tm, D_MODEL), lambda i: (i, col))
        return pl.BlockSpec((tm, D_MODEL), lambda i, j: (i, col))
    tiles_per_seq = rows_per_seq // tm
    if nargs == 1:
        return pl.BlockSpec((None, 1, D_MODEL), lambda i: (i // tiles_per_seq, 0, col))
    return pl.BlockSpec((None, 1, D_MODEL), lambda i, j: (i // tiles_per_seq, 0, col))


def _proj(x, mod, per_row, rows_per_seq, g_pre, w_in):
    rows = x.shape[0]
    tm = min(PROJ_TM, rows)
    return pl.pallas_call(
        _proj_kernel,
        grid=(rows // tm, IN_W // PROJ_TN),
        in_specs=[pl.BlockSpec((tm, D_MODEL), lambda i, j: (i, 0)),
                  _mod_spec(per_row, tm, rows_per_seq, 1, 2),
                  _mod_spec(per_row, tm, rows_per_seq, 0, 2),
                  pl.BlockSpec((1, D_MODEL), lambda i, j: (0, 0)),
                  pl.BlockSpec((D_MODEL, PROJ_TN), lambda i, j: (0, j))],
        out_specs=pl.BlockSpec((tm, PROJ_TN), lambda i, j: (i, j)),
        out_shape=jax.ShapeDtypeStruct((rows, IN_W), F32),
        scratch_shapes=[pltpu.VMEM((tm, D_MODEL), BF16)],
        compiler_params=_cparams(("arbitrary", "arbitrary")),
        name="proj",
    )(x, mod, mod, g_pre.reshape(1, D_MODEL), w_in)


def _attn_prompt_kernel(q_ref, kc_ref, kp_ref, vc_ref, vp_ref, bias_ref, sink_ref, o_ref):
    has_prev = pl.program_id(1) > 0
    q = q_ref[...]
    k = jnp.concatenate([kp_ref[...], kc_ref[...]], axis=0)
    v = jnp.concatenate([vp_ref[...], vc_ref[...]], axis=0)
    col = lax.broadcasted_iota(I32, (WINDOW, 2 * WINDOW), 1)
    key_ok = (col >= WINDOW) | has_prev
    outs = []
    for g in range(N_KV_HEADS):
        kg = k[:, g * HEAD_DIM:(g + 1) * HEAD_DIM].astype(BF16)
        vg = v[:, g * HEAD_DIM:(g + 1) * HEAD_DIM].astype(BF16)
        for hh in range(Q_GROUP):
            h = g * Q_GROUP + hh
            qh = q[:, h * HEAD_DIM:(h + 1) * HEAD_DIM].astype(BF16)
            s = _dot_nt(qh, kg) * (HEAD_DIM ** -0.5) + bias_ref[h]
            s = jnp.where(key_ok, s, NEG_INF)
            sink = sink_ref[h]
            m = jnp.maximum(jnp.max(s, axis=-1, keepdims=True), sink)
            p = jnp.exp(s - m)
            den = jnp.sum(p, axis=-1, keepdims=True) + jnp.exp(sink - m)
            outs.append(_dot(p.astype(BF16), vg) / den)
    o_ref[...] = jnp.concatenate(outs, axis=-1)


def _attn_prompt(proj, batch, seq, bias, sinks):
    nb = seq // WINDOW
    kcol = Q_W // KV_W
    vcol = kcol + 1
    cur = lambda c: (lambda b, n: (b * nb + n, c))
    prev = lambda c: (lambda b, n: (b * nb + jnp.maximum(n - 1, 0), c))
    return pl.pallas_call(
        _attn_prompt_kernel,
        grid=(batch, nb),
        in_specs=[pl.BlockSpec((WINDOW, Q_W), cur(0)),
                  pl.BlockSpec((WINDOW, KV_W), cur(kcol)),
                  pl.BlockSpec((WINDOW, KV_W), prev(kcol)),
                  pl.BlockSpec((WINDOW, KV_W), cur(vcol)),
                  pl.BlockSpec((WINDOW, KV_W), prev(vcol)),
                  pl.BlockSpec((N_HEADS, WINDOW, 2 * WINDOW), lambda b, n: (0, 0, 0)),
                  pl.BlockSpec(memory_space=pltpu.SMEM)],
        out_specs=pl.BlockSpec((WINDOW, Q_W), lambda b, n: (b * nb + n, 0)),
        out_shape=jax.ShapeDtypeStruct((batch * seq, Q_W), F32),
        compiler_params=_cparams(("arbitrary", "arbitrary")),
        name="attn_prompt",
    )(proj, proj, proj, proj, proj, bias, sinks)


def _attn_sample_kernel(q_ref, kn_ref, vn_ref, ck_ref, cv_ref, bias_ref, sink_ref,
                        o_ref, nk_ref, nv_ref):
    tb = q_ref.shape[0]
    row = lax.broadcasted_iota(I32, (tb, WINDOW, KV_W), 1)
    last = row == WINDOW - 1
    nk = jnp.where(last, kn_ref[...], pltpu.roll(ck_ref[...], WINDOW - 1, 1))
    nv = jnp.where(last, vn_ref[...], pltpu.roll(cv_ref[...], WINDOW - 1, 1))
    nk_ref[...] = nk
    nv_ref[...] = nv
    lane_grp = lax.broadcasted_iota(I32, (N_HEADS, KV_W), 1) // HEAD_DIM
    head_grp = lax.broadcasted_iota(I32, (N_HEADS, KV_W), 0) // Q_GROUP
    gmask = (lane_grp == head_grp).astype(F32)
    q = q_ref[...]
    qrow = jnp.concatenate([q] * N_KV_HEADS, axis=-1) * gmask
    s = jnp.einsum('bhc,brc->bhr', qrow.astype(BF16), nk.astype(BF16),
                   preferred_element_type=F32) * (HEAD_DIM ** -0.5)
    s = s + bias_ref[...]
    sink = sink_ref[...]
    m = jnp.maximum(jnp.max(s, axis=-1, keepdims=True), sink)
    p = jnp.exp(s - m)
    den = jnp.sum(p, axis=-1, keepdims=True) + jnp.exp(sink - m)
    o = jnp.einsum('bhr,brc->bhc', p.astype(BF16), nv.astype(BF16),
                   preferred_element_type=F32) * gmask
    o64 = o[..., 0:HEAD_DIM]
    for g in range(1, N_KV_HEADS):
        o64 = o64 + o[..., g * HEAD_DIM:(g + 1) * HEAD_DIM]
    o_ref[...] = o64 / den


def _attn_sample(q3, kn, vn, cache_k, cache_v, bias, sinks):
    nseq = q3.shape[0]
    tb = 16
    seq3 = lambda w: pl.BlockSpec((tb, WINDOW, w), lambda i: (i, 0, 0))
    return pl.pallas_call(
        _attn_sample_kernel,
        grid=(nseq // tb,),
        in_specs=[pl.BlockSpec((tb, N_HEADS, HEAD_DIM), lambda i: (i, 0, 0)),
                  pl.BlockSpec((tb, 1, KV_W), lambda i: (i, 0, 0)),
                  pl.BlockSpec((tb, 1, KV_W), lambda i: (i, 0, 0)),
                  seq3(KV_W), seq3(KV_W),
                  pl.BlockSpec((N_HEADS, WINDOW), lambda i: (0, 0)),
                  pl.BlockSpec((N_HEADS, 1), lambda i: (0, 0))],
        out_specs=[pl.BlockSpec((tb, N_HEADS, HEAD_DIM), lambda i: (i, 0, 0)),
                   seq3(KV_W), seq3(KV_W)],
        out_shape=[jax.ShapeDtypeStruct((nseq, N_HEADS, HEAD_DIM), F32),
                   jax.ShapeDtypeStruct((nseq, WINDOW, KV_W), F32),
                   jax.ShapeDtypeStruct((nseq, WINDOW, KV_W), F32)],
        compiler_params=_cparams(("arbitrary",)),
        name="attn_sample",
    )(q3, kn, vn, cache_k, cache_v, bias, sinks.reshape(N_HEADS, 1))


def _ssm_disc_kernel(are_ref, aim_ref, ldt_ref, bre_ref, bim_ref,
                     lbr_ref, lbi_ref, bbr_ref, bbi_ref):
    a_re = are_ref[...]
    a_im = aim_ref[...]
    dt = jnp.exp(ldt_ref[...])
    lam_re = a_re * dt
    lam_im = a_im * dt
    mag = jnp.exp(lam_re)
    lb_re = mag * jnp.cos(lam_im)
    lb_im = mag * jnp.sin(lam_im)
    den = a_re * a_re + a_im * a_im
    nr = lb_re - 1.0
    ni = lb_im
    coef_re = (nr * a_re + ni * a_im) / den
    coef_im = (ni * a_re - nr * a_im) / den
    b_re = bre_ref[...]
    b_im = bim_ref[...]
    lbr_ref[...] = lb_re
    lbi_ref[...] = lb_im
    bbr_ref[...] = coef_re * b_re - coef_im * b_im
    bbi_ref[...] = coef_re * b_im + coef_im * b_re


def _ssm_disc(a_re, a_im, log_dt, b_re, b_im):
    g, p, j = N_SSM_GROUPS, SSM_STATE, SSM_GROUP
    vec = jax.ShapeDtypeStruct((g, 1, p), F32)
    mat = jax.ShapeDtypeStruct((g, j, p), F32)
    return pl.pallas_call(
        _ssm_disc_kernel,
        out_shape=[vec, vec, mat, mat],
        name="ssm_disc",
    )(a_re.reshape(g, 1, p), a_im.reshape(g, 1, p), log_dt.reshape(g, 1, 1),
      b_re.transpose(0, 2, 1), b_im.transpose(0, 2, 1))


def _block_diag_tiles(x):
    a, b = x.shape[1], x.shape[2]
    eye = jnp.eye(SUBLANE, dtype=x.dtype)
    y = jnp.einsum('kgab,gh->kgahb', x.reshape(SSM_TILES, SUBLANE, a, b), eye)
    return y.reshape(SSM_TILES, SUBLANE * a, SUBLANE * b)


def _gelu_tanh(x):
    return 0.5 * x * (1.0 + jnp.tanh(math.sqrt(2.0 / math.pi) * (x + 0.044715 * (x * x * x))))


def _ssm_kernel(u_ref, h0r_ref, h0i_ref, lbr_ref, lbi_ref, bb_ref, cc_ref, d_ref, wglu_ref, bglu_ref,
                y_ref, hTr_ref, hTi_ref, hre, him, st_r, st_i, *, nseq, tc):
    paired = nseq == 4

    @pl.when(pl.program_id(0) == 0)
    def _():
        if paired:
            st_r[...] = jnp.concatenate([h0r_ref[...], h0r_ref[...]], axis=0)
            st_i[...] = jnp.concatenate([h0i_ref[...], h0i_ref[...]], axis=0)
        else:
            st_r[...] = h0r_ref[...]
            st_i[...] = h0i_ref[...]

    u = u_ref[...]
    ub = u.astype(BF16)
    half = SSM_W // SSM_TILES
    for k in range(SSM_TILES):
        bu = _dot(ub[:, k * LANE:(k + 1) * LANE], bb_ref[k])
        hre[:, k * half:(k + 1) * half] = bu[:, :half]
        him[:, k * half:(k + 1) * half] = bu[:, half:]

    for blk in range(SSM_W // SSM_LB):
        sl = slice(blk * SSM_LB, (blk + 1) * SSM_LB)
        ar = lbr_ref[:, sl]
        ai = lbi_ref[:, sl]
        if paired:
            lower = lax.broadcasted_iota(I32, (SUBLANE, SSM_LB), 0) < nseq

            def body(m, carry):
                sr, si = carry
                r0 = pl.multiple_of(m * SUBLANE, SUBLANE)
                br = hre[pl.ds(r0, SUBLANE), sl]
                bi = him[pl.ds(r0, SUBLANE), sl]
                xr = pltpu.roll(sr, nseq, 0)
                xi = pltpu.roll(si, nseq, 0)
                h1r = ar * xr - ai * xi + br
                h1i = ar * xi + ai * xr + bi
                yr = pltpu.roll(h1r, nseq, 0)
                yi = pltpu.roll(h1i, nseq, 0)
                h2r = ar * yr - ai * yi + br
                h2i = ar * yi + ai * yr + bi
                hre[pl.ds(r0, SUBLANE), sl] = jnp.where(lower, h1r, h2r)
                him[pl.ds(r0, SUBLANE), sl] = jnp.where(lower, h1i, h2i)
                return h2r, h2i

            sr, si = lax.fori_loop(0, tc * nseq // SUBLANE, body, (st_r[:, sl], st_i[:, sl]))
        else:
            def body(t, carry):
                sr, si = carry
                r0 = pl.multiple_of(t * nseq, SUBLANE)
                br = hre[pl.ds(r0, nseq), sl]
                bi = him[pl.ds(r0, nseq), sl]
                nr = ar * sr - ai * si + br
                ni = ar * si + ai * sr + bi
                hre[pl.ds(r0, nseq), sl] = nr
                him[pl.ds(r0, nseq), sl] = ni
                return nr, ni

            sr, si = lax.fori_loop(0, tc, body, (st_r[:, sl], st_i[:, sl]))
        st_r[:, sl] = sr
        st_i[:, sl] = si

    ys = []
    for k in range(SSM_TILES):
        hr = hre[:, k * half:(k + 1) * half].astype(BF16)
        hi = him[:, k * half:(k + 1) * half].astype(BF16)
        yk = _dot(hr, cc_ref[k, :half, :]) + _dot(hi, cc_ref[k, half:, :])
        ys.append(yk + d_ref[:, k * LANE:(k + 1) * LANE] * u[:, k * LANE:(k + 1) * LANE])
    y = _gelu_tanh(jnp.concatenate(ys, axis=-1))
    z = _dot(y.astype(BF16), wglu_ref[...].astype(BF16)) + bglu_ref[...]
    y_ref[...] = y * _sigmoid(z)

    if paired:
        hTr_ref[...] = st_r[nseq:, :]
        hTi_ref[...] = st_i[nseq:, :]
    else:
        hTr_ref[...] = st_r[...]
        hTi_ref[...] = st_i[...]


def _ssm(u_rows, h0_re, h0_im, nseq, tc, lbr, lbi, bb, cc, d, w_glu, b_glu):
    rows = u_rows.shape[0]
    r = nseq * tc
    st_rows = max(nseq, SUBLANE)
    const2 = lambda shape: pl.BlockSpec(shape, lambda c: (0, 0))
    const3 = lambda shape: pl.BlockSpec(shape, lambda c: (0, 0, 0))
    return pl.pallas_call(
        functools.partial(_ssm_kernel, nseq=nseq, tc=tc),
        grid=(rows // r,),
        in_specs=[pl.BlockSpec((r, D_SSM), lambda c: (c, 0)),
                  const2((nseq, SSM_W)), const2((nseq, SSM_W)),
                  const2((1, SSM_W)), const2((1, SSM_W)),
                  const3((SSM_TILES, LANE, 2 * SSM_W // SSM_TILES)),
                  const3((SSM_TILES, 2 * SSM_W // SSM_TILES, LANE)),
                  const2((1, D_SSM)), const2((D_SSM, D_SSM)), const2((1, D_SSM))],
        out_specs=[pl.BlockSpec((r, D_SSM), lambda c: (c, 0)),
                   const2((nseq, SSM_W)), const2((nseq, SSM_W))],
        out_shape=[jax.ShapeDtypeStruct((rows, D_SSM), F32),
                   jax.ShapeDtypeStruct((nseq, SSM_W), F32),
                   jax.ShapeDtypeStruct((nseq, SSM_W), F32)],
        scratch_shapes=[pltpu.VMEM((r, SSM_W), F32), pltpu.VMEM((r, SSM_W), F32),
                        pltpu.VMEM((st_rows, SSM_W), F32), pltpu.VMEM((st_rows, SSM_W), F32)],
        compiler_params=_cparams(("arbitrary",)),
        name="ssm",
    )(u_rows, h0_re, h0_im, lbr, lbi, bb, cc, d.reshape(1, D_SSM), w_glu, b_glu.reshape(1, D_SSM))


def _split_bf16(x):
    hi = x.astype(BF16)
    lo = (x - hi.astype(F32)).astype(BF16)
    return hi, lo


def _merge_kernel(o_ref, y_ref, ga_ref, gs_ref, wa_ref, ws_ref, wo_ref, mix_ref):
    j = pl.program_id(1)
    a = _dot(o_ref[...].astype(BF16), wa_ref[...].astype(BF16))
    s = _dot(y_ref[...].astype(BF16), ws_ref[...].astype(BF16))
    merged = _sigmoid(ga_ref[...]) * a + _sigmoid(gs_ref[...]) * s
    contrib = _dot(merged.astype(BF16), wo_ref[...].astype(BF16))

    @pl.when(j == 0)
    def _():
        mix_ref[...] = contrib

    @pl.when(j > 0)
    def _():
        mix_ref[...] += contrib


def _merge(o_attn, y_ssm, proj, w_br_attn, w_br_ssm, w_out):
    rows = o_attn.shape[0]
    tm = min(MERGE_TM, rows)
    nk = D_MODEL // MERGE_TK
    ga0 = (Q_W + 2 * KV_W + D_SSM) // MERGE_TK
    gs0 = ga0 + nk
    row2 = lambda w: pl.BlockSpec((tm, w), lambda i, j: (i, 0))
    return pl.pallas_call(
        _merge_kernel,
        grid=(rows // tm, nk),
        in_specs=[row2(Q_W), row2(D_SSM),
                  pl.BlockSpec((tm, MERGE_TK), lambda i, j: (i, ga0 + j)),
                  pl.BlockSpec((tm, MERGE_TK), lambda i, j: (i, gs0 + j)),
                  pl.BlockSpec((Q_W, MERGE_TK), lambda i, j: (0, j)),
                  pl.BlockSpec((D_SSM, MERGE_TK), lambda i, j: (0, j)),
                  pl.BlockSpec((MERGE_TK, D_MODEL), lambda i, j: (j, 0))],
        out_specs=row2(D_MODEL),
        out_shape=jax.ShapeDtypeStruct((rows, D_MODEL), F32),
        compiler_params=_cparams(("arbitrary", "arbitrary")),
        name="merge",
    )(o_attn, y_ssm, proj, proj, w_br_attn, w_br_ssm, w_out)


def _post_kernel(x_ref, mix_ref, gpm_ref, g1_ref, sc2_ref, sh2_ref, gpf_ref, wrt_ref, br_ref,
                 x1_ref, h2_ref, lg_ref):
    x1 = x_ref[...] + g1_ref[...] * _rms(mix_ref[...], gpm_ref[...])
    x1_ref[...] = x1
    h2 = _rms(x1, gpf_ref[...]) * (1.0 + sc2_ref[...]) + sh2_ref[...]
    h2_ref[...] = h2
    h_hi, h_lo = _split_bf16(h2)
    w_hi, w_lo = _split_bf16(wrt_ref[...])
    lg_ref[...] = (_dot_nt(w_hi, h_hi) + _dot_nt(w_hi, h_lo) + _dot_nt(w_lo, h_hi)) + br_ref[...]


def _post(x, mix, mod, per_row, rows_per_seq, g_post_mix, g_pre_ffn, w_router_t, b_router):
    rows = x.shape[0]
    tm = min(MERGE_TM, rows)
    row = pl.BlockSpec((tm, D_MODEL), lambda i: (i, 0))
    vec = pl.BlockSpec((1, D_MODEL), lambda i: (0, 0))
    return pl.pallas_call(
        _post_kernel,
        grid=(rows // tm,),
        in_specs=[row, row, vec,
                  _mod_spec(per_row, tm, rows_per_seq, 2, 1),
                  _mod_spec(per_row, tm, rows_per_seq, 4, 1),
                  _mod_spec(per_row, tm, rows_per_seq, 3, 1),
                  vec,
                  pl.BlockSpec((N_EXPERTS, D_MODEL), lambda i: (0, 0)),
                  pl.BlockSpec((N_EXPERTS, 1), lambda i: (0, 0))],
        out_specs=[row, row, pl.BlockSpec((N_EXPERTS, tm), lambda i: (0, i))],
        out_shape=[jax.ShapeDtypeStruct((rows, D_MODEL), F32),
                   jax.ShapeDtypeStruct((rows, D_MODEL), F32),
                   jax.ShapeDtypeStruct((N_EXPERTS, rows), F32)],
        compiler_params=_cparams(("arbitrary",)),
        name="post",
    )(x, mix, g_post_mix.reshape(1, D_MODEL), mod, mod, mod, g_pre_ffn.reshape(1, D_MODEL),
      w_router_t, b_router.reshape(N_EXPERTS, 1))


def _bucket_tables():
    ql = np.arange(WINDOW)[:, None]
    kl = np.arange(2 * WINDOW)[None, :]
    dist = ql + WINDOW - kl
    prompt = np.where((dist >= 0) & (dist < WINDOW), _t5_bucket_np(dist), -1).astype(np.int32)
    d_s = (WINDOW - 1 - np.arange(WINDOW))[None, :]
    sample = np.broadcast_to(_t5_bucket_np(d_s), (SUBLANE, WINDOW)).astype(np.int32)
    return prompt, sample


def _front(p):
    l = 0
    batch, seq, _ = p['x_prompt'].shape
    nseq = p['x_sample'].shape[0]
    xp = p['x_prompt'].reshape(batch * seq, D_MODEL)
    xs = p['x_sample'].reshape(nseq, D_MODEL)

    c_all = jnp.concatenate([p['c_prompt'], p['c_sample'],
                             jnp.zeros((SUBLANE - (batch + nseq) % SUBLANE, D_MODEL), F32)], axis=0)
    mod = _ada(c_all, p['w_ada'][l], p['b_ada'][l])
    mod_p = mod[:batch].reshape(batch, 1, 6 * D_MODEL)
    mod_s = mod[batch:batch + nseq]

    bucket_p, bucket_s = _bucket_tables()
    bias_p = _bias_table(bucket_p, p['rel_bias'])
    bias_s = _bias_table(bucket_s, p['rel_bias'])[:, 0, :]
    sinks = p['attn_sinks'][l]

    proj_p = _proj(xp, mod_p, False, seq, p['g_pre_mix'][l], p['w_in'][l])
    proj_s = _proj(xs, mod_s, True, 1, p['g_pre_mix'][l], p['w_in'][l])

    o_p = _attn_prompt(proj_p, batch, seq, bias_p, sinks)
    kv_p = proj_p.reshape(batch, seq, IN_W)[:, seq - WINDOW:, Q_W:Q_W + 2 * KV_W]
    new_k_p = kv_p[..., :KV_W].reshape(1, batch, WINDOW, N_KV_HEADS, HEAD_DIM)
    new_v_p = kv_p[..., KV_W:].reshape(1, batch, WINDOW, N_KV_HEADS, HEAD_DIM)
    o_s3, new_k_s, new_v_s = _attn_sample(
        proj_s[:, :Q_W].reshape(nseq, N_HEADS, HEAD_DIM),
        proj_s[:, Q_W:Q_W + KV_W].reshape(nseq, 1, KV_W),
        proj_s[:, Q_W + KV_W:Q_W + 2 * KV_W].reshape(nseq, 1, KV_W),
        p['cache_win_k'][l].reshape(nseq, WINDOW, KV_W),
        p['cache_win_v'][l].reshape(nseq, WINDOW, KV_W), bias_s, sinks)
    o_s = o_s3.reshape(nseq, Q_W)

    lbr, lbi, bbr, bbi = _ssm_disc(p['ssm_a_re'][l], p['ssm_a_im'][l], p['ssm_log_dt'][l],
                                   p['ssm_b_re'][l], p['ssm_b_im'][l])
    lbr = lbr.reshape(1, SSM_W)
    lbi = lbi.reshape(1, SSM_W)
    bb = jnp.concatenate([_block_diag_tiles(bbr), _block_diag_tiles(bbi)], axis=-1).astype(BF16)
    c_re_t = p['ssm_c_re'][l].transpose(0, 2, 1)
    c_im_t = p['ssm_c_im'][l].transpose(0, 2, 1)
    cc = jnp.concatenate([_block_diag_tiles(c_re_t), -_block_diag_tiles(c_im_t)], axis=1).astype(BF16)
    u0 = Q_W + 2 * KV_W
    u_tb = proj_p[:, u0:u0 + D_SSM].reshape(batch, seq, D_SSM).transpose(1, 0, 2).reshape(seq * batch, D_SSM)
    zeros = jnp.zeros((batch, SSM_W), F32)
    y_tb, hr_p, hi_p = _ssm(u_tb, zeros, zeros, batch, SSM_TC, lbr, lbi, bb, cc,
                            p['ssm_d'][l], p['w_glu'][l], p['b_glu'][l])
    y_p = y_tb.reshape(seq, batch, D_SSM).transpose(1, 0, 2).reshape(batch * seq, D_SSM)
    y_s, hr_s, hi_s = _ssm(proj_s[:, u0:u0 + D_SSM], p['state_ssm_re'][l].reshape(nseq, SSM_W),
                           p['state_ssm_im'][l].reshape(nseq, SSM_W), nseq, 1, lbr, lbi, bb, cc,
                           p['ssm_d'][l], p['w_glu'][l], p['b_glu'][l])

    mix_p = _merge(o_p, y_p, proj_p, p['w_br_attn'][l], p['w_br_ssm'][l], p['w_out'][l])
    mix_s = _merge(o_s, y_s, proj_s, p['w_br_attn'][l], p['w_br_ssm'][l], p['w_out'][l])
    post = functools.partial(_post, g_post_mix=p['g_post_mix'][l], g_pre_ffn=p['g_pre_ffn'][l],
                             w_router_t=p['w_router'][l].T, b_router=p['b_router'][l])
    x1_p, h2_p, lg_p = post(xp, mix_p, mod_p, False, seq)
    x1_s, h2_s, lg_s = post(xs, mix_s, mod_s, True, 1)

    st = lambda h, n: h.reshape(1, n, N_SSM_GROUPS, SSM_STATE)
    return dict(
        mod_p=mod_p, mod_s=mod_s,
        p=dict(proj=proj_p, o_attn=o_p, new_k=new_k_p, new_v=new_v_p, y_ssm=y_p, h_re=st(hr_p, batch),
               h_im=st(hi_p, batch), x1=x1_p, h2=h2_p, logits_t=lg_p),
        s=dict(proj=proj_s, o_attn=o_s, new_k=new_k_s.reshape(1, nseq, WINDOW, N_KV_HEADS, HEAD_DIM),
               new_v=new_v_s.reshape(1, nseq, WINDOW, N_KV_HEADS, HEAD_DIM), y_ssm=y_s,
               h_re=st(hr_s, nseq), h_im=st(hi_s, nseq), x1=x1_s, h2=h2_s, logits_t=lg_s))


def _count_steps(c, step, n_max):
    out = jnp.zeros_like(c)
    for q in range(-(-n_max // step)):
        out = out + jnp.where(c > float(q * step), 1.0, 0.0)
    return out


def _route_kernel(lg_ref, pos_ref, gate_ref, texp_ref, trows_ref, ntiles_ref, pstart_ref, plen_ref):
    lg = lg_ref[...]
    e, tn = lg.shape
    erow = lax.broadcasted_iota(I32, (e, tn), 0).astype(F32)
    work = lg
    vals, hits = [], []
    for _ in range(TOP_K):
        m = jnp.max(work, axis=0, keepdims=True)
        idx = jnp.min(jnp.where(work == m, erow, float(e)), axis=0, keepdims=True)
        hit = erow == idx
        vals.append(m)
        hits.append(hit)
        work = jnp.where(hit, -jnp.inf, work)
    ex = [jnp.exp(v - vals[0]) for v in vals]
    den = ex[0] + ex[1] + ex[2] + ex[3]
    gate_ref[...] = jnp.concatenate([x / den for x in ex], axis=0)

    chosen = jnp.zeros((e, tn), F32)
    for hit in hits:
        chosen = chosen + jnp.where(hit, 1.0, 0.0)
    chosen_b = chosen.astype(BF16)
    tri = (lax.broadcasted_iota(I32, (LANE, LANE), 0) <= lax.broadcasted_iota(I32, (LANE, LANE), 1))
    tri = jnp.where(tri, 1.0, 0.0).astype(BF16)
    carry = jnp.zeros((e, 1), F32)
    ranks = []
    for b in range(tn // LANE):
        blk = chosen[:, b * LANE:(b + 1) * LANE]
        inc = _dot(chosen_b[:, b * LANE:(b + 1) * LANE], tri) + carry
        ranks.append(inc - blk)
        carry = inc[:, LANE - 1:LANE]
    rank = jnp.concatenate(ranks, axis=1)
    cnt_col = carry
    cnt_row = _dot_nt(jnp.ones((SUBLANE, tn), BF16), chosen_b)[0:1, :]

    tiles_col = _count_steps(cnt_col, EXP_TM, tn)
    tiles_row = _count_steps(cnt_row, EXP_TM, tn)
    ee_r = lax.broadcasted_iota(I32, (e, e), 0)
    ee_c = lax.broadcasted_iota(I32, (e, e), 1)
    tstart_col = jnp.sum(jnp.where(ee_c < ee_r, tiles_row, 0.0), axis=1, keepdims=True)
    ntiles = jnp.sum(tiles_row, axis=1, keepdims=True)
    rstart_col = tstart_col * float(EXP_TM)
    pos = [jnp.sum(jnp.where(hit, rstart_col + rank, 0.0), axis=0, keepdims=True) for hit in hits]
    pos_ref[...] = jnp.concatenate(pos, axis=0).astype(I32)

    mm = lax.broadcasted_iota(I32, (e, LANE), 1).astype(F32)
    e_col = lax.broadcasted_iota(I32, (e, LANE), 0).astype(F32)
    own = (mm >= tstart_col) & (mm < tstart_col + tiles_col)
    texp = jnp.sum(jnp.where(own, e_col, 0.0), axis=0, keepdims=True)
    rows_here = jnp.minimum(float(EXP_TM), cnt_col - (mm - tstart_col) * float(EXP_TM))
    trows = jnp.sum(jnp.where(own, rows_here, 0.0), axis=0, keepdims=True)
    last_e = jnp.max(jnp.where(tiles_col > 0.0, e_col, 0.0), axis=0, keepdims=True)
    texp = jnp.where(mm[0:1, :] < ntiles, texp, last_e)
    texp_ref[...] = texp.astype(I32)
    trows_ref[...] = trows.astype(I32)
    ntiles_ref[...] = jnp.broadcast_to(ntiles, (1, LANE)).astype(I32)
    nsub_col = _count_steps(cnt_col, EXP_SUB, tn)
    pstart_ref[...] = jnp.broadcast_to(rstart_col + cnt_col, (e, LANE)).astype(I32)
    plen_ref[...] = jnp.broadcast_to(nsub_col * float(EXP_SUB) - cnt_col, (e, LANE)).astype(I32)


def _route(lg_t):
    e, tn = lg_t.shape
    i32 = lambda shape: jax.ShapeDtypeStruct(shape, I32)
    return pl.pallas_call(
        _route_kernel,
        out_shape=[i32((TOP_K, tn)), jax.ShapeDtypeStruct((TOP_K, tn), F32),
                   i32((1, LANE)), i32((1, LANE)), i32((1, LANE)), i32((e, LANE)), i32((e, LANE))],
        compiler_params=pltpu.CompilerParams(vmem_limit_bytes=VMEM_LIMIT),
        name="route",
    )(lg_t)


def _max_tiles(n_tok):
    return (n_tok * TOP_K) // EXP_TM + N_EXPERTS


def _dispatch_kernel(pos_ref, pstart_ref, plen_ref, h2_ref, zero_ref, xs_ref, sem):
    i = pl.program_id(0)

    def row_copy(src, s, d):
        return pltpu.make_async_copy(src.at[pl.ds(s, 1)], xs_ref.at[pl.ds(d, 1)], sem)

    def issue(t, c):
        for k in range(TOP_K):
            row_copy(h2_ref, t, pos_ref[k, t]).start()
        return c

    def drain(t, c):
        row_copy(h2_ref, 0, 0).wait()
        return c

    lax.fori_loop(0, TOK_BLK, issue, 0)
    lax.fori_loop(0, TOK_BLK * TOP_K, drain, 0)

    @pl.when(i == 0)
    def _():
        def per_expert(ex, c):
            n = plen_ref[ex]
            s = pstart_ref[ex]

            def zissue(r, cc):
                row_copy(zero_ref, 0, s + r).start()
                return cc

            def zdrain(r, cc):
                row_copy(zero_ref, 0, 0).wait()
                return cc

            lax.fori_loop(0, n, zissue, 0)
            lax.fori_loop(0, n, zdrain, 0)
            return c

        lax.fori_loop(0, N_EXPERTS, per_expert, 0)


def _dispatch(pos3, pstart, plen, h2_all):
    n_tok = h2_all.shape[0]
    n_rows = _max_tiles(n_tok) * EXP_TM
    smem = pl.BlockSpec(memory_space=pltpu.SMEM)
    hbm = pl.BlockSpec(memory_space=pl.ANY)
    return pl.pallas_call(
        _dispatch_kernel,
        grid=(n_tok // TOK_BLK,),
        in_specs=[pl.BlockSpec((None, TOP_K, TOK_BLK), lambda i: (i, 0, 0), memory_space=pltpu.SMEM),
                  smem, smem,
                  pl.BlockSpec((TOK_BLK, D_MODEL), lambda i: (i, 0)),
                  pl.BlockSpec((SUBLANE, D_MODEL), lambda i: (0, 0))],
        out_specs=hbm,
        out_shape=jax.ShapeDtypeStruct((n_rows, D_MODEL), F32),
        scratch_shapes=[pltpu.SemaphoreType.DMA(())],
        compiler_params=_cparams(("arbitrary",)),
        name="dispatch",
    )(pos3, pstart, plen, h2_all, jnp.zeros((SUBLANE, D_MODEL), F32))


def _expert_kernel(texp_ref, trows_ref, nt_ref, x_ref, wg_ref, wl_ref, bg_ref, bl_ref, wd_ref, bd_ref,
                   o_ref, act_scr, wg_scr, wl_scr, wd_scr):
    m = pl.program_id(0)
    s = pl.program_id(1)
    valid = m < nt_ref[0]
    rows = trows_ref[m]
    nbig = lax.shift_right_logical(rows, int(math.log2(EXP_BIG)))
    big_rows = nbig * EXP_BIG
    nsmall = lax.shift_right_logical(rows - big_rows + (EXP_SUB - 1), int(math.log2(EXP_SUB)))
    nsub_done = nbig * (EXP_BIG // EXP_SUB) + nsmall

    def over_rows(step):
        def big(r, c):
            step(pl.multiple_of(r * EXP_BIG, EXP_BIG), EXP_BIG)
            return c

        def small(r, c):
            step(pl.multiple_of(big_rows + r * EXP_SUB, EXP_SUB), EXP_SUB)
            return c

        lax.fori_loop(0, nbig, big, 0)
        lax.fori_loop(0, nsmall, small, 0)

    @pl.when(valid & (s < EXP_NF))
    def _():
        wg_scr[...] = wg_ref[...].astype(BF16)
        wl_scr[...] = wl_ref[...].astype(BF16)

        def step(r0, n):
            xb = x_ref[pl.ds(r0, n), :].astype(BF16)
            hg = _dot(xb, wg_scr[...]) + bg_ref[...]
            hl = _dot(xb, wl_scr[...]) + bl_ref[...]
            x_glu = jnp.minimum(hg, SWIGLU_LIMIT)
            x_lin = jnp.clip(hl, -SWIGLU_LIMIT, SWIGLU_LIMIT)
            act = x_glu * _sigmoid(SWIGLU_ALPHA * x_glu) * (x_lin + 1.0)
            act_scr[s, pl.ds(r0, n), :] = act.astype(BF16)

        over_rows(step)

    @pl.when(valid & (s >= EXP_NF))
    def _():
        wd_scr[...] = wd_ref[...].astype(BF16)

        def step(r0, n):
            acc = jnp.broadcast_to(bd_ref[...], (n, EXP_TD))
            for f in range(EXP_NF):
                acc = acc + _dot(act_scr[f, pl.ds(r0, n), :], wd_scr[f * EXP_TF:(f + 1) * EXP_TF, :])
            o_ref[pl.ds(r0, n), :] = acc

        def zero(r, c):
            r0 = pl.multiple_of(r * EXP_SUB, EXP_SUB)
            o_ref[pl.ds(r0, EXP_SUB), :] = jnp.zeros((EXP_SUB, EXP_TD), F32)
            return c

        over_rows(step)
        lax.fori_loop(nsub_done, EXP_TM // EXP_SUB, zero, 0)


def _experts(texp, trows, ntiles, xs, w_gate_up, b_gate_up, w_down, b_down):
    n_tiles = xs.shape[0] // EXP_TM
    nsteps = EXP_NF + EXP_ND

    def tile(m, nt):
        return jnp.minimum(m, nt[0] - 1)

    def fa(m, s, nt):
        return jnp.where(m < nt[0], jnp.minimum(s, EXP_NF - 1), EXP_NF - 1)

    def fb(m, s, nt):
        return jnp.where(m < nt[0], jnp.maximum(s - EXP_NF, 0), EXP_ND - 1)

    grid_spec = pltpu.PrefetchScalarGridSpec(
        num_scalar_prefetch=3,
        grid=(n_tiles, nsteps),
        in_specs=[
            pl.BlockSpec((EXP_TM, D_MODEL), lambda m, s, te, tr, nt: (tile(m, nt), 0)),
            pl.BlockSpec((None, D_MODEL, EXP_TF), lambda m, s, te, tr, nt: (te[m], 0, fa(m, s, nt))),
            pl.BlockSpec((None, D_MODEL, EXP_TF), lambda m, s, te, tr, nt: (te[m], 0, EXP_NF + fa(m, s, nt))),
            pl.BlockSpec((None, 1, EXP_TF), lambda m, s, te, tr, nt: (te[m], 0, fa(m, s, nt))),
            pl.BlockSpec((None, 1, EXP_TF), lambda m, s, te, tr, nt: (te[m], 0, EXP_NF + fa(m, s, nt))),
            pl.BlockSpec((None, D_FF, EXP_TD), lambda m, s, te, tr, nt: (te[m], 0, fb(m, s, nt))),
            pl.BlockSpec((None, 1, EXP_TD), lambda m, s, te, tr, nt: (te[m], 0, fb(m, s, nt))),
        ],
        out_specs=pl.BlockSpec((EXP_TM, EXP_TD), lambda m, s, te, tr, nt: (tile(m, nt), fb(m, s, nt))),
        scratch_shapes=[pltpu.VMEM((EXP_NF, EXP_TM, EXP_TF), BF16),
                        pltpu.VMEM((D_MODEL, EXP_TF), BF16), pltpu.VMEM((D_MODEL, EXP_TF), BF16),
                        pltpu.VMEM((D_FF, EXP_TD), BF16)],
    )
    return pl.pallas_call(
        _expert_kernel,
        grid_spec=grid_spec,
        out_shape=jax.ShapeDtypeStruct((xs.shape[0], D_MODEL), F32),
        compiler_params=_cparams(("arbitrary", "arbitrary")),
        name="experts",
    )(texp, trows, ntiles, xs, w_gate_up, w_gate_up,
      b_gate_up.reshape(N_EXPERTS, 1, 2 * D_FF), b_gate_up.reshape(N_EXPERTS, 1, 2 * D_FF),
      w_down, b_down.reshape(N_EXPERTS, 1, D_MODEL))


def _combine_kernel(pos_ref, gate_ref, ys_ref, x1_ref, g2_ref, gpf_ref, o_ref, buf, sem):
    def row_copy(s, k, t):
        return pltpu.make_async_copy(ys_ref.at[pl.ds(s, 1)], buf.at[k, pl.ds(t, 1)], sem)

    def issue(t, c):
        for k in range(TOP_K):
            row_copy(pos_ref[k, t], k, t).start()
        return c

    def drain(t, c):
        row_copy(0, 0, 0).wait()
        return c

    lax.fori_loop(0, TOK_BLK, issue, 0)
    lax.fori_loop(0, TOK_BLK * TOP_K, drain, 0)
    g = gate_ref[...]
    f = g[:, 0:1] * buf[0]
    for k in range(1, TOP_K):
        f = f + g[:, k:k + 1] * buf[k]
    o_ref[...] = x1_ref[...] + g2_ref[...] * _rms(f, gpf_ref[...])


def _combine(pos3, gates_t, ys, x1, mod, per_row, rows_per_seq, g_post_ffn, blk0):
    rows = x1.shape[0]
    return pl.pallas_call(
        _combine_kernel,
        grid=(rows // TOK_BLK,),
        in_specs=[pl.BlockSpec((None, TOP_K, TOK_BLK), lambda i: (blk0 + i, 0, 0), memory_space=pltpu.SMEM),
                  pl.BlockSpec((TOK_BLK, TOP_K), lambda i: (blk0 + i, 0)),
                  pl.BlockSpec(memory_space=pl.ANY),
                  pl.BlockSpec((TOK_BLK, D_MODEL), lambda i: (i, 0)),
                  _mod_spec(per_row, TOK_BLK, rows_per_seq, 5, 1),
                  pl.BlockSpec((1, D_MODEL), lambda i: (0, 0))],
        out_specs=pl.BlockSpec((TOK_BLK, D_MODEL), lambda i: (i, 0)),
        out_shape=jax.ShapeDtypeStruct((rows, D_MODEL), F32),
        scratch_shapes=[pltpu.VMEM((TOP_K, TOK_BLK, D_MODEL), F32), pltpu.SemaphoreType.DMA(())],
        compiler_params=_cparams(("arbitrary",)),
        name="combine",
    )(pos3, gates_t, ys, x1, mod, g_post_ffn.reshape(1, D_MODEL))


def _moe(fr, p, batch, seq, nseq):
    l = 0
    h2_all = jnp.concatenate([fr['p']['h2'], fr['s']['h2']], axis=0)
    lg_all = jnp.concatenate([fr['p']['logits_t'], fr['s']['logits_t']], axis=1)
    n_tok = h2_all.shape[0]
    pos, gates, texp, trows, ntiles, pstart, plen = _route(lg_all)
    pos3 = pos.reshape(TOP_K, n_tok // TOK_BLK, TOK_BLK).transpose(1, 0, 2)
    xs = _dispatch(pos3, pstart[:, 0], plen[:, 0], h2_all)
    ys = _experts(texp[0], trows[0], ntiles[0, :1], xs, p['w_gate_up'][l], p['b_gate_up'][l],
                  p['w_down'][l], p['b_down'][l])
    gates_t = gates.T
    y_p = _combine(pos3, gates_t, ys, fr['p']['x1'], fr['mod_p'], False, seq, p['g_post_ffn'][l], 0)
    y_s = _combine(pos3, gates_t, ys, fr['s']['x1'], fr['mod_s'], True, 1, p['g_post_ffn'][l],
                   batch * seq // TOK_BLK)
    return y_p, y_s


def kernel(x_prompt, x_sample, c_prompt, c_sample, cache_win_k, cache_win_v, state_ssm_re, state_ssm_im, w_ada, b_ada, g_pre_mix, g_post_mix, g_pre_ffn, g_post_ffn, w_in, attn_sinks, rel_bias, ssm_a_re, ssm_a_im, ssm_log_dt, ssm_b_re, ssm_b_im, ssm_c_re, ssm_c_im, ssm_d, w_glu, b_glu, w_br_attn, w_br_ssm, w_out, w_router, b_router, w_gate_up, b_gate_up, w_down, b_down):
    p = dict(locals())
    batch, seq, _ = x_prompt.shape
    nseq = x_sample.shape[0]
    fr = _front(p)
    y_p, y_s = _moe(fr, p, batch, seq, nseq)
    fp, fs = fr['p'], fr['s']
    return (y_p.reshape(batch, seq, D_MODEL), y_s.reshape(nseq, 1, D_MODEL),
            fp['new_k'], fp['new_v'], fp['h_re'], fp['h_im'],
            fs['new_k'], fs['new_v'], fs['h_re'], fs['h_im'])
```

```python
import functools
import math

import numpy as np
import jax
import jax.numpy as jnp
from jax import lax
from jax.experimental import pallas as pl
from jax.experimental.pallas import tpu as pltpu

F32 = jnp.float32
BF16 = jnp.bfloat16
I32 = jnp.int32

D_MODEL = 2048
N_HEADS = 16
N_KV_HEADS = 4
HEAD_DIM = 64
Q_GROUP = N_HEADS // N_KV_HEADS
WINDOW = 128
N_BUCKETS = 32
MAX_EXACT = N_BUCKETS // 2
MAX_DISTANCE = 128
D_SSM = 1024
SSM_GROUP = 16
N_SSM_GROUPS = 64
SSM_STATE = 64
N_EXPERTS = 32
TOP_K = 4
D_FF = 2048
SWIGLU_LIMIT = 7.0
SWIGLU_ALPHA = 1.702
NORM_EPS = 1e-6
NEG_INF = -1e30
Q_W = N_HEADS * HEAD_DIM
KV_W = N_KV_HEADS * HEAD_DIM
IN_W = Q_W + 2 * KV_W + D_SSM + 2 * D_MODEL
SSM_W = N_SSM_GROUPS * SSM_STATE

LANE = 128
SUBLANE = 8
VMEM_LIMIT = 56 * 1024 * 1024

PROJ_TM = 1024
PROJ_TN = 512
MERGE_TM = 512
MERGE_TK = 512
SSM_TC = 128
SSM_LB = 512
SSM_TILES = D_SSM // LANE
TOK_BLK = 128
EXP_TM = 1280
EXP_BIG = 512
EXP_SUB = 128
EXP_TF = 512
EXP_TD = 512
EXP_NF = D_FF // EXP_TF
EXP_ND = D_MODEL // EXP_TD


def _cparams(sem):
    return pltpu.CompilerParams(dimension_semantics=sem, vmem_limit_bytes=VMEM_LIMIT)


def _sigmoid(x):
    return 1.0 / (1.0 + jnp.exp(-x))


def _rms(x, g):
    return x * lax.rsqrt(jnp.mean(x * x, axis=-1, keepdims=True) + NORM_EPS) * g


def _dot(a, b):
    return jnp.dot(a, b, preferred_element_type=F32)


def _dot_nt(a, b):
    return lax.dot_general(a, b, (((1,), (1,)), ((), ())), preferred_element_type=F32)


def _ada_kernel(c_ref, w_ref, b_ref, o_ref):
    c = c_ref[...]
    s = (c * _sigmoid(c)).astype(BF16)
    o_ref[...] = _dot(s, w_ref[...].astype(BF16)) + b_ref[...]


def _ada(c_all, w_ada, b_ada):
    rows = c_all.shape[0]
    tn = 1024
    n = w_ada.shape[1]
    return pl.pallas_call(
        _ada_kernel,
        grid=(n // tn,),
        in_specs=[pl.BlockSpec((rows, D_MODEL), lambda j: (0, 0)),
                  pl.BlockSpec((D_MODEL, tn), lambda j: (0, j)),
                  pl.BlockSpec((1, tn), lambda j: (0, j))],
        out_specs=pl.BlockSpec((rows, tn), lambda j: (0, j)),
        out_shape=jax.ShapeDtypeStruct((rows, n), F32),
        compiler_params=_cparams(("arbitrary",)),
        name="ada",
    )(c_all, w_ada, b_ada.reshape(1, n))


def _t5_bucket_np(dist):
    n = np.maximum(dist, 0)
    nf = np.maximum(n, 1).astype(np.float64)
    large = MAX_EXACT + (np.log(nf / MAX_EXACT) / math.log(MAX_DISTANCE / MAX_EXACT)
                         * (N_BUCKETS - MAX_EXACT)).astype(np.int32)
    large = np.minimum(large, N_BUCKETS - 1)
    return np.where(n < MAX_EXACT, n, large).astype(np.int32)


def _bias_kernel(bucket_ref, rb_ref, o_ref):
    h = pl.program_id(0)
    bucket = bucket_ref[...]
    acc = jnp.full(bucket.shape, NEG_INF, F32)
    for b in range(N_BUCKETS):
        acc = jnp.where(bucket == b, rb_ref[b, h], acc)
    o_ref[...] = acc


def _bias_table(bucket_np, rel_bias):
    r, c = bucket_np.shape
    return pl.pallas_call(
        _bias_kernel,
        grid=(N_HEADS,),
        in_specs=[pl.BlockSpec((r, c), lambda h: (0, 0)),
                  pl.BlockSpec(memory_space=pltpu.SMEM)],
        out_specs=pl.BlockSpec((None, r, c), lambda h: (h, 0, 0)),
        out_shape=jax.ShapeDtypeStruct((N_HEADS, r, c), F32),
        compiler_params=_cparams(("arbitrary",)),
        name="bias",
    )(jnp.asarray(bucket_np), rel_bias)


def _proj_kernel(x_ref, sc_ref, sh_ref, g_ref, w_ref, o_ref, h_scr):
    @pl.when(pl.program_id(1) == 0)
    def _():
        h = _rms(x_ref[...], g_ref[...]) * (1.0 + sc_ref[...]) + sh_ref[...]
        h_scr[...] = h.astype(BF16)

    o_ref[...] = _dot(h_scr[...], w_ref[...])


def _mod_spec(per_row, tm, rows_per_seq, col, nargs):
    if per_row:
        if nargs == 1:
            return pl.BlockSpec((tm, D_MODEL), lambda i: (i, col))
        return pl.BlockSpec((tm, D_MODEL), lambda i, j: (i, col))
    tiles_per_seq = rows_per_seq // tm
    if nargs == 1:
        return pl.BlockSpec((None, 1, D_MODEL), lambda i: (i // tiles_per_seq, 0, col))
    return pl.BlockSpec((None, 1, D_MODEL), lambda i, j: (i // tiles_per_seq, 0, col))


def _proj(x, mod, per_row, rows_per_seq, g_pre, w_in):
    rows = x.shape[0]
    tm = min(PROJ_TM, rows)
    return pl.pallas_call(
        _proj_kernel,
        grid=(rows // tm, IN_W // PROJ_TN),
        in_specs=[pl.BlockSpec((tm, D_MODEL), lambda i, j: (i, 0)),
                  _mod_spec(per_row, tm, rows_per_seq, 1, 2),
                  _mod_spec(per_row, tm, rows_per_seq, 0, 2),
                  pl.BlockSpec((1, D_MODEL), lambda i, j: (0, 0)),
                  pl.BlockSpec((None, D_MODEL, PROJ_TN), lambda i, j: (j, 0, 0))],
        out_specs=pl.BlockSpec((tm, PROJ_TN), lambda i, j: (i, j)),
        out_shape=jax.ShapeDtypeStruct((rows, IN_W), F32),
        scratch_shapes=[pltpu.VMEM((tm, D_MODEL), BF16)],
        compiler_params=_cparams(("arbitrary", "arbitrary")),
        name="proj",
    )(x, mod, mod, g_pre.reshape(1, D_MODEL), w_in)


def _attn_prompt_kernel(q_ref, kc_ref, kp_ref, vc_ref, vp_ref, bias_ref, sink_ref, o_ref):
    has_prev = pl.program_id(1) > 0
    q = q_ref[...]
    k = jnp.concatenate([kp_ref[...], kc_ref[...]], axis=0)
    v = jnp.concatenate([vp_ref[...], vc_ref[...]], axis=0)
    col = lax.broadcasted_iota(I32, (WINDOW, 2 * WINDOW), 1)
    key_ok = (col >= WINDOW) | has_prev
    outs = []
    for g in range(N_KV_HEADS):
        kg = k[:, g * HEAD_DIM:(g + 1) * HEAD_DIM].astype(BF16)
        vg = v[:, g * HEAD_DIM:(g + 1) * HEAD_DIM].astype(BF16)
        for hh in range(Q_GROUP):
            h = g * Q_GROUP + hh
            qh = q[:, h * HEAD_DIM:(h + 1) * HEAD_DIM].astype(BF16)
            s = _dot_nt(qh, kg) * (HEAD_DIM ** -0.5) + bias_ref[h]
            s = jnp.where(key_ok, s, NEG_INF)
            sink = sink_ref[h]
            m = jnp.maximum(jnp.max(s, axis=-1, keepdims=True), sink)
            p = jnp.exp(s - m)
            den = jnp.sum(p, axis=-1, keepdims=True) + jnp.exp(sink - m)
            outs.append(_dot(p.astype(BF16), vg) / den)
    o_ref[...] = jnp.concatenate(outs, axis=-1)


def _attn_prompt(proj, batch, seq, bias, sinks):
    nb = seq // WINDOW
    kcol = Q_W // KV_W
    vcol = kcol + 1
    cur = lambda c: (lambda b, n: (b * nb + n, c))
    prev = lambda c: (lambda b, n: (b * nb + jnp.maximum(n - 1, 0), c))
    return pl.pallas_call(
        _attn_prompt_kernel,
        grid=(batch, nb),
        in_specs=[pl.BlockSpec((WINDOW, Q_W), cur(0)),
                  pl.BlockSpec((WINDOW, KV_W), cur(kcol)),
                  pl.BlockSpec((WINDOW, KV_W), prev(kcol)),
                  pl.BlockSpec((WINDOW, KV_W), cur(vcol)),
                  pl.BlockSpec((WINDOW, KV_W), prev(vcol)),
                  pl.BlockSpec((N_HEADS, WINDOW, 2 * WINDOW), lambda b, n: (0, 0, 0)),
                  pl.BlockSpec(memory_space=pltpu.SMEM)],
        out_specs=pl.BlockSpec((WINDOW, Q_W), lambda b, n: (b * nb + n, 0)),
        out_shape=jax.ShapeDtypeStruct((batch * seq, Q_W), F32),
        compiler_params=_cparams(("arbitrary", "arbitrary")),
        name="attn_prompt",
    )(proj, proj, proj, proj, proj, bias, sinks)


def _attn_sample_kernel(q_ref, kn_ref, vn_ref, ck_ref, cv_ref, bias_ref, sink_ref,
                        o_ref, nk_ref, nv_ref):
    tb = q_ref.shape[0]
    row = lax.broadcasted_iota(I32, (tb, WINDOW, KV_W), 1)
    last = row == WINDOW - 1
    nk = jnp.where(last, kn_ref[...], pltpu.roll(ck_ref[...], WINDOW - 1, 1))
    nv = jnp.where(last, vn_ref[...], pltpu.roll(cv_ref[...], WINDOW - 1, 1))
    nk_ref[...] = nk
    nv_ref[...] = nv
    lane_grp = lax.broadcasted_iota(I32, (N_HEADS, KV_W), 1) // HEAD_DIM
    head_grp = lax.broadcasted_iota(I32, (N_HEADS, KV_W), 0) // Q_GROUP
    gmask = (lane_grp == head_grp).astype(F32)
    q = q_ref[...]
    qrow = jnp.concatenate([q] * N_KV_HEADS, axis=-1) * gmask
    s = jnp.einsum('bhc,brc->bhr', qrow.astype(BF16), nk.astype(BF16),
                   preferred_element_type=F32) * (HEAD_DIM ** -0.5)
    s = s + bias_ref[...]
    sink = sink_ref[...]
    m = jnp.maximum(jnp.max(s, axis=-1, keepdims=True), sink)
    p = jnp.exp(s - m)
    den = jnp.sum(p, axis=-1, keepdims=True) + jnp.exp(sink - m)
    o = jnp.einsum('bhr,brc->bhc', p.astype(BF16), nv.astype(BF16),
                   preferred_element_type=F32) * gmask
    o64 = o[..., 0:HEAD_DIM]
    for g in range(1, N_KV_HEADS):
        o64 = o64 + o[..., g * HEAD_DIM:(g + 1) * HEAD_DIM]
    o_ref[...] = o64 / den


def _attn_sample(q3, kn, vn, cache_k, cache_v, bias, sinks):
    nseq = q3.shape[0]
    tb = 16
    seq3 = lambda w: pl.BlockSpec((tb, WINDOW, w), lambda i: (i, 0, 0))
    return pl.pallas_call(
        _attn_sample_kernel,
        grid=(nseq // tb,),
        in_specs=[pl.BlockSpec((tb, N_HEADS, HEAD_DIM), lambda i: (i, 0, 0)),
                  pl.BlockSpec((tb, 1, KV_W), lambda i: (i, 0, 0)),
                  pl.BlockSpec((tb, 1, KV_W), lambda i: (i, 0, 0)),
                  seq3(KV_W), seq3(KV_W),
                  pl.BlockSpec((N_HEADS, WINDOW), lambda i: (0, 0)),
                  pl.BlockSpec((N_HEADS, 1), lambda i: (0, 0))],
        out_specs=[pl.BlockSpec((tb, N_HEADS, HEAD_DIM), lambda i: (i, 0, 0)),
                   seq3(KV_W), seq3(KV_W)],
        out_shape=[jax.ShapeDtypeStruct((nseq, N_HEADS, HEAD_DIM), F32),
                   jax.ShapeDtypeStruct((nseq, WINDOW, KV_W), F32),
                   jax.ShapeDtypeStruct((nseq, WINDOW, KV_W), F32)],
        compiler_params=_cparams(("arbitrary",)),
        name="attn_sample",
    )(q3, kn, vn, cache_k, cache_v, bias, sinks.reshape(N_HEADS, 1))


def _ssm_disc_kernel(are_ref, aim_ref, ldt_ref, bre_ref, bim_ref,
                     lbr_ref, lbi_ref, bbr_ref, bbi_ref):
    a_re = are_ref[...]
    a_im = aim_ref[...]
    dt = jnp.exp(ldt_ref[...])
    lam_re = a_re * dt
    lam_im = a_im * dt
    mag = jnp.exp(lam_re)
    lb_re = mag * jnp.cos(lam_im)
    lb_im = mag * jnp.sin(lam_im)
    den = a_re * a_re + a_im * a_im
    nr = lb_re - 1.0
    ni = lb_im
    coef_re = (nr * a_re + ni * a_im) / den
    coef_im = (ni * a_re - nr * a_im) / den
    b_re = bre_ref[...]
    b_im = bim_ref[...]
    lbr_ref[...] = lb_re
    lbi_ref[...] = lb_im
    bbr_ref[...] = coef_re * b_re - coef_im * b_im
    bbi_ref[...] = coef_re * b_im + coef_im * b_re


def _ssm_disc(a_re, a_im, log_dt, b_re, b_im):
    g, p, j = N_SSM_GROUPS, SSM_STATE, SSM_GROUP
    vec = jax.ShapeDtypeStruct((g, 1, p), F32)
    mat = jax.ShapeDtypeStruct((g, j, p), F32)
    return pl.pallas_call(
        _ssm_disc_kernel,
        out_shape=[vec, vec, mat, mat],
        name="ssm_disc",
    )(a_re.reshape(g, 1, p), a_im.reshape(g, 1, p), log_dt.reshape(g, 1, 1),
      b_re.transpose(0, 2, 1), b_im.transpose(0, 2, 1))


def _block_diag_tiles(x):
    a, b = x.shape[1], x.shape[2]
    eye = jnp.eye(SUBLANE, dtype=x.dtype)
    y = jnp.einsum('kgab,gh->kgahb', x.reshape(SSM_TILES, SUBLANE, a, b), eye)
    return y.reshape(SSM_TILES, SUBLANE * a, SUBLANE * b)


def _gelu_tanh(x):
    return 0.5 * x * (1.0 + jnp.tanh(math.sqrt(2.0 / math.pi) * (x + 0.044715 * (x * x * x))))


def _ssm_kernel(u_ref, h0r_ref, h0i_ref, lbr_ref, lbi_ref, bb_ref, cc_ref, d_ref, wglu_ref, bglu_ref,
                y_ref, hTr_ref, hTi_ref, hre, him, st_r, st_i, *, nseq, tc):
    paired = nseq == 4

    @pl.when(pl.program_id(0) == 0)
    def _():
        if paired:
            st_r[...] = jnp.concatenate([h0r_ref[...], h0r_ref[...]], axis=0)
            st_i[...] = jnp.concatenate([h0i_ref[...], h0i_ref[...]], axis=0)
        else:
            st_r[...] = h0r_ref[...]
            st_i[...] = h0i_ref[...]

    u = u_ref[...]
    ub = u.astype(BF16)
    half = SSM_W // SSM_TILES
    for k in range(SSM_TILES):
        bu = _dot(ub[:, k * LANE:(k + 1) * LANE], bb_ref[k])
        hre[:, k * half:(k + 1) * half] = bu[:, :half]
        him[:, k * half:(k + 1) * half] = bu[:, half:]

    for blk in range(SSM_W // SSM_LB):
        sl = slice(blk * SSM_LB, (blk + 1) * SSM_LB)
        ar = lbr_ref[:, sl]
        ai = lbi_ref[:, sl]
        if paired:
            lower = lax.broadcasted_iota(I32, (SUBLANE, SSM_LB), 0) < nseq

            def body(m, carry):
                sr, si = carry
                r0 = pl.multiple_of(m * SUBLANE, SUBLANE)
                br = hre[pl.ds(r0, SUBLANE), sl]
                bi = him[pl.ds(r0, SUBLANE), sl]
                xr = pltpu.roll(sr, nseq, 0)
                xi = pltpu.roll(si, nseq, 0)
                h1r = ar * xr - ai * xi + br
                h1i = ar * xi + ai * xr + bi
                yr = pltpu.roll(h1r, nseq, 0)
                yi = pltpu.roll(h1i, nseq, 0)
                h2r = ar * yr - ai * yi + br
                h2i = ar * yi + ai * yr + bi
                hre[pl.ds(r0, SUBLANE), sl] = jnp.where(lower, h1r, h2r)
                him[pl.ds(r0, SUBLANE), sl] = jnp.where(lower, h1i, h2i)
                return h2r, h2i

            sr, si = lax.fori_loop(0, tc * nseq // SUBLANE, body, (st_r[:, sl], st_i[:, sl]))
        else:
            def body(t, carry):
                sr, si = carry
                r0 = pl.multiple_of(t * nseq, SUBLANE)
                br = hre[pl.ds(r0, nseq), sl]
                bi = him[pl.ds(r0, nseq), sl]
                nr = ar * sr - ai * si + br
                ni = ar * si + ai * sr + bi
                hre[pl.ds(r0, nseq), sl] = nr
                him[pl.ds(r0, nseq), sl] = ni
                return nr, ni

            sr, si = lax.fori_loop(0, tc, body, (st_r[:, sl], st_i[:, sl]))
        st_r[:, sl] = sr
        st_i[:, sl] = si

    ys = []
    for k in range(SSM_TILES):
        hr = hre[:, k * half:(k + 1) * half].astype(BF16)
        hi = him[:, k * half:(k + 1) * half].astype(BF16)
        yk = _dot(hr, cc_ref[k, :half, :]) + _dot(hi, cc_ref[k, half:, :])
        ys.append(yk + d_ref[:, k * LANE:(k + 1) * LANE] * u[:, k * LANE:(k + 1) * LANE])
    y = _gelu_tanh(jnp.concatenate(ys, axis=-1))
    z = _dot(y.astype(BF16), wglu_ref[...]) + bglu_ref[...]
    y_ref[...] = y * _sigmoid(z)

    if paired:
        hTr_ref[...] = st_r[nseq:, :]
        hTi_ref[...] = st_i[nseq:, :]
    else:
        hTr_ref[...] = st_r[...]
        hTi_ref[...] = st_i[...]


def _ssm(u_rows, h0_re, h0_im, nseq, tc, lbr, lbi, bb, cc, d, w_glu, b_glu):
    rows = u_rows.shape[0]
    r = nseq * tc
    st_rows = max(nseq, SUBLANE)
    const2 = lambda shape: pl.BlockSpec(shape, lambda c: (0, 0))
    const3 = lambda shape: pl.BlockSpec(shape, lambda c: (0, 0, 0))
    return pl.pallas_call(
        functools.partial(_ssm_kernel, nseq=nseq, tc=tc),
        grid=(rows // r,),
        in_specs=[pl.BlockSpec((r, D_SSM), lambda c: (c, 0)),
                  const2((nseq, SSM_W)), const2((nseq, SSM_W)),
                  const2((1, SSM_W)), const2((1, SSM_W)),
                  const3((SSM_TILES, LANE, 2 * SSM_W // SSM_TILES)),
                  const3((SSM_TILES, 2 * SSM_W // SSM_TILES, LANE)),
                  const2((1, D_SSM)), const2((D_SSM, D_SSM)), const2((1, D_SSM))],
        out_specs=[pl.BlockSpec((r, D_SSM), lambda c: (c, 0)),
                   const2((nseq, SSM_W)), const2((nseq, SSM_W))],
        out_shape=[jax.ShapeDtypeStruct((rows, D_SSM), F32),
                   jax.ShapeDtypeStruct((nseq, SSM_W), F32),
                   jax.ShapeDtypeStruct((nseq, SSM_W), F32)],
        scratch_shapes=[pltpu.VMEM((r, SSM_W), F32), pltpu.VMEM((r, SSM_W), F32),
                        pltpu.VMEM((st_rows, SSM_W), F32), pltpu.VMEM((st_rows, SSM_W), F32)],
        compiler_params=_cparams(("arbitrary",)),
        name="ssm",
    )(u_rows, h0_re, h0_im, lbr, lbi, bb, cc, d.reshape(1, D_SSM), w_glu, b_glu.reshape(1, D_SSM))


def _split_bf16(x):
    hi = x.astype(BF16)
    lo = (x - hi.astype(F32)).astype(BF16)
    return hi, lo


def _merge_kernel(o_ref, y_ref, ga_ref, gs_ref, wa_ref, ws_ref, wo_ref, mix_ref):
    j = pl.program_id(1)
    a = _dot(o_ref[...].astype(BF16), wa_ref[...])
    s = _dot(y_ref[...].astype(BF16), ws_ref[...])
    merged = _sigmoid(ga_ref[...]) * a + _sigmoid(gs_ref[...]) * s
    contrib = _dot(merged.astype(BF16), wo_ref[...])

    @pl.when(j == 0)
    def _():
        mix_ref[...] = contrib

    @pl.when(j > 0)
    def _():
        mix_ref[...] += contrib


def _merge(o_attn, y_ssm, proj, w_br_attn, w_br_ssm, w_out):
    rows = o_attn.shape[0]
    tm = min(MERGE_TM, rows)
    nk = D_MODEL // MERGE_TK
    ga0 = (Q_W + 2 * KV_W + D_SSM) // MERGE_TK
    gs0 = ga0 + nk
    row2 = lambda w: pl.BlockSpec((tm, w), lambda i, j: (i, 0))
    return pl.pallas_call(
        _merge_kernel,
        grid=(rows // tm, nk),
        in_specs=[row2(Q_W), row2(D_SSM),
                  pl.BlockSpec((tm, MERGE_TK), lambda i, j: (i, ga0 + j)),
                  pl.BlockSpec((tm, MERGE_TK), lambda i, j: (i, gs0 + j)),
                  pl.BlockSpec((None, Q_W, MERGE_TK), lambda i, j: (j, 0, 0)),
                  pl.BlockSpec((None, D_SSM, MERGE_TK), lambda i, j: (j, 0, 0)),
                  pl.BlockSpec((MERGE_TK, D_MODEL), lambda i, j: (j, 0))],
        out_specs=row2(D_MODEL),
        out_shape=jax.ShapeDtypeStruct((rows, D_MODEL), F32),
        compiler_params=_cparams(("arbitrary", "arbitrary")),
        name="merge",
    )(o_attn, y_ssm, proj, proj, w_br_attn, w_br_ssm, w_out)


def _post_kernel(x_ref, mix_ref, gpm_ref, g1_ref, sc2_ref, sh2_ref, gpf_ref, wrt_ref, br_ref, *refs):
    x1_ref, h2_ref, lg_ref = refs[-3:]
    x1 = x_ref[...] + g1_ref[...] * _rms(mix_ref[...], gpm_ref[...])
    x1_ref[...] = x1
    h2 = _rms(x1, gpf_ref[...]) * (1.0 + sc2_ref[...]) + sh2_ref[...]
    h2_ref[...] = h2
    h_hi, h_lo = _split_bf16(h2)
    w_hi, w_lo = _split_bf16(wrt_ref[...])
    lg_ref[...] = (_dot_nt(w_hi, h_hi) + _dot_nt(w_hi, h_lo) + _dot_nt(w_lo, h_hi)) + br_ref[...]


def _post(x, mix, mod, per_row, rows_per_seq, n_total, row0, shared, g_post_mix, g_pre_ffn, w_router_t,
          b_router):
    rows = x.shape[0]
    tm = min(MERGE_TM, rows)
    blk0 = row0 // tm
    row = pl.BlockSpec((tm, D_MODEL), lambda i: (i, 0))
    vec = pl.BlockSpec((1, D_MODEL), lambda i: (0, 0))
    in_specs = [row, row, vec,
                _mod_spec(per_row, tm, rows_per_seq, 2, 1),
                _mod_spec(per_row, tm, rows_per_seq, 4, 1),
                _mod_spec(per_row, tm, rows_per_seq, 3, 1),
                vec,
                pl.BlockSpec((N_EXPERTS, D_MODEL), lambda i: (0, 0)),
                pl.BlockSpec((N_EXPERTS, 1), lambda i: (0, 0))]
    args = [x, mix, g_post_mix.reshape(1, D_MODEL), mod, mod, mod, g_pre_ffn.reshape(1, D_MODEL),
            w_router_t, b_router.reshape(N_EXPERTS, 1)]
    aliases = {}
    if shared is not None:
        aliases = {len(args): 1, len(args) + 1: 2}
        in_specs += [pl.BlockSpec(memory_space=pl.ANY), pl.BlockSpec(memory_space=pl.ANY)]
        args += list(shared)
    return pl.pallas_call(
        _post_kernel,
        grid=(rows // tm,),
        in_specs=in_specs,
        out_specs=[row,
                   pl.BlockSpec((tm, D_MODEL), lambda i: (blk0 + i, 0)),
                   pl.BlockSpec((N_EXPERTS, tm), lambda i: (0, blk0 + i))],
        out_shape=[jax.ShapeDtypeStruct((rows, D_MODEL), F32),
                   jax.ShapeDtypeStruct((n_total, D_MODEL), F32),
                   jax.ShapeDtypeStruct((N_EXPERTS, n_total), F32)],
        input_output_aliases=aliases,
        compiler_params=_cparams(("arbitrary",)),
        name="post",
    )(*args)


def _bucket_tables():
    ql = np.arange(WINDOW)[:, None]
    kl = np.arange(2 * WINDOW)[None, :]
    dist = ql + WINDOW - kl
    prompt = np.where((dist >= 0) & (dist < WINDOW), _t5_bucket_np(dist), -1).astype(np.int32)
    d_s = (WINDOW - 1 - np.arange(WINDOW))[None, :]
    sample = np.broadcast_to(_t5_bucket_np(d_s), (SUBLANE, WINDOW)).astype(np.int32)
    return prompt, sample


def _front(p):
    l = 0
    batch, seq, _ = p['x_prompt'].shape
    nseq = p['x_sample'].shape[0]
    xp = p['x_prompt'].reshape(batch * seq, D_MODEL)
    xs = p['x_sample'].reshape(nseq, D_MODEL)

    c_all = jnp.concatenate([p['c_prompt'], p['c_sample'],
                             jnp.zeros((SUBLANE - (batch + nseq) % SUBLANE, D_MODEL), F32)], axis=0)
    mod = _ada(c_all, p['w_ada'][l], p['b_ada'][l])
    mod_p = mod[:batch].reshape(batch, 1, 6 * D_MODEL)
    mod_s = mod[batch:batch + nseq]

    bucket_p, bucket_s = _bucket_tables()
    bias_p = _bias_table(bucket_p, p['rel_bias'])
    bias_s = _bias_table(bucket_s, p['rel_bias'])[:, 0, :]
    sinks = p['attn_sinks'][l]

    def col_tiles(w, tn):
        k, n = w.shape
        return w.astype(BF16).reshape(k, n // tn, tn).transpose(1, 0, 2)

    w_in_t = col_tiles(p['w_in'][l], PROJ_TN)
    proj_p = _proj(xp, mod_p, False, seq, p['g_pre_mix'][l], w_in_t)
    proj_s = _proj(xs, mod_s, True, 1, p['g_pre_mix'][l], w_in_t)

    o_p = _attn_prompt(proj_p, batch, seq, bias_p, sinks)
    kv_p = proj_p.reshape(batch, seq, IN_W)[:, seq - WINDOW:, Q_W:Q_W + 2 * KV_W]
    new_k_p = kv_p[..., :KV_W].reshape(1, batch, WINDOW, N_KV_HEADS, HEAD_DIM)
    new_v_p = kv_p[..., KV_W:].reshape(1, batch, WINDOW, N_KV_HEADS, HEAD_DIM)
    o_s3, new_k_s, new_v_s = _attn_sample(
        proj_s[:, :Q_W].reshape(nseq, N_HEADS, HEAD_DIM),
        proj_s[:, Q_W:Q_W + KV_W].reshape(nseq, 1, KV_W),
        proj_s[:, Q_W + KV_W:Q_W + 2 * KV_W].reshape(nseq, 1, KV_W),
        p['cache_win_k'][l].reshape(nseq, WINDOW, KV_W),
        p['cache_win_v'][l].reshape(nseq, WINDOW, KV_W), bias_s, sinks)
    o_s = o_s3.reshape(nseq, Q_W)

    lbr, lbi, bbr, bbi = _ssm_disc(p['ssm_a_re'][l], p['ssm_a_im'][l], p['ssm_log_dt'][l],
                                   p['ssm_b_re'][l], p['ssm_b_im'][l])
    lbr = lbr.reshape(1, SSM_W)
    lbi = lbi.reshape(1, SSM_W)
    bb = jnp.concatenate([_block_diag_tiles(bbr), _block_diag_tiles(bbi)], axis=-1).astype(BF16)
    c_re_t = p['ssm_c_re'][l].transpose(0, 2, 1)
    c_im_t = p['ssm_c_im'][l].transpose(0, 2, 1)
    cc = jnp.concatenate([_block_diag_tiles(c_re_t), -_block_diag_tiles(c_im_t)], axis=1).astype(BF16)
    u0 = Q_W + 2 * KV_W
    u_tb = proj_p[:, u0:u0 + D_SSM].reshape(batch, seq, D_SSM).transpose(1, 0, 2).reshape(seq * batch, D_SSM)
    zeros = jnp.zeros((batch, SSM_W), F32)
    w_glu = p['w_glu'][l].astype(BF16)
    y_tb, hr_p, hi_p = _ssm(u_tb, zeros, zeros, batch, SSM_TC, lbr, lbi, bb, cc,
                            p['ssm_d'][l], w_glu, p['b_glu'][l])
    y_p = y_tb.reshape(seq, batch, D_SSM).transpose(1, 0, 2).reshape(batch * seq, D_SSM)
    y_s, hr_s, hi_s = _ssm(proj_s[:, u0:u0 + D_SSM], p['state_ssm_re'][l].reshape(nseq, SSM_W),
                           p['state_ssm_im'][l].reshape(nseq, SSM_W), nseq, 1, lbr, lbi, bb, cc,
                           p['ssm_d'][l], w_glu, p['b_glu'][l])

    wa_t = col_tiles(p['w_br_attn'][l], MERGE_TK)
    ws_t = col_tiles(p['w_br_ssm'][l], MERGE_TK)
    wo = p['w_out'][l].astype(BF16)
    mix_p = _merge(o_p, y_p, proj_p, wa_t, ws_t, wo)
    mix_s = _merge(o_s, y_s, proj_s, wa_t, ws_t, wo)
    n_total = batch * seq + nseq
    post = functools.partial(_post, g_post_mix=p['g_post_mix'][l], g_pre_ffn=p['g_pre_ffn'][l],
                             w_router_t=p['w_router'][l].T, b_router=p['b_router'][l])
    x1_p, h2_buf, lg_buf = post(xp, mix_p, mod_p, False, seq, n_total, 0, None)
    x1_s, h2_all, lg_all = post(xs, mix_s, mod_s, True, 1, n_total, batch * seq, (h2_buf, lg_buf))

    st = lambda h, n: h.reshape(1, n, N_SSM_GROUPS, SSM_STATE)
    return dict(
        mod_p=mod_p, mod_s=mod_s, h2_all=h2_all, lg_all=lg_all,
        p=dict(proj=proj_p, o_attn=o_p, new_k=new_k_p, new_v=new_v_p, y_ssm=y_p, h_re=st(hr_p, batch),
               h_im=st(hi_p, batch), x1=x1_p),
        s=dict(proj=proj_s, o_attn=o_s, new_k=new_k_s.reshape(1, nseq, WINDOW, N_KV_HEADS, HEAD_DIM),
               new_v=new_v_s.reshape(1, nseq, WINDOW, N_KV_HEADS, HEAD_DIM), y_ssm=y_s,
               h_re=st(hr_s, nseq), h_im=st(hi_s, nseq), x1=x1_s))


def _count_steps(c, step, n_max):
    out = jnp.zeros_like(c)
    for q in range(-(-n_max // step)):
        out = out + jnp.where(c > float(q * step), 1.0, 0.0)
    return out


def _route_kernel(lg_ref, pos_ref, gate_ref, texp_ref, trows_ref, ntiles_ref, pstart_ref, plen_ref):
    lg = lg_ref[...]
    e, tn = lg.shape
    erow = lax.broadcasted_iota(I32, (e, tn), 0).astype(F32)
    work = lg
    vals, hits = [], []
    for _ in range(TOP_K):
        m = jnp.max(work, axis=0, keepdims=True)
        idx = jnp.min(jnp.where(work == m, erow, float(e)), axis=0, keepdims=True)
        hit = erow == idx
        vals.append(m)
        hits.append(hit)
        work = jnp.where(hit, -jnp.inf, work)
    ex = [jnp.exp(v - vals[0]) for v in vals]
    den = ex[0] + ex[1] + ex[2] + ex[3]
    gate_ref[...] = jnp.concatenate([x / den for x in ex], axis=0)

    chosen = jnp.zeros((e, tn), F32)
    for hit in hits:
        chosen = chosen + jnp.where(hit, 1.0, 0.0)
    chosen_b = chosen.astype(BF16)
    tri = (lax.broadcasted_iota(I32, (LANE, LANE), 0) <= lax.broadcasted_iota(I32, (LANE, LANE), 1))
    tri = jnp.where(tri, 1.0, 0.0).astype(BF16)
    carry = jnp.zeros((e, 1), F32)
    ranks = []
    for b in range(tn // LANE):
        blk = chosen[:, b * LANE:(b + 1) * LANE]
        inc = _dot(chosen_b[:, b * LANE:(b + 1) * LANE], tri) + carry
        ranks.append(inc - blk)
        carry = inc[:, LANE - 1:LANE]
    rank = jnp.concatenate(ranks, axis=1)
    cnt_col = carry
    cnt_row = _dot_nt(jnp.ones((SUBLANE, tn), BF16), chosen_b)[0:1, :]

    tiles_col = _count_steps(cnt_col, EXP_TM, tn)
    tiles_row = _count_steps(cnt_row, EXP_TM, tn)
    ee_r = lax.broadcasted_iota(I32, (e, e), 0)
    ee_c = lax.broadcasted_iota(I32, (e, e), 1)
    tstart_col = jnp.sum(jnp.where(ee_c < ee_r, tiles_row, 0.0), axis=1, keepdims=True)
    ntiles = jnp.sum(tiles_row, axis=1, keepdims=True)
    rstart_col = tstart_col * float(EXP_TM)
    pos = [jnp.sum(jnp.where(hit, rstart_col + rank, 0.0), axis=0, keepdims=True) for hit in hits]
    pos_ref[...] = jnp.concatenate(pos, axis=0).astype(I32)

    mm = lax.broadcasted_iota(I32, (e, LANE), 1).astype(F32)
    e_col = lax.broadcasted_iota(I32, (e, LANE), 0).astype(F32)
    own = (mm >= tstart_col) & (mm < tstart_col + tiles_col)
    texp = jnp.sum(jnp.where(own, e_col, 0.0), axis=0, keepdims=True)
    rows_here = jnp.minimum(float(EXP_TM), cnt_col - (mm - tstart_col) * float(EXP_TM))
    trows = jnp.sum(jnp.where(own, rows_here, 0.0), axis=0, keepdims=True)
    last_e = jnp.max(jnp.where(tiles_col > 0.0, e_col, 0.0), axis=0, keepdims=True)
    texp = jnp.where(mm[0:1, :] < ntiles, texp, last_e)
    texp_ref[...] = texp.astype(I32)
    trows_ref[...] = trows.astype(I32)
    ntiles_ref[...] = jnp.broadcast_to(ntiles, (1, LANE)).astype(I32)
    nsub_col = _count_steps(cnt_col, EXP_SUB, tn)
    pstart_ref[...] = jnp.broadcast_to(rstart_col + cnt_col, (e, LANE)).astype(I32)
    plen_ref[...] = jnp.broadcast_to(nsub_col * float(EXP_SUB) - cnt_col, (e, LANE)).astype(I32)


def _route(lg_t):
    e, tn = lg_t.shape
    i32 = lambda shape: jax.ShapeDtypeStruct(shape, I32)
    return pl.pallas_call(
        _route_kernel,
        out_shape=[i32((TOP_K, tn)), jax.ShapeDtypeStruct((TOP_K, tn), F32),
                   i32((1, LANE)), i32((1, LANE)), i32((1, LANE)), i32((e, LANE)), i32((e, LANE))],
        compiler_params=pltpu.CompilerParams(vmem_limit_bytes=VMEM_LIMIT),
        name="route",
    )(lg_t)


def _max_tiles(n_tok):
    return (n_tok * TOP_K) // EXP_TM + N_EXPERTS


def _dispatch_kernel(pos_ref, pstart_ref, plen_ref, h2_ref, zero_ref, xs_ref, sem):
    i = pl.program_id(0)

    def row_copy(src, s, d):
        return pltpu.make_async_copy(src.at[pl.ds(s, 1)], xs_ref.at[pl.ds(d, 1)], sem)

    def issue(t, c):
        for k in range(TOP_K):
            row_copy(h2_ref, t, pos_ref[k, t]).start()
        return c

    lax.fori_loop(0, TOK_BLK, issue, 0)
    for _ in range(TOP_K):
        pltpu.make_async_copy(h2_ref, xs_ref.at[pl.ds(0, TOK_BLK)], sem).wait()

    @pl.when(i == 0)
    def _():
        def per_expert(ex, c):
            n = plen_ref[ex]
            s = pstart_ref[ex]

            def zissue(r, cc):
                row_copy(zero_ref, 0, s + r).start()
                return cc

            def zdrain(r, cc):
                row_copy(zero_ref, 0, 0).wait()
                return cc

            lax.fori_loop(0, n, zissue, 0)
            lax.fori_loop(0, n, zdrain, 0)
            return c

        lax.fori_loop(0, N_EXPERTS, per_expert, 0)


def _dispatch(pos3, pstart, plen, h2_all):
    n_tok = h2_all.shape[0]
    n_rows = _max_tiles(n_tok) * EXP_TM
    smem = pl.BlockSpec(memory_space=pltpu.SMEM)
    hbm = pl.BlockSpec(memory_space=pl.ANY)
    return pl.pallas_call(
        _dispatch_kernel,
        grid=(n_tok // TOK_BLK,),
        in_specs=[pl.BlockSpec((None, TOP_K, TOK_BLK), lambda i: (i, 0, 0), memory_space=pltpu.SMEM),
                  smem, smem,
                  pl.BlockSpec((TOK_BLK, D_MODEL), lambda i: (i, 0)),
                  pl.BlockSpec((SUBLANE, D_MODEL), lambda i: (0, 0))],
        out_specs=hbm,
        out_shape=jax.ShapeDtypeStruct((n_rows, D_MODEL), F32),
        scratch_shapes=[pltpu.SemaphoreType.DMA(())],
        compiler_params=_cparams(("arbitrary",)),
        name="dispatch",
    )(pos3, pstart, plen, h2_all, jnp.zeros((SUBLANE, D_MODEL), F32))


def _expert_kernel(texp_ref, trows_ref, nt_ref, xs_hbm, wg_ref, wl_ref, bg_ref, bl_ref, wd_ref, bd_ref,
                   o_ref, x_ref, x_sem, act_scr, wg_scr, wl_scr, wd_scr):
    m = pl.program_id(0)
    s = pl.program_id(1)
    n_tiles = nt_ref[0]
    valid = m < n_tiles

    def x_copy(tile):
        return pltpu.make_async_copy(xs_hbm.at[pl.ds(pl.multiple_of(tile * EXP_TM, EXP_TM), EXP_TM)],
                                     x_ref, x_sem)

    @pl.when((m == 0) & (s == 0))
    def _():
        x_copy(0).start()

    @pl.when(valid & (s == 0))
    def _():
        x_copy(m).wait()

    @pl.when(valid & (s == EXP_NF) & (m + 1 < n_tiles))
    def _():
        x_copy(m + 1).start()

    rows = trows_ref[m]
    nbig = lax.shift_right_logical(rows, int(math.log2(EXP_BIG)))
    big_rows = nbig * EXP_BIG
    nsmall = lax.shift_right_logical(rows - big_rows + (EXP_SUB - 1), int(math.log2(EXP_SUB)))
    nsub_done = nbig * (EXP_BIG // EXP_SUB) + nsmall

    def over_rows(step):
        def big(r, c):
            step(pl.multiple_of(r * EXP_BIG, EXP_BIG), EXP_BIG)
            return c

        def small(r, c):
            step(pl.multiple_of(big_rows + r * EXP_SUB, EXP_SUB), EXP_SUB)
            return c

        lax.fori_loop(0, nbig, big, 0)
        lax.fori_loop(0, nsmall, small, 0)

    @pl.when(valid & (s < EXP_NF))
    def _():
        wg_scr[...] = wg_ref[...].astype(BF16)
        wl_scr[...] = wl_ref[...].astype(BF16)

        def step(r0, n):
            xb = x_ref[pl.ds(r0, n), :].astype(BF16)
            hg = _dot(xb, wg_scr[...]) + bg_ref[...]
            hl = _dot(xb, wl_scr[...]) + bl_ref[...]
            x_glu = jnp.minimum(hg, SWIGLU_LIMIT)
            x_lin = jnp.clip(hl, -SWIGLU_LIMIT, SWIGLU_LIMIT)
            act = x_glu * _sigmoid(SWIGLU_ALPHA * x_glu) * (x_lin + 1.0)
            act_scr[s, pl.ds(r0, n), :] = act.astype(BF16)

        over_rows(step)

    @pl.when(valid & (s >= EXP_NF))
    def _():
        wd_scr[...] = wd_ref[...].astype(BF16)

        def step(r0, n):
            acc = jnp.broadcast_to(bd_ref[...], (n, EXP_TD))
            for f in range(EXP_NF):
                acc = acc + _dot(act_scr[f, pl.ds(r0, n), :], wd_scr[f * EXP_TF:(f + 1) * EXP_TF, :])
            o_ref[pl.ds(r0, n), :] = acc

        def zero(r, c):
            r0 = pl.multiple_of(r * EXP_SUB, EXP_SUB)
            o_ref[pl.ds(r0, EXP_SUB), :] = jnp.zeros((EXP_SUB, EXP_TD), F32)
            return c

        over_rows(step)
        lax.fori_loop(nsub_done, EXP_TM // EXP_SUB, zero, 0)


def _experts(texp, trows, ntiles, xs, w_gate_up, b_gate_up, w_down, b_down):
    n_tiles = xs.shape[0] // EXP_TM
    nsteps = EXP_NF + EXP_ND

    def tile(m, nt):
        return jnp.minimum(m, nt[0] - 1)

    def fa(m, s, nt):
        return jnp.where(m < nt[0], jnp.minimum(s, EXP_NF - 1), EXP_NF - 1)

    def fb(m, s, nt):
        return jnp.where(m < nt[0], jnp.maximum(s - EXP_NF, 0), EXP_ND - 1)

    grid_spec = pltpu.PrefetchScalarGridSpec(
        num_scalar_prefetch=3,
        grid=(n_tiles, nsteps),
        in_specs=[
            pl.BlockSpec(memory_space=pl.ANY),
            pl.BlockSpec((None, D_MODEL, EXP_TF), lambda m, s, te, tr, nt: (te[m], 0, fa(m, s, nt))),
            pl.BlockSpec((None, D_MODEL, EXP_TF), lambda m, s, te, tr, nt: (te[m], 0, EXP_NF + fa(m, s, nt))),
            pl.BlockSpec((None, 1, EXP_TF), lambda m, s, te, tr, nt: (te[m], 0, fa(m, s, nt))),
            pl.BlockSpec((None, 1, EXP_TF), lambda m, s, te, tr, nt: (te[m], 0, EXP_NF + fa(m, s, nt))),
            pl.BlockSpec((None, D_FF, EXP_TD), lambda m, s, te, tr, nt: (te[m], 0, fb(m, s, nt))),
            pl.BlockSpec((None, 1, EXP_TD), lambda m, s, te, tr, nt: (te[m], 0, fb(m, s, nt))),
        ],
        out_specs=pl.BlockSpec((EXP_TM, EXP_TD), lambda m, s, te, tr, nt: (tile(m, nt), fb(m, s, nt))),
        scratch_shapes=[pltpu.VMEM((EXP_TM, D_MODEL), F32), pltpu.SemaphoreType.DMA(()),
                        pltpu.VMEM((EXP_NF, EXP_TM, EXP_TF), BF16),
                        pltpu.VMEM((D_MODEL, EXP_TF), BF16), pltpu.VMEM((D_MODEL, EXP_TF), BF16),
                        pltpu.VMEM((D_FF, EXP_TD), BF16)],
    )
    return pl.pallas_call(
        _expert_kernel,
        grid_spec=grid_spec,
        out_shape=jax.ShapeDtypeStruct((xs.shape[0], D_MODEL), F32),
        compiler_params=_cparams(("arbitrary", "arbitrary")),
        name="experts",
    )(texp, trows, ntiles, xs, w_gate_up, w_gate_up,
      b_gate_up.reshape(N_EXPERTS, 1, 2 * D_FF), b_gate_up.reshape(N_EXPERTS, 1, 2 * D_FF),
      w_down, b_down.reshape(N_EXPERTS, 1, D_MODEL))


def _combine_kernel(pos_ref, pos_next_ref, gate_ref, ys_ref, x1_ref, g2_ref, gpf_ref, o_ref, buf, sem):
    i = pl.program_id(0)
    slot = lax.rem(i, 2)

    def gather(p_ref, sl):
        def issue(t, c):
            for k in range(TOP_K):
                pltpu.make_async_copy(ys_ref.at[pl.ds(p_ref[k, t], 1)], buf.at[sl, k, pl.ds(t, 1)],
                                      sem.at[sl]).start()
            return c

        lax.fori_loop(0, TOK_BLK, issue, 0)

    @pl.when(i == 0)
    def _():
        gather(pos_ref, 0)

    @pl.when(i + 1 < pl.num_programs(0))
    def _():
        gather(pos_next_ref, 1 - slot)

    for k in range(TOP_K):
        pltpu.make_async_copy(ys_ref.at[pl.ds(0, TOK_BLK)], buf.at[slot, k], sem.at[slot]).wait()
    g = gate_ref[...]
    f = g[:, 0:1] * buf[slot, 0]
    for k in range(1, TOP_K):
        f = f + g[:, k:k + 1] * buf[slot, k]
    o_ref[...] = x1_ref[...] + g2_ref[...] * _rms(f, gpf_ref[...])


def _combine(pos3, gates_t, ys, x1, mod, per_row, rows_per_seq, g_post_ffn, blk0):
    rows = x1.shape[0]
    nblk = rows // TOK_BLK
    return pl.pallas_call(
        _combine_kernel,
        grid=(nblk,),
        in_specs=[pl.BlockSpec((None, TOP_K, TOK_BLK), lambda i: (blk0 + i, 0, 0), memory_space=pltpu.SMEM),
                  pl.BlockSpec((None, TOP_K, TOK_BLK), lambda i: (blk0 + jnp.minimum(i + 1, nblk - 1), 0, 0),
                               memory_space=pltpu.SMEM),
                  pl.BlockSpec((TOK_BLK, TOP_K), lambda i: (blk0 + i, 0)),
                  pl.BlockSpec(memory_space=pl.ANY),
                  pl.BlockSpec((TOK_BLK, D_MODEL), lambda i: (i, 0)),
                  _mod_spec(per_row, TOK_BLK, rows_per_seq, 5, 1),
                  pl.BlockSpec((1, D_MODEL), lambda i: (0, 0))],
        out_specs=pl.BlockSpec((TOK_BLK, D_MODEL), lambda i: (i, 0)),
        out_shape=jax.ShapeDtypeStruct((rows, D_MODEL), F32),
        scratch_shapes=[pltpu.VMEM((2, TOP_K, TOK_BLK, D_MODEL), F32), pltpu.SemaphoreType.DMA((2,))],
        compiler_params=_cparams(("arbitrary",)),
        name="combine",
    )(pos3, pos3, gates_t, ys, x1, mod, g_post_ffn.reshape(1, D_MODEL))


def _moe(fr, p, batch, seq, nseq):
    l = 0
    h2_all, lg_all = fr['h2_all'], fr['lg_all']
    n_tok = h2_all.shape[0]
    pos, gates, texp, trows, ntiles, pstart, plen = _route(lg_all)
    pos3 = pos.reshape(TOP_K, n_tok // TOK_BLK, TOK_BLK).transpose(1, 0, 2)
    xs = _dispatch(pos3, pstart[:, 0], plen[:, 0], h2_all)
    ys = _experts(texp[0], trows[0], ntiles[0, :1], xs, p['w_gate_up'][l], p['b_gate_up'][l],
                  p['w_down'][l], p['b_down'][l])
    gates_t = gates.T
    y_p = _combine(pos3, gates_t, ys, fr['p']['x1'], fr['mod_p'], False, seq, p['g_post_ffn'][l], 0)
    y_s = _combine(pos3, gates_t, ys, fr['s']['x1'], fr['mod_s'], True, 1, p['g_post_ffn'][l],
                   batch * seq // TOK_BLK)
    return y_p, y_s


def kernel(x_prompt, x_sample, c_prompt, c_sample, cache_win_k, cache_win_v, state_ssm_re, state_ssm_im, w_ada, b_ada, g_pre_mix, g_post_mix, g_pre_ffn, g_post_ffn, w_in, attn_sinks, rel_bias, ssm_a_re, ssm_a_im, ssm_log_dt, ssm_b_re, ssm_b_im, ssm_c_re, ssm_c_im, ssm_d, w_glu, b_glu, w_br_attn, w_br_ssm, w_out, w_router, b_router, w_gate_up, b_gate_up, w_down, b_down):
    p = dict(locals())
    batch, seq, _ = x_prompt.shape
    nseq = x_sample.shape[0]
    fr = _front(p)
    y_p, y_s = _moe(fr, p, batch, seq, nseq)
    fp, fs = fr['p'], fr['s']
    return (y_p.reshape(batch, seq, D_MODEL), y_s.reshape(nseq, 1, D_MODEL),
            fp['new_k'], fp['new_v'], fp['h_re'], fp['h_im'],
            fs['new_k'], fs['new_v'], fs['h_re'], fs['h_im'])
```

```python
import functools
import math

import numpy as np
import jax
import jax.numpy as jnp
from jax import lax
from jax.experimental import pallas as pl
from jax.experimental.pallas import tpu as pltpu

F32 = jnp.float32
BF16 = jnp.bfloat16
I32 = jnp.int32

D_MODEL = 2048
N_HEADS = 16
N_KV_HEADS = 4
HEAD_DIM = 64
Q_GROUP = N_HEADS // N_KV_HEADS
WINDOW = 128
N_BUCKETS = 32
MAX_EXACT = N_BUCKETS // 2
MAX_DISTANCE = 128
D_SSM = 1024
SSM_GROUP = 16
N_SSM_GROUPS = 64
SSM_STATE = 64
N_EXPERTS = 32
TOP_K = 4
D_FF = 2048
SWIGLU_LIMIT = 7.0
SWIGLU_ALPHA = 1.702
NORM_EPS = 1e-6
NEG_INF = -1e30
Q_W = N_HEADS * HEAD_DIM
KV_W = N_KV_HEADS * HEAD_DIM
IN_W = Q_W + 2 * KV_W + D_SSM + 2 * D_MODEL
SSM_W = N_SSM_GROUPS * SSM_STATE

LANE = 128
SUBLANE = 8
VMEM_LIMIT = 56 * 1024 * 1024

PROJ_TM = 1024
PROJ_TN = 512
MERGE_TM = 512
MERGE_TK = 512
SSM_TC = 128
SSM_LB = 512
SSM_TILES = D_SSM // LANE
TOK_BLK = 128
EXP_TM = 1280
EXP_BIG = 512
EXP_SUB = 128
EXP_TF = 512
EXP_TD = 512
EXP_NF = D_FF // EXP_TF
EXP_ND = D_MODEL // EXP_TD


def _cparams(sem):
    return pltpu.CompilerParams(dimension_semantics=sem, vmem_limit_bytes=VMEM_LIMIT)


def _sigmoid(x):
    return 1.0 / (1.0 + jnp.exp(-x))


def _rms(x, g):
    return x * lax.rsqrt(jnp.mean(x * x, axis=-1, keepdims=True) + NORM_EPS) * g


def _dot(a, b):
    return jnp.dot(a, b, preferred_element_type=F32)


def _dot_nt(a, b):
    return lax.dot_general(a, b, (((1,), (1,)), ((), ())), preferred_element_type=F32)


def _ada_kernel(c_ref, w_ref, b_ref, o_ref):
    c = c_ref[...]
    s = (c * _sigmoid(c)).astype(BF16)
    o_ref[...] = _dot(s, w_ref[...].astype(BF16)) + b_ref[...]


def _ada(c_all, w_ada, b_ada):
    rows = c_all.shape[0]
    tn = 1024
    n = w_ada.shape[1]
    return pl.pallas_call(
        _ada_kernel,
        grid=(n // tn,),
        in_specs=[pl.BlockSpec((rows, D_MODEL), lambda j: (0, 0)),
                  pl.BlockSpec((D_MODEL, tn), lambda j: (0, j)),
                  pl.BlockSpec((1, tn), lambda j: (0, j))],
        out_specs=pl.BlockSpec((rows, tn), lambda j: (0, j)),
        out_shape=jax.ShapeDtypeStruct((rows, n), F32),
        compiler_params=_cparams(("arbitrary",)),
        name="ada",
    )(c_all, w_ada, b_ada.reshape(1, n))


def _t5_bucket_np(dist):
    n = np.maximum(dist, 0)
    nf = np.maximum(n, 1).astype(np.float64)
    large = MAX_EXACT + (np.log(nf / MAX_EXACT) / math.log(MAX_DISTANCE / MAX_EXACT)
                         * (N_BUCKETS - MAX_EXACT)).astype(np.int32)
    large = np.minimum(large, N_BUCKETS - 1)
    return np.where(n < MAX_EXACT, n, large).astype(np.int32)


def _bias_kernel(bucket_ref, rb_ref, o_ref):
    h = pl.program_id(0)
    bucket = bucket_ref[...]
    acc = jnp.full(bucket.shape, NEG_INF, F32)
    for b in range(N_BUCKETS):
        acc = jnp.where(bucket == b, rb_ref[b, h], acc)
    o_ref[...] = acc


def _bias_table(bucket_np, rel_bias):
    r, c = bucket_np.shape
    return pl.pallas_call(
        _bias_kernel,
        grid=(N_HEADS,),
        in_specs=[pl.BlockSpec((r, c), lambda h: (0, 0)),
                  pl.BlockSpec(memory_space=pltpu.SMEM)],
        out_specs=pl.BlockSpec((None, r, c), lambda h: (h, 0, 0)),
        out_shape=jax.ShapeDtypeStruct((N_HEADS, r, c), F32),
        compiler_params=_cparams(("arbitrary",)),
        name="bias",
    )(jnp.asarray(bucket_np), rel_bias)


def _proj_kernel(x_ref, sc_ref, sh_ref, g_ref, w_ref, o_ref, h_scr):
    @pl.when(pl.program_id(1) == 0)
    def _():
        h = _rms(x_ref[...], g_ref[...]) * (1.0 + sc_ref[...]) + sh_ref[...]
        h_scr[...] = h.astype(BF16)

    o_ref[...] = _dot(h_scr[...], w_ref[...])


def _mod_spec(per_row, tm, rows_per_seq, col, nargs):
    if per_row:
        if nargs == 1:
            return pl.BlockSpec((tm, D_MODEL), lambda i: (i, col))
        return pl.BlockSpec((tm, D_MODEL), lambda i, j: (i, col))
    tiles_per_seq = rows_per_seq // tm
    if nargs == 1:
        return pl.BlockSpec((None, 1, D_MODEL), lambda i: (i // tiles_per_seq, 0, col))
    return pl.BlockSpec((None, 1, D_MODEL), lambda i, j: (i // tiles_per_seq, 0, col))


def _proj(x, mod, per_row, rows_per_seq, g_pre, w_in):
    rows = x.shape[0]
    tm = min(PROJ_TM, rows)
    return pl.pallas_call(
        _proj_kernel,
        grid=(rows // tm, IN_W // PROJ_TN),
        in_specs=[pl.BlockSpec((tm, D_MODEL), lambda i, j: (i, 0)),
                  _mod_spec(per_row, tm, rows_per_seq, 1, 2),
                  _mod_spec(per_row, tm, rows_per_seq, 0, 2),
                  pl.BlockSpec((1, D_MODEL), lambda i, j: (0, 0)),
                  pl.BlockSpec((D_MODEL, PROJ_TN), lambda i, j: (0, j))],
        out_specs=pl.BlockSpec((tm, PROJ_TN), lambda i, j: (i, j)),
        out_shape=jax.ShapeDtypeStruct((rows, IN_W), F32),
        scratch_shapes=[pltpu.VMEM((tm, D_MODEL), BF16)],
        compiler_params=_cparams(("arbitrary", "arbitrary")),
        name="proj",
    )(x, mod, mod, g_pre.reshape(1, D_MODEL), w_in)


def _attn_prompt_kernel(q_ref, kc_ref, kp_ref, vc_ref, vp_ref, bias_ref, sink_ref, o_ref):
    has_prev = pl.program_id(1) > 0
    q = q_ref[...]
    k = jnp.concatenate([kp_ref[...], kc_ref[...]], axis=0)
    v = jnp.concatenate([vp_ref[...], vc_ref[...]], axis=0)
    col = lax.broadcasted_iota(I32, (WINDOW, 2 * WINDOW), 1)
    key_ok = (col >= WINDOW) | has_prev
    outs = []
    for g in range(N_KV_HEADS):
        kg = k[:, g * HEAD_DIM:(g + 1) * HEAD_DIM].astype(BF16)
        vg = v[:, g * HEAD_DIM:(g + 1) * HEAD_DIM].astype(BF16)
        for hh in range(Q_GROUP):
            h = g * Q_GROUP + hh
            qh = q[:, h * HEAD_DIM:(h + 1) * HEAD_DIM].astype(BF16)
            s = _dot_nt(qh, kg) * (HEAD_DIM ** -0.5) + bias_ref[h]
            s = jnp.where(key_ok, s, NEG_INF)
            sink = sink_ref[h]
            m = jnp.maximum(jnp.max(s, axis=-1, keepdims=True), sink)
            p = jnp.exp(s - m)
            den = jnp.sum(p, axis=-1, keepdims=True) + jnp.exp(sink - m)
            outs.append(_dot(p.astype(BF16), vg) / den)
    o_ref[...] = jnp.concatenate(outs, axis=-1)


def _attn_prompt(proj, batch, seq, bias, sinks):
    nb = seq // WINDOW
    kcol = Q_W // KV_W
    vcol = kcol + 1
    cur = lambda c: (lambda b, n: (b * nb + n, c))
    prev = lambda c: (lambda b, n: (b * nb + jnp.maximum(n - 1, 0), c))
    return pl.pallas_call(
        _attn_prompt_kernel,
        grid=(batch, nb),
        in_specs=[pl.BlockSpec((WINDOW, Q_W), cur(0)),
                  pl.BlockSpec((WINDOW, KV_W), cur(kcol)),
                  pl.BlockSpec((WINDOW, KV_W), prev(kcol)),
                  pl.BlockSpec((WINDOW, KV_W), cur(vcol)),
                  pl.BlockSpec((WINDOW, KV_W), prev(vcol)),
                  pl.BlockSpec((N_HEADS, WINDOW, 2 * WINDOW), lambda b, n: (0, 0, 0)),
                  pl.BlockSpec(memory_space=pltpu.SMEM)],
        out_specs=pl.BlockSpec((WINDOW, Q_W), lambda b, n: (b * nb + n, 0)),
        out_shape=jax.ShapeDtypeStruct((batch * seq, Q_W), F32),
        compiler_params=_cparams(("arbitrary", "arbitrary")),
        name="attn_prompt",
    )(proj, proj, proj, proj, proj, bias, sinks)


def _attn_sample_kernel(q_ref, kn_ref, vn_ref, ck_ref, cv_ref, bias_ref, sink_ref,
                        o_ref, nk_ref, nv_ref):
    tb = q_ref.shape[0]
    row = lax.broadcasted_iota(I32, (tb, WINDOW, KV_W), 1)
    last = row == WINDOW - 1
    nk = jnp.where(last, kn_ref[...], pltpu.roll(ck_ref[...], WINDOW - 1, 1))
    nv = jnp.where(last, vn_ref[...], pltpu.roll(cv_ref[...], WINDOW - 1, 1))
    nk_ref[...] = nk
    nv_ref[...] = nv
    lane_grp = lax.broadcasted_iota(I32, (N_HEADS, KV_W), 1) // HEAD_DIM
    head_grp = lax.broadcasted_iota(I32, (N_HEADS, KV_W), 0) // Q_GROUP
    gmask = (lane_grp == head_grp).astype(F32)
    q = q_ref[...]
    qrow = jnp.concatenate([q] * N_KV_HEADS, axis=-1) * gmask
    s = jnp.einsum('bhc,brc->bhr', qrow.astype(BF16), nk.astype(BF16),
                   preferred_element_type=F32) * (HEAD_DIM ** -0.5)
    s = s + bias_ref[...]
    sink = sink_ref[...]
    m = jnp.maximum(jnp.max(s, axis=-1, keepdims=True), sink)
    p = jnp.exp(s - m)
    den = jnp.sum(p, axis=-1, keepdims=True) + jnp.exp(sink - m)
    o = jnp.einsum('bhr,brc->bhc', p.astype(BF16), nv.astype(BF16),
                   preferred_element_type=F32) * gmask
    o64 = o[..., 0:HEAD_DIM]
    for g in range(1, N_KV_HEADS):
        o64 = o64 + o[..., g * HEAD_DIM:(g + 1) * HEAD_DIM]
    o_ref[...] = o64 / den


def _attn_sample(q3, kn, vn, cache_k, cache_v, bias, sinks):
    nseq = q3.shape[0]
    tb = 16
    seq3 = lambda w: pl.BlockSpec((tb, WINDOW, w), lambda i: (i, 0, 0))
    return pl.pallas_call(
        _attn_sample_kernel,
        grid=(nseq // tb,),
        in_specs=[pl.BlockSpec((tb, N_HEADS, HEAD_DIM), lambda i: (i, 0, 0)),
                  pl.BlockSpec((tb, 1, KV_W), lambda i: (i, 0, 0)),
                  pl.BlockSpec((tb, 1, KV_W), lambda i: (i, 0, 0)),
                  seq3(KV_W), seq3(KV_W),
                  pl.BlockSpec((N_HEADS, WINDOW), lambda i: (0, 0)),
                  pl.BlockSpec((N_HEADS, 1), lambda i: (0, 0))],
        out_specs=[pl.BlockSpec((tb, N_HEADS, HEAD_DIM), lambda i: (i, 0, 0)),
                   seq3(KV_W), seq3(KV_W)],
        out_shape=[jax.ShapeDtypeStruct((nseq, N_HEADS, HEAD_DIM), F32),
                   jax.ShapeDtypeStruct((nseq, WINDOW, KV_W), F32),
                   jax.ShapeDtypeStruct((nseq, WINDOW, KV_W), F32)],
        compiler_params=_cparams(("arbitrary",)),
        name="attn_sample",
    )(q3, kn, vn, cache_k, cache_v, bias, sinks.reshape(N_HEADS, 1))


def _ssm_disc_kernel(are_ref, aim_ref, ldt_ref, bre_ref, bim_ref,
                     lbr_ref, lbi_ref, bbr_ref, bbi_ref):
    a_re = are_ref[...]
    a_im = aim_ref[...]
    dt = jnp.exp(ldt_ref[...])
    lam_re = a_re * dt
    lam_im = a_im * dt
    mag = jnp.exp(lam_re)
    lb_re = mag * jnp.cos(lam_im)
    lb_im = mag * jnp.sin(lam_im)
    den = a_re * a_re + a_im * a_im
    nr = lb_re - 1.0
    ni = lb_im
    coef_re = (nr * a_re + ni * a_im) / den
    coef_im = (ni * a_re - nr * a_im) / den
    b_re = bre_ref[...]
    b_im = bim_ref[...]
    lbr_ref[...] = lb_re
    lbi_ref[...] = lb_im
    bbr_ref[...] = coef_re * b_re - coef_im * b_im
    bbi_ref[...] = coef_re * b_im + coef_im * b_re


def _ssm_disc(a_re, a_im, log_dt, b_re, b_im):
    g, p, j = N_SSM_GROUPS, SSM_STATE, SSM_GROUP
    vec = jax.ShapeDtypeStruct((g, 1, p), F32)
    mat = jax.ShapeDtypeStruct((g, j, p), F32)
    return pl.pallas_call(
        _ssm_disc_kernel,
        out_shape=[vec, vec, mat, mat],
        name="ssm_disc",
    )(a_re.reshape(g, 1, p), a_im.reshape(g, 1, p), log_dt.reshape(g, 1, 1),
      b_re.transpose(0, 2, 1), b_im.transpose(0, 2, 1))


def _block_diag_tiles(x):
    a, b = x.shape[1], x.shape[2]
    eye = jnp.eye(SUBLANE, dtype=x.dtype)
    y = jnp.einsum('kgab,gh->kgahb', x.reshape(SSM_TILES, SUBLANE, a, b), eye)
    return y.reshape(SSM_TILES, SUBLANE * a, SUBLANE * b)


def _gelu_tanh(x):
    return 0.5 * x * (1.0 + jnp.tanh(math.sqrt(2.0 / math.pi) * (x + 0.044715 * (x * x * x))))


def _ssm_kernel(*refs, nseq, tc, seq_major):
    if seq_major:
        ua_ref, ub_ref, perm_ref, perm_t_ref = refs[:4]
        refs = refs[4:]
        u = jnp.concatenate([ua_ref[...].reshape(nseq * tc, D_SSM // 2),
                             ub_ref[...].reshape(nseq * tc, D_SSM // 2)], axis=1)
        ub = _dot(perm_ref[...], u.astype(BF16)).astype(BF16)
    else:
        u = refs[0][...]
        refs = refs[1:]
        ub = u.astype(BF16)
    (h0r_ref, h0i_ref, lbr_ref, lbi_ref, bb_ref, cc_ref, d_ref, wglu_ref, bglu_ref,
     y_ref, hTr_ref, hTi_ref, hre, him, st_r, st_i) = refs
    paired = nseq == 4

    @pl.when(pl.program_id(0) == 0)
    def _():
        if paired:
            st_r[...] = jnp.concatenate([h0r_ref[...], h0r_ref[...]], axis=0)
            st_i[...] = jnp.concatenate([h0i_ref[...], h0i_ref[...]], axis=0)
        else:
            st_r[...] = h0r_ref[...]
            st_i[...] = h0i_ref[...]

    half = SSM_W // SSM_TILES
    for k in range(SSM_TILES):
        bu = _dot(ub[:, k * LANE:(k + 1) * LANE], bb_ref[k])
        hre[:, k * half:(k + 1) * half] = bu[:, :half]
        him[:, k * half:(k + 1) * half] = bu[:, half:]

    for blk in range(SSM_W // SSM_LB):
        sl = slice(blk * SSM_LB, (blk + 1) * SSM_LB)
        ar = lbr_ref[:, sl]
        ai = lbi_ref[:, sl]
        if paired:
            lower = lax.broadcasted_iota(I32, (SUBLANE, SSM_LB), 0) < nseq

            def body(m, carry):
                sr, si = carry
                r0 = pl.multiple_of(m * SUBLANE, SUBLANE)
                br = hre[pl.ds(r0, SUBLANE), sl]
                bi = him[pl.ds(r0, SUBLANE), sl]
                xr = pltpu.roll(sr, nseq, 0)
                xi = pltpu.roll(si, nseq, 0)
                h1r = ar * xr - ai * xi + br
                h1i = ar * xi + ai * xr + bi
                yr = pltpu.roll(h1r, nseq, 0)
                yi = pltpu.roll(h1i, nseq, 0)
                h2r = ar * yr - ai * yi + br
                h2i = ar * yi + ai * yr + bi
                hre[pl.ds(r0, SUBLANE), sl] = jnp.where(lower, h1r, h2r)
                him[pl.ds(r0, SUBLANE), sl] = jnp.where(lower, h1i, h2i)
                return h2r, h2i

            sr, si = lax.fori_loop(0, tc * nseq // SUBLANE, body, (st_r[:, sl], st_i[:, sl]))
        else:
            def body(t, carry):
                sr, si = carry
                r0 = pl.multiple_of(t * nseq, SUBLANE)
                br = hre[pl.ds(r0, nseq), sl]
                bi = him[pl.ds(r0, nseq), sl]
                nr = ar * sr - ai * si + br
                ni = ar * si + ai * sr + bi
                hre[pl.ds(r0, nseq), sl] = nr
                him[pl.ds(r0, nseq), sl] = ni
                return nr, ni

            sr, si = lax.fori_loop(0, tc, body, (st_r[:, sl], st_i[:, sl]))
        st_r[:, sl] = sr
        st_i[:, sl] = si

    ys = []
    for k in range(SSM_TILES):
        hr = hre[:, k * half:(k + 1) * half].astype(BF16)
        hi = him[:, k * half:(k + 1) * half].astype(BF16)
        ys.append(_dot(hr, cc_ref[k, :half, :]) + _dot(hi, cc_ref[k, half:, :]))
    yc = jnp.concatenate(ys, axis=-1)
    if seq_major:
        yc_hi, yc_lo = _split_bf16(yc)
        yc = _dot(perm_t_ref[...], yc_hi) + _dot(perm_t_ref[...], yc_lo)
    y = _gelu_tanh(yc + d_ref[...] * u)
    z = _dot(y.astype(BF16), wglu_ref[...]) + bglu_ref[...]
    y_ref[...] = (y * _sigmoid(z)).reshape(y_ref.shape)

    if paired:
        hTr_ref[...] = st_r[nseq:, :]
        hTi_ref[...] = st_i[nseq:, :]
    else:
        hTr_ref[...] = st_r[...]
        hTi_ref[...] = st_i[...]


def _ssm(u_src, h0_re, h0_im, nseq, tc, lbr, lbi, bb, cc, d, w_glu, b_glu, seq_major=False):
    r = nseq * tc
    st_rows = max(nseq, SUBLANE)
    const2 = lambda shape: pl.BlockSpec(shape, lambda c: (0, 0))
    const3 = lambda shape: pl.BlockSpec(shape, lambda c: (0, 0, 0))
    if seq_major:
        steps = u_src.shape[1] // tc
        half_w = D_SSM // 2
        col0 = (Q_W + 2 * KV_W) // half_w
        t_idx, s_idx = np.divmod(np.arange(r), nseq)
        perm = np.zeros((r, r), np.float32)
        perm[np.arange(r), s_idx * tc + t_idx] = 1.0
        u_specs = [pl.BlockSpec((nseq, tc, half_w), lambda c: (0, c, col0)),
                   pl.BlockSpec((nseq, tc, half_w), lambda c: (0, c, col0 + 1)),
                   const2((r, r)), const2((r, r))]
        u_args = [u_src, u_src, jnp.asarray(perm, BF16), jnp.asarray(perm.T, BF16)]
        y_spec = pl.BlockSpec((nseq, tc, D_SSM), lambda c: (0, c, 0))
        y_shape = jax.ShapeDtypeStruct((nseq, steps * tc, D_SSM), F32)
    else:
        steps = u_src.shape[0] // r
        u_specs = [pl.BlockSpec((r, D_SSM), lambda c: (c, 0))]
        u_args = [u_src]
        y_spec = pl.BlockSpec((r, D_SSM), lambda c: (c, 0))
        y_shape = jax.ShapeDtypeStruct((steps * r, D_SSM), F32)
    return pl.pallas_call(
        functools.partial(_ssm_kernel, nseq=nseq, tc=tc, seq_major=seq_major),
        grid=(steps,),
        in_specs=u_specs + [
            const2((nseq, SSM_W)), const2((nseq, SSM_W)),
            const2((1, SSM_W)), const2((1, SSM_W)),
            const3((SSM_TILES, LANE, 2 * SSM_W // SSM_TILES)),
            const3((SSM_TILES, 2 * SSM_W // SSM_TILES, LANE)),
            const2((1, D_SSM)), const2((D_SSM, D_SSM)), const2((1, D_SSM))],
        out_specs=[y_spec, const2((nseq, SSM_W)), const2((nseq, SSM_W))],
        out_shape=[y_shape,
                   jax.ShapeDtypeStruct((nseq, SSM_W), F32),
                   jax.ShapeDtypeStruct((nseq, SSM_W), F32)],
        scratch_shapes=[pltpu.VMEM((r, SSM_W), F32), pltpu.VMEM((r, SSM_W), F32),
                        pltpu.VMEM((st_rows, SSM_W), F32), pltpu.VMEM((st_rows, SSM_W), F32)],
        compiler_params=_cparams(("arbitrary",)),
        name="ssm",
    )(*u_args, h0_re, h0_im, lbr, lbi, bb, cc, d.reshape(1, D_SSM), w_glu, b_glu.reshape(1, D_SSM))


def _split_bf16(x):
    hi = x.astype(BF16)
    lo = (x - hi.astype(F32)).astype(BF16)
    return hi, lo


def _merge_kernel(o_ref, y_ref, ga_ref, gs_ref, wa_ref, ws_ref, wo_ref, mix_ref):
    j = pl.program_id(1)
    a = _dot(o_ref[...].astype(BF16), wa_ref[...])
    s = _dot(y_ref[...].astype(BF16), ws_ref[...])
    merged = _sigmoid(ga_ref[...]) * a + _sigmoid(gs_ref[...]) * s
    contrib = _dot(merged.astype(BF16), wo_ref[...])

    @pl.when(j == 0)
    def _():
        mix_ref[...] = contrib

    @pl.when(j > 0)
    def _():
        mix_ref[...] += contrib


def _merge(o_attn, y_ssm, proj, w_br_attn, w_br_ssm, w_out):
    rows = o_attn.shape[0]
    tm = min(MERGE_TM, rows)
    nk = D_MODEL // MERGE_TK
    ga0 = (Q_W + 2 * KV_W + D_SSM) // MERGE_TK
    gs0 = ga0 + nk
    row2 = lambda w: pl.BlockSpec((tm, w), lambda i, j: (i, 0))
    return pl.pallas_call(
        _merge_kernel,
        grid=(rows // tm, nk),
        in_specs=[row2(Q_W), row2(D_SSM),
                  pl.BlockSpec((tm, MERGE_TK), lambda i, j: (i, ga0 + j)),
                  pl.BlockSpec((tm, MERGE_TK), lambda i, j: (i, gs0 + j)),
                  pl.BlockSpec((Q_W, MERGE_TK), lambda i, j: (0, j)),
                  pl.BlockSpec((D_SSM, MERGE_TK), lambda i, j: (0, j)),
                  pl.BlockSpec((MERGE_TK, D_MODEL), lambda i, j: (j, 0))],
        out_specs=row2(D_MODEL),
        out_shape=jax.ShapeDtypeStruct((rows, D_MODEL), F32),
        compiler_params=_cparams(("arbitrary", "arbitrary")),
        name="merge",
    )(o_attn, y_ssm, proj, proj, w_br_attn, w_br_ssm, w_out)


def _post_kernel(x_ref, mix_ref, gpm_ref, g1_ref, sc2_ref, sh2_ref, gpf_ref, wrt_ref, br_ref, *refs):
    x1_ref, h2_ref, lg_ref = refs[-3:]
    x1 = x_ref[...] + g1_ref[...] * _rms(mix_ref[...], gpm_ref[...])
    x1_ref[...] = x1
    h2 = _rms(x1, gpf_ref[...]) * (1.0 + sc2_ref[...]) + sh2_ref[...]
    h2_ref[...] = h2
    h_hi, h_lo = _split_bf16(h2)
    w_hi, w_lo = _split_bf16(wrt_ref[...])
    lg_ref[...] = (_dot_nt(w_hi, h_hi) + _dot_nt(w_hi, h_lo) + _dot_nt(w_lo, h_hi)) + br_ref[...]


def _post(x, mix, mod, per_row, rows_per_seq, n_total, row0, shared, g_post_mix, g_pre_ffn, w_router_t,
          b_router):
    rows = x.shape[0]
    tm = min(MERGE_TM, rows)
    blk0 = row0 // tm
    row = pl.BlockSpec((tm, D_MODEL), lambda i: (i, 0))
    vec = pl.BlockSpec((1, D_MODEL), lambda i: (0, 0))
    in_specs = [row, row, vec,
                _mod_spec(per_row, tm, rows_per_seq, 2, 1),
                _mod_spec(per_row, tm, rows_per_seq, 4, 1),
                _mod_spec(per_row, tm, rows_per_seq, 3, 1),
                vec,
                pl.BlockSpec((N_EXPERTS, D_MODEL), lambda i: (0, 0)),
                pl.BlockSpec((N_EXPERTS, 1), lambda i: (0, 0))]
    args = [x, mix, g_post_mix.reshape(1, D_MODEL), mod, mod, mod, g_pre_ffn.reshape(1, D_MODEL),
            w_router_t, b_router.reshape(N_EXPERTS, 1)]
    aliases = {}
    if shared is not None:
        aliases = {len(args): 1, len(args) + 1: 2}
        in_specs += [pl.BlockSpec(memory_space=pl.ANY), pl.BlockSpec(memory_space=pl.ANY)]
        args += list(shared)
    return pl.pallas_call(
        _post_kernel,
        grid=(rows // tm,),
        in_specs=in_specs,
        out_specs=[row,
                   pl.BlockSpec((tm, D_MODEL), lambda i: (blk0 + i, 0)),
                   pl.BlockSpec((N_EXPERTS, tm), lambda i: (0, blk0 + i))],
        out_shape=[jax.ShapeDtypeStruct((rows, D_MODEL), F32),
                   jax.ShapeDtypeStruct((n_total, D_MODEL), F32),
                   jax.ShapeDtypeStruct((N_EXPERTS, n_total), F32)],
        input_output_aliases=aliases,
        compiler_params=_cparams(("arbitrary",)),
        name="post",
    )(*args)


def _bucket_tables():
    ql = np.arange(WINDOW)[:, None]
    kl = np.arange(2 * WINDOW)[None, :]
    dist = ql + WINDOW - kl
    prompt = np.where((dist >= 0) & (dist < WINDOW), _t5_bucket_np(dist), -1).astype(np.int32)
    d_s = (WINDOW - 1 - np.arange(WINDOW))[None, :]
    sample = np.broadcast_to(_t5_bucket_np(d_s), (SUBLANE, WINDOW)).astype(np.int32)
    return prompt, sample


def _front(p):
    l = 0
    batch, seq, _ = p['x_prompt'].shape
    nseq = p['x_sample'].shape[0]
    xp = p['x_prompt'].reshape(batch * seq, D_MODEL)
    xs = p['x_sample'].reshape(nseq, D_MODEL)

    c_all = jnp.concatenate([p['c_prompt'], p['c_sample'],
                             jnp.zeros((SUBLANE - (batch + nseq) % SUBLANE, D_MODEL), F32)], axis=0)
    mod = _ada(c_all, p['w_ada'][l], p['b_ada'][l])
    mod_p = mod[:batch].reshape(batch, 1, 6 * D_MODEL)
    mod_s = mod[batch:batch + nseq]

    bucket_p, bucket_s = _bucket_tables()
    bias_p = _bias_table(bucket_p, p['rel_bias'])
    bias_s = _bias_table(bucket_s, p['rel_bias'])[:, 0, :]
    sinks = p['attn_sinks'][l]

    w_in_t = p['w_in'][l].astype(BF16)
    proj_p = _proj(xp, mod_p, False, seq, p['g_pre_mix'][l], w_in_t)
    proj_s = _proj(xs, mod_s, True, 1, p['g_pre_mix'][l], w_in_t)

    o_p = _attn_prompt(proj_p, batch, seq, bias_p, sinks)
    kv_p = proj_p.reshape(batch, seq, IN_W)[:, seq - WINDOW:, Q_W:Q_W + 2 * KV_W]
    new_k_p = kv_p[..., :KV_W].reshape(1, batch, WINDOW, N_KV_HEADS, HEAD_DIM)
    new_v_p = kv_p[..., KV_W:].reshape(1, batch, WINDOW, N_KV_HEADS, HEAD_DIM)
    o_s3, new_k_s, new_v_s = _attn_sample(
        proj_s[:, :Q_W].reshape(nseq, N_HEADS, HEAD_DIM),
        proj_s[:, Q_W:Q_W + KV_W].reshape(nseq, 1, KV_W),
        proj_s[:, Q_W + KV_W:Q_W + 2 * KV_W].reshape(nseq, 1, KV_W),
        p['cache_win_k'][l].reshape(nseq, WINDOW, KV_W),
        p['cache_win_v'][l].reshape(nseq, WINDOW, KV_W), bias_s, sinks)
    o_s = o_s3.reshape(nseq, Q_W)

    lbr, lbi, bbr, bbi = _ssm_disc(p['ssm_a_re'][l], p['ssm_a_im'][l], p['ssm_log_dt'][l],
                                   p['ssm_b_re'][l], p['ssm_b_im'][l])
    lbr = lbr.reshape(1, SSM_W)
    lbi = lbi.reshape(1, SSM_W)
    bb = jnp.concatenate([_block_diag_tiles(bbr), _block_diag_tiles(bbi)], axis=-1).astype(BF16)
    c_re_t = p['ssm_c_re'][l].transpose(0, 2, 1)
    c_im_t = p['ssm_c_im'][l].transpose(0, 2, 1)
    cc = jnp.concatenate([_block_diag_tiles(c_re_t), -_block_diag_tiles(c_im_t)], axis=1).astype(BF16)
    u0 = Q_W + 2 * KV_W
    zeros = jnp.zeros((batch, SSM_W), F32)
    w_glu = p['w_glu'][l].astype(BF16)
    y_p3, hr_p, hi_p = _ssm(proj_p.reshape(batch, seq, IN_W), zeros, zeros, batch, SSM_TC, lbr, lbi, bb, cc,
                            p['ssm_d'][l], w_glu, p['b_glu'][l], seq_major=True)
    y_p = y_p3.reshape(batch * seq, D_SSM)
    y_s, hr_s, hi_s = _ssm(proj_s[:, u0:u0 + D_SSM], p['state_ssm_re'][l].reshape(nseq, SSM_W),
                           p['state_ssm_im'][l].reshape(nseq, SSM_W), nseq, 1, lbr, lbi, bb, cc,
                           p['ssm_d'][l], w_glu, p['b_glu'][l])

    wa_t = p['w_br_attn'][l].astype(BF16)
    ws_t = p['w_br_ssm'][l].astype(BF16)
    wo = p['w_out'][l].astype(BF16)
    mix_p = _merge(o_p, y_p, proj_p, wa_t, ws_t, wo)
    mix_s = _merge(o_s, y_s, proj_s, wa_t, ws_t, wo)
    n_total = batch * seq + nseq
    post = functools.partial(_post, g_post_mix=p['g_post_mix'][l], g_pre_ffn=p['g_pre_ffn'][l],
                             w_router_t=p['w_router'][l].T, b_router=p['b_router'][l])
    x1_p, h2_buf, lg_buf = post(xp, mix_p, mod_p, False, seq, n_total, 0, None)
    x1_s, h2_all, lg_all = post(xs, mix_s, mod_s, True, 1, n_total, batch * seq, (h2_buf, lg_buf))

    st = lambda h, n: h.reshape(1, n, N_SSM_GROUPS, SSM_STATE)
    return dict(
        mod_p=mod_p, mod_s=mod_s, h2_all=h2_all, lg_all=lg_all,
        p=dict(proj=proj_p, o_attn=o_p, new_k=new_k_p, new_v=new_v_p, y_ssm=y_p, h_re=st(hr_p, batch),
               h_im=st(hi_p, batch), x1=x1_p),
        s=dict(proj=proj_s, o_attn=o_s, new_k=new_k_s.reshape(1, nseq, WINDOW, N_KV_HEADS, HEAD_DIM),
               new_v=new_v_s.reshape(1, nseq, WINDOW, N_KV_HEADS, HEAD_DIM), y_ssm=y_s,
               h_re=st(hr_s, nseq), h_im=st(hi_s, nseq), x1=x1_s))


def _count_steps(c, step, n_max):
    out = jnp.zeros_like(c)
    for q in range(-(-n_max // step)):
        out = out + jnp.where(c > float(q * step), 1.0, 0.0)
    return out


def _route_kernel(lg_ref, pos_ref, gate_ref, texp_ref, trows_ref, ntiles_ref, pstart_ref, plen_ref):
    lg = lg_ref[...]
    e, tn = lg.shape
    erow = lax.broadcasted_iota(I32, (e, tn), 0).astype(F32)
    work = lg
    vals, hits = [], []
    for _ in range(TOP_K):
        m = jnp.max(work, axis=0, keepdims=True)
        idx = jnp.min(jnp.where(work == m, erow, float(e)), axis=0, keepdims=True)
        hit = erow == idx
        vals.append(m)
        hits.append(hit)
        work = jnp.where(hit, -jnp.inf, work)
    ex = [jnp.exp(v - vals[0]) for v in vals]
    den = ex[0] + ex[1] + ex[2] + ex[3]
    gate_ref[...] = jnp.concatenate([x / den for x in ex], axis=0)

    chosen = jnp.zeros((e, tn), F32)
    for hit in hits:
        chosen = chosen + jnp.where(hit, 1.0, 0.0)
    chosen_b = chosen.astype(BF16)
    tri = (lax.broadcasted_iota(I32, (LANE, LANE), 0) <= lax.broadcasted_iota(I32, (LANE, LANE), 1))
    tri = jnp.where(tri, 1.0, 0.0).astype(BF16)
    carry = jnp.zeros((e, 1), F32)
    ranks = []
    for b in range(tn // LANE):
        blk = chosen[:, b * LANE:(b + 1) * LANE]
        inc = _dot(chosen_b[:, b * LANE:(b + 1) * LANE], tri) + carry
        ranks.append(inc - blk)
        carry = inc[:, LANE - 1:LANE]
    rank = jnp.concatenate(ranks, axis=1)
    cnt_col = carry
    cnt_row = _dot_nt(jnp.ones((SUBLANE, tn), BF16), chosen_b)[0:1, :]

    tiles_col = _count_steps(cnt_col, EXP_TM, tn)
    tiles_row = _count_steps(cnt_row, EXP_TM, tn)
    ee_r = lax.broadcasted_iota(I32, (e, e), 0)
    ee_c = lax.broadcasted_iota(I32, (e, e), 1)
    tstart_col = jnp.sum(jnp.where(ee_c < ee_r, tiles_row, 0.0), axis=1, keepdims=True)
    ntiles = jnp.sum(tiles_row, axis=1, keepdims=True)
    rstart_col = tstart_col * float(EXP_TM)
    pos = [jnp.sum(jnp.where(hit, rstart_col + rank, 0.0), axis=0, keepdims=True) for hit in hits]
    pos_ref[...] = jnp.concatenate(pos, axis=0).astype(I32)

    mm = lax.broadcasted_iota(I32, (e, LANE), 1).astype(F32)
    e_col = lax.broadcasted_iota(I32, (e, LANE), 0).astype(F32)
    own = (mm >= tstart_col) & (mm < tstart_col + tiles_col)
    texp = jnp.sum(jnp.where(own, e_col, 0.0), axis=0, keepdims=True)
    rows_here = jnp.minimum(float(EXP_TM), cnt_col - (mm - tstart_col) * float(EXP_TM))
    trows = jnp.sum(jnp.where(own, rows_here, 0.0), axis=0, keepdims=True)
    last_e = jnp.max(jnp.where(tiles_col > 0.0, e_col, 0.0), axis=0, keepdims=True)
    texp = jnp.where(mm[0:1, :] < ntiles, texp, last_e)
    texp_ref[...] = texp.astype(I32)
    trows_ref[...] = trows.astype(I32)
    ntiles_ref[...] = jnp.broadcast_to(ntiles, (1, LANE)).astype(I32)
    nsub_col = _count_steps(cnt_col, EXP_SUB, tn)
    pstart_ref[...] = jnp.broadcast_to(rstart_col + cnt_col, (e, LANE)).astype(I32)
    plen_ref[...] = jnp.broadcast_to(nsub_col * float(EXP_SUB) - cnt_col, (e, LANE)).astype(I32)


def _route(lg_t):
    e, tn = lg_t.shape
    i32 = lambda shape: jax.ShapeDtypeStruct(shape, I32)
    return pl.pallas_call(
        _route_kernel,
        out_shape=[i32((TOP_K, tn)), jax.ShapeDtypeStruct((TOP_K, tn), F32),
                   i32((1, LANE)), i32((1, LANE)), i32((1, LANE)), i32((e, LANE)), i32((e, LANE))],
        compiler_params=pltpu.CompilerParams(vmem_limit_bytes=VMEM_LIMIT),
        name="route",
    )(lg_t)


def _max_tiles(n_tok):
    return (n_tok * TOP_K) // EXP_TM + N_EXPERTS


def _dispatch_kernel(pos_ref, pstart_ref, plen_ref, h2_ref, zero_ref, xs_ref, sem):
    i = pl.program_id(0)

    def row_copy(src, s, d):
        return pltpu.make_async_copy(src.at[pl.ds(s, 1)], xs_ref.at[pl.ds(d, 1)], sem)

    def issue(t, c):
        for k in range(TOP_K):
            row_copy(h2_ref, t, pos_ref[k, t]).start(priority=k % 2)
        return c

    lax.fori_loop(0, TOK_BLK, issue, 0)
    for _ in range(TOP_K):
        pltpu.make_async_copy(h2_ref, xs_ref.at[pl.ds(0, TOK_BLK)], sem).wait()

    @pl.when(i == 0)
    def _():
        def per_expert(ex, c):
            n = plen_ref[ex]
            s = pstart_ref[ex]

            def zissue(r, cc):
                row_copy(zero_ref, 0, s + r).start()
                return cc

            def zdrain(r, cc):
                row_copy(zero_ref, 0, 0).wait()
                return cc

            lax.fori_loop(0, n, zissue, 0)
            lax.fori_loop(0, n, zdrain, 0)
            return c

        lax.fori_loop(0, N_EXPERTS, per_expert, 0)


def _dispatch(pos3, pstart, plen, h2_all):
    n_tok = h2_all.shape[0]
    n_rows = _max_tiles(n_tok) * EXP_TM
    smem = pl.BlockSpec(memory_space=pltpu.SMEM)
    hbm = pl.BlockSpec(memory_space=pl.ANY)
    return pl.pallas_call(
        _dispatch_kernel,
        grid=(n_tok // TOK_BLK,),
        in_specs=[pl.BlockSpec((None, TOP_K, TOK_BLK), lambda i: (i, 0, 0), memory_space=pltpu.SMEM),
                  smem, smem,
                  pl.BlockSpec((TOK_BLK, D_MODEL), lambda i: (i, 0)),
                  pl.BlockSpec((SUBLANE, D_MODEL), lambda i: (0, 0))],
        out_specs=hbm,
        out_shape=jax.ShapeDtypeStruct((n_rows, D_MODEL), F32),
        scratch_shapes=[pltpu.SemaphoreType.DMA(())],
        compiler_params=_cparams(("arbitrary",)),
        name="dispatch",
    )(pos3, pstart, plen, h2_all, jnp.zeros((SUBLANE, D_MODEL), F32))


def _expert_kernel(texp_ref, trows_ref, nt_ref, xs_hbm, wg_ref, wl_ref, bg_ref, bl_ref, wd_ref, bd_ref,
                   o_ref, x_ref, x_sem, act_scr, wg_scr, wl_scr, wd_scr):
    m = pl.program_id(0)
    s = pl.program_id(1)
    n_tiles = nt_ref[0]
    valid = m < n_tiles

    def x_copy(tile):
        return pltpu.make_async_copy(xs_hbm.at[pl.ds(pl.multiple_of(tile * EXP_TM, EXP_TM), EXP_TM)],
                                     x_ref, x_sem)

    @pl.when((m == 0) & (s == 0))
    def _():
        x_copy(0).start()

    @pl.when(valid & (s == 0))
    def _():
        x_copy(m).wait()

    @pl.when(valid & (s == EXP_NF) & (m + 1 < n_tiles))
    def _():
        x_copy(m + 1).start()

    rows = trows_ref[m]
    nbig = lax.shift_right_logical(rows, int(math.log2(EXP_BIG)))
    big_rows = nbig * EXP_BIG
    nsmall = lax.shift_right_logical(rows - big_rows + (EXP_SUB - 1), int(math.log2(EXP_SUB)))
    nsub_done = nbig * (EXP_BIG // EXP_SUB) + nsmall

    def over_rows(step):
        def big(r, c):
            step(pl.multiple_of(r * EXP_BIG, EXP_BIG), EXP_BIG)
            return c

        def small(r, c):
            step(pl.multiple_of(big_rows + r * EXP_SUB, EXP_SUB), EXP_SUB)
            return c

        lax.fori_loop(0, nbig, big, 0)
        lax.fori_loop(0, nsmall, small, 0)

    @pl.when(valid & (s < EXP_NF))
    def _():
        wg_scr[...] = wg_ref[...].astype(BF16)
        wl_scr[...] = wl_ref[...].astype(BF16)

        def step(r0, n):
            xb = x_ref[pl.ds(r0, n), :].astype(BF16)
            hg = _dot(xb, wg_scr[...]) + bg_ref[...]
            hl = _dot(xb, wl_scr[...]) + bl_ref[...]
            x_glu = jnp.minimum(hg, SWIGLU_LIMIT)
            x_lin = jnp.clip(hl, -SWIGLU_LIMIT, SWIGLU_LIMIT)
            act = x_glu * _sigmoid(SWIGLU_ALPHA * x_glu) * (x_lin + 1.0)
            act_scr[s, pl.ds(r0, n), :] = act.astype(BF16)

        over_rows(step)

    @pl.when(valid & (s >= EXP_NF))
    def _():
        wd_scr[...] = wd_ref[...].astype(BF16)

        def step(r0, n):
            acc = jnp.broadcast_to(bd_ref[...], (n, EXP_TD))
            for f in range(EXP_NF):
                acc = acc + _dot(act_scr[f, pl.ds(r0, n), :], wd_scr[f * EXP_TF:(f + 1) * EXP_TF, :])
            o_ref[pl.ds(r0, n), :] = acc

        def zero(r, c):
            r0 = pl.multiple_of(r * EXP_SUB, EXP_SUB)
            o_ref[pl.ds(r0, EXP_SUB), :] = jnp.zeros((EXP_SUB, EXP_TD), F32)
            return c

        over_rows(step)
        lax.fori_loop(nsub_done, EXP_TM // EXP_SUB, zero, 0)


def _experts(texp, trows, ntiles, xs, w_gate_up, b_gate_up, w_down, b_down):
    n_tiles = xs.shape[0] // EXP_TM
    nsteps = EXP_NF + EXP_ND

    def tile(m, nt):
        return jnp.minimum(m, nt[0] - 1)

    def ea(m, s, te, nt):
        return te[jnp.where(s < EXP_NF, m, jnp.minimum(m + 1, nt[0] - 1))]

    def fa(m, s, nt):
        return jnp.where((m < nt[0]) & (s < EXP_NF), s, 0)

    def fb(m, s, nt):
        return jnp.where(m < nt[0], jnp.maximum(s - EXP_NF, 0), EXP_ND - 1)

    grid_spec = pltpu.PrefetchScalarGridSpec(
        num_scalar_prefetch=3,
        grid=(n_tiles, nsteps),
        in_specs=[
            pl.BlockSpec(memory_space=pl.ANY),
            pl.BlockSpec((None, D_MODEL, EXP_TF),
                         lambda m, s, te, tr, nt: (ea(m, s, te, nt), 0, fa(m, s, nt))),
            pl.BlockSpec((None, D_MODEL, EXP_TF),
                         lambda m, s, te, tr, nt: (ea(m, s, te, nt), 0, EXP_NF + fa(m, s, nt))),
            pl.BlockSpec((None, 1, EXP_TF), lambda m, s, te, tr, nt: (ea(m, s, te, nt), 0, fa(m, s, nt))),
            pl.BlockSpec((None, 1, EXP_TF),
                         lambda m, s, te, tr, nt: (ea(m, s, te, nt), 0, EXP_NF + fa(m, s, nt))),
            pl.BlockSpec((None, D_FF, EXP_TD), lambda m, s, te, tr, nt: (te[m], 0, fb(m, s, nt))),
            pl.BlockSpec((None, 1, EXP_TD), lambda m, s, te, tr, nt: (te[m], 0, fb(m, s, nt))),
        ],
        out_specs=pl.BlockSpec((EXP_TM, EXP_TD), lambda m, s, te, tr, nt: (tile(m, nt), fb(m, s, nt))),
        scratch_shapes=[pltpu.VMEM((EXP_TM, D_MODEL), F32), pltpu.SemaphoreType.DMA(()),
                        pltpu.VMEM((EXP_NF, EXP_TM, EXP_TF), BF16),
                        pltpu.VMEM((D_MODEL, EXP_TF), BF16), pltpu.VMEM((D_MODEL, EXP_TF), BF16),
                        pltpu.VMEM((D_FF, EXP_TD), BF16)],
    )
    return pl.pallas_call(
        _expert_kernel,
        grid_spec=grid_spec,
        out_shape=jax.ShapeDtypeStruct((xs.shape[0], D_MODEL), F32),
        compiler_params=_cparams(("arbitrary", "arbitrary")),
        name="experts",
    )(texp, trows, ntiles, xs, w_gate_up, w_gate_up,
      b_gate_up.reshape(N_EXPERTS, 1, 2 * D_FF), b_gate_up.reshape(N_EXPERTS, 1, 2 * D_FF),
      w_down, b_down.reshape(N_EXPERTS, 1, D_MODEL))


def _combine_kernel(pos_ref, pos_next_ref, gate_ref, ys_ref, x1_ref, g2_ref, gpf_ref, o_ref, buf, sem):
    i = pl.program_id(0)
    slot = lax.rem(i, 2)

    def gather(p_ref, sl):
        def issue(t, c):
            for k in range(TOP_K):
                pltpu.make_async_copy(ys_ref.at[pl.ds(p_ref[k, t], 1)], buf.at[sl, k, pl.ds(t, 1)],
                                      sem.at[sl]).start(priority=k % 2)
            return c

        lax.fori_loop(0, TOK_BLK, issue, 0)

    @pl.when(i == 0)
    def _():
        gather(pos_ref, 0)

    @pl.when(i + 1 < pl.num_programs(0))
    def _():
        gather(pos_next_ref, 1 - slot)

    for k in range(TOP_K):
        pltpu.make_async_copy(ys_ref.at[pl.ds(0, TOK_BLK)], buf.at[slot, k], sem.at[slot]).wait()
    g = gate_ref[...]
    f = g[:, 0:1] * buf[slot, 0]
    for k in range(1, TOP_K):
        f = f + g[:, k:k + 1] * buf[slot, k]
    o_ref[...] = x1_ref[...] + g2_ref[...] * _rms(f, gpf_ref[...])


def _combine(pos3, gates_t, ys, x1, mod, per_row, rows_per_seq, g_post_ffn, blk0):
    rows = x1.shape[0]
    nblk = rows // TOK_BLK
    return pl.pallas_call(
        _combine_kernel,
        grid=(nblk,),
        in_specs=[pl.BlockSpec((None, TOP_K, TOK_BLK), lambda i: (blk0 + i, 0, 0), memory_space=pltpu.SMEM),
                  pl.BlockSpec((None, TOP_K, TOK_BLK), lambda i: (blk0 + jnp.minimum(i + 1, nblk - 1), 0, 0),
                               memory_space=pltpu.SMEM),
                  pl.BlockSpec((TOK_BLK, TOP_K), lambda i: (blk0 + i, 0)),
                  pl.BlockSpec(memory_space=pl.ANY),
                  pl.BlockSpec((TOK_BLK, D_MODEL), lambda i: (i, 0)),
                  _mod_spec(per_row, TOK_BLK, rows_per_seq, 5, 1),
                  pl.BlockSpec((1, D_MODEL), lambda i: (0, 0))],
        out_specs=pl.BlockSpec((TOK_BLK, D_MODEL), lambda i: (i, 0)),
        out_shape=jax.ShapeDtypeStruct((rows, D_MODEL), F32),
        scratch_shapes=[pltpu.VMEM((2, TOP_K, TOK_BLK, D_MODEL), F32), pltpu.SemaphoreType.DMA((2,))],
        compiler_params=_cparams(("arbitrary",)),
        name="combine",
    )(pos3, pos3, gates_t, ys, x1, mod, g_post_ffn.reshape(1, D_MODEL))


def _moe(fr, p, batch, seq, nseq):
    l = 0
    h2_all, lg_all = fr['h2_all'], fr['lg_all']
    n_tok = h2_all.shape[0]
    pos, gates, texp, trows, ntiles, pstart, plen = _route(lg_all)
    pos3 = pos.reshape(TOP_K, n_tok // TOK_BLK, TOK_BLK).transpose(1, 0, 2)
    xs = _dispatch(pos3, pstart[:, 0], plen[:, 0], h2_all)
    ys = _experts(texp[0], trows[0], ntiles[0, :1], xs, p['w_gate_up'][l], p['b_gate_up'][l],
                  p['w_down'][l], p['b_down'][l])
    gates_t = gates.T
    y_p = _combine(pos3, gates_t, ys, fr['p']['x1'], fr['mod_p'], False, seq, p['g_post_ffn'][l], 0)
    y_s = _combine(pos3, gates_t, ys, fr['s']['x1'], fr['mod_s'], True, 1, p['g_post_ffn'][l],
                   batch * seq // TOK_BLK)
    return y_p, y_s


def kernel(x_prompt, x_sample, c_prompt, c_sample, cache_win_k, cache_win_v, state_ssm_re, state_ssm_im, w_ada, b_ada, g_pre_mix, g_post_mix, g_pre_ffn, g_post_ffn, w_in, attn_sinks, rel_bias, ssm_a_re, ssm_a_im, ssm_log_dt, ssm_b_re, ssm_b_im, ssm_c_re, ssm_c_im, ssm_d, w_glu, b_glu, w_br_attn, w_br_ssm, w_out, w_router, b_router, w_gate_up, b_gate_up, w_down, b_down):
    p = dict(locals())
    batch, seq, _ = x_prompt.shape
    nseq = x_sample.shape[0]
    fr = _front(p)
    y_p, y_s = _moe(fr, p, batch, seq, nseq)
    fp, fs = fr['p'], fr['s']
    return (y_p.reshape(batch, seq, D_MODEL), y_s.reshape(nseq, 1, D_MODEL),
            fp['new_k'], fp['new_v'], fp['h_re'], fp['h_im'],
            fs['new_k'], fs['new_v'], fs['h_re'], fs['h_im'])
```

```python
import functools
import math

import numpy as np
import jax
import jax.numpy as jnp
from jax import lax
from jax.experimental import pallas as pl
from jax.experimental.pallas import tpu as pltpu

F32 = jnp.float32
BF16 = jnp.bfloat16
I32 = jnp.int32

D_MODEL = 2048
N_HEADS = 16
N_KV_HEADS = 4
HEAD_DIM = 64
Q_GROUP = N_HEADS // N_KV_HEADS
WINDOW = 128
N_BUCKETS = 32
MAX_EXACT = N_BUCKETS // 2
MAX_DISTANCE = 128
D_SSM = 1024
SSM_GROUP = 16
N_SSM_GROUPS = 64
SSM_STATE = 64
N_EXPERTS = 32
TOP_K = 4
D_FF = 2048
SWIGLU_LIMIT = 7.0
SWIGLU_ALPHA = 1.702
NORM_EPS = 1e-6
NEG_INF = -1e30
Q_W = N_HEADS * HEAD_DIM
KV_W = N_KV_HEADS * HEAD_DIM
IN_W = Q_W + 2 * KV_W + D_SSM + 2 * D_MODEL
SSM_W = N_SSM_GROUPS * SSM_STATE

LANE = 128
SUBLANE = 8
VMEM_LIMIT = 56 * 1024 * 1024

PROJ_TM = 1024
PROJ_TN = 512
MERGE_TM = 512
MERGE_TK = 512
SSM_TC = 128
SSM_LB = 512
SSM_TILES = D_SSM // LANE
TOK_BLK = 128
EXP_TM = 1280
EXP_BIG = 512
EXP_SUB = 128
EXP_TF = 512
EXP_TD = 512
EXP_KQ = 512
EXP_NF = D_FF // EXP_TF
EXP_ND = D_MODEL // EXP_TD


def _cparams(sem):
    return pltpu.CompilerParams(dimension_semantics=sem, vmem_limit_bytes=VMEM_LIMIT)


def _sigmoid(x):
    return 1.0 / (1.0 + jnp.exp(-x))


def _rms(x, g):
    return x * lax.rsqrt(jnp.mean(x * x, axis=-1, keepdims=True) + NORM_EPS) * g


def _dot(a, b):
    return jnp.dot(a, b, preferred_element_type=F32)


def _dot_nt(a, b):
    return lax.dot_general(a, b, (((1,), (1,)), ((), ())), preferred_element_type=F32)


def _ada_kernel(c_ref, w_ref, b_ref, o_ref):
    c = c_ref[...]
    s = (c * _sigmoid(c)).astype(BF16)
    o_ref[...] = _dot(s, w_ref[...].astype(BF16)) + b_ref[...]


def _ada(c_all, w_ada, b_ada):
    rows = c_all.shape[0]
    tn = 1024
    n = w_ada.shape[1]
    return pl.pallas_call(
        _ada_kernel,
        grid=(n // tn,),
        in_specs=[pl.BlockSpec((rows, D_MODEL), lambda j: (0, 0)),
                  pl.BlockSpec((D_MODEL, tn), lambda j: (0, j)),
                  pl.BlockSpec((1, tn), lambda j: (0, j))],
        out_specs=pl.BlockSpec((rows, tn), lambda j: (0, j)),
        out_shape=jax.ShapeDtypeStruct((rows, n), F32),
        compiler_params=_cparams(("arbitrary",)),
        name="ada",
    )(c_all, w_ada, b_ada.reshape(1, n))


def _t5_bucket_np(dist):
    n = np.maximum(dist, 0)
    nf = np.maximum(n, 1).astype(np.float64)
    large = MAX_EXACT + (np.log(nf / MAX_EXACT) / math.log(MAX_DISTANCE / MAX_EXACT)
                         * (N_BUCKETS - MAX_EXACT)).astype(np.int32)
    large = np.minimum(large, N_BUCKETS - 1)
    return np.where(n < MAX_EXACT, n, large).astype(np.int32)


def _bias_kernel(bucket_ref, rb_ref, o_ref):
    h = pl.program_id(0)
    bucket = bucket_ref[...]
    acc = jnp.full(bucket.shape, NEG_INF, F32)
    for b in range(N_BUCKETS):
        acc = jnp.where(bucket == b, rb_ref[b, h], acc)
    o_ref[...] = acc


def _bias_table(bucket_np, rel_bias):
    r, c = bucket_np.shape
    return pl.pallas_call(
        _bias_kernel,
        grid=(N_HEADS,),
        in_specs=[pl.BlockSpec((r, c), lambda h: (0, 0)),
                  pl.BlockSpec(memory_space=pltpu.SMEM)],
        out_specs=pl.BlockSpec((None, r, c), lambda h: (h, 0, 0)),
        out_shape=jax.ShapeDtypeStruct((N_HEADS, r, c), F32),
        compiler_params=_cparams(("arbitrary",)),
        name="bias",
    )(jnp.asarray(bucket_np), rel_bias)


def _proj_kernel(x_ref, sc_ref, sh_ref, g_ref, w_ref, o_ref, h_scr):
    @pl.when(pl.program_id(1) == 0)
    def _():
        h = _rms(x_ref[...], g_ref[...]) * (1.0 + sc_ref[...]) + sh_ref[...]
        h_scr[...] = h.astype(BF16)

    o_ref[...] = _dot(h_scr[...], w_ref[...])


def _mod_spec(per_row, tm, rows_per_seq, col, nargs):
    if per_row:
        if nargs == 1:
            return pl.BlockSpec((tm, D_MODEL), lambda i: (i, col))
        return pl.BlockSpec((tm, D_MODEL), lambda i, j: (i, col))
    tiles_per_seq = rows_per_seq // tm
    if nargs == 1:
        return pl.BlockSpec((None, 1, D_MODEL), lambda i: (i // tiles_per_seq, 0, col))
    return pl.BlockSpec((None, 1, D_MODEL), lambda i, j: (i // tiles_per_seq, 0, col))


def _proj(x, mod, per_row, rows_per_seq, g_pre, w_in):
    rows = x.shape[0]
    tm = min(PROJ_TM, rows)
    return pl.pallas_call(
        _proj_kernel,
        grid=(rows // tm, IN_W // PROJ_TN),
        in_specs=[pl.BlockSpec((tm, D_MODEL), lambda i, j: (i, 0)),
                  _mod_spec(per_row, tm, rows_per_seq, 1, 2),
                  _mod_spec(per_row, tm, rows_per_seq, 0, 2),
                  pl.BlockSpec((1, D_MODEL), lambda i, j: (0, 0)),
                  pl.BlockSpec((D_MODEL, PROJ_TN), lambda i, j: (0, j))],
        out_specs=pl.BlockSpec((tm, PROJ_TN), lambda i, j: (i, j)),
        out_shape=jax.ShapeDtypeStruct((rows, IN_W), F32),
        scratch_shapes=[pltpu.VMEM((tm, D_MODEL), BF16)],
        compiler_params=_cparams(("arbitrary", "arbitrary")),
        name="proj",
    )(x, mod, mod, g_pre.reshape(1, D_MODEL), w_in)


def _attn_prompt_kernel(q_ref, kc_ref, kp_ref, vc_ref, vp_ref, bias_ref, sink_ref, o_ref):
    has_prev = pl.program_id(1) > 0
    q = q_ref[...]
    k = jnp.concatenate([kp_ref[...], kc_ref[...]], axis=0)
    v = jnp.concatenate([vp_ref[...], vc_ref[...]], axis=0)
    col = lax.broadcasted_iota(I32, (WINDOW, 2 * WINDOW), 1)
    key_ok = (col >= WINDOW) | has_prev
    outs = []
    for g in range(N_KV_HEADS):
        kg = k[:, g * HEAD_DIM:(g + 1) * HEAD_DIM].astype(BF16)
        vg = v[:, g * HEAD_DIM:(g + 1) * HEAD_DIM].astype(BF16)
        for hh in range(Q_GROUP):
            h = g * Q_GROUP + hh
            qh = q[:, h * HEAD_DIM:(h + 1) * HEAD_DIM].astype(BF16)
            s = _dot_nt(qh, kg) * (HEAD_DIM ** -0.5) + bias_ref[h]
            s = jnp.where(key_ok, s, NEG_INF)
            sink = sink_ref[h]
            m = jnp.maximum(jnp.max(s, axis=-1, keepdims=True), sink)
            p = jnp.exp(s - m)
            den = jnp.sum(p, axis=-1, keepdims=True) + jnp.exp(sink - m)
            outs.append(_dot(p.astype(BF16), vg) / den)
    o_ref[...] = jnp.concatenate(outs, axis=-1)


def _attn_prompt(proj, batch, seq, bias, sinks):
    nb = seq // WINDOW
    kcol = Q_W // KV_W
    vcol = kcol + 1
    cur = lambda c: (lambda b, n: (b * nb + n, c))
    prev = lambda c: (lambda b, n: (b * nb + jnp.maximum(n - 1, 0), c))
    return pl.pallas_call(
        _attn_prompt_kernel,
        grid=(batch, nb),
        in_specs=[pl.BlockSpec((WINDOW, Q_W), cur(0)),
                  pl.BlockSpec((WINDOW, KV_W), cur(kcol)),
                  pl.BlockSpec((WINDOW, KV_W), prev(kcol)),
                  pl.BlockSpec((WINDOW, KV_W), cur(vcol)),
                  pl.BlockSpec((WINDOW, KV_W), prev(vcol)),
                  pl.BlockSpec((N_HEADS, WINDOW, 2 * WINDOW), lambda b, n: (0, 0, 0)),
                  pl.BlockSpec(memory_space=pltpu.SMEM)],
        out_specs=pl.BlockSpec((WINDOW, Q_W), lambda b, n: (b * nb + n, 0)),
        out_shape=jax.ShapeDtypeStruct((batch * seq, Q_W), F32),
        compiler_params=_cparams(("arbitrary", "arbitrary")),
        name="attn_prompt",
    )(proj, proj, proj, proj, proj, bias, sinks)


def _attn_sample_kernel(q_ref, kn_ref, vn_ref, ck_ref, cv_ref, bias_ref, sink_ref,
                        o_ref, nk_ref, nv_ref):
    tb = q_ref.shape[0]
    row = lax.broadcasted_iota(I32, (tb, WINDOW, KV_W), 1)
    last = row == WINDOW - 1
    nk = jnp.where(last, kn_ref[...], pltpu.roll(ck_ref[...], WINDOW - 1, 1))
    nv = jnp.where(last, vn_ref[...], pltpu.roll(cv_ref[...], WINDOW - 1, 1))
    nk_ref[...] = nk
    nv_ref[...] = nv
    lane_grp = lax.broadcasted_iota(I32, (N_HEADS, KV_W), 1) // HEAD_DIM
    head_grp = lax.broadcasted_iota(I32, (N_HEADS, KV_W), 0) // Q_GROUP
    gmask = (lane_grp == head_grp).astype(F32)
    q = q_ref[...]
    qrow = jnp.concatenate([q] * N_KV_HEADS, axis=-1) * gmask
    s = jnp.einsum('bhc,brc->bhr', qrow.astype(BF16), nk.astype(BF16),
                   preferred_element_type=F32) * (HEAD_DIM ** -0.5)
    s = s + bias_ref[...]
    sink = sink_ref[...]
    m = jnp.maximum(jnp.max(s, axis=-1, keepdims=True), sink)
    p = jnp.exp(s - m)
    den = jnp.sum(p, axis=-1, keepdims=True) + jnp.exp(sink - m)
    o = jnp.einsum('bhr,brc->bhc', p.astype(BF16), nv.astype(BF16),
                   preferred_element_type=F32) * gmask
    o64 = o[..., 0:HEAD_DIM]
    for g in range(1, N_KV_HEADS):
        o64 = o64 + o[..., g * HEAD_DIM:(g + 1) * HEAD_DIM]
    o_ref[...] = o64 / den


def _attn_sample(q3, kn, vn, cache_k, cache_v, bias, sinks):
    nseq = q3.shape[0]
    tb = 16
    seq3 = lambda w: pl.BlockSpec((tb, WINDOW, w), lambda i: (i, 0, 0))
    return pl.pallas_call(
        _attn_sample_kernel,
        grid=(nseq // tb,),
        in_specs=[pl.BlockSpec((tb, N_HEADS, HEAD_DIM), lambda i: (i, 0, 0)),
                  pl.BlockSpec((tb, 1, KV_W), lambda i: (i, 0, 0)),
                  pl.BlockSpec((tb, 1, KV_W), lambda i: (i, 0, 0)),
                  seq3(KV_W), seq3(KV_W),
                  pl.BlockSpec((N_HEADS, WINDOW), lambda i: (0, 0)),
                  pl.BlockSpec((N_HEADS, 1), lambda i: (0, 0))],
        out_specs=[pl.BlockSpec((tb, N_HEADS, HEAD_DIM), lambda i: (i, 0, 0)),
                   seq3(KV_W), seq3(KV_W)],
        out_shape=[jax.ShapeDtypeStruct((nseq, N_HEADS, HEAD_DIM), F32),
                   jax.ShapeDtypeStruct((nseq, WINDOW, KV_W), F32),
                   jax.ShapeDtypeStruct((nseq, WINDOW, KV_W), F32)],
        compiler_params=_cparams(("arbitrary",)),
        name="attn_sample",
    )(q3, kn, vn, cache_k, cache_v, bias, sinks.reshape(N_HEADS, 1))


def _ssm_disc_kernel(are_ref, aim_ref, ldt_ref, bre_ref, bim_ref,
                     lbr_ref, lbi_ref, bbr_ref, bbi_ref):
    a_re = are_ref[...]
    a_im = aim_ref[...]
    dt = jnp.exp(ldt_ref[...])
    lam_re = a_re * dt
    lam_im = a_im * dt
    mag = jnp.exp(lam_re)
    lb_re = mag * jnp.cos(lam_im)
    lb_im = mag * jnp.sin(lam_im)
    den = a_re * a_re + a_im * a_im
    nr = lb_re - 1.0
    ni = lb_im
    coef_re = (nr * a_re + ni * a_im) / den
    coef_im = (ni * a_re - nr * a_im) / den
    b_re = bre_ref[...]
    b_im = bim_ref[...]
    lbr_ref[...] = lb_re
    lbi_ref[...] = lb_im
    bbr_ref[...] = coef_re * b_re - coef_im * b_im
    bbi_ref[...] = coef_re * b_im + coef_im * b_re


def _ssm_disc(a_re, a_im, log_dt, b_re, b_im):
    g, p, j = N_SSM_GROUPS, SSM_STATE, SSM_GROUP
    vec = jax.ShapeDtypeStruct((g, 1, p), F32)
    mat = jax.ShapeDtypeStruct((g, j, p), F32)
    return pl.pallas_call(
        _ssm_disc_kernel,
        out_shape=[vec, vec, mat, mat],
        name="ssm_disc",
    )(a_re.reshape(g, 1, p), a_im.reshape(g, 1, p), log_dt.reshape(g, 1, 1),
      b_re.transpose(0, 2, 1), b_im.transpose(0, 2, 1))


def _block_diag_tiles(x):
    a, b = x.shape[1], x.shape[2]
    eye = jnp.eye(SUBLANE, dtype=x.dtype)
    y = jnp.einsum('kgab,gh->kgahb', x.reshape(SSM_TILES, SUBLANE, a, b), eye)
    return y.reshape(SSM_TILES, SUBLANE * a, SUBLANE * b)


def _gelu_tanh(x):
    return 0.5 * x * (1.0 + jnp.tanh(math.sqrt(2.0 / math.pi) * (x + 0.044715 * (x * x * x))))


def _ssm_kernel(*refs, nseq, tc, seq_major):
    if seq_major:
        ua_ref, ub_ref, perm_ref, perm_t_ref = refs[:4]
        refs = refs[4:]
        u = jnp.concatenate([ua_ref[...].reshape(nseq * tc, D_SSM // 2),
                             ub_ref[...].reshape(nseq * tc, D_SSM // 2)], axis=1)
        ub = _dot(perm_ref[...], u.astype(BF16)).astype(BF16)
    else:
        u = refs[0][...]
        refs = refs[1:]
        ub = u.astype(BF16)
    (h0r_ref, h0i_ref, lbr_ref, lbi_ref, bb_ref, cc_ref, d_ref, wglu_ref, bglu_ref,
     y_ref, hTr_ref, hTi_ref, hre, him, st_r, st_i) = refs
    paired = nseq == 4

    @pl.when(pl.program_id(0) == 0)
    def _():
        if paired:
            st_r[...] = jnp.concatenate([h0r_ref[...], h0r_ref[...]], axis=0)
            st_i[...] = jnp.concatenate([h0i_ref[...], h0i_ref[...]], axis=0)
        else:
            st_r[...] = h0r_ref[...]
            st_i[...] = h0i_ref[...]

    half = SSM_W // SSM_TILES
    for k in range(SSM_TILES):
        bu = _dot(ub[:, k * LANE:(k + 1) * LANE], bb_ref[k])
        hre[:, k * half:(k + 1) * half] = bu[:, :half]
        him[:, k * half:(k + 1) * half] = bu[:, half:]

    for blk in range(SSM_W // SSM_LB):
        sl = slice(blk * SSM_LB, (blk + 1) * SSM_LB)
        ar = lbr_ref[:, sl]
        ai = lbi_ref[:, sl]
        if paired:
            lower = lax.broadcasted_iota(I32, (SUBLANE, SSM_LB), 0) < nseq

            def body(m, carry):
                sr, si = carry
                r0 = pl.multiple_of(m * SUBLANE, SUBLANE)
                br = hre[pl.ds(r0, SUBLANE), sl]
                bi = him[pl.ds(r0, SUBLANE), sl]
                xr = pltpu.roll(sr, nseq, 0)
                xi = pltpu.roll(si, nseq, 0)
                h1r = ar * xr - ai * xi + br
                h1i = ar * xi + ai * xr + bi
                yr = pltpu.roll(h1r, nseq, 0)
                yi = pltpu.roll(h1i, nseq, 0)
                h2r = ar * yr - ai * yi + br
                h2i = ar * yi + ai * yr + bi
                hre[pl.ds(r0, SUBLANE), sl] = jnp.where(lower, h1r, h2r)
                him[pl.ds(r0, SUBLANE), sl] = jnp.where(lower, h1i, h2i)
                return h2r, h2i

            sr, si = lax.fori_loop(0, tc * nseq // SUBLANE, body, (st_r[:, sl], st_i[:, sl]))
        else:
            def body(t, carry):
                sr, si = carry
                r0 = pl.multiple_of(t * nseq, SUBLANE)
                br = hre[pl.ds(r0, nseq), sl]
                bi = him[pl.ds(r0, nseq), sl]
                nr = ar * sr - ai * si + br
                ni = ar * si + ai * sr + bi
                hre[pl.ds(r0, nseq), sl] = nr
                him[pl.ds(r0, nseq), sl] = ni
                return nr, ni

            sr, si = lax.fori_loop(0, tc, body, (st_r[:, sl], st_i[:, sl]))
        st_r[:, sl] = sr
        st_i[:, sl] = si

    ys = []
    for k in range(SSM_TILES):
        hr = hre[:, k * half:(k + 1) * half].astype(BF16)
        hi = him[:, k * half:(k + 1) * half].astype(BF16)
        ys.append(_dot(hr, cc_ref[k, :half, :]) + _dot(hi, cc_ref[k, half:, :]))
    yc = jnp.concatenate(ys, axis=-1)
    if seq_major:
        yc_hi, yc_lo = _split_bf16(yc)
        yc = _dot(perm_t_ref[...], yc_hi) + _dot(perm_t_ref[...], yc_lo)
    y = _gelu_tanh(yc + d_ref[...] * u)
    z = _dot(y.astype(BF16), wglu_ref[...]) + bglu_ref[...]
    y_ref[...] = (y * _sigmoid(z)).reshape(y_ref.shape)

    if paired:
        hTr_ref[...] = st_r[nseq:, :]
        hTi_ref[...] = st_i[nseq:, :]
    else:
        hTr_ref[...] = st_r[...]
        hTi_ref[...] = st_i[...]


def _ssm(u_src, h0_re, h0_im, nseq, tc, lbr, lbi, bb, cc, d, w_glu, b_glu, seq_major=False):
    r = nseq * tc
    st_rows = max(nseq, SUBLANE)
    const2 = lambda shape: pl.BlockSpec(shape, lambda c: (0, 0))
    const3 = lambda shape: pl.BlockSpec(shape, lambda c: (0, 0, 0))
    if seq_major:
        steps = u_src.shape[1] // tc
        half_w = D_SSM // 2
        col0 = (Q_W + 2 * KV_W) // half_w
        t_idx, s_idx = np.divmod(np.arange(r), nseq)
        perm = np.zeros((r, r), np.float32)
        perm[np.arange(r), s_idx * tc + t_idx] = 1.0
        u_specs = [pl.BlockSpec((nseq, tc, half_w), lambda c: (0, c, col0)),
                   pl.BlockSpec((nseq, tc, half_w), lambda c: (0, c, col0 + 1)),
                   const2((r, r)), const2((r, r))]
        u_args = [u_src, u_src, jnp.asarray(perm, BF16), jnp.asarray(perm.T, BF16)]
        y_spec = pl.BlockSpec((nseq, tc, D_SSM), lambda c: (0, c, 0))
        y_shape = jax.ShapeDtypeStruct((nseq, steps * tc, D_SSM), F32)
    else:
        steps = u_src.shape[0] // r
        u_specs = [pl.BlockSpec((r, D_SSM), lambda c: (c, 0))]
        u_args = [u_src]
        y_spec = pl.BlockSpec((r, D_SSM), lambda c: (c, 0))
        y_shape = jax.ShapeDtypeStruct((steps * r, D_SSM), F32)
    return pl.pallas_call(
        functools.partial(_ssm_kernel, nseq=nseq, tc=tc, seq_major=seq_major),
        grid=(steps,),
        in_specs=u_specs + [
            const2((nseq, SSM_W)), const2((nseq, SSM_W)),
            const2((1, SSM_W)), const2((1, SSM_W)),
            const3((SSM_TILES, LANE, 2 * SSM_W // SSM_TILES)),
            const3((SSM_TILES, 2 * SSM_W // SSM_TILES, LANE)),
            const2((1, D_SSM)), const2((D_SSM, D_SSM)), const2((1, D_SSM))],
        out_specs=[y_spec, const2((nseq, SSM_W)), const2((nseq, SSM_W))],
        out_shape=[y_shape,
                   jax.ShapeDtypeStruct((nseq, SSM_W), F32),
                   jax.ShapeDtypeStruct((nseq, SSM_W), F32)],
        scratch_shapes=[pltpu.VMEM((r, SSM_W), F32), pltpu.VMEM((r, SSM_W), F32),
                        pltpu.VMEM((st_rows, SSM_W), F32), pltpu.VMEM((st_rows, SSM_W), F32)],
        compiler_params=_cparams(("arbitrary",)),
        name="ssm",
    )(*u_args, h0_re, h0_im, lbr, lbi, bb, cc, d.reshape(1, D_SSM), w_glu, b_glu.reshape(1, D_SSM))


def _split_bf16(x):
    hi = x.astype(BF16)
    lo = (x - hi.astype(F32)).astype(BF16)
    return hi, lo


N_MERGE_IN = 15


def _merge_kernel(*refs):
    (o_ref, y_ref, ga_ref, gs_ref, wa_ref, ws_ref, wo_ref, x_ref, gpm_ref, g1_ref, sc2_ref, sh2_ref,
     gpf_ref, wrt_ref, br_ref) = refs[:N_MERGE_IN]
    x1_ref, h2_ref, lg_ref, mix = refs[-4:]
    j = pl.program_id(1)
    a = _dot(o_ref[...].astype(BF16), wa_ref[...])
    s = _dot(y_ref[...].astype(BF16), ws_ref[...])
    merged = _sigmoid(ga_ref[...]) * a + _sigmoid(gs_ref[...]) * s
    contrib = _dot(merged.astype(BF16), wo_ref[...])

    @pl.when(j == 0)
    def _():
        mix[...] = contrib

    @pl.when(j > 0)
    def _():
        mix[...] += contrib

    @pl.when(j == pl.num_programs(1) - 1)
    def _():
        x1 = x_ref[...] + g1_ref[...] * _rms(mix[...], gpm_ref[...])
        x1_ref[...] = x1
        h2 = _rms(x1, gpf_ref[...]) * (1.0 + sc2_ref[...]) + sh2_ref[...]
        h2_ref[...] = h2
        h_hi, h_lo = _split_bf16(h2)
        w_hi, w_lo = _split_bf16(wrt_ref[...])
        lg_ref[...] = (_dot_nt(w_hi, h_hi) + _dot_nt(w_hi, h_lo) + _dot_nt(w_lo, h_hi)) + br_ref[...]


def _merge(o_attn, y_ssm, proj, x, mod, per_row, rows_per_seq, n_total, row0, shared, w_br_attn, w_br_ssm,
           w_out, g_post_mix, g_pre_ffn, w_router_t, b_router):
    rows = x.shape[0]
    tm = min(MERGE_TM, rows)
    blk0 = row0 // tm
    nk = D_MODEL // MERGE_TK
    ga0 = (Q_W + 2 * KV_W + D_SSM) // MERGE_TK
    gs0 = ga0 + nk
    row2 = lambda w: pl.BlockSpec((tm, w), lambda i, j: (i, 0))
    vec = pl.BlockSpec((1, D_MODEL), lambda i, j: (0, 0))
    in_specs = [row2(Q_W), row2(D_SSM),
                pl.BlockSpec((tm, MERGE_TK), lambda i, j: (i, ga0 + j)),
                pl.BlockSpec((tm, MERGE_TK), lambda i, j: (i, gs0 + j)),
                pl.BlockSpec((Q_W, MERGE_TK), lambda i, j: (0, j)),
                pl.BlockSpec((D_SSM, MERGE_TK), lambda i, j: (0, j)),
                pl.BlockSpec((MERGE_TK, D_MODEL), lambda i, j: (j, 0)),
                row2(D_MODEL), vec,
                _mod_spec(per_row, tm, rows_per_seq, 2, 2),
                _mod_spec(per_row, tm, rows_per_seq, 4, 2),
                _mod_spec(per_row, tm, rows_per_seq, 3, 2),
                vec,
                pl.BlockSpec((N_EXPERTS, D_MODEL), lambda i, j: (0, 0)),
                pl.BlockSpec((N_EXPERTS, 1), lambda i, j: (0, 0))]
    args = [o_attn, y_ssm, proj, proj, w_br_attn, w_br_ssm, w_out, x, g_post_mix.reshape(1, D_MODEL),
            mod, mod, mod, g_pre_ffn.reshape(1, D_MODEL), w_router_t, b_router.reshape(N_EXPERTS, 1)]
    assert len(args) == N_MERGE_IN
    aliases = {}
    if shared is not None:
        aliases = {len(args): 1, len(args) + 1: 2}
        in_specs += [pl.BlockSpec(memory_space=pl.ANY), pl.BlockSpec(memory_space=pl.ANY)]
        args += list(shared)
    return pl.pallas_call(
        _merge_kernel,
        grid=(rows // tm, nk),
        in_specs=in_specs,
        out_specs=[row2(D_MODEL),
                   pl.BlockSpec((tm, D_MODEL), lambda i, j: (blk0 + i, 0)),
                   pl.BlockSpec((N_EXPERTS, tm), lambda i, j: (0, blk0 + i))],
        out_shape=[jax.ShapeDtypeStruct((rows, D_MODEL), F32),
                   jax.ShapeDtypeStruct((n_total, D_MODEL), F32),
                   jax.ShapeDtypeStruct((N_EXPERTS, n_total), F32)],
        scratch_shapes=[pltpu.VMEM((tm, D_MODEL), F32)],
        input_output_aliases=aliases,
        compiler_params=_cparams(("arbitrary", "arbitrary")),
        name="merge",
    )(*args)


def _bucket_tables():
    ql = np.arange(WINDOW)[:, None]
    kl = np.arange(2 * WINDOW)[None, :]
    dist = ql + WINDOW - kl
    prompt = np.where((dist >= 0) & (dist < WINDOW), _t5_bucket_np(dist), -1).astype(np.int32)
    d_s = (WINDOW - 1 - np.arange(WINDOW))[None, :]
    sample = np.broadcast_to(_t5_bucket_np(d_s), (SUBLANE, WINDOW)).astype(np.int32)
    return prompt, sample


def _front(p):
    l = 0
    batch, seq, _ = p['x_prompt'].shape
    nseq = p['x_sample'].shape[0]
    xp = p['x_prompt'].reshape(batch * seq, D_MODEL)
    xs = p['x_sample'].reshape(nseq, D_MODEL)

    c_all = jnp.concatenate([p['c_prompt'], p['c_sample'],
                             jnp.zeros((SUBLANE - (batch + nseq) % SUBLANE, D_MODEL), F32)], axis=0)
    mod = _ada(c_all, p['w_ada'][l], p['b_ada'][l])
    mod_p = mod[:batch].reshape(batch, 1, 6 * D_MODEL)
    mod_s = mod[batch:batch + nseq]

    bucket_p, bucket_s = _bucket_tables()
    bias_p = _bias_table(bucket_p, p['rel_bias'])
    bias_s = _bias_table(bucket_s, p['rel_bias'])[:, 0, :]
    sinks = p['attn_sinks'][l]

    w_in_t = p['w_in'][l].astype(BF16)
    proj_p = _proj(xp, mod_p, False, seq, p['g_pre_mix'][l], w_in_t)
    proj_s = _proj(xs, mod_s, True, 1, p['g_pre_mix'][l], w_in_t)

    o_p = _attn_prompt(proj_p, batch, seq, bias_p, sinks)
    kv_p = proj_p.reshape(batch, seq, IN_W)[:, seq - WINDOW:, Q_W:Q_W + 2 * KV_W]
    new_k_p = kv_p[..., :KV_W].reshape(1, batch, WINDOW, N_KV_HEADS, HEAD_DIM)
    new_v_p = kv_p[..., KV_W:].reshape(1, batch, WINDOW, N_KV_HEADS, HEAD_DIM)
    o_s3, new_k_s, new_v_s = _attn_sample(
        proj_s[:, :Q_W].reshape(nseq, N_HEADS, HEAD_DIM),
        proj_s[:, Q_W:Q_W + KV_W].reshape(nseq, 1, KV_W),
        proj_s[:, Q_W + KV_W:Q_W + 2 * KV_W].reshape(nseq, 1, KV_W),
        p['cache_win_k'][l].reshape(nseq, WINDOW, KV_W),
        p['cache_win_v'][l].reshape(nseq, WINDOW, KV_W), bias_s, sinks)
    o_s = o_s3.reshape(nseq, Q_W)

    lbr, lbi, bbr, bbi = _ssm_disc(p['ssm_a_re'][l], p['ssm_a_im'][l], p['ssm_log_dt'][l],
                                   p['ssm_b_re'][l], p['ssm_b_im'][l])
    lbr = lbr.reshape(1, SSM_W)
    lbi = lbi.reshape(1, SSM_W)
    bb = jnp.concatenate([_block_diag_tiles(bbr), _block_diag_tiles(bbi)], axis=-1).astype(BF16)
    c_re_t = p['ssm_c_re'][l].transpose(0, 2, 1)
    c_im_t = p['ssm_c_im'][l].transpose(0, 2, 1)
    cc = jnp.concatenate([_block_diag_tiles(c_re_t), -_block_diag_tiles(c_im_t)], axis=1).astype(BF16)
    u0 = Q_W + 2 * KV_W
    zeros = jnp.zeros((batch, SSM_W), F32)
    w_glu = p['w_glu'][l].astype(BF16)
    y_p3, hr_p, hi_p = _ssm(proj_p.reshape(batch, seq, IN_W), zeros, zeros, batch, SSM_TC, lbr, lbi, bb, cc,
                            p['ssm_d'][l], w_glu, p['b_glu'][l], seq_major=True)
    y_p = y_p3.reshape(batch * seq, D_SSM)
    y_s, hr_s, hi_s = _ssm(proj_s[:, u0:u0 + D_SSM], p['state_ssm_re'][l].reshape(nseq, SSM_W),
                           p['state_ssm_im'][l].reshape(nseq, SSM_W), nseq, 1, lbr, lbi, bb, cc,
                           p['ssm_d'][l], w_glu, p['b_glu'][l])

    wa_t = p['w_br_attn'][l].astype(BF16)
    ws_t = p['w_br_ssm'][l].astype(BF16)
    wo = p['w_out'][l].astype(BF16)
    n_total = batch * seq + nseq
    merge = functools.partial(_merge, w_br_attn=wa_t, w_br_ssm=ws_t, w_out=wo, g_post_mix=p['g_post_mix'][l],
                              g_pre_ffn=p['g_pre_ffn'][l], w_router_t=p['w_router'][l].T,
                              b_router=p['b_router'][l])
    x1_p, h2_buf, lg_buf = merge(o_p, y_p, proj_p, xp, mod_p, False, seq, n_total, 0, None)
    x1_s, h2_all, lg_all = merge(o_s, y_s, proj_s, xs, mod_s, True, 1, n_total, batch * seq, (h2_buf, lg_buf))

    st = lambda h, n: h.reshape(1, n, N_SSM_GROUPS, SSM_STATE)
    return dict(
        mod_p=mod_p, mod_s=mod_s, h2_all=h2_all, lg_all=lg_all,
        p=dict(proj=proj_p, o_attn=o_p, new_k=new_k_p, new_v=new_v_p, y_ssm=y_p, h_re=st(hr_p, batch),
               h_im=st(hi_p, batch), x1=x1_p),
        s=dict(proj=proj_s, o_attn=o_s, new_k=new_k_s.reshape(1, nseq, WINDOW, N_KV_HEADS, HEAD_DIM),
               new_v=new_v_s.reshape(1, nseq, WINDOW, N_KV_HEADS, HEAD_DIM), y_ssm=y_s,
               h_re=st(hr_s, nseq), h_im=st(hi_s, nseq), x1=x1_s))


def _count_steps(c, step, n_max):
    out = jnp.zeros_like(c)
    for q in range(-(-n_max // step)):
        out = out + jnp.where(c > float(q * step), 1.0, 0.0)
    return out


def _route_kernel(lg_ref, pos_ref, gate_ref, texp_ref, trows_ref, ntiles_ref, pstart_ref, plen_ref):
    lg = lg_ref[...]
    e, tn = lg.shape
    erow = lax.broadcasted_iota(I32, (e, tn), 0).astype(F32)
    work = lg
    vals, hits = [], []
    for _ in range(TOP_K):
        m = jnp.max(work, axis=0, keepdims=True)
        idx = jnp.min(jnp.where(work == m, erow, float(e)), axis=0, keepdims=True)
        hit = erow == idx
        vals.append(m)
        hits.append(hit)
        work = jnp.where(hit, -jnp.inf, work)
    ex = [jnp.exp(v - vals[0]) for v in vals]
    den = ex[0] + ex[1] + ex[2] + ex[3]
    gate_ref[...] = jnp.concatenate([x / den for x in ex], axis=0)

    chosen = jnp.zeros((e, tn), F32)
    for hit in hits:
        chosen = chosen + jnp.where(hit, 1.0, 0.0)
    chosen_b = chosen.astype(BF16)
    tri = (lax.broadcasted_iota(I32, (LANE, LANE), 0) <= lax.broadcasted_iota(I32, (LANE, LANE), 1))
    tri = jnp.where(tri, 1.0, 0.0).astype(BF16)
    carry = jnp.zeros((e, 1), F32)
    ranks = []
    for b in range(tn // LANE):
        blk = chosen[:, b * LANE:(b + 1) * LANE]
        inc = _dot(chosen_b[:, b * LANE:(b + 1) * LANE], tri) + carry
        ranks.append(inc - blk)
        carry = inc[:, LANE - 1:LANE]
    rank = jnp.concatenate(ranks, axis=1)
    cnt_col = carry
    cnt_row = _dot_nt(jnp.ones((SUBLANE, tn), BF16), chosen_b)[0:1, :]

    tiles_col = _count_steps(cnt_col, EXP_TM, tn)
    tiles_row = _count_steps(cnt_row, EXP_TM, tn)
    ee_r = lax.broadcasted_iota(I32, (e, e), 0)
    ee_c = lax.broadcasted_iota(I32, (e, e), 1)
    tstart_col = jnp.sum(jnp.where(ee_c < ee_r, tiles_row, 0.0), axis=1, keepdims=True)
    ntiles = jnp.sum(tiles_row, axis=1, keepdims=True)
    rstart_col = tstart_col * float(EXP_TM)
    pos = [jnp.sum(jnp.where(hit, rstart_col + rank, 0.0), axis=0, keepdims=True) for hit in hits]
    pos_ref[...] = jnp.concatenate(pos, axis=0).astype(I32)

    mm = lax.broadcasted_iota(I32, (e, LANE), 1).astype(F32)
    e_col = lax.broadcasted_iota(I32, (e, LANE), 0).astype(F32)
    own = (mm >= tstart_col) & (mm < tstart_col + tiles_col)
    texp = jnp.sum(jnp.where(own, e_col, 0.0), axis=0, keepdims=True)
    rows_here = jnp.minimum(float(EXP_TM), cnt_col - (mm - tstart_col) * float(EXP_TM))
    trows = jnp.sum(jnp.where(own, rows_here, 0.0), axis=0, keepdims=True)
    last_e = jnp.max(jnp.where(tiles_col > 0.0, e_col, 0.0), axis=0, keepdims=True)
    texp = jnp.where(mm[0:1, :] < ntiles, texp, last_e)
    texp_ref[...] = texp.astype(I32)
    trows_ref[...] = trows.astype(I32)
    ntiles_ref[...] = jnp.broadcast_to(ntiles, (1, LANE)).astype(I32)
    nsub_col = _count_steps(cnt_col, EXP_SUB, tn)
    pstart_ref[...] = jnp.broadcast_to(rstart_col + cnt_col, (e, LANE)).astype(I32)
    plen_ref[...] = jnp.broadcast_to(nsub_col * float(EXP_SUB) - cnt_col, (e, LANE)).astype(I32)


def _route(lg_t):
    e, tn = lg_t.shape
    i32 = lambda shape: jax.ShapeDtypeStruct(shape, I32)
    return pl.pallas_call(
        _route_kernel,
        out_shape=[i32((TOP_K, tn)), jax.ShapeDtypeStruct((TOP_K, tn), F32),
                   i32((1, LANE)), i32((1, LANE)), i32((1, LANE)), i32((e, LANE)), i32((e, LANE))],
        compiler_params=pltpu.CompilerParams(vmem_limit_bytes=VMEM_LIMIT),
        name="route",
    )(lg_t)


def _max_tiles(n_tok):
    return (n_tok * TOP_K) // EXP_TM + N_EXPERTS


def _dispatch_kernel(pos_ref, pstart_ref, plen_ref, h2_ref, zero_ref, xs_ref, sem):
    i = pl.program_id(0)

    def row_copy(src, s, d):
        return pltpu.make_async_copy(src.at[pl.ds(s, 1)], xs_ref.at[pl.ds(d, 1)], sem)

    def issue(t, c):
        for k in range(TOP_K):
            row_copy(h2_ref, t, pos_ref[k, t]).start(priority=k % 2)
        return c

    lax.fori_loop(0, TOK_BLK, issue, 0, unroll=4)
    for _ in range(TOP_K):
        pltpu.make_async_copy(h2_ref, xs_ref.at[pl.ds(0, TOK_BLK)], sem).wait()

    @pl.when(i == 0)
    def _():
        def per_expert(ex, c):
            n = plen_ref[ex]
            s = pstart_ref[ex]

            def zissue(r, cc):
                row_copy(zero_ref, 0, s + r).start()
                return cc

            def zdrain(r, cc):
                row_copy(zero_ref, 0, 0).wait()
                return cc

            lax.fori_loop(0, n, zissue, 0)
            lax.fori_loop(0, n, zdrain, 0)
            return c

        lax.fori_loop(0, N_EXPERTS, per_expert, 0)


def _dispatch(pos3, pstart, plen, h2_all):
    n_tok = h2_all.shape[0]
    n_rows = _max_tiles(n_tok) * EXP_TM
    smem = pl.BlockSpec(memory_space=pltpu.SMEM)
    hbm = pl.BlockSpec(memory_space=pl.ANY)
    return pl.pallas_call(
        _dispatch_kernel,
        grid=(n_tok // TOK_BLK,),
        in_specs=[pl.BlockSpec((None, TOP_K, TOK_BLK), lambda i: (i, 0, 0), memory_space=pltpu.SMEM),
                  smem, smem,
                  pl.BlockSpec((TOK_BLK, D_MODEL), lambda i: (i, 0)),
                  pl.BlockSpec((SUBLANE, D_MODEL), lambda i: (0, 0))],
        out_specs=hbm,
        out_shape=jax.ShapeDtypeStruct((n_rows, D_MODEL), F32),
        scratch_shapes=[pltpu.SemaphoreType.DMA(())],
        compiler_params=_cparams(("arbitrary",)),
        name="dispatch",
    )(pos3, pstart, plen, h2_all, jnp.zeros((SUBLANE, D_MODEL), F32))


def _expert_kernel(texp_ref, trows_ref, nt_ref, xs_hbm, wg_ref, wl_ref, bg_ref, bl_ref, wd_ref, bd_ref,
                   o_ref, x_ref, x_sem, act_scr, wg_scr, wl_scr, wd_scr):
    m = pl.program_id(0)
    s = pl.program_id(1)
    n_tiles = nt_ref[0]
    valid = m < n_tiles

    def x_copy(tile):
        return pltpu.make_async_copy(xs_hbm.at[pl.ds(pl.multiple_of(tile * EXP_TM, EXP_TM), EXP_TM)],
                                     x_ref, x_sem)

    @pl.when((m == 0) & (s == 0))
    def _():
        x_copy(0).start()

    @pl.when(valid & (s == 0))
    def _():
        x_copy(m).wait()

    @pl.when(valid & (s == EXP_NF) & (m + 1 < n_tiles))
    def _():
        x_copy(m + 1).start()

    rows = trows_ref[m]
    nbig = lax.shift_right_logical(rows, int(math.log2(EXP_BIG)))
    big_rows = nbig * EXP_BIG
    nsmall = lax.shift_right_logical(rows - big_rows + (EXP_SUB - 1), int(math.log2(EXP_SUB)))
    nsub_done = nbig * (EXP_BIG // EXP_SUB) + nsmall

    def over_rows(first, step):
        def big(r, c):
            step(pl.multiple_of(r * EXP_BIG, EXP_BIG), EXP_BIG)
            return c

        def small(r, c):
            step(pl.multiple_of(big_rows + r * EXP_SUB, EXP_SUB), EXP_SUB)
            return c

        @pl.when(nbig > 0)
        def _():
            first(EXP_BIG)
            lax.fori_loop(1, nbig, big, 0)
            lax.fori_loop(0, nsmall, small, 0)

        @pl.when(nbig == 0)
        def _():
            first(EXP_SUB)
            lax.fori_loop(1, nsmall, small, 0)

    @pl.when(valid & (s < EXP_NF))
    def _():
        def finish(r0, n, hg, hl):
            x_glu = jnp.minimum(hg, SWIGLU_LIMIT)
            x_lin = jnp.clip(hl, -SWIGLU_LIMIT, SWIGLU_LIMIT)
            act = x_glu * _sigmoid(SWIGLU_ALPHA * x_glu) * (x_lin + 1.0)
            act_scr[s, pl.ds(r0, n), :] = act.astype(BF16)

        def first(n):
            xb = x_ref[0:n, :].astype(BF16)
            hg = jnp.broadcast_to(bg_ref[...], (n, EXP_TF))
            hl = jnp.broadcast_to(bl_ref[...], (n, EXP_TF))
            for q in range(D_MODEL // EXP_KQ):
                ks = slice(q * EXP_KQ, (q + 1) * EXP_KQ)
                wgq = wg_ref[ks, :].astype(BF16)
                wlq = wl_ref[ks, :].astype(BF16)
                wg_scr[ks, :] = wgq
                wl_scr[ks, :] = wlq
                hg = hg + _dot(xb[:, ks], wgq)
                hl = hl + _dot(xb[:, ks], wlq)
            finish(0, n, hg, hl)

        def step(r0, n):
            xb = x_ref[pl.ds(r0, n), :].astype(BF16)
            finish(r0, n, _dot(xb, wg_scr[...]) + bg_ref[...], _dot(xb, wl_scr[...]) + bl_ref[...])

        over_rows(first, step)

    @pl.when(valid & (s >= EXP_NF))
    def _():
        def first(n):
            acc = jnp.broadcast_to(bd_ref[...], (n, EXP_TD))
            for f in range(EXP_NF):
                fs = slice(f * EXP_TF, (f + 1) * EXP_TF)
                wdq = wd_ref[fs, :].astype(BF16)
                wd_scr[fs, :] = wdq
                acc = acc + _dot(act_scr[f, 0:n, :], wdq)
            o_ref[0:n, :] = acc

        def step(r0, n):
            acc = jnp.broadcast_to(bd_ref[...], (n, EXP_TD))
            for f in range(EXP_NF):
                acc = acc + _dot(act_scr[f, pl.ds(r0, n), :], wd_scr[f * EXP_TF:(f + 1) * EXP_TF, :])
            o_ref[pl.ds(r0, n), :] = acc

        def zero(r, c):
            r0 = pl.multiple_of(r * EXP_SUB, EXP_SUB)
            o_ref[pl.ds(r0, EXP_SUB), :] = jnp.zeros((EXP_SUB, EXP_TD), F32)
            return c

        over_rows(first, step)
        lax.fori_loop(nsub_done, EXP_TM // EXP_SUB, zero, 0)


def _experts(texp, trows, ntiles, xs, w_gate_up, b_gate_up, w_down, b_down):
    n_tiles = xs.shape[0] // EXP_TM
    nsteps = EXP_NF + EXP_ND

    def tile(m, nt):
        return jnp.minimum(m, nt[0] - 1)

    def ea(m, s, te, nt):
        return te[jnp.where(s < EXP_NF, m, jnp.minimum(m + 1, nt[0] - 1))]

    def fa(m, s, nt):
        return jnp.where((m < nt[0]) & (s < EXP_NF), s, 0)

    def fb(m, s, nt):
        return jnp.where(m < nt[0], jnp.maximum(s - EXP_NF, 0), EXP_ND - 1)

    grid_spec = pltpu.PrefetchScalarGridSpec(
        num_scalar_prefetch=3,
        grid=(n_tiles, nsteps),
        in_specs=[
            pl.BlockSpec(memory_space=pl.ANY),
            pl.BlockSpec((None, D_MODEL, EXP_TF),
                         lambda m, s, te, tr, nt: (ea(m, s, te, nt), 0, fa(m, s, nt))),
            pl.BlockSpec((None, D_MODEL, EXP_TF),
                         lambda m, s, te, tr, nt: (ea(m, s, te, nt), 0, EXP_NF + fa(m, s, nt))),
            pl.BlockSpec((None, 1, EXP_TF), lambda m, s, te, tr, nt: (ea(m, s, te, nt), 0, fa(m, s, nt))),
            pl.BlockSpec((None, 1, EXP_TF),
                         lambda m, s, te, tr, nt: (ea(m, s, te, nt), 0, EXP_NF + fa(m, s, nt))),
            pl.BlockSpec((None, D_FF, EXP_TD), lambda m, s, te, tr, nt: (te[m], 0, fb(m, s, nt))),
            pl.BlockSpec((None, 1, EXP_TD), lambda m, s, te, tr, nt: (te[m], 0, fb(m, s, nt))),
        ],
        out_specs=pl.BlockSpec((EXP_TM, EXP_TD), lambda m, s, te, tr, nt: (tile(m, nt), fb(m, s, nt))),
        scratch_shapes=[pltpu.VMEM((EXP_TM, D_MODEL), F32), pltpu.SemaphoreType.DMA(()),
                        pltpu.VMEM((EXP_NF, EXP_TM, EXP_TF), BF16),
                        pltpu.VMEM((D_MODEL, EXP_TF), BF16), pltpu.VMEM((D_MODEL, EXP_TF), BF16),
                        pltpu.VMEM((D_FF, EXP_TD), BF16)],
    )
    return pl.pallas_call(
        _expert_kernel,
        grid_spec=grid_spec,
        out_shape=jax.ShapeDtypeStruct((xs.shape[0], D_MODEL), F32),
        compiler_params=_cparams(("arbitrary", "arbitrary")),
        name="experts",
    )(texp, trows, ntiles, xs, w_gate_up, w_gate_up,
      b_gate_up.reshape(N_EXPERTS, 1, 2 * D_FF), b_gate_up.reshape(N_EXPERTS, 1, 2 * D_FF),
      w_down, b_down.reshape(N_EXPERTS, 1, D_MODEL))


def _combine_kernel(pos_ref, pos_next_ref, gate_ref, ys_ref, x1_ref, g2_ref, gpf_ref, o_ref, buf, sem):
    i = pl.program_id(0)
    slot = lax.rem(i, 2)

    def gather(p_ref, sl):
        def issue(t, c):
            for k in range(TOP_K):
                pltpu.make_async_copy(ys_ref.at[pl.ds(p_ref[k, t], 1)], buf.at[sl, k, pl.ds(t, 1)],
                                      sem.at[sl]).start(priority=k % 2)
            return c

        lax.fori_loop(0, TOK_BLK, issue, 0, unroll=4)

    @pl.when(i == 0)
    def _():
        gather(pos_ref, 0)

    @pl.when(i + 1 < pl.num_programs(0))
    def _():
        gather(pos_next_ref, 1 - slot)

    for k in range(TOP_K):
        pltpu.make_async_copy(ys_ref.at[pl.ds(0, TOK_BLK)], buf.at[slot, k], sem.at[slot]).wait()
    g = gate_ref[...]
    f = g[:, 0:1] * buf[slot, 0]
    for k in range(1, TOP_K):
        f = f + g[:, k:k + 1] * buf[slot, k]
    o_ref[...] = x1_ref[...] + g2_ref[...] * _rms(f, gpf_ref[...])


def _combine(pos3, gates_t, ys, x1, mod, per_row, rows_per_seq, g_post_ffn, blk0):
    rows = x1.shape[0]
    nblk = rows // TOK_BLK
    return pl.pallas_call(
        _combine_kernel,
        grid=(nblk,),
        in_specs=[pl.BlockSpec((None, TOP_K, TOK_BLK), lambda i: (blk0 + i, 0, 0), memory_space=pltpu.SMEM),
                  pl.BlockSpec((None, TOP_K, TOK_BLK), lambda i: (blk0 + jnp.minimum(i + 1, nblk - 1), 0, 0),
                               memory_space=pltpu.SMEM),
                  pl.BlockSpec((TOK_BLK, TOP_K), lambda i: (blk0 + i, 0)),
                  pl.BlockSpec(memory_space=pl.ANY),
                  pl.BlockSpec((TOK_BLK, D_MODEL), lambda i: (i, 0)),
                  _mod_spec(per_row, TOK_BLK, rows_per_seq, 5, 1),
                  pl.BlockSpec((1, D_MODEL), lambda i: (0, 0))],
        out_specs=pl.BlockSpec((TOK_BLK, D_MODEL), lambda i: (i, 0)),
        out_shape=jax.ShapeDtypeStruct((rows, D_MODEL), F32),
        scratch_shapes=[pltpu.VMEM((2, TOP_K, TOK_BLK, D_MODEL), F32), pltpu.SemaphoreType.DMA((2,))],
        compiler_params=_cparams(("arbitrary",)),
        name="combine",
    )(pos3, pos3, gates_t, ys, x1, mod, g_post_ffn.reshape(1, D_MODEL))


def _moe(fr, p, batch, seq, nseq):
    l = 0
    h2_all, lg_all = fr['h2_all'], fr['lg_all']
    n_tok = h2_all.shape[0]
    pos, gates, texp, trows, ntiles, pstart, plen = _route(lg_all)
    pos3 = pos.reshape(TOP_K, n_tok // TOK_BLK, TOK_BLK).transpose(1, 0, 2)
    xs = _dispatch(pos3, pstart[:, 0], plen[:, 0], h2_all)
    ys = _experts(texp[0], trows[0], ntiles[0, :1], xs, p['w_gate_up'][l], p['b_gate_up'][l],
                  p['w_down'][l], p['b_down'][l])
    gates_t = gates.T
    y_p = _combine(pos3, gates_t, ys, fr['p']['x1'], fr['mod_p'], False, seq, p['g_post_ffn'][l], 0)
    y_s = _combine(pos3, gates_t, ys, fr['s']['x1'], fr['mod_s'], True, 1, p['g_post_ffn'][l],
                   batch * seq // TOK_BLK)
    return y_p, y_s


def kernel(x_prompt, x_sample, c_prompt, c_sample, cache_win_k, cache_win_v, state_ssm_re, state_ssm_im, w_ada, b_ada, g_pre_mix, g_post_mix, g_pre_ffn, g_post_ffn, w_in, attn_sinks, rel_bias, ssm_a_re, ssm_a_im, ssm_log_dt, ssm_b_re, ssm_b_im, ssm_c_re, ssm_c_im, ssm_d, w_glu, b_glu, w_br_attn, w_br_ssm, w_out, w_router, b_router, w_gate_up, b_gate_up, w_down, b_down):
    p = dict(locals())
    batch, seq, _ = x_prompt.shape
    nseq = x_sample.shape[0]
    fr = _front(p)
    y_p, y_s = _moe(fr, p, batch, seq, nseq)
    fp, fs = fr['p'], fr['s']
    return (y_p.reshape(batch, seq, D_MODEL), y_s.reshape(nseq, 1, D_MODEL),
            fp['new_k'], fp['new_v'], fp['h_re'], fp['h_im'],
            fs['new_k'], fs['new_v'], fs['h_re'], fs['h_im'])
```

```python
import functools
import math

import numpy as np
import jax
import jax.numpy as jnp
from jax import lax
from jax.experimental import pallas as pl
from jax.experimental.pallas import tpu as pltpu

F32 = jnp.float32
BF16 = jnp.bfloat16
I32 = jnp.int32

D_MODEL = 2048
N_HEADS = 16
N_KV_HEADS = 4
HEAD_DIM = 64
Q_GROUP = N_HEADS // N_KV_HEADS
WINDOW = 128
N_BUCKETS = 32
MAX_EXACT = N_BUCKETS // 2
MAX_DISTANCE = 128
D_SSM = 1024
SSM_GROUP = 16
N_SSM_GROUPS = 64
SSM_STATE = 64
N_EXPERTS = 32
TOP_K = 4
D_FF = 2048
SWIGLU_LIMIT = 7.0
SWIGLU_ALPHA = 1.702
NORM_EPS = 1e-6
NEG_INF = -1e30
Q_W = N_HEADS * HEAD_DIM
KV_W = N_KV_HEADS * HEAD_DIM
IN_W = Q_W + 2 * KV_W + D_SSM + 2 * D_MODEL
SSM_W = N_SSM_GROUPS * SSM_STATE

LANE = 128
SUBLANE = 8
VMEM_LIMIT = 56 * 1024 * 1024
EXP_VMEM_LIMIT = 60 * 1024 * 1024

PROJ_TM = 1024
PROJ_TN = 512
MERGE_TM = 512
MERGE_TK = 512
SSM_TC = 128
SSM_LB = 512
SSM_TILES = D_SSM // LANE
TOK_BLK = 128
EXP_TM = 1152
EXP_BIG = 512
EXP_SUB = 128
EXP_TF = 512
EXP_TD = 512
EXP_KQ = 512
EXP_NF = D_FF // EXP_TF
EXP_ND = D_MODEL // EXP_TD


def _cparams(sem):
    return pltpu.CompilerParams(dimension_semantics=sem, vmem_limit_bytes=VMEM_LIMIT)


def _sigmoid(x):
    return 1.0 / (1.0 + jnp.exp(-x))


def _rms(x, g):
    return x * lax.rsqrt(jnp.mean(x * x, axis=-1, keepdims=True) + NORM_EPS) * g


def _dot(a, b):
    return jnp.dot(a, b, preferred_element_type=F32)


def _dot_nt(a, b):
    return lax.dot_general(a, b, (((1,), (1,)), ((), ())), preferred_element_type=F32)


def _ada_kernel(c_ref, w_ref, b_ref, o_ref):
    c = c_ref[...]
    s = (c * _sigmoid(c)).astype(BF16)
    o_ref[...] = _dot(s, w_ref[...].astype(BF16)) + b_ref[...]


def _ada(c_all, w_ada, b_ada):
    rows = c_all.shape[0]
    tn = 1024
    n = w_ada.shape[1]
    return pl.pallas_call(
        _ada_kernel,
        grid=(n // tn,),
        in_specs=[pl.BlockSpec((rows, D_MODEL), lambda j: (0, 0)),
                  pl.BlockSpec((D_MODEL, tn), lambda j: (0, j)),
                  pl.BlockSpec((1, tn), lambda j: (0, j))],
        out_specs=pl.BlockSpec((rows, tn), lambda j: (0, j)),
        out_shape=jax.ShapeDtypeStruct((rows, n), F32),
        compiler_params=_cparams(("arbitrary",)),
        name="ada",
    )(c_all, w_ada, b_ada.reshape(1, n))


def _t5_bucket_np(dist):
    n = np.maximum(dist, 0)
    nf = np.maximum(n, 1).astype(np.float64)
    large = MAX_EXACT + (np.log(nf / MAX_EXACT) / math.log(MAX_DISTANCE / MAX_EXACT)
                         * (N_BUCKETS - MAX_EXACT)).astype(np.int32)
    large = np.minimum(large, N_BUCKETS - 1)
    return np.where(n < MAX_EXACT, n, large).astype(np.int32)


def _bias_kernel(bucket_ref, rb_ref, o_ref):
    h = pl.program_id(0)
    bucket = bucket_ref[...]
    acc = jnp.full(bucket.shape, NEG_INF, F32)
    for b in range(N_BUCKETS):
        acc = jnp.where(bucket == b, rb_ref[b, h], acc)
    o_ref[...] = acc


def _bias_table(bucket_np, rel_bias):
    r, c = bucket_np.shape
    return pl.pallas_call(
        _bias_kernel,
        grid=(N_HEADS,),
        in_specs=[pl.BlockSpec((r, c), lambda h: (0, 0)),
                  pl.BlockSpec(memory_space=pltpu.SMEM)],
        out_specs=pl.BlockSpec((None, r, c), lambda h: (h, 0, 0)),
        out_shape=jax.ShapeDtypeStruct((N_HEADS, r, c), F32),
        compiler_params=_cparams(("arbitrary",)),
        name="bias",
    )(jnp.asarray(bucket_np), rel_bias)


def _proj_kernel(x_ref, sc_ref, sh_ref, g_ref, w_ref, o_ref, h_scr):
    @pl.when(pl.program_id(1) == 0)
    def _():
        h = _rms(x_ref[...], g_ref[...]) * (1.0 + sc_ref[...]) + sh_ref[...]
        h_scr[...] = h.astype(BF16)

    o_ref[...] = _dot(h_scr[...], w_ref[...])


def _mod_spec(per_row, tm, rows_per_seq, col, nargs):
    if per_row:
        if nargs == 1:
            return pl.BlockSpec((tm, D_MODEL), lambda i: (i, col))
        return pl.BlockSpec((tm, D_MODEL), lambda i, j: (i, col))
    tiles_per_seq = rows_per_seq // tm
    if nargs == 1:
        return pl.BlockSpec((None, 1, D_MODEL), lambda i: (i // tiles_per_seq, 0, col))
    return pl.BlockSpec((None, 1, D_MODEL), lambda i, j: (i // tiles_per_seq, 0, col))


def _proj(x, mod, per_row, rows_per_seq, g_pre, w_in):
    rows = x.shape[0]
    tm = min(PROJ_TM, rows)
    return pl.pallas_call(
        _proj_kernel,
        grid=(rows // tm, IN_W // PROJ_TN),
        in_specs=[pl.BlockSpec((tm, D_MODEL), lambda i, j: (i, 0)),
                  _mod_spec(per_row, tm, rows_per_seq, 1, 2),
                  _mod_spec(per_row, tm, rows_per_seq, 0, 2),
                  pl.BlockSpec((1, D_MODEL), lambda i, j: (0, 0)),
                  pl.BlockSpec((D_MODEL, PROJ_TN), lambda i, j: (0, j))],
        out_specs=pl.BlockSpec((tm, PROJ_TN), lambda i, j: (i, j)),
        out_shape=jax.ShapeDtypeStruct((rows, IN_W), F32),
        scratch_shapes=[pltpu.VMEM((tm, D_MODEL), BF16)],
        compiler_params=_cparams(("arbitrary", "arbitrary")),
        name="proj",
    )(x, mod, mod, g_pre.reshape(1, D_MODEL), w_in)


def _attn_prompt_kernel(q_ref, kc_ref, kp_ref, vc_ref, vp_ref, bias_ref, sink_ref, o_ref):
    has_prev = pl.program_id(1) > 0
    q = q_ref[...]
    k = jnp.concatenate([kp_ref[...], kc_ref[...]], axis=0)
    v = jnp.concatenate([vp_ref[...], vc_ref[...]], axis=0)
    col = lax.broadcasted_iota(I32, (WINDOW, 2 * WINDOW), 1)
    key_ok = (col >= WINDOW) | has_prev
    outs = []
    for g in range(N_KV_HEADS):
        kg = k[:, g * HEAD_DIM:(g + 1) * HEAD_DIM].astype(BF16)
        vg = v[:, g * HEAD_DIM:(g + 1) * HEAD_DIM].astype(BF16)
        for hh in range(Q_GROUP):
            h = g * Q_GROUP + hh
            qh = q[:, h * HEAD_DIM:(h + 1) * HEAD_DIM].astype(BF16)
            s = _dot_nt(qh, kg) * (HEAD_DIM ** -0.5) + bias_ref[h]
            s = jnp.where(key_ok, s, NEG_INF)
            sink = sink_ref[h]
            m = jnp.maximum(jnp.max(s, axis=-1, keepdims=True), sink)
            p = jnp.exp(s - m)
            den = jnp.sum(p, axis=-1, keepdims=True) + jnp.exp(sink - m)
            outs.append(_dot(p.astype(BF16), vg) / den)
    o_ref[...] = jnp.concatenate(outs, axis=-1)


def _attn_prompt(proj, batch, seq, bias, sinks):
    nb = seq // WINDOW
    kcol = Q_W // KV_W
    vcol = kcol + 1
    cur = lambda c: (lambda b, n: (b * nb + n, c))
    prev = lambda c: (lambda b, n: (b * nb + jnp.maximum(n - 1, 0), c))
    return pl.pallas_call(
        _attn_prompt_kernel,
        grid=(batch, nb),
        in_specs=[pl.BlockSpec((WINDOW, Q_W), cur(0)),
                  pl.BlockSpec((WINDOW, KV_W), cur(kcol)),
                  pl.BlockSpec((WINDOW, KV_W), prev(kcol)),
                  pl.BlockSpec((WINDOW, KV_W), cur(vcol)),
                  pl.BlockSpec((WINDOW, KV_W), prev(vcol)),
                  pl.BlockSpec((N_HEADS, WINDOW, 2 * WINDOW), lambda b, n: (0, 0, 0)),
                  pl.BlockSpec(memory_space=pltpu.SMEM)],
        out_specs=pl.BlockSpec((WINDOW, Q_W), lambda b, n: (b * nb + n, 0)),
        out_shape=jax.ShapeDtypeStruct((batch * seq, Q_W), F32),
        compiler_params=_cparams(("arbitrary", "arbitrary")),
        name="attn_prompt",
    )(proj, proj, proj, proj, proj, bias, sinks)


def _attn_sample_kernel(q_ref, kn_ref, vn_ref, ck_ref, cv_ref, bias_ref, sink_ref,
                        o_ref, nk_ref, nv_ref):
    tb = q_ref.shape[0]
    row = lax.broadcasted_iota(I32, (tb, WINDOW, KV_W), 1)
    last = row == WINDOW - 1
    nk = jnp.where(last, kn_ref[...], pltpu.roll(ck_ref[...], WINDOW - 1, 1))
    nv = jnp.where(last, vn_ref[...], pltpu.roll(cv_ref[...], WINDOW - 1, 1))
    nk_ref[...] = nk
    nv_ref[...] = nv
    lane_grp = lax.broadcasted_iota(I32, (N_HEADS, KV_W), 1) // HEAD_DIM
    head_grp = lax.broadcasted_iota(I32, (N_HEADS, KV_W), 0) // Q_GROUP
    gmask = (lane_grp == head_grp).astype(F32)
    q = q_ref[...]
    qrow = jnp.concatenate([q] * N_KV_HEADS, axis=-1) * gmask
    s = jnp.einsum('bhc,brc->bhr', qrow.astype(BF16), nk.astype(BF16),
                   preferred_element_type=F32) * (HEAD_DIM ** -0.5)
    s = s + bias_ref[...]
    sink = sink_ref[...]
    m = jnp.maximum(jnp.max(s, axis=-1, keepdims=True), sink)
    p = jnp.exp(s - m)
    den = jnp.sum(p, axis=-1, keepdims=True) + jnp.exp(sink - m)
    o = jnp.einsum('bhr,brc->bhc', p.astype(BF16), nv.astype(BF16),
                   preferred_element_type=F32) * gmask
    o64 = o[..., 0:HEAD_DIM]
    for g in range(1, N_KV_HEADS):
        o64 = o64 + o[..., g * HEAD_DIM:(g + 1) * HEAD_DIM]
    o_ref[...] = o64 / den


def _attn_sample(q3, kn, vn, cache_k, cache_v, bias, sinks):
    nseq = q3.shape[0]
    tb = 16
    seq3 = lambda w: pl.BlockSpec((tb, WINDOW, w), lambda i: (i, 0, 0))
    return pl.pallas_call(
        _attn_sample_kernel,
        grid=(nseq // tb,),
        in_specs=[pl.BlockSpec((tb, N_HEADS, HEAD_DIM), lambda i: (i, 0, 0)),
                  pl.BlockSpec((tb, 1, KV_W), lambda i: (i, 0, 0)),
                  pl.BlockSpec((tb, 1, KV_W), lambda i: (i, 0, 0)),
                  seq3(KV_W), seq3(KV_W),
                  pl.BlockSpec((N_HEADS, WINDOW), lambda i: (0, 0)),
                  pl.BlockSpec((N_HEADS, 1), lambda i: (0, 0))],
        out_specs=[pl.BlockSpec((tb, N_HEADS, HEAD_DIM), lambda i: (i, 0, 0)),
                   seq3(KV_W), seq3(KV_W)],
        out_shape=[jax.ShapeDtypeStruct((nseq, N_HEADS, HEAD_DIM), F32),
                   jax.ShapeDtypeStruct((nseq, WINDOW, KV_W), F32),
                   jax.ShapeDtypeStruct((nseq, WINDOW, KV_W), F32)],
        compiler_params=_cparams(("arbitrary",)),
        name="attn_sample",
    )(q3, kn, vn, cache_k, cache_v, bias, sinks.reshape(N_HEADS, 1))


def _ssm_disc_kernel(are_ref, aim_ref, ldt_ref, bre_ref, bim_ref,
                     lbr_ref, lbi_ref, bbr_ref, bbi_ref):
    a_re = are_ref[...]
    a_im = aim_ref[...]
    dt = jnp.exp(ldt_ref[...])
    lam_re = a_re * dt
    lam_im = a_im * dt
    mag = jnp.exp(lam_re)
    lb_re = mag * jnp.cos(lam_im)
    lb_im = mag * jnp.sin(lam_im)
    den = a_re * a_re + a_im * a_im
    nr = lb_re - 1.0
    ni = lb_im
    coef_re = (nr * a_re + ni * a_im) / den
    coef_im = (ni * a_re - nr * a_im) / den
    b_re = bre_ref[...]
    b_im = bim_ref[...]
    lbr_ref[...] = lb_re
    lbi_ref[...] = lb_im
    bbr_ref[...] = coef_re * b_re - coef_im * b_im
    bbi_ref[...] = coef_re * b_im + coef_im * b_re


def _ssm_disc(a_re, a_im, log_dt, b_re, b_im):
    g, p, j = N_SSM_GROUPS, SSM_STATE, SSM_GROUP
    vec = jax.ShapeDtypeStruct((g, 1, p), F32)
    mat = jax.ShapeDtypeStruct((g, j, p), F32)
    return pl.pallas_call(
        _ssm_disc_kernel,
        out_shape=[vec, vec, mat, mat],
        name="ssm_disc",
    )(a_re.reshape(g, 1, p), a_im.reshape(g, 1, p), log_dt.reshape(g, 1, 1),
      b_re.transpose(0, 2, 1), b_im.transpose(0, 2, 1))


def _block_diag_tiles(x):
    a, b = x.shape[1], x.shape[2]
    eye = jnp.eye(SUBLANE, dtype=x.dtype)
    y = jnp.einsum('kgab,gh->kgahb', x.reshape(SSM_TILES, SUBLANE, a, b), eye)
    return y.reshape(SSM_TILES, SUBLANE * a, SUBLANE * b)


def _gelu_tanh(x):
    return 0.5 * x * (1.0 + jnp.tanh(math.sqrt(2.0 / math.pi) * (x + 0.044715 * (x * x * x))))


def _ssm_kernel(*refs, nseq, tc, seq_major):
    if seq_major:
        ua_ref, ub_ref, perm_ref, perm_t_ref = refs[:4]
        refs = refs[4:]
        u = jnp.concatenate([ua_ref[...].reshape(nseq * tc, D_SSM // 2),
                             ub_ref[...].reshape(nseq * tc, D_SSM // 2)], axis=1)
        ub = _dot(perm_ref[...], u.astype(BF16)).astype(BF16)
    else:
        u = refs[0][...]
        refs = refs[1:]
        ub = u.astype(BF16)
    (h0r_ref, h0i_ref, lbr_ref, lbi_ref, bb_ref, cc_ref, d_ref, wglu_ref, bglu_ref,
     y_ref, hTr_ref, hTi_ref, hre, him, st_r, st_i) = refs
    paired = nseq == 4

    @pl.when(pl.program_id(0) == 0)
    def _():
        if paired:
            st_r[...] = jnp.concatenate([h0r_ref[...], h0r_ref[...]], axis=0)
            st_i[...] = jnp.concatenate([h0i_ref[...], h0i_ref[...]], axis=0)
        else:
            st_r[...] = h0r_ref[...]
            st_i[...] = h0i_ref[...]

    half = SSM_W // SSM_TILES
    for k in range(SSM_TILES):
        bu = _dot(ub[:, k * LANE:(k + 1) * LANE], bb_ref[k])
        hre[:, k * half:(k + 1) * half] = bu[:, :half]
        him[:, k * half:(k + 1) * half] = bu[:, half:]

    for blk in range(SSM_W // SSM_LB):
        sl = slice(blk * SSM_LB, (blk + 1) * SSM_LB)
        ar = lbr_ref[:, sl]
        ai = lbi_ref[:, sl]
        if paired:
            lower = lax.broadcasted_iota(I32, (SUBLANE, SSM_LB), 0) < nseq

            def body(m, carry):
                sr, si = carry
                r0 = pl.multiple_of(m * SUBLANE, SUBLANE)
                br = hre[pl.ds(r0, SUBLANE), sl]
                bi = him[pl.ds(r0, SUBLANE), sl]
                xr = pltpu.roll(sr, nseq, 0)
                xi = pltpu.roll(si, nseq, 0)
                h1r = ar * xr - ai * xi + br
                h1i = ar * xi + ai * xr + bi
                yr = pltpu.roll(h1r, nseq, 0)
                yi = pltpu.roll(h1i, nseq, 0)
                h2r = ar * yr - ai * yi + br
                h2i = ar * yi + ai * yr + bi
                hre[pl.ds(r0, SUBLANE), sl] = jnp.where(lower, h1r, h2r)
                him[pl.ds(r0, SUBLANE), sl] = jnp.where(lower, h1i, h2i)
                return h2r, h2i

            sr, si = lax.fori_loop(0, tc * nseq // SUBLANE, body, (st_r[:, sl], st_i[:, sl]))
        else:
            def body(t, carry):
                sr, si = carry
                r0 = pl.multiple_of(t * nseq, SUBLANE)
                br = hre[pl.ds(r0, nseq), sl]
                bi = him[pl.ds(r0, nseq), sl]
                nr = ar * sr - ai * si + br
                ni = ar * si + ai * sr + bi
                hre[pl.ds(r0, nseq), sl] = nr
                him[pl.ds(r0, nseq), sl] = ni
                return nr, ni

            sr, si = lax.fori_loop(0, tc, body, (st_r[:, sl], st_i[:, sl]))
        st_r[:, sl] = sr
        st_i[:, sl] = si

    ys = []
    for k in range(SSM_TILES):
        hr = hre[:, k * half:(k + 1) * half].astype(BF16)
        hi = him[:, k * half:(k + 1) * half].astype(BF16)
        ys.append(_dot(hr, cc_ref[k, :half, :]) + _dot(hi, cc_ref[k, half:, :]))
    yc = jnp.concatenate(ys, axis=-1)
    if seq_major:
        yc_hi, yc_lo = _split_bf16(yc)
        yc = _dot(perm_t_ref[...], yc_hi) + _dot(perm_t_ref[...], yc_lo)
    y = _gelu_tanh(yc + d_ref[...] * u)
    z = _dot(y.astype(BF16), wglu_ref[...]) + bglu_ref[...]
    y_ref[...] = (y * _sigmoid(z)).reshape(y_ref.shape)

    if paired:
        hTr_ref[...] = st_r[nseq:, :]
        hTi_ref[...] = st_i[nseq:, :]
    else:
        hTr_ref[...] = st_r[...]
        hTi_ref[...] = st_i[...]


def _ssm(u_src, h0_re, h0_im, nseq, tc, lbr, lbi, bb, cc, d, w_glu, b_glu, seq_major=False):
    r = nseq * tc
    st_rows = max(nseq, SUBLANE)
    const2 = lambda shape: pl.BlockSpec(shape, lambda c: (0, 0))
    const3 = lambda shape: pl.BlockSpec(shape, lambda c: (0, 0, 0))
    if seq_major:
        steps = u_src.shape[1] // tc
        half_w = D_SSM // 2
        col0 = (Q_W + 2 * KV_W) // half_w
        t_idx, s_idx = np.divmod(np.arange(r), nseq)
        perm = np.zeros((r, r), np.float32)
        perm[np.arange(r), s_idx * tc + t_idx] = 1.0
        u_specs = [pl.BlockSpec((nseq, tc, half_w), lambda c: (0, c, col0)),
                   pl.BlockSpec((nseq, tc, half_w), lambda c: (0, c, col0 + 1)),
                   const2((r, r)), const2((r, r))]
        u_args = [u_src, u_src, jnp.asarray(perm, BF16), jnp.asarray(perm.T, BF16)]
        y_spec = pl.BlockSpec((nseq, tc, D_SSM), lambda c: (0, c, 0))
        y_shape = jax.ShapeDtypeStruct((nseq, steps * tc, D_SSM), F32)
    else:
        steps = u_src.shape[0] // r
        u_specs = [pl.BlockSpec((r, D_SSM), lambda c: (c, 0))]
        u_args = [u_src]
        y_spec = pl.BlockSpec((r, D_SSM), lambda c: (c, 0))
        y_shape = jax.ShapeDtypeStruct((steps * r, D_SSM), F32)
    return pl.pallas_call(
        functools.partial(_ssm_kernel, nseq=nseq, tc=tc, seq_major=seq_major),
        grid=(steps,),
        in_specs=u_specs + [
            const2((nseq, SSM_W)), const2((nseq, SSM_W)),
            const2((1, SSM_W)), const2((1, SSM_W)),
            const3((SSM_TILES, LANE, 2 * SSM_W // SSM_TILES)),
            const3((SSM_TILES, 2 * SSM_W // SSM_TILES, LANE)),
            const2((1, D_SSM)), const2((D_SSM, D_SSM)), const2((1, D_SSM))],
        out_specs=[y_spec, const2((nseq, SSM_W)), const2((nseq, SSM_W))],
        out_shape=[y_shape,
                   jax.ShapeDtypeStruct((nseq, SSM_W), F32),
                   jax.ShapeDtypeStruct((nseq, SSM_W), F32)],
        scratch_shapes=[pltpu.VMEM((r, SSM_W), F32), pltpu.VMEM((r, SSM_W), F32),
                        pltpu.VMEM((st_rows, SSM_W), F32), pltpu.VMEM((st_rows, SSM_W), F32)],
        compiler_params=_cparams(("arbitrary",)),
        name="ssm",
    )(*u_args, h0_re, h0_im, lbr, lbi, bb, cc, d.reshape(1, D_SSM), w_glu, b_glu.reshape(1, D_SSM))


def _split_bf16(x):
    hi = x.astype(BF16)
    lo = (x - hi.astype(F32)).astype(BF16)
    return hi, lo


N_MERGE_IN = 15


def _merge_kernel(*refs):
    (o_ref, y_ref, ga_ref, gs_ref, wa_ref, ws_ref, wo_ref, x_ref, gpm_ref, g1_ref, sc2_ref, sh2_ref,
     gpf_ref, wrt_ref, br_ref) = refs[:N_MERGE_IN]
    x1_ref, h2_ref, lg_ref, mix = refs[-4:]
    j = pl.program_id(1)
    a = _dot(o_ref[...].astype(BF16), wa_ref[...])
    s = _dot(y_ref[...].astype(BF16), ws_ref[...])
    merged = _sigmoid(ga_ref[...]) * a + _sigmoid(gs_ref[...]) * s
    contrib = _dot(merged.astype(BF16), wo_ref[...])

    @pl.when(j == 0)
    def _():
        mix[...] = contrib

    @pl.when(j > 0)
    def _():
        mix[...] += contrib

    @pl.when(j == pl.num_programs(1) - 1)
    def _():
        x1 = x_ref[...] + g1_ref[...] * _rms(mix[...], gpm_ref[...])
        x1_ref[...] = x1
        h2 = _rms(x1, gpf_ref[...]) * (1.0 + sc2_ref[...]) + sh2_ref[...]
        h2_ref[...] = h2
        h_hi, h_lo = _split_bf16(h2)
        w_hi, w_lo = _split_bf16(wrt_ref[...])
        lg_ref[...] = (_dot_nt(w_hi, h_hi) + _dot_nt(w_hi, h_lo) + _dot_nt(w_lo, h_hi)) + br_ref[...]


def _merge(o_attn, y_ssm, proj, x, mod, per_row, rows_per_seq, n_total, row0, shared, w_br_attn, w_br_ssm,
           w_out, g_post_mix, g_pre_ffn, w_router_t, b_router):
    rows = x.shape[0]
    tm = min(MERGE_TM, rows)
    blk0 = row0 // tm
    nk = D_MODEL // MERGE_TK
    ga0 = (Q_W + 2 * KV_W + D_SSM) // MERGE_TK
    gs0 = ga0 + nk
    row2 = lambda w: pl.BlockSpec((tm, w), lambda i, j: (i, 0))
    vec = pl.BlockSpec((1, D_MODEL), lambda i, j: (0, 0))
    in_specs = [row2(Q_W), row2(D_SSM),
                pl.BlockSpec((tm, MERGE_TK), lambda i, j: (i, ga0 + j)),
                pl.BlockSpec((tm, MERGE_TK), lambda i, j: (i, gs0 + j)),
                pl.BlockSpec((Q_W, MERGE_TK), lambda i, j: (0, j)),
                pl.BlockSpec((D_SSM, MERGE_TK), lambda i, j: (0, j)),
                pl.BlockSpec((MERGE_TK, D_MODEL), lambda i, j: (j, 0)),
                row2(D_MODEL), vec,
                _mod_spec(per_row, tm, rows_per_seq, 2, 2),
                _mod_spec(per_row, tm, rows_per_seq, 4, 2),
                _mod_spec(per_row, tm, rows_per_seq, 3, 2),
                vec,
                pl.BlockSpec((N_EXPERTS, D_MODEL), lambda i, j: (0, 0)),
                pl.BlockSpec((N_EXPERTS, 1), lambda i, j: (0, 0))]
    args = [o_attn, y_ssm, proj, proj, w_br_attn, w_br_ssm, w_out, x, g_post_mix.reshape(1, D_MODEL),
            mod, mod, mod, g_pre_ffn.reshape(1, D_MODEL), w_router_t, b_router.reshape(N_EXPERTS, 1)]
    assert len(args) == N_MERGE_IN
    aliases = {}
    if shared is not None:
        aliases = {len(args): 1, len(args) + 1: 2}
        in_specs += [pl.BlockSpec(memory_space=pl.ANY), pl.BlockSpec(memory_space=pl.ANY)]
        args += list(shared)
    return pl.pallas_call(
        _merge_kernel,
        grid=(rows // tm, nk),
        in_specs=in_specs,
        out_specs=[row2(D_MODEL),
                   pl.BlockSpec((tm, D_MODEL), lambda i, j: (blk0 + i, 0)),
                   pl.BlockSpec((N_EXPERTS, tm), lambda i, j: (0, blk0 + i))],
        out_shape=[jax.ShapeDtypeStruct((rows, D_MODEL), F32),
                   jax.ShapeDtypeStruct((n_total, D_MODEL), F32),
                   jax.ShapeDtypeStruct((N_EXPERTS, n_total), F32)],
        scratch_shapes=[pltpu.VMEM((tm, D_MODEL), F32)],
        input_output_aliases=aliases,
        compiler_params=_cparams(("arbitrary", "arbitrary")),
        name="merge",
    )(*args)


def _bucket_tables():
    ql = np.arange(WINDOW)[:, None]
    kl = np.arange(2 * WINDOW)[None, :]
    dist = ql + WINDOW - kl
    prompt = np.where((dist >= 0) & (dist < WINDOW), _t5_bucket_np(dist), -1).astype(np.int32)
    d_s = (WINDOW - 1 - np.arange(WINDOW))[None, :]
    sample = np.broadcast_to(_t5_bucket_np(d_s), (SUBLANE, WINDOW)).astype(np.int32)
    return prompt, sample


def _front(p):
    l = 0
    batch, seq, _ = p['x_prompt'].shape
    nseq = p['x_sample'].shape[0]
    xp = p['x_prompt'].reshape(batch * seq, D_MODEL)
    xs = p['x_sample'].reshape(nseq, D_MODEL)

    c_all = jnp.concatenate([p['c_prompt'], p['c_sample'],
                             jnp.zeros((SUBLANE - (batch + nseq) % SUBLANE, D_MODEL), F32)], axis=0)
    mod = _ada(c_all, p['w_ada'][l], p['b_ada'][l])
    mod_p = mod[:batch].reshape(batch, 1, 6 * D_MODEL)
    mod_s = mod[batch:batch + nseq]

    bucket_p, bucket_s = _bucket_tables()
    bias_p = _bias_table(bucket_p, p['rel_bias'])
    bias_s = _bias_table(bucket_s, p['rel_bias'])[:, 0, :]
    sinks = p['attn_sinks'][l]

    w_in_t = p['w_in'][l].astype(BF16)
    proj_p = _proj(xp, mod_p, False, seq, p['g_pre_mix'][l], w_in_t)
    proj_s = _proj(xs, mod_s, True, 1, p['g_pre_mix'][l], w_in_t)

    o_p = _attn_prompt(proj_p, batch, seq, bias_p, sinks)
    kv_p = proj_p.reshape(batch, seq, IN_W)[:, seq - WINDOW:, Q_W:Q_W + 2 * KV_W]
    new_k_p = kv_p[..., :KV_W].reshape(1, batch, WINDOW, N_KV_HEADS, HEAD_DIM)
    new_v_p = kv_p[..., KV_W:].reshape(1, batch, WINDOW, N_KV_HEADS, HEAD_DIM)
    o_s3, new_k_s, new_v_s = _attn_sample(
        proj_s[:, :Q_W].reshape(nseq, N_HEADS, HEAD_DIM),
        proj_s[:, Q_W:Q_W + KV_W].reshape(nseq, 1, KV_W),
        proj_s[:, Q_W + KV_W:Q_W + 2 * KV_W].reshape(nseq, 1, KV_W),
        p['cache_win_k'][l].reshape(nseq, WINDOW, KV_W),
        p['cache_win_v'][l].reshape(nseq, WINDOW, KV_W), bias_s, sinks)
    o_s = o_s3.reshape(nseq, Q_W)

    lbr, lbi, bbr, bbi = _ssm_disc(p['ssm_a_re'][l], p['ssm_a_im'][l], p['ssm_log_dt'][l],
                                   p['ssm_b_re'][l], p['ssm_b_im'][l])
    lbr = lbr.reshape(1, SSM_W)
    lbi = lbi.reshape(1, SSM_W)
    bb = jnp.concatenate([_block_diag_tiles(bbr), _block_diag_tiles(bbi)], axis=-1).astype(BF16)
    c_re_t = p['ssm_c_re'][l].transpose(0, 2, 1)
    c_im_t = p['ssm_c_im'][l].transpose(0, 2, 1)
    cc = jnp.concatenate([_block_diag_tiles(c_re_t), -_block_diag_tiles(c_im_t)], axis=1).astype(BF16)
    u0 = Q_W + 2 * KV_W
    zeros = jnp.zeros((batch, SSM_W), F32)
    w_glu = p['w_glu'][l].astype(BF16)
    y_p3, hr_p, hi_p = _ssm(proj_p.reshape(batch, seq, IN_W), zeros, zeros, batch, SSM_TC, lbr, lbi, bb, cc,
                            p['ssm_d'][l], w_glu, p['b_glu'][l], seq_major=True)
    y_p = y_p3.reshape(batch * seq, D_SSM)
    y_s, hr_s, hi_s = _ssm(proj_s[:, u0:u0 + D_SSM], p['state_ssm_re'][l].reshape(nseq, SSM_W),
                           p['state_ssm_im'][l].reshape(nseq, SSM_W), nseq, 1, lbr, lbi, bb, cc,
                           p['ssm_d'][l], w_glu, p['b_glu'][l])

    wa_t = p['w_br_attn'][l].astype(BF16)
    ws_t = p['w_br_ssm'][l].astype(BF16)
    wo = p['w_out'][l].astype(BF16)
    n_total = batch * seq + nseq
    merge = functools.partial(_merge, w_br_attn=wa_t, w_br_ssm=ws_t, w_out=wo, g_post_mix=p['g_post_mix'][l],
                              g_pre_ffn=p['g_pre_ffn'][l], w_router_t=p['w_router'][l].T,
                              b_router=p['b_router'][l])
    x1_p, h2_buf, lg_buf = merge(o_p, y_p, proj_p, xp, mod_p, False, seq, n_total, 0, None)
    x1_s, h2_all, lg_all = merge(o_s, y_s, proj_s, xs, mod_s, True, 1, n_total, batch * seq, (h2_buf, lg_buf))

    st = lambda h, n: h.reshape(1, n, N_SSM_GROUPS, SSM_STATE)
    return dict(
        mod_p=mod_p, mod_s=mod_s, h2_all=h2_all, lg_all=lg_all,
        p=dict(proj=proj_p, o_attn=o_p, new_k=new_k_p, new_v=new_v_p, y_ssm=y_p, h_re=st(hr_p, batch),
               h_im=st(hi_p, batch), x1=x1_p),
        s=dict(proj=proj_s, o_attn=o_s, new_k=new_k_s.reshape(1, nseq, WINDOW, N_KV_HEADS, HEAD_DIM),
               new_v=new_v_s.reshape(1, nseq, WINDOW, N_KV_HEADS, HEAD_DIM), y_ssm=y_s,
               h_re=st(hr_s, nseq), h_im=st(hi_s, nseq), x1=x1_s))


def _count_steps(c, step, n_max):
    out = jnp.zeros_like(c)
    for q in range(-(-n_max // step)):
        out = out + jnp.where(c > float(q * step), 1.0, 0.0)
    return out


def _route_kernel(lg_ref, pos_ref, gate_ref, texp_ref, trows_ref, ntiles_ref, pstart_ref, plen_ref):
    lg = lg_ref[...]
    e, tn = lg.shape
    erow = lax.broadcasted_iota(I32, (e, tn), 0).astype(F32)
    work = lg
    vals, hits = [], []
    for _ in range(TOP_K):
        m = jnp.max(work, axis=0, keepdims=True)
        idx = jnp.min(jnp.where(work == m, erow, float(e)), axis=0, keepdims=True)
        hit = erow == idx
        vals.append(m)
        hits.append(hit)
        work = jnp.where(hit, -jnp.inf, work)
    ex = [jnp.exp(v - vals[0]) for v in vals]
    den = ex[0] + ex[1] + ex[2] + ex[3]
    gate_ref[...] = jnp.concatenate([x / den for x in ex], axis=0)

    chosen = jnp.zeros((e, tn), F32)
    for hit in hits:
        chosen = chosen + jnp.where(hit, 1.0, 0.0)
    chosen_b = chosen.astype(BF16)
    tri = (lax.broadcasted_iota(I32, (LANE, LANE), 0) <= lax.broadcasted_iota(I32, (LANE, LANE), 1))
    tri = jnp.where(tri, 1.0, 0.0).astype(BF16)
    carry = jnp.zeros((e, 1), F32)
    ranks = []
    for b in range(tn // LANE):
        blk = chosen[:, b * LANE:(b + 1) * LANE]
        inc = _dot(chosen_b[:, b * LANE:(b + 1) * LANE], tri) + carry
        ranks.append(inc - blk)
        carry = inc[:, LANE - 1:LANE]
    rank = jnp.concatenate(ranks, axis=1)
    cnt_col = carry
    cnt_row = _dot_nt(jnp.ones((SUBLANE, tn), BF16), chosen_b)[0:1, :]

    tiles_col = _count_steps(cnt_col, EXP_TM, tn)
    tiles_row = _count_steps(cnt_row, EXP_TM, tn)
    ee_r = lax.broadcasted_iota(I32, (e, e), 0)
    ee_c = lax.broadcasted_iota(I32, (e, e), 1)
    tstart_col = jnp.sum(jnp.where(ee_c < ee_r, tiles_row, 0.0), axis=1, keepdims=True)
    ntiles = jnp.sum(tiles_row, axis=1, keepdims=True)
    rstart_col = tstart_col * float(EXP_TM)
    pos = [jnp.sum(jnp.where(hit, rstart_col + rank, 0.0), axis=0, keepdims=True) for hit in hits]
    pos_ref[...] = jnp.concatenate(pos, axis=0).astype(I32)

    mm = lax.broadcasted_iota(I32, (e, LANE), 1).astype(F32)
    e_col = lax.broadcasted_iota(I32, (e, LANE), 0).astype(F32)
    own = (mm >= tstart_col) & (mm < tstart_col + tiles_col)
    texp = jnp.sum(jnp.where(own, e_col, 0.0), axis=0, keepdims=True)
    rows_here = jnp.minimum(float(EXP_TM), cnt_col - (mm - tstart_col) * float(EXP_TM))
    trows = jnp.sum(jnp.where(own, rows_here, 0.0), axis=0, keepdims=True)
    last_e = jnp.max(jnp.where(tiles_col > 0.0, e_col, 0.0), axis=0, keepdims=True)
    texp = jnp.where(mm[0:1, :] < ntiles, texp, last_e)
    texp_ref[...] = texp.astype(I32)
    trows_ref[...] = trows.astype(I32)
    ntiles_ref[...] = jnp.broadcast_to(ntiles, (1, LANE)).astype(I32)
    nsub_col = _count_steps(cnt_col, EXP_SUB, tn)
    pstart_ref[...] = jnp.broadcast_to(rstart_col + cnt_col, (e, LANE)).astype(I32)
    plen_ref[...] = jnp.broadcast_to(nsub_col * float(EXP_SUB) - cnt_col, (e, LANE)).astype(I32)


def _route(lg_t):
    e, tn = lg_t.shape
    i32 = lambda shape: jax.ShapeDtypeStruct(shape, I32)
    return pl.pallas_call(
        _route_kernel,
        out_shape=[i32((TOP_K, tn)), jax.ShapeDtypeStruct((TOP_K, tn), F32),
                   i32((1, LANE)), i32((1, LANE)), i32((1, LANE)), i32((e, LANE)), i32((e, LANE))],
        compiler_params=pltpu.CompilerParams(vmem_limit_bytes=VMEM_LIMIT),
        name="route",
    )(lg_t)


def _max_tiles(n_tok):
    return (n_tok * TOP_K) // EXP_TM + N_EXPERTS


def _dispatch_kernel(pos_ref, pstart_ref, plen_ref, h2_ref, zero_ref, xs_ref, sem):
    i = pl.program_id(0)

    def row_copy(src, s, d):
        return pltpu.make_async_copy(src.at[pl.ds(s, 1)], xs_ref.at[pl.ds(d, 1)], sem)

    def issue(t, c):
        for k in range(TOP_K):
            row_copy(h2_ref, t, pos_ref[k, t]).start(priority=k % 2)
        return c

    lax.fori_loop(0, TOK_BLK, issue, 0, unroll=4)
    for _ in range(TOP_K):
        pltpu.make_async_copy(h2_ref, xs_ref.at[pl.ds(0, TOK_BLK)], sem).wait()

    @pl.when(i == 0)
    def _():
        def per_expert(ex, c):
            n = plen_ref[ex]
            s = pstart_ref[ex]

            def zissue(r, cc):
                row_copy(zero_ref, 0, s + r).start()
                return cc

            def zdrain(r, cc):
                row_copy(zero_ref, 0, 0).wait()
                return cc

            lax.fori_loop(0, n, zissue, 0)
            lax.fori_loop(0, n, zdrain, 0)
            return c

        lax.fori_loop(0, N_EXPERTS, per_expert, 0)


def _dispatch(pos3, pstart, plen, h2_all):
    n_tok = h2_all.shape[0]
    n_rows = _max_tiles(n_tok) * EXP_TM
    smem = pl.BlockSpec(memory_space=pltpu.SMEM)
    hbm = pl.BlockSpec(memory_space=pl.ANY)
    return pl.pallas_call(
        _dispatch_kernel,
        grid=(n_tok // TOK_BLK,),
        in_specs=[pl.BlockSpec((None, TOP_K, TOK_BLK), lambda i: (i, 0, 0), memory_space=pltpu.SMEM),
                  smem, smem,
                  pl.BlockSpec((TOK_BLK, D_MODEL), lambda i: (i, 0)),
                  pl.BlockSpec((SUBLANE, D_MODEL), lambda i: (0, 0))],
        out_specs=hbm,
        out_shape=jax.ShapeDtypeStruct((n_rows, D_MODEL), F32),
        scratch_shapes=[pltpu.SemaphoreType.DMA(())],
        compiler_params=_cparams(("arbitrary",)),
        name="dispatch",
    )(pos3, pstart, plen, h2_all, jnp.zeros((SUBLANE, D_MODEL), F32))


def _expert_kernel(texp_ref, trows_ref, nt_ref, xs_hbm, wg_ref, wl_ref, bg_ref, bl_ref, wd_ref, bd_ref,
                   o_ref, x_ref, x_sem, xb_scr, act_scr, wg_scr, wl_scr, wd_scr):
    m = pl.program_id(0)
    s = pl.program_id(1)
    n_tiles = nt_ref[0]
    valid = m < n_tiles

    rows = trows_ref[m]
    nbig = lax.shift_right_logical(rows, int(math.log2(EXP_BIG)))
    big_rows = nbig * EXP_BIG
    nsmall = lax.shift_right_logical(rows - big_rows + (EXP_SUB - 1), int(math.log2(EXP_SUB)))
    nsub_done = nbig * (EXP_BIG // EXP_SUB) + nsmall

    def x_copy(tile):
        return pltpu.make_async_copy(xs_hbm.at[pl.ds(pl.multiple_of(tile * EXP_TM, EXP_TM), EXP_TM)],
                                     x_ref, x_sem)

    @pl.when((m == 0) & (s == 0))
    def _():
        x_copy(0).start()

    @pl.when(valid & (s == 0))
    def _():
        x_copy(m).wait()

        def to_bf16(r, c):
            r0 = pl.multiple_of(r * EXP_SUB, EXP_SUB)
            xb_scr[pl.ds(r0, EXP_SUB), :] = x_ref[pl.ds(r0, EXP_SUB), :].astype(BF16)
            return c

        lax.fori_loop(0, nsub_done, to_bf16, 0)

        @pl.when(m + 1 < n_tiles)
        def _():
            x_copy(m + 1).start()

    def over_rows(first, step):
        def big(r, c):
            step(pl.multiple_of(r * EXP_BIG, EXP_BIG), EXP_BIG)
            return c

        def small(r, c):
            step(pl.multiple_of(big_rows + r * EXP_SUB, EXP_SUB), EXP_SUB)
            return c

        @pl.when(nbig > 0)
        def _():
            first(EXP_BIG)
            lax.fori_loop(1, nbig, big, 0)
            lax.fori_loop(0, nsmall, small, 0)

        @pl.when(nbig == 0)
        def _():
            first(EXP_SUB)
            lax.fori_loop(1, nsmall, small, 0)

    @pl.when(valid & (s < EXP_NF))
    def _():
        def finish(r0, n, hg, hl):
            x_glu = jnp.minimum(hg, SWIGLU_LIMIT)
            x_lin = jnp.clip(hl, -SWIGLU_LIMIT, SWIGLU_LIMIT)
            act = x_glu * _sigmoid(SWIGLU_ALPHA * x_glu) * (x_lin + 1.0)
            act_scr[s, pl.ds(r0, n), :] = act.astype(BF16)

        def first(n):
            xb = xb_scr[0:n, :]
            hg = jnp.broadcast_to(bg_ref[...], (n, EXP_TF))
            hl = jnp.broadcast_to(bl_ref[...], (n, EXP_TF))
            for q in range(D_MODEL // EXP_KQ):
                ks = slice(q * EXP_KQ, (q + 1) * EXP_KQ)
                wgq = wg_ref[ks, :].astype(BF16)
                wlq = wl_ref[ks, :].astype(BF16)
                wg_scr[ks, :] = wgq
                wl_scr[ks, :] = wlq
                hg = hg + _dot(xb[:, ks], wgq)
                hl = hl + _dot(xb[:, ks], wlq)
            finish(0, n, hg, hl)

        def step(r0, n):
            xb = xb_scr[pl.ds(r0, n), :]
            finish(r0, n, _dot(xb, wg_scr[...]) + bg_ref[...], _dot(xb, wl_scr[...]) + bl_ref[...])

        over_rows(first, step)

    @pl.when(valid & (s >= EXP_NF))
    def _():
        def first(n):
            acc = jnp.broadcast_to(bd_ref[...], (n, EXP_TD))
            for f in range(EXP_NF):
                fs = slice(f * EXP_TF, (f + 1) * EXP_TF)
                wdq = wd_ref[fs, :].astype(BF16)
                wd_scr[fs, :] = wdq
                acc = acc + _dot(act_scr[f, 0:n, :], wdq)
            o_ref[0:n, :] = acc

        def step(r0, n):
            acc = jnp.broadcast_to(bd_ref[...], (n, EXP_TD))
            for f in range(EXP_NF):
                acc = acc + _dot(act_scr[f, pl.ds(r0, n), :], wd_scr[f * EXP_TF:(f + 1) * EXP_TF, :])
            o_ref[pl.ds(r0, n), :] = acc

        def zero(r, c):
            r0 = pl.multiple_of(r * EXP_SUB, EXP_SUB)
            o_ref[pl.ds(r0, EXP_SUB), :] = jnp.zeros((EXP_SUB, EXP_TD), F32)
            return c

        over_rows(first, step)
        lax.fori_loop(nsub_done, EXP_TM // EXP_SUB, zero, 0)


def _experts(texp, trows, ntiles, xs, w_gate_up, b_gate_up, w_down, b_down):
    n_tiles = xs.shape[0] // EXP_TM
    nsteps = EXP_NF + EXP_ND

    def tile(m, nt):
        return jnp.minimum(m, nt[0] - 1)

    def ea(m, s, te, nt):
        return te[jnp.where(s < EXP_NF, m, jnp.minimum(m + 1, nt[0] - 1))]

    def fa(m, s, nt):
        return jnp.where((m < nt[0]) & (s < EXP_NF), s, 0)

    def fb(m, s, nt):
        return jnp.where(m < nt[0], jnp.maximum(s - EXP_NF, 0), EXP_ND - 1)

    grid_spec = pltpu.PrefetchScalarGridSpec(
        num_scalar_prefetch=3,
        grid=(n_tiles, nsteps),
        in_specs=[
            pl.BlockSpec(memory_space=pl.ANY),
            pl.BlockSpec((None, D_MODEL, EXP_TF),
                         lambda m, s, te, tr, nt: (ea(m, s, te, nt), 0, fa(m, s, nt))),
            pl.BlockSpec((None, D_MODEL, EXP_TF),
                         lambda m, s, te, tr, nt: (ea(m, s, te, nt), 0, EXP_NF + fa(m, s, nt))),
            pl.BlockSpec((None, 1, EXP_TF), lambda m, s, te, tr, nt: (ea(m, s, te, nt), 0, fa(m, s, nt))),
            pl.BlockSpec((None, 1, EXP_TF),
                         lambda m, s, te, tr, nt: (ea(m, s, te, nt), 0, EXP_NF + fa(m, s, nt))),
            pl.BlockSpec((None, D_FF, EXP_TD), lambda m, s, te, tr, nt: (te[m], 0, fb(m, s, nt))),
            pl.BlockSpec((None, 1, EXP_TD), lambda m, s, te, tr, nt: (te[m], 0, fb(m, s, nt))),
        ],
        out_specs=pl.BlockSpec((EXP_TM, EXP_TD), lambda m, s, te, tr, nt: (tile(m, nt), fb(m, s, nt))),
        scratch_shapes=[pltpu.VMEM((EXP_TM, D_MODEL), F32), pltpu.SemaphoreType.DMA(()),
                        pltpu.VMEM((EXP_TM, D_MODEL), BF16),
                        pltpu.VMEM((EXP_NF, EXP_TM, EXP_TF), BF16),
                        pltpu.VMEM((D_MODEL, EXP_TF), BF16), pltpu.VMEM((D_MODEL, EXP_TF), BF16),
                        pltpu.VMEM((D_FF, EXP_TD), BF16)],
    )
    return pl.pallas_call(
        _expert_kernel,
        grid_spec=grid_spec,
        out_shape=jax.ShapeDtypeStruct((xs.shape[0], D_MODEL), F32),
        compiler_params=pltpu.CompilerParams(dimension_semantics=("arbitrary", "arbitrary"),
                                             vmem_limit_bytes=EXP_VMEM_LIMIT),
        name="experts",
    )(texp, trows, ntiles, xs, w_gate_up, w_gate_up,
      b_gate_up.reshape(N_EXPERTS, 1, 2 * D_FF), b_gate_up.reshape(N_EXPERTS, 1, 2 * D_FF),
      w_down, b_down.reshape(N_EXPERTS, 1, D_MODEL))


def _combine_kernel(pos_ref, pos_next_ref, gate_ref, ys_ref, x1_ref, g2_ref, gpf_ref, o_ref, buf, sem):
    i = pl.program_id(0)
    slot = lax.rem(i, 2)

    def gather(p_ref, sl):
        def issue(t, c):
            for k in range(TOP_K):
                pltpu.make_async_copy(ys_ref.at[pl.ds(p_ref[k, t], 1)], buf.at[sl, k, pl.ds(t, 1)],
                                      sem.at[sl]).start(priority=k % 2)
            return c

        lax.fori_loop(0, TOK_BLK, issue, 0, unroll=4)

    @pl.when(i == 0)
    def _():
        gather(pos_ref, 0)

    @pl.when(i + 1 < pl.num_programs(0))
    def _():
        gather(pos_next_ref, 1 - slot)

    for k in range(TOP_K):
        pltpu.make_async_copy(ys_ref.at[pl.ds(0, TOK_BLK)], buf.at[slot, k], sem.at[slot]).wait()
    g = gate_ref[...]
    f = g[:, 0:1] * buf[slot, 0]
    for k in range(1, TOP_K):
        f = f + g[:, k:k + 1] * buf[slot, k]
    o_ref[...] = x1_ref[...] + g2_ref[...] * _rms(f, gpf_ref[...])


def _combine(pos3, gates_t, ys, x1, mod, per_row, rows_per_seq, g_post_ffn, blk0):
    rows = x1.shape[0]
    nblk = rows // TOK_BLK
    return pl.pallas_call(
        _combine_kernel,
        grid=(nblk,),
        in_specs=[pl.BlockSpec((None, TOP_K, TOK_BLK), lambda i: (blk0 + i, 0, 0), memory_space=pltpu.SMEM),
                  pl.BlockSpec((None, TOP_K, TOK_BLK), lambda i: (blk0 + jnp.minimum(i + 1, nblk - 1), 0, 0),
                               memory_space=pltpu.SMEM),
                  pl.BlockSpec((TOK_BLK, TOP_K), lambda i: (blk0 + i, 0)),
                  pl.BlockSpec(memory_space=pl.ANY),
                  pl.BlockSpec((TOK_BLK, D_MODEL), lambda i: (i, 0)),
                  _mod_spec(per_row, TOK_BLK, rows_per_seq, 5, 1),
                  pl.BlockSpec((1, D_MODEL), lambda i: (0, 0))],
        out_specs=pl.BlockSpec((TOK_BLK, D_MODEL), lambda i: (i, 0)),
        out_shape=jax.ShapeDtypeStruct((rows, D_MODEL), F32),
        scratch_shapes=[pltpu.VMEM((2, TOP_K, TOK_BLK, D_MODEL), F32), pltpu.SemaphoreType.DMA((2,))],
        compiler_params=_cparams(("arbitrary",)),
        name="combine",
    )(pos3, pos3, gates_t, ys, x1, mod, g_post_ffn.reshape(1, D_MODEL))


def _moe(fr, p, batch, seq, nseq):
    l = 0
    h2_all, lg_all = fr['h2_all'], fr['lg_all']
    n_tok = h2_all.shape[0]
    pos, gates, texp, trows, ntiles, pstart, plen = _route(lg_all)
    pos3 = pos.reshape(TOP_K, n_tok // TOK_BLK, TOK_BLK).transpose(1, 0, 2)
    xs = _dispatch(pos3, pstart[:, 0], plen[:, 0], h2_all)
    ys = _experts(texp[0], trows[0], ntiles[0, :1], xs, p['w_gate_up'][l], p['b_gate_up'][l],
                  p['w_down'][l], p['b_down'][l])
    gates_t = gates.T
    y_p = _combine(pos3, gates_t, ys, fr['p']['x1'], fr['mod_p'], False, seq, p['g_post_ffn'][l], 0)
    y_s = _combine(pos3, gates_t, ys, fr['s']['x1'], fr['mod_s'], True, 1, p['g_post_ffn'][l],
                   batch * seq // TOK_BLK)
    return y_p, y_s


def kernel(x_prompt, x_sample, c_prompt, c_sample, cache_win_k, cache_win_v, state_ssm_re, state_ssm_im, w_ada, b_ada, g_pre_mix, g_post_mix, g_pre_ffn, g_post_ffn, w_in, attn_sinks, rel_bias, ssm_a_re, ssm_a_im, ssm_log_dt, ssm_b_re, ssm_b_im, ssm_c_re, ssm_c_im, ssm_d, w_glu, b_glu, w_br_attn, w_br_ssm, w_out, w_router, b_router, w_gate_up, b_gate_up, w_down, b_down):
    p = dict(locals())
    batch, seq, _ = x_prompt.shape
    nseq = x_sample.shape[0]
    fr = _front(p)
    y_p, y_s = _moe(fr, p, batch, seq, nseq)
    fp, fs = fr['p'], fr['s']
    return (y_p.reshape(batch, seq, D_MODEL), y_s.reshape(nseq, 1, D_MODEL),
            fp['new_k'], fp['new_v'], fp['h_re'], fp['h_im'],
            fs['new_k'], fs['new_v'], fs['h_re'], fs['h_im'])
```

```python
import functools
import math

import numpy as np
import jax
import jax.numpy as jnp
from jax import lax
from jax.experimental import pallas as pl
from jax.experimental.pallas import tpu as pltpu

F32 = jnp.float32
BF16 = jnp.bfloat16
I32 = jnp.int32

D_MODEL = 2048
N_HEADS = 16
N_KV_HEADS = 4
HEAD_DIM = 64
Q_GROUP = N_HEADS // N_KV_HEADS
WINDOW = 128
N_BUCKETS = 32
MAX_EXACT = N_BUCKETS // 2
MAX_DISTANCE = 128
D_SSM = 1024
SSM_GROUP = 16
N_SSM_GROUPS = 64
SSM_STATE = 64
N_EXPERTS = 32
TOP_K = 4
D_FF = 2048
SWIGLU_LIMIT = 7.0
SWIGLU_ALPHA = 1.702
NORM_EPS = 1e-6
NEG_INF = -1e30
Q_W = N_HEADS * HEAD_DIM
KV_W = N_KV_HEADS * HEAD_DIM
IN_W = Q_W + 2 * KV_W + D_SSM + 2 * D_MODEL
SSM_W = N_SSM_GROUPS * SSM_STATE

LANE = 128
SUBLANE = 8
VMEM_LIMIT = 56 * 1024 * 1024
EXP_VMEM_LIMIT = 60 * 1024 * 1024

PROJ_TM = 1024
PROJ_TN = 512
MERGE_TM = 512
MERGE_TK = 512
SSM_TC = 128
SSM_LB = 512
SSM_TILES = D_SSM // LANE
TOK_BLK = 128
EXP_TM = 1152
EXP_BIG = 512
EXP_SUB = 128
EXP_TF = 512
EXP_TD = 512
EXP_KQ = 512
EXP_NF = D_FF // EXP_TF
EXP_ND = D_MODEL // EXP_TD


def _cparams(sem):
    return pltpu.CompilerParams(dimension_semantics=sem, vmem_limit_bytes=VMEM_LIMIT)


def _sigmoid(x):
    return 1.0 / (1.0 + jnp.exp(-x))


def _rms(x, g):
    return x * lax.rsqrt(jnp.mean(x * x, axis=-1, keepdims=True) + NORM_EPS) * g


def _dot(a, b):
    return jnp.dot(a, b, preferred_element_type=F32)


def _dot_nt(a, b):
    return lax.dot_general(a, b, (((1,), (1,)), ((), ())), preferred_element_type=F32)


def _ada_kernel(c_ref, w_ref, b_ref, o_ref):
    c = c_ref[...]
    s = (c * _sigmoid(c)).astype(BF16)
    o_ref[...] = _dot(s, w_ref[...].astype(BF16)) + b_ref[...]


def _ada(c_all, w_ada, b_ada):
    rows = c_all.shape[0]
    tn = 1024
    n = w_ada.shape[1]
    return pl.pallas_call(
        _ada_kernel,
        grid=(n // tn,),
        in_specs=[pl.BlockSpec((rows, D_MODEL), lambda j: (0, 0)),
                  pl.BlockSpec((D_MODEL, tn), lambda j: (0, j)),
                  pl.BlockSpec((1, tn), lambda j: (0, j))],
        out_specs=pl.BlockSpec((rows, tn), lambda j: (0, j)),
        out_shape=jax.ShapeDtypeStruct((rows, n), F32),
        compiler_params=_cparams(("arbitrary",)),
        name="ada",
    )(c_all, w_ada, b_ada.reshape(1, n))


def _t5_bucket_np(dist):
    n = np.maximum(dist, 0)
    nf = np.maximum(n, 1).astype(np.float64)
    large = MAX_EXACT + (np.log(nf / MAX_EXACT) / math.log(MAX_DISTANCE / MAX_EXACT)
                         * (N_BUCKETS - MAX_EXACT)).astype(np.int32)
    large = np.minimum(large, N_BUCKETS - 1)
    return np.where(n < MAX_EXACT, n, large).astype(np.int32)


def _bias_kernel(bucket_ref, rb_ref, o_ref):
    h = pl.program_id(0)
    bucket = bucket_ref[...]
    acc = jnp.full(bucket.shape, NEG_INF, F32)
    for b in range(N_BUCKETS):
        acc = jnp.where(bucket == b, rb_ref[b, h], acc)
    o_ref[...] = acc


def _bias_table(bucket_np, rel_bias):
    r, c = bucket_np.shape
    return pl.pallas_call(
        _bias_kernel,
        grid=(N_HEADS,),
        in_specs=[pl.BlockSpec((r, c), lambda h: (0, 0)),
                  pl.BlockSpec(memory_space=pltpu.SMEM)],
        out_specs=pl.BlockSpec((None, r, c), lambda h: (h, 0, 0)),
        out_shape=jax.ShapeDtypeStruct((N_HEADS, r, c), F32),
        compiler_params=_cparams(("arbitrary",)),
        name="bias",
    )(jnp.asarray(bucket_np), rel_bias)


def _proj_kernel(x_ref, sc_ref, sh_ref, g_ref, w_ref, o_ref, h_scr):
    @pl.when(pl.program_id(1) == 0)
    def _():
        h = _rms(x_ref[...], g_ref[...]) * (1.0 + sc_ref[...]) + sh_ref[...]
        h_scr[...] = h.astype(BF16)

    o_ref[...] = _dot(h_scr[...], w_ref[...])


def _mod_spec(per_row, tm, rows_per_seq, col, nargs):
    if per_row:
        if nargs == 1:
            return pl.BlockSpec((tm, D_MODEL), lambda i: (i, col))
        return pl.BlockSpec((tm, D_MODEL), lambda i, j: (i, col))
    tiles_per_seq = rows_per_seq // tm
    if nargs == 1:
        return pl.BlockSpec((None, 1, D_MODEL), lambda i: (i // tiles_per_seq, 0, col))
    return pl.BlockSpec((None, 1, D_MODEL), lambda i, j: (i // tiles_per_seq, 0, col))


def _proj(x, mod, per_row, rows_per_seq, g_pre, w_in):
    rows = x.shape[0]
    tm = min(PROJ_TM, rows)
    return pl.pallas_call(
        _proj_kernel,
        grid=(rows // tm, IN_W // PROJ_TN),
        in_specs=[pl.BlockSpec((tm, D_MODEL), lambda i, j: (i, 0)),
                  _mod_spec(per_row, tm, rows_per_seq, 1, 2),
                  _mod_spec(per_row, tm, rows_per_seq, 0, 2),
                  pl.BlockSpec((1, D_MODEL), lambda i, j: (0, 0)),
                  pl.BlockSpec((D_MODEL, PROJ_TN), lambda i, j: (0, j))],
        out_specs=pl.BlockSpec((tm, PROJ_TN), lambda i, j: (i, j)),
        out_shape=jax.ShapeDtypeStruct((rows, IN_W), F32),
        scratch_shapes=[pltpu.VMEM((tm, D_MODEL), BF16)],
        compiler_params=_cparams(("arbitrary", "arbitrary")),
        name="proj",
    )(x, mod, mod, g_pre.reshape(1, D_MODEL), w_in)


def _attn_prompt_kernel(q_ref, kc_ref, kp_ref, vc_ref, vp_ref, bias_ref, sink_ref, o_ref):
    has_prev = pl.program_id(1) > 0
    q = q_ref[...]
    k = jnp.concatenate([kp_ref[...], kc_ref[...]], axis=0)
    v = jnp.concatenate([vp_ref[...], vc_ref[...]], axis=0)
    col = lax.broadcasted_iota(I32, (WINDOW, 2 * WINDOW), 1)
    key_ok = (col >= WINDOW) | has_prev
    outs = []
    for g in range(N_KV_HEADS):
        kg = k[:, g * HEAD_DIM:(g + 1) * HEAD_DIM].astype(BF16)
        vg = v[:, g * HEAD_DIM:(g + 1) * HEAD_DIM].astype(BF16)
        for hh in range(Q_GROUP):
            h = g * Q_GROUP + hh
            qh = q[:, h * HEAD_DIM:(h + 1) * HEAD_DIM].astype(BF16)
            s = _dot_nt(qh, kg) * (HEAD_DIM ** -0.5) + bias_ref[h]
            s = jnp.where(key_ok, s, NEG_INF)
            sink = sink_ref[h]
            m = jnp.maximum(jnp.max(s, axis=-1, keepdims=True), sink)
            p = jnp.exp(s - m)
            den = jnp.sum(p, axis=-1, keepdims=True) + jnp.exp(sink - m)
            outs.append(_dot(p.astype(BF16), vg) / den)
    o_ref[...] = jnp.concatenate(outs, axis=-1)


def _attn_prompt(proj, batch, seq, bias, sinks):
    nb = seq // WINDOW
    kcol = Q_W // KV_W
    vcol = kcol + 1
    cur = lambda c: (lambda b, n: (b * nb + n, c))
    prev = lambda c: (lambda b, n: (b * nb + jnp.maximum(n - 1, 0), c))
    return pl.pallas_call(
        _attn_prompt_kernel,
        grid=(batch, nb),
        in_specs=[pl.BlockSpec((WINDOW, Q_W), cur(0)),
                  pl.BlockSpec((WINDOW, KV_W), cur(kcol)),
                  pl.BlockSpec((WINDOW, KV_W), prev(kcol)),
                  pl.BlockSpec((WINDOW, KV_W), cur(vcol)),
                  pl.BlockSpec((WINDOW, KV_W), prev(vcol)),
                  pl.BlockSpec((N_HEADS, WINDOW, 2 * WINDOW), lambda b, n: (0, 0, 0)),
                  pl.BlockSpec(memory_space=pltpu.SMEM)],
        out_specs=pl.BlockSpec((WINDOW, Q_W), lambda b, n: (b * nb + n, 0)),
        out_shape=jax.ShapeDtypeStruct((batch * seq, Q_W), F32),
        compiler_params=_cparams(("arbitrary", "arbitrary")),
        name="attn_prompt",
    )(proj, proj, proj, proj, proj, bias, sinks)


def _attn_sample_kernel(q_ref, kn_ref, vn_ref, ck_ref, cv_ref, bias_ref, sink_ref,
                        o_ref, nk_ref, nv_ref):
    tb = q_ref.shape[0]
    row = lax.broadcasted_iota(I32, (tb, WINDOW, KV_W), 1)
    last = row == WINDOW - 1
    nk = jnp.where(last, kn_ref[...], pltpu.roll(ck_ref[...], WINDOW - 1, 1))
    nv = jnp.where(last, vn_ref[...], pltpu.roll(cv_ref[...], WINDOW - 1, 1))
    nk_ref[...] = nk
    nv_ref[...] = nv
    lane_grp = lax.broadcasted_iota(I32, (N_HEADS, KV_W), 1) // HEAD_DIM
    head_grp = lax.broadcasted_iota(I32, (N_HEADS, KV_W), 0) // Q_GROUP
    gmask = (lane_grp == head_grp).astype(F32)
    q = q_ref[...]
    qrow = jnp.concatenate([q] * N_KV_HEADS, axis=-1) * gmask
    s = jnp.einsum('bhc,brc->bhr', qrow.astype(BF16), nk.astype(BF16),
                   preferred_element_type=F32) * (HEAD_DIM ** -0.5)
    s = s + bias_ref[...]
    sink = sink_ref[...]
    m = jnp.maximum(jnp.max(s, axis=-1, keepdims=True), sink)
    p = jnp.exp(s - m)
    den = jnp.sum(p, axis=-1, keepdims=True) + jnp.exp(sink - m)
    o = jnp.einsum('bhr,brc->bhc', p.astype(BF16), nv.astype(BF16),
                   preferred_element_type=F32) * gmask
    o64 = o[..., 0:HEAD_DIM]
    for g in range(1, N_KV_HEADS):
        o64 = o64 + o[..., g * HEAD_DIM:(g + 1) * HEAD_DIM]
    o_ref[...] = o64 / den


def _attn_sample(q3, kn, vn, cache_k, cache_v, bias, sinks):
    nseq = q3.shape[0]
    tb = 16
    seq3 = lambda w: pl.BlockSpec((tb, WINDOW, w), lambda i: (i, 0, 0))
    return pl.pallas_call(
        _attn_sample_kernel,
        grid=(nseq // tb,),
        in_specs=[pl.BlockSpec((tb, N_HEADS, HEAD_DIM), lambda i: (i, 0, 0)),
                  pl.BlockSpec((tb, 1, KV_W), lambda i: (i, 0, 0)),
                  pl.BlockSpec((tb, 1, KV_W), lambda i: (i, 0, 0)),
                  seq3(KV_W), seq3(KV_W),
                  pl.BlockSpec((N_HEADS, WINDOW), lambda i: (0, 0)),
                  pl.BlockSpec((N_HEADS, 1), lambda i: (0, 0))],
        out_specs=[pl.BlockSpec((tb, N_HEADS, HEAD_DIM), lambda i: (i, 0, 0)),
                   seq3(KV_W), seq3(KV_W)],
        out_shape=[jax.ShapeDtypeStruct((nseq, N_HEADS, HEAD_DIM), F32),
                   jax.ShapeDtypeStruct((nseq, WINDOW, KV_W), F32),
                   jax.ShapeDtypeStruct((nseq, WINDOW, KV_W), F32)],
        compiler_params=_cparams(("arbitrary",)),
        name="attn_sample",
    )(q3, kn, vn, cache_k, cache_v, bias, sinks.reshape(N_HEADS, 1))


def _ssm_disc_kernel(are_ref, aim_ref, ldt_ref, bre_ref, bim_ref,
                     lbr_ref, lbi_ref, bbr_ref, bbi_ref):
    a_re = are_ref[...]
    a_im = aim_ref[...]
    dt = jnp.exp(ldt_ref[...])
    lam_re = a_re * dt
    lam_im = a_im * dt
    mag = jnp.exp(lam_re)
    lb_re = mag * jnp.cos(lam_im)
    lb_im = mag * jnp.sin(lam_im)
    den = a_re * a_re + a_im * a_im
    nr = lb_re - 1.0
    ni = lb_im
    coef_re = (nr * a_re + ni * a_im) / den
    coef_im = (ni * a_re - nr * a_im) / den
    b_re = bre_ref[...]
    b_im = bim_ref[...]
    lbr_ref[...] = lb_re
    lbi_ref[...] = lb_im
    bbr_ref[...] = coef_re * b_re - coef_im * b_im
    bbi_ref[...] = coef_re * b_im + coef_im * b_re


def _ssm_disc(a_re, a_im, log_dt, b_re, b_im):
    g, p, j = N_SSM_GROUPS, SSM_STATE, SSM_GROUP
    vec = jax.ShapeDtypeStruct((g, 1, p), F32)
    mat = jax.ShapeDtypeStruct((g, j, p), F32)
    return pl.pallas_call(
        _ssm_disc_kernel,
        out_shape=[vec, vec, mat, mat],
        name="ssm_disc",
    )(a_re.reshape(g, 1, p), a_im.reshape(g, 1, p), log_dt.reshape(g, 1, 1),
      b_re.transpose(0, 2, 1), b_im.transpose(0, 2, 1))


def _block_diag_tiles(x):
    a, b = x.shape[1], x.shape[2]
    eye = jnp.eye(SUBLANE, dtype=x.dtype)
    y = jnp.einsum('kgab,gh->kgahb', x.reshape(SSM_TILES, SUBLANE, a, b), eye)
    return y.reshape(SSM_TILES, SUBLANE * a, SUBLANE * b)


def _gelu_tanh(x):
    return 0.5 * x * (1.0 + jnp.tanh(math.sqrt(2.0 / math.pi) * (x + 0.044715 * (x * x * x))))


def _ssm_kernel(*refs, nseq, tc, seq_major):
    if seq_major:
        ua_ref, ub_ref, perm_ref, perm_t_ref = refs[:4]
        refs = refs[4:]
        u = jnp.concatenate([ua_ref[...].reshape(nseq * tc, D_SSM // 2),
                             ub_ref[...].reshape(nseq * tc, D_SSM // 2)], axis=1)
        ub = _dot(perm_ref[...], u.astype(BF16)).astype(BF16)
    else:
        u = refs[0][...]
        refs = refs[1:]
        ub = u.astype(BF16)
    (h0r_ref, h0i_ref, lbr_ref, lbi_ref, bb_ref, cc_ref, d_ref, wglu_ref, bglu_ref,
     y_ref, hTr_ref, hTi_ref, hre, him, st_r, st_i) = refs
    paired = nseq == 4

    @pl.when(pl.program_id(0) == 0)
    def _():
        if paired:
            st_r[...] = jnp.concatenate([h0r_ref[...], h0r_ref[...]], axis=0)
            st_i[...] = jnp.concatenate([h0i_ref[...], h0i_ref[...]], axis=0)
        else:
            st_r[...] = h0r_ref[...]
            st_i[...] = h0i_ref[...]

    half = SSM_W // SSM_TILES
    for k in range(SSM_TILES):
        bu = _dot(ub[:, k * LANE:(k + 1) * LANE], bb_ref[k])
        hre[:, k * half:(k + 1) * half] = bu[:, :half]
        him[:, k * half:(k + 1) * half] = bu[:, half:]

    for blk in range(SSM_W // SSM_LB):
        sl = slice(blk * SSM_LB, (blk + 1) * SSM_LB)
        ar = lbr_ref[:, sl]
        ai = lbi_ref[:, sl]
        if paired:
            lower = lax.broadcasted_iota(I32, (SUBLANE, SSM_LB), 0) < nseq

            def body(m, carry):
                sr, si = carry
                r0 = pl.multiple_of(m * SUBLANE, SUBLANE)
                br = hre[pl.ds(r0, SUBLANE), sl]
                bi = him[pl.ds(r0, SUBLANE), sl]
                xr = pltpu.roll(sr, nseq, 0)
                xi = pltpu.roll(si, nseq, 0)
                h1r = ar * xr - ai * xi + br
                h1i = ar * xi + ai * xr + bi
                yr = pltpu.roll(h1r, nseq, 0)
                yi = pltpu.roll(h1i, nseq, 0)
                h2r = ar * yr - ai * yi + br
                h2i = ar * yi + ai * yr + bi
                hre[pl.ds(r0, SUBLANE), sl] = jnp.where(lower, h1r, h2r)
                him[pl.ds(r0, SUBLANE), sl] = jnp.where(lower, h1i, h2i)
                return h2r, h2i

            sr, si = lax.fori_loop(0, tc * nseq // SUBLANE, body, (st_r[:, sl], st_i[:, sl]))
        else:
            def body(t, carry):
                sr, si = carry
                r0 = pl.multiple_of(t * nseq, SUBLANE)
                br = hre[pl.ds(r0, nseq), sl]
                bi = him[pl.ds(r0, nseq), sl]
                nr = ar * sr - ai * si + br
                ni = ar * si + ai * sr + bi
                hre[pl.ds(r0, nseq), sl] = nr
                him[pl.ds(r0, nseq), sl] = ni
                return nr, ni

            sr, si = lax.fori_loop(0, tc, body, (st_r[:, sl], st_i[:, sl]))
        st_r[:, sl] = sr
        st_i[:, sl] = si

    ys = []
    for k in range(SSM_TILES):
        hr = hre[:, k * half:(k + 1) * half].astype(BF16)
        hi = him[:, k * half:(k + 1) * half].astype(BF16)
        ys.append(_dot(hr, cc_ref[k, :half, :]) + _dot(hi, cc_ref[k, half:, :]))
    yc = jnp.concatenate(ys, axis=-1)
    if seq_major:
        yc_hi, yc_lo = _split_bf16(yc)
        yc = _dot(perm_t_ref[...], yc_hi) + _dot(perm_t_ref[...], yc_lo)
    y = _gelu_tanh(yc + d_ref[...] * u)
    z = _dot(y.astype(BF16), wglu_ref[...]) + bglu_ref[...]
    y_ref[...] = (y * _sigmoid(z)).reshape(y_ref.shape)

    if paired:
        hTr_ref[...] = st_r[nseq:, :]
        hTi_ref[...] = st_i[nseq:, :]
    else:
        hTr_ref[...] = st_r[...]
        hTi_ref[...] = st_i[...]


def _ssm(u_src, h0_re, h0_im, nseq, tc, lbr, lbi, bb, cc, d, w_glu, b_glu, seq_major=False):
    r = nseq * tc
    st_rows = max(nseq, SUBLANE)
    const2 = lambda shape: pl.BlockSpec(shape, lambda c: (0, 0))
    const3 = lambda shape: pl.BlockSpec(shape, lambda c: (0, 0, 0))
    if seq_major:
        steps = u_src.shape[1] // tc
        half_w = D_SSM // 2
        col0 = (Q_W + 2 * KV_W) // half_w
        t_idx, s_idx = np.divmod(np.arange(r), nseq)
        perm = np.zeros((r, r), np.float32)
        perm[np.arange(r), s_idx * tc + t_idx] = 1.0
        u_specs = [pl.BlockSpec((nseq, tc, half_w), lambda c: (0, c, col0)),
                   pl.BlockSpec((nseq, tc, half_w), lambda c: (0, c, col0 + 1)),
                   const2((r, r)), const2((r, r))]
        u_args = [u_src, u_src, jnp.asarray(perm, BF16), jnp.asarray(perm.T, BF16)]
        y_spec = pl.BlockSpec((nseq, tc, D_SSM), lambda c: (0, c, 0))
        y_shape = jax.ShapeDtypeStruct((nseq, steps * tc, D_SSM), F32)
    else:
        steps = u_src.shape[0] // r
        u_specs = [pl.BlockSpec((r, D_SSM), lambda c: (c, 0))]
        u_args = [u_src]
        y_spec = pl.BlockSpec((r, D_SSM), lambda c: (c, 0))
        y_shape = jax.ShapeDtypeStruct((steps * r, D_SSM), F32)
    return pl.pallas_call(
        functools.partial(_ssm_kernel, nseq=nseq, tc=tc, seq_major=seq_major),
        grid=(steps,),
        in_specs=u_specs + [
            const2((nseq, SSM_W)), const2((nseq, SSM_W)),
            const2((1, SSM_W)), const2((1, SSM_W)),
            const3((SSM_TILES, LANE, 2 * SSM_W // SSM_TILES)),
            const3((SSM_TILES, 2 * SSM_W // SSM_TILES, LANE)),
            const2((1, D_SSM)), const2((D_SSM, D_SSM)), const2((1, D_SSM))],
        out_specs=[y_spec, const2((nseq, SSM_W)), const2((nseq, SSM_W))],
        out_shape=[y_shape,
                   jax.ShapeDtypeStruct((nseq, SSM_W), F32),
                   jax.ShapeDtypeStruct((nseq, SSM_W), F32)],
        scratch_shapes=[pltpu.VMEM((r, SSM_W), F32), pltpu.VMEM((r, SSM_W), F32),
                        pltpu.VMEM((st_rows, SSM_W), F32), pltpu.VMEM((st_rows, SSM_W), F32)],
        compiler_params=_cparams(("arbitrary",)),
        name="ssm",
    )(*u_args, h0_re, h0_im, lbr, lbi, bb, cc, d.reshape(1, D_SSM), w_glu, b_glu.reshape(1, D_SSM))


def _split_bf16(x):
    hi = x.astype(BF16)
    lo = (x - hi.astype(F32)).astype(BF16)
    return hi, lo


N_MERGE_IN = 15


def _merge_kernel(*refs):
    (o_ref, y_ref, ga_ref, gs_ref, wa_ref, ws_ref, wo_ref, x_ref, gpm_ref, g1_ref, sc2_ref, sh2_ref,
     gpf_ref, wrt_ref, br_ref) = refs[:N_MERGE_IN]
    x1_ref, h2_ref, lg_ref, mix = refs[-4:]
    j = pl.program_id(1)
    a = _dot(o_ref[...].astype(BF16), wa_ref[...])
    s = _dot(y_ref[...].astype(BF16), ws_ref[...])
    merged = _sigmoid(ga_ref[...]) * a + _sigmoid(gs_ref[...]) * s
    contrib = _dot(merged.astype(BF16), wo_ref[...])

    @pl.when(j == 0)
    def _():
        mix[...] = contrib

    @pl.when(j > 0)
    def _():
        mix[...] += contrib

    @pl.when(j == pl.num_programs(1) - 1)
    def _():
        x1 = x_ref[...] + g1_ref[...] * _rms(mix[...], gpm_ref[...])
        x1_ref[...] = x1
        h2 = _rms(x1, gpf_ref[...]) * (1.0 + sc2_ref[...]) + sh2_ref[...]
        h2_ref[...] = h2
        h_hi, h_lo = _split_bf16(h2)
        w_hi, w_lo = _split_bf16(wrt_ref[...])
        lg_ref[...] = (_dot_nt(w_hi, h_hi) + _dot_nt(w_hi, h_lo) + _dot_nt(w_lo, h_hi)) + br_ref[...]


def _merge(o_attn, y_ssm, proj, x, mod, per_row, rows_per_seq, n_total, row0, shared, w_br_attn, w_br_ssm,
           w_out, g_post_mix, g_pre_ffn, w_router_t, b_router):
    rows = x.shape[0]
    tm = min(MERGE_TM, rows)
    blk0 = row0 // tm
    nk = D_MODEL // MERGE_TK
    ga0 = (Q_W + 2 * KV_W + D_SSM) // MERGE_TK
    gs0 = ga0 + nk
    row2 = lambda w: pl.BlockSpec((tm, w), lambda i, j: (i, 0))
    vec = pl.BlockSpec((1, D_MODEL), lambda i, j: (0, 0))
    in_specs = [row2(Q_W), row2(D_SSM),
                pl.BlockSpec((tm, MERGE_TK), lambda i, j: (i, ga0 + j)),
                pl.BlockSpec((tm, MERGE_TK), lambda i, j: (i, gs0 + j)),
                pl.BlockSpec((Q_W, MERGE_TK), lambda i, j: (0, j)),
                pl.BlockSpec((D_SSM, MERGE_TK), lambda i, j: (0, j)),
                pl.BlockSpec((MERGE_TK, D_MODEL), lambda i, j: (j, 0)),
                row2(D_MODEL), vec,
                _mod_spec(per_row, tm, rows_per_seq, 2, 2),
                _mod_spec(per_row, tm, rows_per_seq, 4, 2),
                _mod_spec(per_row, tm, rows_per_seq, 3, 2),
                vec,
                pl.BlockSpec((N_EXPERTS, D_MODEL), lambda i, j: (0, 0)),
                pl.BlockSpec((N_EXPERTS, 1), lambda i, j: (0, 0))]
    args = [o_attn, y_ssm, proj, proj, w_br_attn, w_br_ssm, w_out, x, g_post_mix.reshape(1, D_MODEL),
            mod, mod, mod, g_pre_ffn.reshape(1, D_MODEL), w_router_t, b_router.reshape(N_EXPERTS, 1)]
    assert len(args) == N_MERGE_IN
    aliases = {}
    if shared is not None:
        aliases = {len(args): 1, len(args) + 1: 2}
        in_specs += [pl.BlockSpec(memory_space=pl.ANY), pl.BlockSpec(memory_space=pl.ANY)]
        args += list(shared)
    return pl.pallas_call(
        _merge_kernel,
        grid=(rows // tm, nk),
        in_specs=in_specs,
        out_specs=[row2(D_MODEL),
                   pl.BlockSpec((tm, D_MODEL), lambda i, j: (blk0 + i, 0)),
                   pl.BlockSpec((N_EXPERTS, tm), lambda i, j: (0, blk0 + i))],
        out_shape=[jax.ShapeDtypeStruct((rows, D_MODEL), F32),
                   jax.ShapeDtypeStruct((n_total, D_MODEL), F32),
                   jax.ShapeDtypeStruct((N_EXPERTS, n_total), F32)],
        scratch_shapes=[pltpu.VMEM((tm, D_MODEL), F32)],
        input_output_aliases=aliases,
        compiler_params=_cparams(("arbitrary", "arbitrary")),
        name="merge",
    )(*args)


def _bucket_tables():
    ql = np.arange(WINDOW)[:, None]
    kl = np.arange(2 * WINDOW)[None, :]
    dist = ql + WINDOW - kl
    prompt = np.where((dist >= 0) & (dist < WINDOW), _t5_bucket_np(dist), -1).astype(np.int32)
    d_s = (WINDOW - 1 - np.arange(WINDOW))[None, :]
    sample = np.broadcast_to(_t5_bucket_np(d_s), (SUBLANE, WINDOW)).astype(np.int32)
    return prompt, sample


def _front(p):
    l = 0
    batch, seq, _ = p['x_prompt'].shape
    nseq = p['x_sample'].shape[0]
    xp = p['x_prompt'].reshape(batch * seq, D_MODEL)
    xs = p['x_sample'].reshape(nseq, D_MODEL)

    c_all = jnp.concatenate([p['c_prompt'], p['c_sample'],
                             jnp.zeros((SUBLANE - (batch + nseq) % SUBLANE, D_MODEL), F32)], axis=0)
    mod = _ada(c_all, p['w_ada'][l], p['b_ada'][l])
    mod_p = mod[:batch].reshape(batch, 1, 6 * D_MODEL)
    mod_s = mod[batch:batch + nseq]

    bucket_p, bucket_s = _bucket_tables()
    bias_p = _bias_table(bucket_p, p['rel_bias'])
    bias_s = _bias_table(bucket_s, p['rel_bias'])[:, 0, :]
    sinks = p['attn_sinks'][l]

    w_in_t = p['w_in'][l].astype(BF16)
    proj_p = _proj(xp, mod_p, False, seq, p['g_pre_mix'][l], w_in_t)
    proj_s = _proj(xs, mod_s, True, 1, p['g_pre_mix'][l], w_in_t)

    o_p = _attn_prompt(proj_p, batch, seq, bias_p, sinks)
    kv_p = proj_p.reshape(batch, seq, IN_W)[:, seq - WINDOW:, Q_W:Q_W + 2 * KV_W]
    new_k_p = kv_p[..., :KV_W].reshape(1, batch, WINDOW, N_KV_HEADS, HEAD_DIM)
    new_v_p = kv_p[..., KV_W:].reshape(1, batch, WINDOW, N_KV_HEADS, HEAD_DIM)
    o_s3, new_k_s, new_v_s = _attn_sample(
        proj_s[:, :Q_W].reshape(nseq, N_HEADS, HEAD_DIM),
        proj_s[:, Q_W:Q_W + KV_W].reshape(nseq, 1, KV_W),
        proj_s[:, Q_W + KV_W:Q_W + 2 * KV_W].reshape(nseq, 1, KV_W),
        p['cache_win_k'][l].reshape(nseq, WINDOW, KV_W),
        p['cache_win_v'][l].reshape(nseq, WINDOW, KV_W), bias_s, sinks)
    o_s = o_s3.reshape(nseq, Q_W)

    lbr, lbi, bbr, bbi = _ssm_disc(p['ssm_a_re'][l], p['ssm_a_im'][l], p['ssm_log_dt'][l],
                                   p['ssm_b_re'][l], p['ssm_b_im'][l])
    lbr = lbr.reshape(1, SSM_W)
    lbi = lbi.reshape(1, SSM_W)
    bb = jnp.concatenate([_block_diag_tiles(bbr), _block_diag_tiles(bbi)], axis=-1).astype(BF16)
    c_re_t = p['ssm_c_re'][l].transpose(0, 2, 1)
    c_im_t = p['ssm_c_im'][l].transpose(0, 2, 1)
    cc = jnp.concatenate([_block_diag_tiles(c_re_t), -_block_diag_tiles(c_im_t)], axis=1).astype(BF16)
    u0 = Q_W + 2 * KV_W
    zeros = jnp.zeros((batch, SSM_W), F32)
    w_glu = p['w_glu'][l].astype(BF16)
    y_p3, hr_p, hi_p = _ssm(proj_p.reshape(batch, seq, IN_W), zeros, zeros, batch, SSM_TC, lbr, lbi, bb, cc,
                            p['ssm_d'][l], w_glu, p['b_glu'][l], seq_major=True)
    y_p = y_p3.reshape(batch * seq, D_SSM)
    y_s, hr_s, hi_s = _ssm(proj_s[:, u0:u0 + D_SSM], p['state_ssm_re'][l].reshape(nseq, SSM_W),
                           p['state_ssm_im'][l].reshape(nseq, SSM_W), nseq, 1, lbr, lbi, bb, cc,
                           p['ssm_d'][l], w_glu, p['b_glu'][l])

    wa_t = p['w_br_attn'][l].astype(BF16)
    ws_t = p['w_br_ssm'][l].astype(BF16)
    wo = p['w_out'][l].astype(BF16)
    n_total = batch * seq + nseq
    merge = functools.partial(_merge, w_br_attn=wa_t, w_br_ssm=ws_t, w_out=wo, g_post_mix=p['g_post_mix'][l],
                              g_pre_ffn=p['g_pre_ffn'][l], w_router_t=p['w_router'][l].T,
                              b_router=p['b_router'][l])
    x1_p, h2_buf, lg_buf = merge(o_p, y_p, proj_p, xp, mod_p, False, seq, n_total, 0, None)
    x1_s, h2_all, lg_all = merge(o_s, y_s, proj_s, xs, mod_s, True, 1, n_total, batch * seq, (h2_buf, lg_buf))

    st = lambda h, n: h.reshape(1, n, N_SSM_GROUPS, SSM_STATE)
    return dict(
        mod_p=mod_p, mod_s=mod_s, h2_all=h2_all, lg_all=lg_all,
        p=dict(proj=proj_p, o_attn=o_p, new_k=new_k_p, new_v=new_v_p, y_ssm=y_p, h_re=st(hr_p, batch),
               h_im=st(hi_p, batch), x1=x1_p),
        s=dict(proj=proj_s, o_attn=o_s, new_k=new_k_s.reshape(1, nseq, WINDOW, N_KV_HEADS, HEAD_DIM),
               new_v=new_v_s.reshape(1, nseq, WINDOW, N_KV_HEADS, HEAD_DIM), y_ssm=y_s,
               h_re=st(hr_s, nseq), h_im=st(hi_s, nseq), x1=x1_s))


def _count_steps(c, step, n_max):
    out = jnp.zeros_like(c)
    for q in range(-(-n_max // step)):
        out = out + jnp.where(c > float(q * step), 1.0, 0.0)
    return out


def _route_kernel(lg_ref, pos_ref, gate_ref, texp_ref, trows_ref, ntiles_ref, pstart_ref, plen_ref):
    lg = lg_ref[...]
    e, tn = lg.shape
    erow = lax.broadcasted_iota(I32, (e, tn), 0).astype(F32)
    work = lg
    vals, hits = [], []
    for _ in range(TOP_K):
        m = jnp.max(work, axis=0, keepdims=True)
        idx = jnp.min(jnp.where(work == m, erow, float(e)), axis=0, keepdims=True)
        hit = erow == idx
        vals.append(m)
        hits.append(hit)
        work = jnp.where(hit, -jnp.inf, work)
    ex = [jnp.exp(v - vals[0]) for v in vals]
    den = ex[0] + ex[1] + ex[2] + ex[3]
    gate_ref[...] = jnp.concatenate([x / den for x in ex], axis=0)

    chosen = jnp.zeros((e, tn), F32)
    for hit in hits:
        chosen = chosen + jnp.where(hit, 1.0, 0.0)
    chosen_b = chosen.astype(BF16)
    tri = (lax.broadcasted_iota(I32, (LANE, LANE), 0) <= lax.broadcasted_iota(I32, (LANE, LANE), 1))
    tri = jnp.where(tri, 1.0, 0.0).astype(BF16)
    carry = jnp.zeros((e, 1), F32)
    ranks = []
    for b in range(tn // LANE):
        blk = chosen[:, b * LANE:(b + 1) * LANE]
        inc = _dot(chosen_b[:, b * LANE:(b + 1) * LANE], tri) + carry
        ranks.append(inc - blk)
        carry = inc[:, LANE - 1:LANE]
    rank = jnp.concatenate(ranks, axis=1)
    cnt_col = carry
    cnt_row = _dot_nt(jnp.ones((SUBLANE, tn), BF16), chosen_b)[0:1, :]

    tiles_col = _count_steps(cnt_col, EXP_TM, tn)
    tiles_row = _count_steps(cnt_row, EXP_TM, tn)
    ee_r = lax.broadcasted_iota(I32, (e, e), 0)
    ee_c = lax.broadcasted_iota(I32, (e, e), 1)
    tstart_col = jnp.sum(jnp.where(ee_c < ee_r, tiles_row, 0.0), axis=1, keepdims=True)
    ntiles = jnp.sum(tiles_row, axis=1, keepdims=True)
    rstart_col = tstart_col * float(EXP_TM)
    pos = [jnp.sum(jnp.where(hit, rstart_col + rank, 0.0), axis=0, keepdims=True) for hit in hits]
    pos_ref[...] = jnp.concatenate(pos, axis=0).astype(I32)

    mm = lax.broadcasted_iota(I32, (e, LANE), 1).astype(F32)
    e_col = lax.broadcasted_iota(I32, (e, LANE), 0).astype(F32)
    own = (mm >= tstart_col) & (mm < tstart_col + tiles_col)
    texp = jnp.sum(jnp.where(own, e_col, 0.0), axis=0, keepdims=True)
    rows_here = jnp.minimum(float(EXP_TM), cnt_col - (mm - tstart_col) * float(EXP_TM))
    trows = jnp.sum(jnp.where(own, rows_here, 0.0), axis=0, keepdims=True)
    last_e = jnp.max(jnp.where(tiles_col > 0.0, e_col, 0.0), axis=0, keepdims=True)
    texp = jnp.where(mm[0:1, :] < ntiles, texp, last_e)
    texp_ref[...] = texp.astype(I32)
    trows_ref[...] = trows.astype(I32)
    ntiles_ref[...] = jnp.broadcast_to(ntiles, (1, LANE)).astype(I32)
    nsub_col = _count_steps(cnt_col, EXP_SUB, tn)
    pstart_ref[...] = jnp.broadcast_to(rstart_col + cnt_col, (e, LANE)).astype(I32)
    plen_ref[...] = jnp.broadcast_to(nsub_col * float(EXP_SUB) - cnt_col, (e, LANE)).astype(I32)


def _route(lg_t):
    e, tn = lg_t.shape
    i32 = lambda shape: jax.ShapeDtypeStruct(shape, I32)
    return pl.pallas_call(
        _route_kernel,
        out_shape=[i32((TOP_K, tn)), jax.ShapeDtypeStruct((TOP_K, tn), F32),
                   i32((1, LANE)), i32((1, LANE)), i32((1, LANE)), i32((e, LANE)), i32((e, LANE))],
        compiler_params=pltpu.CompilerParams(vmem_limit_bytes=VMEM_LIMIT),
        name="route",
    )(lg_t)


def _max_tiles(n_tok):
    return (n_tok * TOP_K) // EXP_TM + N_EXPERTS


def _dispatch_kernel(pos_ref, pstart_ref, plen_ref, h2_ref, zero_ref, xs_ref, sem):
    i = pl.program_id(0)

    def row_copy(src, s, d):
        return pltpu.make_async_copy(src.at[pl.ds(s, 1)], xs_ref.at[pl.ds(d, 1)], sem)

    for t in range(TOK_BLK):
        for k in range(TOP_K):
            row_copy(h2_ref, t, pos_ref[k, t]).start(priority=k % 2)
    for _ in range(TOP_K):
        pltpu.make_async_copy(h2_ref, xs_ref.at[pl.ds(0, TOK_BLK)], sem).wait()

    @pl.when(i == 0)
    def _():
        def per_expert(ex, c):
            n = plen_ref[ex]
            s = pstart_ref[ex]

            def zissue(r, cc):
                row_copy(zero_ref, 0, s + r).start()
                return cc

            def zdrain(r, cc):
                row_copy(zero_ref, 0, 0).wait()
                return cc

            lax.fori_loop(0, n, zissue, 0)
            lax.fori_loop(0, n, zdrain, 0)
            return c

        lax.fori_loop(0, N_EXPERTS, per_expert, 0)


def _dispatch(pos3, pstart, plen, h2_all):
    n_tok = h2_all.shape[0]
    n_rows = _max_tiles(n_tok) * EXP_TM
    smem = pl.BlockSpec(memory_space=pltpu.SMEM)
    hbm = pl.BlockSpec(memory_space=pl.ANY)
    return pl.pallas_call(
        _dispatch_kernel,
        grid=(n_tok // TOK_BLK,),
        in_specs=[pl.BlockSpec((None, TOP_K, TOK_BLK), lambda i: (i, 0, 0), memory_space=pltpu.SMEM),
                  smem, smem,
                  pl.BlockSpec((TOK_BLK, D_MODEL), lambda i: (i, 0)),
                  pl.BlockSpec((SUBLANE, D_MODEL), lambda i: (0, 0))],
        out_specs=hbm,
        out_shape=jax.ShapeDtypeStruct((n_rows, D_MODEL), F32),
        scratch_shapes=[pltpu.SemaphoreType.DMA(())],
        compiler_params=_cparams(("arbitrary",)),
        name="dispatch",
    )(pos3, pstart, plen, h2_all, jnp.zeros((SUBLANE, D_MODEL), F32))


def _expert_kernel(texp_ref, trows_ref, nt_ref, xs_hbm, wg_ref, wl_ref, bg_ref, bl_ref, wd_ref, bd_ref,
                   o_ref, x_ref, x_sem, xb_scr, act_scr, wg_scr, wl_scr, wd_scr):
    m = pl.program_id(0)
    s = pl.program_id(1)
    n_tiles = nt_ref[0]
    valid = m < n_tiles

    rows = trows_ref[m]
    nbig = lax.shift_right_logical(rows, int(math.log2(EXP_BIG)))
    big_rows = nbig * EXP_BIG
    nsmall = lax.shift_right_logical(rows - big_rows + (EXP_SUB - 1), int(math.log2(EXP_SUB)))
    nsub_done = nbig * (EXP_BIG // EXP_SUB) + nsmall

    def x_copy(tile):
        return pltpu.make_async_copy(xs_hbm.at[pl.ds(pl.multiple_of(tile * EXP_TM, EXP_TM), EXP_TM)],
                                     x_ref, x_sem)

    @pl.when((m == 0) & (s == 0))
    def _():
        x_copy(0).start()

    @pl.when(valid & (s == 0))
    def _():
        x_copy(m).wait()

        def to_bf16(r, c):
            r0 = pl.multiple_of(r * EXP_SUB, EXP_SUB)
            xb_scr[pl.ds(r0, EXP_SUB), :] = x_ref[pl.ds(r0, EXP_SUB), :].astype(BF16)
            return c

        lax.fori_loop(0, nsub_done, to_bf16, 0)

        @pl.when(m + 1 < n_tiles)
        def _():
            x_copy(m + 1).start()

    def over_rows(first, step):
        def big(r, c):
            step(pl.multiple_of(r * EXP_BIG, EXP_BIG), EXP_BIG)
            return c

        def small(r, c):
            step(pl.multiple_of(big_rows + r * EXP_SUB, EXP_SUB), EXP_SUB)
            return c

        @pl.when(nbig > 0)
        def _():
            first(EXP_BIG)
            lax.fori_loop(1, nbig, big, 0)
            lax.fori_loop(0, nsmall, small, 0)

        @pl.when(nbig == 0)
        def _():
            first(EXP_SUB)
            lax.fori_loop(1, nsmall, small, 0)

    @pl.when(valid & (s < EXP_NF))
    def _():
        def finish(r0, n, hg, hl):
            x_glu = jnp.minimum(hg, SWIGLU_LIMIT)
            x_lin = jnp.clip(hl, -SWIGLU_LIMIT, SWIGLU_LIMIT)
            act = x_glu * _sigmoid(SWIGLU_ALPHA * x_glu) * (x_lin + 1.0)
            act_scr[s, pl.ds(r0, n), :] = act.astype(BF16)

        def first(n):
            xb = xb_scr[0:n, :]
            hg = jnp.broadcast_to(bg_ref[...], (n, EXP_TF))
            hl = jnp.broadcast_to(bl_ref[...], (n, EXP_TF))
            for q in range(D_MODEL // EXP_KQ):
                ks = slice(q * EXP_KQ, (q + 1) * EXP_KQ)
                wgq = wg_ref[ks, :].astype(BF16)
                wlq = wl_ref[ks, :].astype(BF16)
                wg_scr[ks, :] = wgq
                wl_scr[ks, :] = wlq
                hg = hg + _dot(xb[:, ks], wgq)
                hl = hl + _dot(xb[:, ks], wlq)
            finish(0, n, hg, hl)

        def step(r0, n):
            xb = xb_scr[pl.ds(r0, n), :]
            finish(r0, n, _dot(xb, wg_scr[...]) + bg_ref[...], _dot(xb, wl_scr[...]) + bl_ref[...])

        over_rows(first, step)

    @pl.when(valid & (s >= EXP_NF))
    def _():
        def first(n):
            acc = jnp.broadcast_to(bd_ref[...], (n, EXP_TD))
            for f in range(EXP_NF):
                fs = slice(f * EXP_TF, (f + 1) * EXP_TF)
                wdq = wd_ref[fs, :].astype(BF16)
                wd_scr[fs, :] = wdq
                acc = acc + _dot(act_scr[f, 0:n, :], wdq)
            o_ref[0:n, :] = acc

        def step(r0, n):
            acc = jnp.broadcast_to(bd_ref[...], (n, EXP_TD))
            for f in range(EXP_NF):
                acc = acc + _dot(act_scr[f, pl.ds(r0, n), :], wd_scr[f * EXP_TF:(f + 1) * EXP_TF, :])
            o_ref[pl.ds(r0, n), :] = acc

        def zero(r, c):
            r0 = pl.multiple_of(r * EXP_SUB, EXP_SUB)
            o_ref[pl.ds(r0, EXP_SUB), :] = jnp.zeros((EXP_SUB, EXP_TD), F32)
            return c

        over_rows(first, step)
        lax.fori_loop(nsub_done, EXP_TM // EXP_SUB, zero, 0)


def _experts(texp, trows, ntiles, xs, w_gate_up, b_gate_up, w_down, b_down):
    n_tiles = xs.shape[0] // EXP_TM
    nsteps = EXP_NF + EXP_ND

    def tile(m, nt):
        return jnp.minimum(m, nt[0] - 1)

    def ea(m, s, te, nt):
        return te[jnp.where(s < EXP_NF, m, jnp.minimum(m + 1, nt[0] - 1))]

    def fa(m, s, nt):
        return jnp.where((m < nt[0]) & (s < EXP_NF), s, 0)

    def fb(m, s, nt):
        return jnp.where(m < nt[0], jnp.maximum(s - EXP_NF, 0), EXP_ND - 1)

    grid_spec = pltpu.PrefetchScalarGridSpec(
        num_scalar_prefetch=3,
        grid=(n_tiles, nsteps),
        in_specs=[
            pl.BlockSpec(memory_space=pl.ANY),
            pl.BlockSpec((None, D_MODEL, EXP_TF),
                         lambda m, s, te, tr, nt: (ea(m, s, te, nt), 0, fa(m, s, nt))),
            pl.BlockSpec((None, D_MODEL, EXP_TF),
                         lambda m, s, te, tr, nt: (ea(m, s, te, nt), 0, EXP_NF + fa(m, s, nt))),
            pl.BlockSpec((None, 1, EXP_TF), lambda m, s, te, tr, nt: (ea(m, s, te, nt), 0, fa(m, s, nt))),
            pl.BlockSpec((None, 1, EXP_TF),
                         lambda m, s, te, tr, nt: (ea(m, s, te, nt), 0, EXP_NF + fa(m, s, nt))),
            pl.BlockSpec((None, D_FF, EXP_TD), lambda m, s, te, tr, nt: (te[m], 0, fb(m, s, nt))),
            pl.BlockSpec((None, 1, EXP_TD), lambda m, s, te, tr, nt: (te[m], 0, fb(m, s, nt))),
        ],
        out_specs=pl.BlockSpec((EXP_TM, EXP_TD), lambda m, s, te, tr, nt: (tile(m, nt), fb(m, s, nt))),
        scratch_shapes=[pltpu.VMEM((EXP_TM, D_MODEL), F32), pltpu.SemaphoreType.DMA(()),
                        pltpu.VMEM((EXP_TM, D_MODEL), BF16),
                        pltpu.VMEM((EXP_NF, EXP_TM, EXP_TF), BF16),
                        pltpu.VMEM((D_MODEL, EXP_TF), BF16), pltpu.VMEM((D_MODEL, EXP_TF), BF16),
                        pltpu.VMEM((D_FF, EXP_TD), BF16)],
    )
    return pl.pallas_call(
        _expert_kernel,
        grid_spec=grid_spec,
        out_shape=jax.ShapeDtypeStruct((xs.shape[0], D_MODEL), F32),
        compiler_params=pltpu.CompilerParams(dimension_semantics=("arbitrary", "arbitrary"),
                                             vmem_limit_bytes=EXP_VMEM_LIMIT),
        name="experts",
    )(texp, trows, ntiles, xs, w_gate_up, w_gate_up,
      b_gate_up.reshape(N_EXPERTS, 1, 2 * D_FF), b_gate_up.reshape(N_EXPERTS, 1, 2 * D_FF),
      w_down, b_down.reshape(N_EXPERTS, 1, D_MODEL))


def _combine_kernel(pos_ref, pos_next_ref, gate_ref, ys_ref, x1_ref, g2_ref, gpf_ref, o_ref, buf, sem):
    i = pl.program_id(0)
    slot = lax.rem(i, 2)

    def gather(p_ref, sl):
        for t in range(TOK_BLK):
            for k in range(TOP_K):
                pltpu.make_async_copy(ys_ref.at[pl.ds(p_ref[k, t], 1)], buf.at[sl, k, pl.ds(t, 1)],
                                      sem.at[sl]).start(priority=k % 2)

    @pl.when(i == 0)
    def _():
        gather(pos_ref, 0)

    @pl.when(i + 1 < pl.num_programs(0))
    def _():
        gather(pos_next_ref, 1 - slot)

    for k in range(TOP_K):
        pltpu.make_async_copy(ys_ref.at[pl.ds(0, TOK_BLK)], buf.at[slot, k], sem.at[slot]).wait()
    g = gate_ref[...]
    f = g[:, 0:1] * buf[slot, 0]
    for k in range(1, TOP_K):
        f = f + g[:, k:k + 1] * buf[slot, k]
    o_ref[...] = x1_ref[...] + g2_ref[...] * _rms(f, gpf_ref[...])


def _combine(pos3, gates_t, ys, x1, mod, per_row, rows_per_seq, g_post_ffn, blk0):
    rows = x1.shape[0]
    nblk = rows // TOK_BLK
    return pl.pallas_call(
        _combine_kernel,
        grid=(nblk,),
        in_specs=[pl.BlockSpec((None, TOP_K, TOK_BLK), lambda i: (blk0 + i, 0, 0), memory_space=pltpu.SMEM),
                  pl.BlockSpec((None, TOP_K, TOK_BLK), lambda i: (blk0 + jnp.minimum(i + 1, nblk - 1), 0, 0),
                               memory_space=pltpu.SMEM),
                  pl.BlockSpec((TOK_BLK, TOP_K), lambda i: (blk0 + i, 0)),
                  pl.BlockSpec(memory_space=pl.ANY),
                  pl.BlockSpec((TOK_BLK, D_MODEL), lambda i: (i, 0)),
                  _mod_spec(per_row, TOK_BLK, rows_per_seq, 5, 1),
                  pl.BlockSpec((1, D_MODEL), lambda i: (0, 0))],
        out_specs=pl.BlockSpec((TOK_BLK, D_MODEL), lambda i: (i, 0)),
        out_shape=jax.ShapeDtypeStruct((rows, D_MODEL), F32),
        scratch_shapes=[pltpu.VMEM((2, TOP_K, TOK_BLK, D_MODEL), F32), pltpu.SemaphoreType.DMA((2,))],
        compiler_params=_cparams(("arbitrary",)),
        name="combine",
    )(pos3, pos3, gates_t, ys, x1, mod, g_post_ffn.reshape(1, D_MODEL))


def _moe(fr, p, batch, seq, nseq):
    l = 0
    h2_all, lg_all = fr['h2_all'], fr['lg_all']
    n_tok = h2_all.shape[0]
    pos, gates, texp, trows, ntiles, pstart, plen = _route(lg_all)
    pos3 = pos.reshape(TOP_K, n_tok // TOK_BLK, TOK_BLK).transpose(1, 0, 2)
    xs = _dispatch(pos3, pstart[:, 0], plen[:, 0], h2_all)
    ys = _experts(texp[0], trows[0], ntiles[0, :1], xs, p['w_gate_up'][l], p['b_gate_up'][l],
                  p['w_down'][l], p['b_down'][l])
    gates_t = gates.T
    y_p = _combine(pos3, gates_t, ys, fr['p']['x1'], fr['mod_p'], False, seq, p['g_post_ffn'][l], 0)
    y_s = _combine(pos3, gates_t, ys, fr['s']['x1'], fr['mod_s'], True, 1, p['g_post_ffn'][l],
                   batch * seq // TOK_BLK)
    return y_p, y_s


def kernel(x_prompt, x_sample, c_prompt, c_sample, cache_win_k, cache_win_v, state_ssm_re, state_ssm_im, w_ada, b_ada, g_pre_mix, g_post_mix, g_pre_ffn, g_post_ffn, w_in, attn_sinks, rel_bias, ssm_a_re, ssm_a_im, ssm_log_dt, ssm_b_re, ssm_b_im, ssm_c_re, ssm_c_im, ssm_d, w_glu, b_glu, w_br_attn, w_br_ssm, w_out, w_router, b_router, w_gate_up, b_gate_up, w_down, b_down):
    p = dict(locals())
    batch, seq, _ = x_prompt.shape
    nseq = x_sample.shape[0]
    fr = _front(p)
    y_p, y_s = _moe(fr, p, batch, seq, nseq)
    fp, fs = fr['p'], fr['s']
    return (y_p.reshape(batch, seq, D_MODEL), y_s.reshape(nseq, 1, D_MODEL),
            fp['new_k'], fp['new_v'], fp['h_re'], fp['h_im'],
            fs['new_k'], fs['new_v'], fs['h_re'], fs['h_im'])
```

```python
import functools
import math

import numpy as np
import jax
import jax.numpy as jnp
from jax import lax
from jax.experimental import pallas as pl
from jax.experimental.pallas import tpu as pltpu

F32 = jnp.float32
BF16 = jnp.bfloat16
I32 = jnp.int32

D_MODEL = 2048
N_HEADS = 16
N_KV_HEADS = 4
HEAD_DIM = 64
Q_GROUP = N_HEADS // N_KV_HEADS
WINDOW = 128
N_BUCKETS = 32
MAX_EXACT = N_BUCKETS // 2
MAX_DISTANCE = 128
D_SSM = 1024
SSM_GROUP = 16
N_SSM_GROUPS = 64
SSM_STATE = 64
N_EXPERTS = 32
TOP_K = 4
D_FF = 2048
SWIGLU_LIMIT = 7.0
SWIGLU_ALPHA = 1.702
NORM_EPS = 1e-6
NEG_INF = -1e30
Q_W = N_HEADS * HEAD_DIM
KV_W = N_KV_HEADS * HEAD_DIM
IN_W = Q_W + 2 * KV_W + D_SSM + 2 * D_MODEL
SSM_W = N_SSM_GROUPS * SSM_STATE

LANE = 128
SUBLANE = 8
VMEM_LIMIT = 56 * 1024 * 1024
EXP_VMEM_LIMIT = 60 * 1024 * 1024

PROJ_TM = 1024
PROJ_TN = 512
MERGE_TM = 512
MERGE_TK = 512
SSM_TC = 128
SSM_LB = 512
SSM_TILES = D_SSM // LANE
TOK_BLK = 128
EXP_TM = 1152
EXP_BIG = 512
EXP_SUB = 128
EXP_TF = 512
EXP_TD = 512
EXP_KQ = 512
EXP_NF = D_FF // EXP_TF
EXP_ND = D_MODEL // EXP_TD


def _cparams(sem):
    return pltpu.CompilerParams(dimension_semantics=sem, vmem_limit_bytes=VMEM_LIMIT)


def _sigmoid(x):
    return 1.0 / (1.0 + jnp.exp(-x))


def _rms(x, g):
    return x * lax.rsqrt(jnp.mean(x * x, axis=-1, keepdims=True) + NORM_EPS) * g


def _dot(a, b):
    return jnp.dot(a, b, preferred_element_type=F32)


def _dot_nt(a, b):
    return lax.dot_general(a, b, (((1,), (1,)), ((), ())), preferred_element_type=F32)


def _ada_kernel(c_ref, w_ref, b_ref, o_ref):
    c = c_ref[...]
    s = (c * _sigmoid(c)).astype(BF16)
    o_ref[...] = _dot(s, w_ref[...].astype(BF16)) + b_ref[...]


def _ada(c_all, w_ada, b_ada):
    rows = c_all.shape[0]
    tn = 1024
    n = w_ada.shape[1]
    return pl.pallas_call(
        _ada_kernel,
        grid=(n // tn,),
        in_specs=[pl.BlockSpec((rows, D_MODEL), lambda j: (0, 0)),
                  pl.BlockSpec((D_MODEL, tn), lambda j: (0, j)),
                  pl.BlockSpec((1, tn), lambda j: (0, j))],
        out_specs=pl.BlockSpec((rows, tn), lambda j: (0, j)),
        out_shape=jax.ShapeDtypeStruct((rows, n), F32),
        compiler_params=_cparams(("arbitrary",)),
        name="ada",
    )(c_all, w_ada, b_ada.reshape(1, n))


def _t5_bucket_np(dist):
    n = np.maximum(dist, 0)
    nf = np.maximum(n, 1).astype(np.float64)
    large = MAX_EXACT + (np.log(nf / MAX_EXACT) / math.log(MAX_DISTANCE / MAX_EXACT)
                         * (N_BUCKETS - MAX_EXACT)).astype(np.int32)
    large = np.minimum(large, N_BUCKETS - 1)
    return np.where(n < MAX_EXACT, n, large).astype(np.int32)


def _bias_kernel(bucket_ref, rb_ref, o_ref):
    h = pl.program_id(0)
    bucket = bucket_ref[...]
    acc = jnp.full(bucket.shape, NEG_INF, F32)
    for b in range(N_BUCKETS):
        acc = jnp.where(bucket == b, rb_ref[b, h], acc)
    o_ref[...] = acc


def _bias_table(bucket_np, rel_bias):
    r, c = bucket_np.shape
    return pl.pallas_call(
        _bias_kernel,
        grid=(N_HEADS,),
        in_specs=[pl.BlockSpec((r, c), lambda h: (0, 0)),
                  pl.BlockSpec(memory_space=pltpu.SMEM)],
        out_specs=pl.BlockSpec((None, r, c), lambda h: (h, 0, 0)),
        out_shape=jax.ShapeDtypeStruct((N_HEADS, r, c), F32),
        compiler_params=_cparams(("arbitrary",)),
        name="bias",
    )(jnp.asarray(bucket_np), rel_bias)


def _proj_kernel(x_ref, sc_ref, sh_ref, g_ref, w_ref, o_ref, h_scr):
    @pl.when(pl.program_id(1) == 0)
    def _():
        h = _rms(x_ref[...], g_ref[...]) * (1.0 + sc_ref[...]) + sh_ref[...]
        h_scr[...] = h.astype(BF16)

    o_ref[...] = _dot(h_scr[...], w_ref[...])


def _mod_spec(per_row, tm, rows_per_seq, col, nargs):
    if per_row:
        if nargs == 1:
            return pl.BlockSpec((tm, D_MODEL), lambda i: (i, col))
        return pl.BlockSpec((tm, D_MODEL), lambda i, j: (i, col))
    tiles_per_seq = rows_per_seq // tm
    if nargs == 1:
        return pl.BlockSpec((None, 1, D_MODEL), lambda i: (i // tiles_per_seq, 0, col))
    return pl.BlockSpec((None, 1, D_MODEL), lambda i, j: (i // tiles_per_seq, 0, col))


def _proj(x, mod, per_row, rows_per_seq, g_pre, w_in):
    rows = x.shape[0]
    tm = min(PROJ_TM, rows)
    return pl.pallas_call(
        _proj_kernel,
        grid=(rows // tm, IN_W // PROJ_TN),
        in_specs=[pl.BlockSpec((tm, D_MODEL), lambda i, j: (i, 0)),
                  _mod_spec(per_row, tm, rows_per_seq, 1, 2),
                  _mod_spec(per_row, tm, rows_per_seq, 0, 2),
                  pl.BlockSpec((1, D_MODEL), lambda i, j: (0, 0)),
                  pl.BlockSpec((D_MODEL, PROJ_TN), lambda i, j: (0, j))],
        out_specs=pl.BlockSpec((tm, PROJ_TN), lambda i, j: (i, j)),
        out_shape=jax.ShapeDtypeStruct((rows, IN_W), F32),
        scratch_shapes=[pltpu.VMEM((tm, D_MODEL), BF16)],
        compiler_params=_cparams(("arbitrary", "arbitrary")),
        name="proj",
    )(x, mod, mod, g_pre.reshape(1, D_MODEL), w_in)


def _attn_prompt_kernel(q_ref, kc_ref, kp_ref, vc_ref, vp_ref, bias_ref, sink_ref, o_ref):
    q = q_ref[...] * (HEAD_DIM ** -0.5)
    k = jnp.concatenate([kp_ref[...], kc_ref[...]], axis=0)
    v = jnp.concatenate([vp_ref[...], vc_ref[...]], axis=0)
    outs = []
    for g in range(N_KV_HEADS):
        kg = k[:, g * HEAD_DIM:(g + 1) * HEAD_DIM].astype(BF16)
        vg = v[:, g * HEAD_DIM:(g + 1) * HEAD_DIM].astype(BF16)
        for hh in range(Q_GROUP):
            h = g * Q_GROUP + hh
            qh = q[:, h * HEAD_DIM:(h + 1) * HEAD_DIM].astype(BF16)
            s = _dot_nt(qh, kg) + bias_ref[h]
            sink = sink_ref[h]
            m = jnp.maximum(jnp.max(s, axis=-1, keepdims=True), sink)
            p = jnp.exp(s - m)
            den = jnp.sum(p, axis=-1, keepdims=True) + jnp.exp(sink - m)
            outs.append(_dot(p.astype(BF16), vg) / den)
    o_ref[...] = jnp.concatenate(outs, axis=-1)


def _attn_prompt(proj, batch, seq, bias, sinks):
    nb = seq // WINDOW
    kcol = Q_W // KV_W
    vcol = kcol + 1
    cur = lambda c: (lambda b, n: (b * nb + n, c))
    prev = lambda c: (lambda b, n: (b * nb + jnp.maximum(n - 1, 0), c))
    return pl.pallas_call(
        _attn_prompt_kernel,
        grid=(batch, nb),
        in_specs=[pl.BlockSpec((WINDOW, Q_W), cur(0)),
                  pl.BlockSpec((WINDOW, KV_W), cur(kcol)),
                  pl.BlockSpec((WINDOW, KV_W), prev(kcol)),
                  pl.BlockSpec((WINDOW, KV_W), cur(vcol)),
                  pl.BlockSpec((WINDOW, KV_W), prev(vcol)),
                  pl.BlockSpec((None, N_HEADS, WINDOW, 2 * WINDOW), lambda b, n: (jnp.minimum(n, 1), 0, 0, 0)),
                  pl.BlockSpec(memory_space=pltpu.SMEM)],
        out_specs=pl.BlockSpec((WINDOW, Q_W), lambda b, n: (b * nb + n, 0)),
        out_shape=jax.ShapeDtypeStruct((batch * seq, Q_W), F32),
        compiler_params=_cparams(("arbitrary", "arbitrary")),
        name="attn_prompt",
    )(proj, proj, proj, proj, proj, bias, sinks)


def _attn_sample_kernel(q_ref, kn_ref, vn_ref, ck_ref, cv_ref, bias_ref, sink_ref,
                        o_ref, nk_ref, nv_ref):
    tb = q_ref.shape[0]
    row = lax.broadcasted_iota(I32, (tb, WINDOW, KV_W), 1)
    last = row == WINDOW - 1
    nk = jnp.where(last, kn_ref[...], pltpu.roll(ck_ref[...], WINDOW - 1, 1))
    nv = jnp.where(last, vn_ref[...], pltpu.roll(cv_ref[...], WINDOW - 1, 1))
    nk_ref[...] = nk
    nv_ref[...] = nv
    lane_grp = lax.broadcasted_iota(I32, (N_HEADS, KV_W), 1) // HEAD_DIM
    head_grp = lax.broadcasted_iota(I32, (N_HEADS, KV_W), 0) // Q_GROUP
    gmask = (lane_grp == head_grp).astype(F32)
    q = q_ref[...]
    qrow = jnp.concatenate([q] * N_KV_HEADS, axis=-1) * gmask
    s = jnp.einsum('bhc,brc->bhr', qrow.astype(BF16), nk.astype(BF16),
                   preferred_element_type=F32) * (HEAD_DIM ** -0.5)
    s = s + bias_ref[...]
    sink = sink_ref[...]
    m = jnp.maximum(jnp.max(s, axis=-1, keepdims=True), sink)
    p = jnp.exp(s - m)
    den = jnp.sum(p, axis=-1, keepdims=True) + jnp.exp(sink - m)
    o = jnp.einsum('bhr,brc->bhc', p.astype(BF16), nv.astype(BF16),
                   preferred_element_type=F32) * gmask
    o64 = o[..., 0:HEAD_DIM]
    for g in range(1, N_KV_HEADS):
        o64 = o64 + o[..., g * HEAD_DIM:(g + 1) * HEAD_DIM]
    o_ref[...] = o64 / den


def _attn_sample(q3, kn, vn, cache_k, cache_v, bias, sinks):
    nseq = q3.shape[0]
    tb = 16
    seq3 = lambda w: pl.BlockSpec((tb, WINDOW, w), lambda i: (i, 0, 0))
    return pl.pallas_call(
        _attn_sample_kernel,
        grid=(nseq // tb,),
        in_specs=[pl.BlockSpec((tb, N_HEADS, HEAD_DIM), lambda i: (i, 0, 0)),
                  pl.BlockSpec((tb, 1, KV_W), lambda i: (i, 0, 0)),
                  pl.BlockSpec((tb, 1, KV_W), lambda i: (i, 0, 0)),
                  seq3(KV_W), seq3(KV_W),
                  pl.BlockSpec((N_HEADS, WINDOW), lambda i: (0, 0)),
                  pl.BlockSpec((N_HEADS, 1), lambda i: (0, 0))],
        out_specs=[pl.BlockSpec((tb, N_HEADS, HEAD_DIM), lambda i: (i, 0, 0)),
                   seq3(KV_W), seq3(KV_W)],
        out_shape=[jax.ShapeDtypeStruct((nseq, N_HEADS, HEAD_DIM), F32),
                   jax.ShapeDtypeStruct((nseq, WINDOW, KV_W), F32),
                   jax.ShapeDtypeStruct((nseq, WINDOW, KV_W), F32)],
        compiler_params=_cparams(("arbitrary",)),
        name="attn_sample",
    )(q3, kn, vn, cache_k, cache_v, bias, sinks.reshape(N_HEADS, 1))


def _ssm_disc_kernel(are_ref, aim_ref, ldt_ref, bre_ref, bim_ref,
                     lbr_ref, lbi_ref, bbr_ref, bbi_ref):
    a_re = are_ref[...]
    a_im = aim_ref[...]
    dt = jnp.exp(ldt_ref[...])
    lam_re = a_re * dt
    lam_im = a_im * dt
    mag = jnp.exp(lam_re)
    lb_re = mag * jnp.cos(lam_im)
    lb_im = mag * jnp.sin(lam_im)
    den = a_re * a_re + a_im * a_im
    nr = lb_re - 1.0
    ni = lb_im
    coef_re = (nr * a_re + ni * a_im) / den
    coef_im = (ni * a_re - nr * a_im) / den
    b_re = bre_ref[...]
    b_im = bim_ref[...]
    lbr_ref[...] = lb_re
    lbi_ref[...] = lb_im
    bbr_ref[...] = coef_re * b_re - coef_im * b_im
    bbi_ref[...] = coef_re * b_im + coef_im * b_re


def _ssm_disc(a_re, a_im, log_dt, b_re, b_im):
    g, p, j = N_SSM_GROUPS, SSM_STATE, SSM_GROUP
    vec = jax.ShapeDtypeStruct((g, 1, p), F32)
    mat = jax.ShapeDtypeStruct((g, j, p), F32)
    return pl.pallas_call(
        _ssm_disc_kernel,
        out_shape=[vec, vec, mat, mat],
        name="ssm_disc",
    )(a_re.reshape(g, 1, p), a_im.reshape(g, 1, p), log_dt.reshape(g, 1, 1),
      b_re.transpose(0, 2, 1), b_im.transpose(0, 2, 1))


def _block_diag_tiles(x):
    a, b = x.shape[1], x.shape[2]
    eye = jnp.eye(SUBLANE, dtype=x.dtype)
    y = jnp.einsum('kgab,gh->kgahb', x.reshape(SSM_TILES, SUBLANE, a, b), eye)
    return y.reshape(SSM_TILES, SUBLANE * a, SUBLANE * b)


def _gelu_tanh(x):
    return 0.5 * x * (1.0 + jnp.tanh(math.sqrt(2.0 / math.pi) * (x + 0.044715 * (x * x * x))))


def _ssm_kernel(*refs, nseq, tc, seq_major):
    if seq_major:
        ua_ref, ub_ref, perm_ref, perm_t_ref = refs[:4]
        refs = refs[4:]
        u = jnp.concatenate([ua_ref[...].reshape(nseq * tc, D_SSM // 2),
                             ub_ref[...].reshape(nseq * tc, D_SSM // 2)], axis=1)
        ub = _dot(perm_ref[...], u.astype(BF16)).astype(BF16)
    else:
        u = refs[0][...]
        refs = refs[1:]
        ub = u.astype(BF16)
    (h0r_ref, h0i_ref, lbr_ref, lbi_ref, bb_ref, cc_ref, d_ref, wglu_ref, bglu_ref,
     y_ref, hTr_ref, hTi_ref, hre, him, st_r, st_i) = refs
    paired = nseq == 4

    @pl.when(pl.program_id(0) == 0)
    def _():
        if paired:
            st_r[...] = jnp.concatenate([h0r_ref[...], h0r_ref[...]], axis=0)
            st_i[...] = jnp.concatenate([h0i_ref[...], h0i_ref[...]], axis=0)
        else:
            st_r[...] = h0r_ref[...]
            st_i[...] = h0i_ref[...]

    half = SSM_W // SSM_TILES
    rows = nseq * tc
    for k in range(SSM_TILES):
        bu = _dot(ub[:, k * LANE:(k + 1) * LANE], bb_ref[k])
        br, bi = bu[:, :half], bu[:, half:]
        if paired:
            ar = lbr_ref[:, k * half:(k + 1) * half]
            ai = lbi_ref[:, k * half:(k + 1) * half]
            tiles = (rows // SUBLANE, SUBLANE, half)
            pr = pltpu.roll(br.reshape(tiles), nseq, 1).reshape(rows, half)
            pi = pltpu.roll(bi.reshape(tiles), nseq, 1).reshape(rows, half)
            first = (lax.broadcasted_iota(I32, (rows, half), 0) & nseq) == 0
            br, bi = (jnp.where(first, br, ar * pr - ai * pi + br),
                      jnp.where(first, bi, ar * pi + ai * pr + bi))
        hre[:, k * half:(k + 1) * half] = br
        him[:, k * half:(k + 1) * half] = bi

    for blk in range(SSM_W // SSM_LB):
        sl = slice(blk * SSM_LB, (blk + 1) * SSM_LB)
        ar = lbr_ref[:, sl]
        ai = lbi_ref[:, sl]
        if paired:
            lower = lax.broadcasted_iota(I32, (SUBLANE, SSM_LB), 0) < nseq
            cr = jnp.where(lower, ar, ar * ar - ai * ai)
            ci = jnp.where(lower, ai, 2.0 * ar * ai)

            def body(m, carry):
                xr, xi = carry
                r0 = pl.multiple_of(m * SUBLANE, SUBLANE)
                hr = cr * xr - ci * xi + hre[pl.ds(r0, SUBLANE), sl]
                hi = cr * xi + ci * xr + him[pl.ds(r0, SUBLANE), sl]
                hre[pl.ds(r0, SUBLANE), sl] = hr
                him[pl.ds(r0, SUBLANE), sl] = hi
                return (jnp.where(lower, pltpu.roll(hr, nseq, 0), hr),
                        jnp.where(lower, pltpu.roll(hi, nseq, 0), hi))

            sr, si = lax.fori_loop(0, tc * nseq // SUBLANE, body, (st_r[:, sl], st_i[:, sl]))
        else:
            def body(t, carry):
                sr, si = carry
                r0 = pl.multiple_of(t * nseq, SUBLANE)
                br = hre[pl.ds(r0, nseq), sl]
                bi = him[pl.ds(r0, nseq), sl]
                nr = ar * sr - ai * si + br
                ni = ar * si + ai * sr + bi
                hre[pl.ds(r0, nseq), sl] = nr
                him[pl.ds(r0, nseq), sl] = ni
                return nr, ni

            sr, si = lax.fori_loop(0, tc, body, (st_r[:, sl], st_i[:, sl]))
        st_r[:, sl] = sr
        st_i[:, sl] = si

    ys = []
    for k in range(SSM_TILES):
        hr = hre[:, k * half:(k + 1) * half].astype(BF16)
        hi = him[:, k * half:(k + 1) * half].astype(BF16)
        ys.append(_dot(hr, cc_ref[k, :half, :]) + _dot(hi, cc_ref[k, half:, :]))
    yc = jnp.concatenate(ys, axis=-1)
    if seq_major:
        yc_hi, yc_lo = _split_bf16(yc)
        yc = _dot(perm_t_ref[...], yc_hi) + _dot(perm_t_ref[...], yc_lo)
    y = _gelu_tanh(yc + d_ref[...] * u)
    z = _dot(y.astype(BF16), wglu_ref[...]) + bglu_ref[...]
    y_ref[...] = (y * _sigmoid(z)).reshape(y_ref.shape)

    if paired:
        hTr_ref[...] = st_r[nseq:, :]
        hTi_ref[...] = st_i[nseq:, :]
    else:
        hTr_ref[...] = st_r[...]
        hTi_ref[...] = st_i[...]


def _ssm(u_src, h0_re, h0_im, nseq, tc, lbr, lbi, bb, cc, d, w_glu, b_glu, seq_major=False):
    r = nseq * tc
    st_rows = max(nseq, SUBLANE)
    const2 = lambda shape: pl.BlockSpec(shape, lambda c: (0, 0))
    const3 = lambda shape: pl.BlockSpec(shape, lambda c: (0, 0, 0))
    if seq_major:
        steps = u_src.shape[1] // tc
        half_w = D_SSM // 2
        col0 = (Q_W + 2 * KV_W) // half_w
        t_idx, s_idx = np.divmod(np.arange(r), nseq)
        perm = np.zeros((r, r), np.float32)
        perm[np.arange(r), s_idx * tc + t_idx] = 1.0
        u_specs = [pl.BlockSpec((nseq, tc, half_w), lambda c: (0, c, col0)),
                   pl.BlockSpec((nseq, tc, half_w), lambda c: (0, c, col0 + 1)),
                   const2((r, r)), const2((r, r))]
        u_args = [u_src, u_src, jnp.asarray(perm, BF16), jnp.asarray(perm.T, BF16)]
        y_spec = pl.BlockSpec((nseq, tc, D_SSM), lambda c: (0, c, 0))
        y_shape = jax.ShapeDtypeStruct((nseq, steps * tc, D_SSM), F32)
    else:
        steps = u_src.shape[0] // r
        u_specs = [pl.BlockSpec((r, D_SSM), lambda c: (c, 0))]
        u_args = [u_src]
        y_spec = pl.BlockSpec((r, D_SSM), lambda c: (c, 0))
        y_shape = jax.ShapeDtypeStruct((steps * r, D_SSM), F32)
    return pl.pallas_call(
        functools.partial(_ssm_kernel, nseq=nseq, tc=tc, seq_major=seq_major),
        grid=(steps,),
        in_specs=u_specs + [
            const2((nseq, SSM_W)), const2((nseq, SSM_W)),
            const2((1, SSM_W)), const2((1, SSM_W)),
            const3((SSM_TILES, LANE, 2 * SSM_W // SSM_TILES)),
            const3((SSM_TILES, 2 * SSM_W // SSM_TILES, LANE)),
            const2((1, D_SSM)), const2((D_SSM, D_SSM)), const2((1, D_SSM))],
        out_specs=[y_spec, const2((nseq, SSM_W)), const2((nseq, SSM_W))],
        out_shape=[y_shape,
                   jax.ShapeDtypeStruct((nseq, SSM_W), F32),
                   jax.ShapeDtypeStruct((nseq, SSM_W), F32)],
        scratch_shapes=[pltpu.VMEM((r, SSM_W), F32), pltpu.VMEM((r, SSM_W), F32),
                        pltpu.VMEM((st_rows, SSM_W), F32), pltpu.VMEM((st_rows, SSM_W), F32)],
        compiler_params=_cparams(("arbitrary",)),
        name="ssm",
    )(*u_args, h0_re, h0_im, lbr, lbi, bb, cc, d.reshape(1, D_SSM), w_glu, b_glu.reshape(1, D_SSM))


def _split_bf16(x):
    hi = x.astype(BF16)
    lo = (x - hi.astype(F32)).astype(BF16)
    return hi, lo


N_MERGE_IN = 15


def _merge_kernel(*refs):
    (o_ref, y_ref, ga_ref, gs_ref, wa_ref, ws_ref, wo_ref, x_ref, gpm_ref, g1_ref, sc2_ref, sh2_ref,
     gpf_ref, wrt_ref, br_ref) = refs[:N_MERGE_IN]
    x1_ref, h2_ref, lg_ref, mix = refs[-4:]
    j = pl.program_id(1)
    a = _dot(o_ref[...].astype(BF16), wa_ref[...])
    s = _dot(y_ref[...].astype(BF16), ws_ref[...])
    merged = _sigmoid(ga_ref[...]) * a + _sigmoid(gs_ref[...]) * s
    contrib = _dot(merged.astype(BF16), wo_ref[...])

    @pl.when(j == 0)
    def _():
        mix[...] = contrib

    @pl.when(j > 0)
    def _():
        mix[...] += contrib

    @pl.when(j == pl.num_programs(1) - 1)
    def _():
        x1 = x_ref[...] + g1_ref[...] * _rms(mix[...], gpm_ref[...])
        x1_ref[...] = x1
        h2 = _rms(x1, gpf_ref[...]) * (1.0 + sc2_ref[...]) + sh2_ref[...]
        h2_ref[...] = h2
        h_hi, h_lo = _split_bf16(h2)
        w_hi, w_lo = _split_bf16(wrt_ref[...])
        lg_ref[...] = (_dot_nt(w_hi, h_hi) + _dot_nt(w_hi, h_lo) + _dot_nt(w_lo, h_hi)) + br_ref[...]


def _merge(o_attn, y_ssm, proj, x, mod, per_row, rows_per_seq, n_total, row0, shared, w_br_attn, w_br_ssm,
           w_out, g_post_mix, g_pre_ffn, w_router_t, b_router):
    rows = x.shape[0]
    tm = min(MERGE_TM, rows)
    blk0 = row0 // tm
    nk = D_MODEL // MERGE_TK
    ga0 = (Q_W + 2 * KV_W + D_SSM) // MERGE_TK
    gs0 = ga0 + nk
    row2 = lambda w: pl.BlockSpec((tm, w), lambda i, j: (i, 0))
    vec = pl.BlockSpec((1, D_MODEL), lambda i, j: (0, 0))
    in_specs = [row2(Q_W), row2(D_SSM),
                pl.BlockSpec((tm, MERGE_TK), lambda i, j: (i, ga0 + j)),
                pl.BlockSpec((tm, MERGE_TK), lambda i, j: (i, gs0 + j)),
                pl.BlockSpec((Q_W, MERGE_TK), lambda i, j: (0, j)),
                pl.BlockSpec((D_SSM, MERGE_TK), lambda i, j: (0, j)),
                pl.BlockSpec((MERGE_TK, D_MODEL), lambda i, j: (j, 0)),
                row2(D_MODEL), vec,
                _mod_spec(per_row, tm, rows_per_seq, 2, 2),
                _mod_spec(per_row, tm, rows_per_seq, 4, 2),
                _mod_spec(per_row, tm, rows_per_seq, 3, 2),
                vec,
                pl.BlockSpec((N_EXPERTS, D_MODEL), lambda i, j: (0, 0)),
                pl.BlockSpec((N_EXPERTS, 1), lambda i, j: (0, 0))]
    args = [o_attn, y_ssm, proj, proj, w_br_attn, w_br_ssm, w_out, x, g_post_mix.reshape(1, D_MODEL),
            mod, mod, mod, g_pre_ffn.reshape(1, D_MODEL), w_router_t, b_router.reshape(N_EXPERTS, 1)]
    assert len(args) == N_MERGE_IN
    aliases = {}
    if shared is not None:
        aliases = {len(args): 1, len(args) + 1: 2}
        in_specs += [pl.BlockSpec(memory_space=pl.ANY), pl.BlockSpec(memory_space=pl.ANY)]
        args += list(shared)
    return pl.pallas_call(
        _merge_kernel,
        grid=(rows // tm, nk),
        in_specs=in_specs,
        out_specs=[row2(D_MODEL),
                   pl.BlockSpec((tm, D_MODEL), lambda i, j: (blk0 + i, 0)),
                   pl.BlockSpec((N_EXPERTS, tm), lambda i, j: (0, blk0 + i))],
        out_shape=[jax.ShapeDtypeStruct((rows, D_MODEL), F32),
                   jax.ShapeDtypeStruct((n_total, D_MODEL), F32),
                   jax.ShapeDtypeStruct((N_EXPERTS, n_total), F32)],
        scratch_shapes=[pltpu.VMEM((tm, D_MODEL), F32)],
        input_output_aliases=aliases,
        compiler_params=_cparams(("arbitrary", "arbitrary")),
        name="merge",
    )(*args)


def _bucket_tables():
    ql = np.arange(WINDOW)[:, None]
    kl = np.arange(2 * WINDOW)[None, :]
    dist = ql + WINDOW - kl
    prompt = np.where((dist >= 0) & (dist < WINDOW), _t5_bucket_np(dist), -1).astype(np.int32)
    first = np.where(kl >= WINDOW, prompt, -1).astype(np.int32)
    d_s = (WINDOW - 1 - np.arange(WINDOW))[None, :]
    sample = np.broadcast_to(_t5_bucket_np(d_s), (SUBLANE, WINDOW)).astype(np.int32)
    return first, prompt, sample


def _front(p):
    l = 0
    batch, seq, _ = p['x_prompt'].shape
    nseq = p['x_sample'].shape[0]
    xp = p['x_prompt'].reshape(batch * seq, D_MODEL)
    xs = p['x_sample'].reshape(nseq, D_MODEL)

    c_all = jnp.concatenate([p['c_prompt'], p['c_sample'],
                             jnp.zeros((SUBLANE - (batch + nseq) % SUBLANE, D_MODEL), F32)], axis=0)
    mod = _ada(c_all, p['w_ada'][l], p['b_ada'][l])
    mod_p = mod[:batch].reshape(batch, 1, 6 * D_MODEL)
    mod_s = mod[batch:batch + nseq]

    bucket_f, bucket_p, bucket_s = _bucket_tables()
    bias_p = jnp.stack([_bias_table(bucket_f, p['rel_bias']), _bias_table(bucket_p, p['rel_bias'])])
    bias_s = _bias_table(bucket_s, p['rel_bias'])[:, 0, :]
    sinks = p['attn_sinks'][l]

    w_in_t = p['w_in'][l].astype(BF16)
    proj_p = _proj(xp, mod_p, False, seq, p['g_pre_mix'][l], w_in_t)
    proj_s = _proj(xs, mod_s, True, 1, p['g_pre_mix'][l], w_in_t)

    o_p = _attn_prompt(proj_p, batch, seq, bias_p, sinks)
    kv_p = proj_p.reshape(batch, seq, IN_W)[:, seq - WINDOW:, Q_W:Q_W + 2 * KV_W]
    new_k_p = kv_p[..., :KV_W].reshape(1, batch, WINDOW, N_KV_HEADS, HEAD_DIM)
    new_v_p = kv_p[..., KV_W:].reshape(1, batch, WINDOW, N_KV_HEADS, HEAD_DIM)
    o_s3, new_k_s, new_v_s = _attn_sample(
        proj_s[:, :Q_W].reshape(nseq, N_HEADS, HEAD_DIM),
        proj_s[:, Q_W:Q_W + KV_W].reshape(nseq, 1, KV_W),
        proj_s[:, Q_W + KV_W:Q_W + 2 * KV_W].reshape(nseq, 1, KV_W),
        p['cache_win_k'][l].reshape(nseq, WINDOW, KV_W),
        p['cache_win_v'][l].reshape(nseq, WINDOW, KV_W), bias_s, sinks)
    o_s = o_s3.reshape(nseq, Q_W)

    lbr, lbi, bbr, bbi = _ssm_disc(p['ssm_a_re'][l], p['ssm_a_im'][l], p['ssm_log_dt'][l],
                                   p['ssm_b_re'][l], p['ssm_b_im'][l])
    lbr = lbr.reshape(1, SSM_W)
    lbi = lbi.reshape(1, SSM_W)
    bb = jnp.concatenate([_block_diag_tiles(bbr), _block_diag_tiles(bbi)], axis=-1).astype(BF16)
    c_re_t = p['ssm_c_re'][l].transpose(0, 2, 1)
    c_im_t = p['ssm_c_im'][l].transpose(0, 2, 1)
    cc = jnp.concatenate([_block_diag_tiles(c_re_t), -_block_diag_tiles(c_im_t)], axis=1).astype(BF16)
    u0 = Q_W + 2 * KV_W
    zeros = jnp.zeros((batch, SSM_W), F32)
    w_glu = p['w_glu'][l].astype(BF16)
    y_p3, hr_p, hi_p = _ssm(proj_p.reshape(batch, seq, IN_W), zeros, zeros, batch, SSM_TC, lbr, lbi, bb, cc,
                            p['ssm_d'][l], w_glu, p['b_glu'][l], seq_major=True)
    y_p = y_p3.reshape(batch * seq, D_SSM)
    y_s, hr_s, hi_s = _ssm(proj_s[:, u0:u0 + D_SSM], p['state_ssm_re'][l].reshape(nseq, SSM_W),
                           p['state_ssm_im'][l].reshape(nseq, SSM_W), nseq, 1, lbr, lbi, bb, cc,
                           p['ssm_d'][l], w_glu, p['b_glu'][l])

    wa_t = p['w_br_attn'][l].astype(BF16)
    ws_t = p['w_br_ssm'][l].astype(BF16)
    wo = p['w_out'][l].astype(BF16)
    n_total = batch * seq + nseq
    merge = functools.partial(_merge, w_br_attn=wa_t, w_br_ssm=ws_t, w_out=wo, g_post_mix=p['g_post_mix'][l],
                              g_pre_ffn=p['g_pre_ffn'][l], w_router_t=p['w_router'][l].T,
                              b_router=p['b_router'][l])
    x1_p, h2_buf, lg_buf = merge(o_p, y_p, proj_p, xp, mod_p, False, seq, n_total, 0, None)
    x1_s, h2_all, lg_all = merge(o_s, y_s, proj_s, xs, mod_s, True, 1, n_total, batch * seq, (h2_buf, lg_buf))

    st = lambda h, n: h.reshape(1, n, N_SSM_GROUPS, SSM_STATE)
    return dict(
        mod_p=mod_p, mod_s=mod_s, h2_all=h2_all, lg_all=lg_all,
        p=dict(proj=proj_p, o_attn=o_p, new_k=new_k_p, new_v=new_v_p, y_ssm=y_p, h_re=st(hr_p, batch),
               h_im=st(hi_p, batch), x1=x1_p),
        s=dict(proj=proj_s, o_attn=o_s, new_k=new_k_s.reshape(1, nseq, WINDOW, N_KV_HEADS, HEAD_DIM),
               new_v=new_v_s.reshape(1, nseq, WINDOW, N_KV_HEADS, HEAD_DIM), y_ssm=y_s,
               h_re=st(hr_s, nseq), h_im=st(hi_s, nseq), x1=x1_s))


def _count_steps(c, step, n_max):
    out = jnp.zeros_like(c)
    for q in range(-(-n_max // step)):
        out = out + jnp.where(c > float(q * step), 1.0, 0.0)
    return out


def _route_kernel(lg_ref, pos_ref, gate_ref, texp_ref, trows_ref, ntiles_ref, pstart_ref, plen_ref):
    lg = lg_ref[...]
    e, tn = lg.shape
    erow = lax.broadcasted_iota(I32, (e, tn), 0).astype(F32)
    work = lg
    vals, hits = [], []
    for _ in range(TOP_K):
        m = jnp.max(work, axis=0, keepdims=True)
        idx = jnp.min(jnp.where(work == m, erow, float(e)), axis=0, keepdims=True)
        hit = erow == idx
        vals.append(m)
        hits.append(hit)
        work = jnp.where(hit, -jnp.inf, work)
    ex = [jnp.exp(v - vals[0]) for v in vals]
    den = ex[0] + ex[1] + ex[2] + ex[3]
    gate_ref[...] = jnp.concatenate([x / den for x in ex], axis=0)

    chosen = jnp.zeros((e, tn), F32)
    for hit in hits:
        chosen = chosen + jnp.where(hit, 1.0, 0.0)
    chosen_b = chosen.astype(BF16)
    tri = (lax.broadcasted_iota(I32, (LANE, LANE), 0) <= lax.broadcasted_iota(I32, (LANE, LANE), 1))
    tri = jnp.where(tri, 1.0, 0.0).astype(BF16)
    carry = jnp.zeros((e, 1), F32)
    ranks = []
    for b in range(tn // LANE):
        blk = chosen[:, b * LANE:(b + 1) * LANE]
        inc = _dot(chosen_b[:, b * LANE:(b + 1) * LANE], tri) + carry
        ranks.append(inc - blk)
        carry = inc[:, LANE - 1:LANE]
    rank = jnp.concatenate(ranks, axis=1)
    cnt_col = carry
    cnt_row = _dot_nt(jnp.ones((SUBLANE, tn), BF16), chosen_b)[0:1, :]

    tiles_col = _count_steps(cnt_col, EXP_TM, tn)
    tiles_row = _count_steps(cnt_row, EXP_TM, tn)
    ee_r = lax.broadcasted_iota(I32, (e, e), 0)
    ee_c = lax.broadcasted_iota(I32, (e, e), 1)
    tstart_col = jnp.sum(jnp.where(ee_c < ee_r, tiles_row, 0.0), axis=1, keepdims=True)
    ntiles = jnp.sum(tiles_row, axis=1, keepdims=True)
    rstart_col = tstart_col * float(EXP_TM)
    pos = [jnp.sum(jnp.where(hit, rstart_col + rank, 0.0), axis=0, keepdims=True) for hit in hits]
    pos_ref[...] = jnp.concatenate(pos, axis=0).astype(I32)

    mm = lax.broadcasted_iota(I32, (e, LANE), 1).astype(F32)
    e_col = lax.broadcasted_iota(I32, (e, LANE), 0).astype(F32)
    own = (mm >= tstart_col) & (mm < tstart_col + tiles_col)
    texp = jnp.sum(jnp.where(own, e_col, 0.0), axis=0, keepdims=True)
    rows_here = jnp.minimum(float(EXP_TM), cnt_col - (mm - tstart_col) * float(EXP_TM))
    trows = jnp.sum(jnp.where(own, rows_here, 0.0), axis=0, keepdims=True)
    last_e = jnp.max(jnp.where(tiles_col > 0.0, e_col, 0.0), axis=0, keepdims=True)
    texp = jnp.where(mm[0:1, :] < ntiles, texp, last_e)
    texp_ref[...] = texp.astype(I32)
    trows_ref[...] = trows.astype(I32)
    ntiles_ref[...] = jnp.broadcast_to(ntiles, (1, LANE)).astype(I32)
    nsub_col = _count_steps(cnt_col, EXP_SUB, tn)
    pstart_ref[...] = jnp.broadcast_to(rstart_col + cnt_col, (e, LANE)).astype(I32)
    plen_ref[...] = jnp.broadcast_to(nsub_col * float(EXP_SUB) - cnt_col, (e, LANE)).astype(I32)


def _route(lg_t):
    e, tn = lg_t.shape
    i32 = lambda shape: jax.ShapeDtypeStruct(shape, I32)
    return pl.pallas_call(
        _route_kernel,
        out_shape=[i32((TOP_K, tn)), jax.ShapeDtypeStruct((TOP_K, tn), F32),
                   i32((1, LANE)), i32((1, LANE)), i32((1, LANE)), i32((e, LANE)), i32((e, LANE))],
        compiler_params=pltpu.CompilerParams(vmem_limit_bytes=VMEM_LIMIT),
        name="route",
    )(lg_t)


def _max_tiles(n_tok):
    return (n_tok * TOP_K) // EXP_TM + N_EXPERTS


def _dispatch_kernel(pos_ref, pstart_ref, plen_ref, h2_ref, zero_ref, xs_ref, sem):
    i = pl.program_id(0)

    def row_copy(src, s, d):
        return pltpu.make_async_copy(src.at[pl.ds(s, 1)], xs_ref.at[pl.ds(d, 1)], sem)

    for t in range(TOK_BLK):
        for k in range(TOP_K):
            row_copy(h2_ref, t, pos_ref[k, t]).start(priority=k % 2)
    for _ in range(TOP_K):
        pltpu.make_async_copy(h2_ref, xs_ref.at[pl.ds(0, TOK_BLK)], sem).wait()

    @pl.when(i == 0)
    def _():
        def per_expert(ex, c):
            n = plen_ref[ex]
            s = pstart_ref[ex]

            def zissue(r, cc):
                row_copy(zero_ref, 0, s + r).start()
                return cc

            def zdrain(r, cc):
                row_copy(zero_ref, 0, 0).wait()
                return cc

            lax.fori_loop(0, n, zissue, 0)
            lax.fori_loop(0, n, zdrain, 0)
            return c

        lax.fori_loop(0, N_EXPERTS, per_expert, 0)


def _dispatch(pos3, pstart, plen, h2_all):
    n_tok = h2_all.shape[0]
    n_rows = _max_tiles(n_tok) * EXP_TM
    smem = pl.BlockSpec(memory_space=pltpu.SMEM)
    hbm = pl.BlockSpec(memory_space=pl.ANY)
    return pl.pallas_call(
        _dispatch_kernel,
        grid=(n_tok // TOK_BLK,),
        in_specs=[pl.BlockSpec((None, TOP_K, TOK_BLK), lambda i: (i, 0, 0), memory_space=pltpu.SMEM),
                  smem, smem,
                  pl.BlockSpec((TOK_BLK, D_MODEL), lambda i: (i, 0)),
                  pl.BlockSpec((SUBLANE, D_MODEL), lambda i: (0, 0))],
        out_specs=hbm,
        out_shape=jax.ShapeDtypeStruct((n_rows, D_MODEL), F32),
        scratch_shapes=[pltpu.SemaphoreType.DMA(())],
        compiler_params=_cparams(("arbitrary",)),
        name="dispatch",
    )(pos3, pstart, plen, h2_all, jnp.zeros((SUBLANE, D_MODEL), F32))


def _expert_kernel(texp_ref, trows_ref, nt_ref, xs_hbm, wg_ref, wl_ref, bg_ref, bl_ref, wd_ref, bd_ref,
                   o_ref, x_ref, x_sem, xb_scr, act_scr, wg_scr, wl_scr, wd_scr):
    m = pl.program_id(0)
    s = pl.program_id(1)
    n_tiles = nt_ref[0]
    valid = m < n_tiles

    rows = trows_ref[m]
    nbig = lax.shift_right_logical(rows, int(math.log2(EXP_BIG)))
    big_rows = nbig * EXP_BIG
    nsmall = lax.shift_right_logical(rows - big_rows + (EXP_SUB - 1), int(math.log2(EXP_SUB)))
    nsub_done = nbig * (EXP_BIG // EXP_SUB) + nsmall

    def x_copy(tile):
        return pltpu.make_async_copy(xs_hbm.at[pl.ds(pl.multiple_of(tile * EXP_TM, EXP_TM), EXP_TM)],
                                     x_ref, x_sem)

    @pl.when((m == 0) & (s == 0))
    def _():
        x_copy(0).start()

    @pl.when(valid & (s == 0))
    def _():
        x_copy(m).wait()

        def to_bf16(r, c):
            r0 = pl.multiple_of(r * EXP_SUB, EXP_SUB)
            xb_scr[pl.ds(r0, EXP_SUB), :] = x_ref[pl.ds(r0, EXP_SUB), :].astype(BF16)
            return c

        lax.fori_loop(0, nsub_done, to_bf16, 0)

        @pl.when(m + 1 < n_tiles)
        def _():
            x_copy(m + 1).start()

    def over_rows(first, step):
        def big(r, c):
            step(pl.multiple_of(r * EXP_BIG, EXP_BIG), EXP_BIG)
            return c

        def small(r, c):
            step(pl.multiple_of(big_rows + r * EXP_SUB, EXP_SUB), EXP_SUB)
            return c

        @pl.when(nbig > 0)
        def _():
            first(EXP_BIG)
            lax.fori_loop(1, nbig, big, 0)
            lax.fori_loop(0, nsmall, small, 0)

        @pl.when(nbig == 0)
        def _():
            first(EXP_SUB)
            lax.fori_loop(1, nsmall, small, 0)

    @pl.when(valid & (s < EXP_NF))
    def _():
        def finish(r0, n, hg, hl):
            x_glu = jnp.minimum(hg, SWIGLU_LIMIT)
            x_lin = jnp.clip(hl, -SWIGLU_LIMIT, SWIGLU_LIMIT)
            act = x_glu * _sigmoid(SWIGLU_ALPHA * x_glu) * (x_lin + 1.0)
            act_scr[s, pl.ds(r0, n), :] = act.astype(BF16)

        def first(n):
            xb = xb_scr[0:n, :]
            hg = jnp.broadcast_to(bg_ref[...], (n, EXP_TF))
            hl = jnp.broadcast_to(bl_ref[...], (n, EXP_TF))
            for q in range(D_MODEL // EXP_KQ):
                ks = slice(q * EXP_KQ, (q + 1) * EXP_KQ)
                wgq = wg_ref[ks, :].astype(BF16)
                wlq = wl_ref[ks, :].astype(BF16)
                wg_scr[ks, :] = wgq
                wl_scr[ks, :] = wlq
                hg = hg + _dot(xb[:, ks], wgq)
                hl = hl + _dot(xb[:, ks], wlq)
            finish(0, n, hg, hl)

        def step(r0, n):
            xb = xb_scr[pl.ds(r0, n), :]
            finish(r0, n, _dot(xb, wg_scr[...]) + bg_ref[...], _dot(xb, wl_scr[...]) + bl_ref[...])

        over_rows(first, step)

    @pl.when(valid & (s >= EXP_NF))
    def _():
        def first(n):
            acc = jnp.broadcast_to(bd_ref[...], (n, EXP_TD))
            for f in range(EXP_NF):
                fs = slice(f * EXP_TF, (f + 1) * EXP_TF)
                wdq = wd_ref[fs, :].astype(BF16)
                wd_scr[fs, :] = wdq
                acc = acc + _dot(act_scr[f, 0:n, :], wdq)
            o_ref[0:n, :] = acc

        def step(r0, n):
            acc = jnp.broadcast_to(bd_ref[...], (n, EXP_TD))
            for f in range(EXP_NF):
                acc = acc + _dot(act_scr[f, pl.ds(r0, n), :], wd_scr[f * EXP_TF:(f + 1) * EXP_TF, :])
            o_ref[pl.ds(r0, n), :] = acc

        def zero(r, c):
            r0 = pl.multiple_of(r * EXP_SUB, EXP_SUB)
            o_ref[pl.ds(r0, EXP_SUB), :] = jnp.zeros((EXP_SUB, EXP_TD), F32)
            return c

        over_rows(first, step)
        lax.fori_loop(nsub_done, EXP_TM // EXP_SUB, zero, 0)


def _experts(texp, trows, ntiles, xs, w_gate_up, b_gate_up, w_down, b_down):
    n_tiles = xs.shape[0] // EXP_TM
    nsteps = EXP_NF + EXP_ND

    def tile(m, nt):
        return jnp.minimum(m, nt[0] - 1)

    def ea(m, s, te, nt):
        return te[jnp.where(s < EXP_NF, m, jnp.minimum(m + 1, nt[0] - 1))]

    def fa(m, s, nt):
        return jnp.where((m < nt[0]) & (s < EXP_NF), s, 0)

    def fb(m, s, nt):
        return jnp.where(m < nt[0], jnp.maximum(s - EXP_NF, 0), EXP_ND - 1)

    grid_spec = pltpu.PrefetchScalarGridSpec(
        num_scalar_prefetch=3,
        grid=(n_tiles, nsteps),
        in_specs=[
            pl.BlockSpec(memory_space=pl.ANY),
            pl.BlockSpec((None, D_MODEL, EXP_TF),
                         lambda m, s, te, tr, nt: (ea(m, s, te, nt), 0, fa(m, s, nt))),
            pl.BlockSpec((None, D_MODEL, EXP_TF),
                         lambda m, s, te, tr, nt: (ea(m, s, te, nt), 0, EXP_NF + fa(m, s, nt))),
            pl.BlockSpec((None, 1, EXP_TF), lambda m, s, te, tr, nt: (ea(m, s, te, nt), 0, fa(m, s, nt))),
            pl.BlockSpec((None, 1, EXP_TF),
                         lambda m, s, te, tr, nt: (ea(m, s, te, nt), 0, EXP_NF + fa(m, s, nt))),
            pl.BlockSpec((None, D_FF, EXP_TD), lambda m, s, te, tr, nt: (te[m], 0, fb(m, s, nt))),
            pl.BlockSpec((None, 1, EXP_TD), lambda m, s, te, tr, nt: (te[m], 0, fb(m, s, nt))),
        ],
        out_specs=pl.BlockSpec((EXP_TM, EXP_TD), lambda m, s, te, tr, nt: (tile(m, nt), fb(m, s, nt))),
        scratch_shapes=[pltpu.VMEM((EXP_TM, D_MODEL), F32), pltpu.SemaphoreType.DMA(()),
                        pltpu.VMEM((EXP_TM, D_MODEL), BF16),
                        pltpu.VMEM((EXP_NF, EXP_TM, EXP_TF), BF16),
                        pltpu.VMEM((D_MODEL, EXP_TF), BF16), pltpu.VMEM((D_MODEL, EXP_TF), BF16),
                        pltpu.VMEM((D_FF, EXP_TD), BF16)],
    )
    return pl.pallas_call(
        _expert_kernel,
        grid_spec=grid_spec,
        out_shape=jax.ShapeDtypeStruct((xs.shape[0], D_MODEL), F32),
        compiler_params=pltpu.CompilerParams(dimension_semantics=("arbitrary", "arbitrary"),
                                             vmem_limit_bytes=EXP_VMEM_LIMIT),
        name="experts",
    )(texp, trows, ntiles, xs, w_gate_up, w_gate_up,
      b_gate_up.reshape(N_EXPERTS, 1, 2 * D_FF), b_gate_up.reshape(N_EXPERTS, 1, 2 * D_FF),
      w_down, b_down.reshape(N_EXPERTS, 1, D_MODEL))


def _combine_kernel(pos_ref, pos_next_ref, gate_ref, ys_ref, x1_ref, g2_ref, gpf_ref, o_ref, buf, sem):
    i = pl.program_id(0)
    slot = lax.rem(i, 2)

    def gather(p_ref, sl):
        for t in range(TOK_BLK):
            for k in range(TOP_K):
                pltpu.make_async_copy(ys_ref.at[pl.ds(p_ref[k, t], 1)], buf.at[sl, k, pl.ds(t, 1)],
                                      sem.at[sl]).start(priority=k % 2)

    @pl.when(i == 0)
    def _():
        gather(pos_ref, 0)

    @pl.when(i + 1 < pl.num_programs(0))
    def _():
        gather(pos_next_ref, 1 - slot)

    for k in range(TOP_K):
        pltpu.make_async_copy(ys_ref.at[pl.ds(0, TOK_BLK)], buf.at[slot, k], sem.at[slot]).wait()
    g = gate_ref[...]
    f = g[:, 0:1] * buf[slot, 0]
    for k in range(1, TOP_K):
        f = f + g[:, k:k + 1] * buf[slot, k]
    o_ref[...] = x1_ref[...] + g2_ref[...] * _rms(f, gpf_ref[...])


def _combine(pos3, gates_t, ys, x1, mod, per_row, rows_per_seq, g_post_ffn, blk0):
    rows = x1.shape[0]
    nblk = rows // TOK_BLK
    return pl.pallas_call(
        _combine_kernel,
        grid=(nblk,),
        in_specs=[pl.BlockSpec((None, TOP_K, TOK_BLK), lambda i: (blk0 + i, 0, 0), memory_space=pltpu.SMEM),
                  pl.BlockSpec((None, TOP_K, TOK_BLK), lambda i: (blk0 + jnp.minimum(i + 1, nblk - 1), 0, 0),
                               memory_space=pltpu.SMEM),
                  pl.BlockSpec((TOK_BLK, TOP_K), lambda i: (blk0 + i, 0)),
                  pl.BlockSpec(memory_space=pl.ANY),
                  pl.BlockSpec((TOK_BLK, D_MODEL), lambda i: (i, 0)),
                  _mod_spec(per_row, TOK_BLK, rows_per_seq, 5, 1),
                  pl.BlockSpec((1, D_MODEL), lambda i: (0, 0))],
        out_specs=pl.BlockSpec((TOK_BLK, D_MODEL), lambda i: (i, 0)),
        out_shape=jax.ShapeDtypeStruct((rows, D_MODEL), F32),
        scratch_shapes=[pltpu.VMEM((2, TOP_K, TOK_BLK, D_MODEL), F32), pltpu.SemaphoreType.DMA((2,))],
        compiler_params=_cparams(("arbitrary",)),
        name="combine",
    )(pos3, pos3, gates_t, ys, x1, mod, g_post_ffn.reshape(1, D_MODEL))


def _moe(fr, p, batch, seq, nseq):
    l = 0
    h2_all, lg_all = fr['h2_all'], fr['lg_all']
    n_tok = h2_all.shape[0]
    pos, gates, texp, trows, ntiles, pstart, plen = _route(lg_all)
    pos3 = pos.reshape(TOP_K, n_tok // TOK_BLK, TOK_BLK).transpose(1, 0, 2)
    xs = _dispatch(pos3, pstart[:, 0], plen[:, 0], h2_all)
    ys = _experts(texp[0], trows[0], ntiles[0, :1], xs, p['w_gate_up'][l], p['b_gate_up'][l],
                  p['w_down'][l], p['b_down'][l])
    gates_t = gates.T
    y_p = _combine(pos3, gates_t, ys, fr['p']['x1'], fr['mod_p'], False, seq, p['g_post_ffn'][l], 0)
    y_s = _combine(pos3, gates_t, ys, fr['s']['x1'], fr['mod_s'], True, 1, p['g_post_ffn'][l],
                   batch * seq // TOK_BLK)
    return y_p, y_s


def kernel(x_prompt, x_sample, c_prompt, c_sample, cache_win_k, cache_win_v, state_ssm_re, state_ssm_im, w_ada, b_ada, g_pre_mix, g_post_mix, g_pre_ffn, g_post_ffn, w_in, attn_sinks, rel_bias, ssm_a_re, ssm_a_im, ssm_log_dt, ssm_b_re, ssm_b_im, ssm_c_re, ssm_c_im, ssm_d, w_glu, b_glu, w_br_attn, w_br_ssm, w_out, w_router, b_router, w_gate_up, b_gate_up, w_down, b_down):
    p = dict(locals())
    batch, seq, _ = x_prompt.shape
    nseq = x_sample.shape[0]
    fr = _front(p)
    y_p, y_s = _moe(fr, p, batch, seq, nseq)
    fp, fs = fr['p'], fr['s']
    return (y_p.reshape(batch, seq, D_MODEL), y_s.reshape(nseq, 1, D_MODEL),
            fp['new_k'], fp['new_v'], fp['h_re'], fp['h_im'],
            fs['new_k'], fs['new_v'], fs['h_re'], fs['h_im'])
```

```python
import functools
import math

import numpy as np
import jax
import jax.numpy as jnp
from jax import lax
from jax.experimental import pallas as pl
from jax.experimental.pallas import tpu as pltpu

F32 = jnp.float32
BF16 = jnp.bfloat16
I32 = jnp.int32

D_MODEL = 2048
N_HEADS = 16
N_KV_HEADS = 4
HEAD_DIM = 64
Q_GROUP = N_HEADS // N_KV_HEADS
WINDOW = 128
N_BUCKETS = 32
MAX_EXACT = N_BUCKETS // 2
MAX_DISTANCE = 128
D_SSM = 1024
SSM_GROUP = 16
N_SSM_GROUPS = 64
SSM_STATE = 64
N_EXPERTS = 32
TOP_K = 4
D_FF = 2048
SWIGLU_LIMIT = 7.0
SWIGLU_ALPHA = 1.702
NORM_EPS = 1e-6
NEG_INF = -1e30
Q_W = N_HEADS * HEAD_DIM
KV_W = N_KV_HEADS * HEAD_DIM
IN_W = Q_W + 2 * KV_W + D_SSM + 2 * D_MODEL
SSM_W = N_SSM_GROUPS * SSM_STATE

LANE = 128
SUBLANE = 8
VMEM_LIMIT = 56 * 1024 * 1024
EXP_VMEM_LIMIT = 60 * 1024 * 1024

PROJ_TM = 1024
PROJ_TN = 512
MERGE_TM = 512
MERGE_TK = 512
SSM_TC = 128
SSM_LB = 512
SSM_TILES = D_SSM // LANE
TOK_BLK = 128
EXP_TM = 1152
EXP_BIG = 512
EXP_SUB = 128
EXP_TF = 512
EXP_TD = 512
EXP_KQ = 512
EXP_NF = D_FF // EXP_TF
EXP_ND = D_MODEL // EXP_TD


def _cparams(sem):
    return pltpu.CompilerParams(dimension_semantics=sem, vmem_limit_bytes=VMEM_LIMIT)


def _sigmoid(x):
    return 1.0 / (1.0 + jnp.exp(-x))


def _rms(x, g):
    return x * lax.rsqrt(jnp.mean(x * x, axis=-1, keepdims=True) + NORM_EPS) * g


def _dot(a, b):
    return jnp.dot(a, b, preferred_element_type=F32)


def _dot_nt(a, b):
    return lax.dot_general(a, b, (((1,), (1,)), ((), ())), preferred_element_type=F32)


def _ada_kernel(c_ref, w_ref, b_ref, o_ref):
    c = c_ref[...]
    s = (c * _sigmoid(c)).astype(BF16)
    o_ref[...] = _dot(s, w_ref[...].astype(BF16)) + b_ref[...]


def _ada(c_all, w_ada, b_ada):
    rows = c_all.shape[0]
    tn = 1024
    n = w_ada.shape[1]
    return pl.pallas_call(
        _ada_kernel,
        grid=(n // tn,),
        in_specs=[pl.BlockSpec((rows, D_MODEL), lambda j: (0, 0)),
                  pl.BlockSpec((D_MODEL, tn), lambda j: (0, j)),
                  pl.BlockSpec((1, tn), lambda j: (0, j))],
        out_specs=pl.BlockSpec((rows, tn), lambda j: (0, j)),
        out_shape=jax.ShapeDtypeStruct((rows, n), F32),
        compiler_params=_cparams(("arbitrary",)),
        name="ada",
    )(c_all, w_ada, b_ada.reshape(1, n))


def _t5_bucket_np(dist):
    n = np.maximum(dist, 0)
    nf = np.maximum(n, 1).astype(np.float64)
    large = MAX_EXACT + (np.log(nf / MAX_EXACT) / math.log(MAX_DISTANCE / MAX_EXACT)
                         * (N_BUCKETS - MAX_EXACT)).astype(np.int32)
    large = np.minimum(large, N_BUCKETS - 1)
    return np.where(n < MAX_EXACT, n, large).astype(np.int32)


def _bias_kernel(bucket_ref, rb_ref, o_ref):
    h = pl.program_id(0)
    bucket = bucket_ref[...]
    acc = jnp.full(bucket.shape, NEG_INF, F32)
    for b in range(N_BUCKETS):
        acc = jnp.where(bucket == b, rb_ref[b, h], acc)
    o_ref[...] = acc


def _bias_table(bucket_np, rel_bias):
    r, c = bucket_np.shape
    return pl.pallas_call(
        _bias_kernel,
        grid=(N_HEADS,),
        in_specs=[pl.BlockSpec((r, c), lambda h: (0, 0)),
                  pl.BlockSpec(memory_space=pltpu.SMEM)],
        out_specs=pl.BlockSpec((None, r, c), lambda h: (h, 0, 0)),
        out_shape=jax.ShapeDtypeStruct((N_HEADS, r, c), F32),
        compiler_params=_cparams(("arbitrary",)),
        name="bias",
    )(jnp.asarray(bucket_np), rel_bias)


def _proj_kernel(x_ref, sc_ref, sh_ref, g_ref, w_ref, o_ref, h_scr):
    @pl.when(pl.program_id(1) == 0)
    def _():
        h = _rms(x_ref[...], g_ref[...]) * (1.0 + sc_ref[...]) + sh_ref[...]
        h_scr[...] = h.astype(BF16)

    o_ref[...] = _dot(h_scr[...], w_ref[...])


def _mod_spec(per_row, tm, rows_per_seq, col, nargs):
    if per_row:
        if nargs == 1:
            return pl.BlockSpec((tm, D_MODEL), lambda i: (i, col))
        return pl.BlockSpec((tm, D_MODEL), lambda i, j: (i, col))
    tiles_per_seq = rows_per_seq // tm
    if nargs == 1:
        return pl.BlockSpec((None, 1, D_MODEL), lambda i: (i // tiles_per_seq, 0, col))
    return pl.BlockSpec((None, 1, D_MODEL), lambda i, j: (i // tiles_per_seq, 0, col))


def _proj(x, mod, per_row, rows_per_seq, g_pre, w_in):
    rows = x.shape[0]
    tm = min(PROJ_TM, rows)
    return pl.pallas_call(
        _proj_kernel,
        grid=(rows // tm, IN_W // PROJ_TN),
        in_specs=[pl.BlockSpec((tm, D_MODEL), lambda i, j: (i, 0)),
                  _mod_spec(per_row, tm, rows_per_seq, 1, 2),
                  _mod_spec(per_row, tm, rows_per_seq, 0, 2),
                  pl.BlockSpec((1, D_MODEL), lambda i, j: (0, 0)),
                  pl.BlockSpec((D_MODEL, PROJ_TN), lambda i, j: (0, j))],
        out_specs=pl.BlockSpec((tm, PROJ_TN), lambda i, j: (i, j)),
        out_shape=jax.ShapeDtypeStruct((rows, IN_W), F32),
        scratch_shapes=[pltpu.VMEM((tm, D_MODEL), BF16)],
        compiler_params=_cparams(("arbitrary", "arbitrary")),
        name="proj",
    )(x, mod, mod, g_pre.reshape(1, D_MODEL), w_in)


def _attn_prompt_kernel(q_ref, kc_ref, kp_ref, vc_ref, vp_ref, bias_ref, sink_ref, o_ref):
    q = q_ref[...] * (HEAD_DIM ** -0.5)
    k = jnp.concatenate([kp_ref[...], kc_ref[...]], axis=0)
    v = jnp.concatenate([vp_ref[...], vc_ref[...]], axis=0)
    outs = []
    for g in range(N_KV_HEADS):
        kg = k[:, g * HEAD_DIM:(g + 1) * HEAD_DIM].astype(BF16)
        vg = v[:, g * HEAD_DIM:(g + 1) * HEAD_DIM].astype(BF16)
        for hh in range(Q_GROUP):
            h = g * Q_GROUP + hh
            qh = q[:, h * HEAD_DIM:(h + 1) * HEAD_DIM].astype(BF16)
            s = _dot_nt(qh, kg) + bias_ref[h]
            sink = sink_ref[h]
            m = jnp.maximum(jnp.max(s, axis=-1, keepdims=True), sink)
            p = jnp.exp(s - m)
            den = jnp.sum(p, axis=-1, keepdims=True) + jnp.exp(sink - m)
            outs.append(_dot(p.astype(BF16), vg) / den)
    o_ref[...] = jnp.concatenate(outs, axis=-1).astype(BF16)


def _attn_prompt(proj, batch, seq, bias, sinks):
    nb = seq // WINDOW
    kcol = Q_W // KV_W
    vcol = kcol + 1
    cur = lambda c: (lambda b, n: (b * nb + n, c))
    prev = lambda c: (lambda b, n: (b * nb + jnp.maximum(n - 1, 0), c))
    return pl.pallas_call(
        _attn_prompt_kernel,
        grid=(batch, nb),
        in_specs=[pl.BlockSpec((WINDOW, Q_W), cur(0)),
                  pl.BlockSpec((WINDOW, KV_W), cur(kcol)),
                  pl.BlockSpec((WINDOW, KV_W), prev(kcol)),
                  pl.BlockSpec((WINDOW, KV_W), cur(vcol)),
                  pl.BlockSpec((WINDOW, KV_W), prev(vcol)),
                  pl.BlockSpec((None, N_HEADS, WINDOW, 2 * WINDOW), lambda b, n: (jnp.minimum(n, 1), 0, 0, 0)),
                  pl.BlockSpec(memory_space=pltpu.SMEM)],
        out_specs=pl.BlockSpec((WINDOW, Q_W), lambda b, n: (b * nb + n, 0)),
        out_shape=jax.ShapeDtypeStruct((batch * seq, Q_W), BF16),
        compiler_params=_cparams(("arbitrary", "arbitrary")),
        name="attn_prompt",
    )(proj, proj, proj, proj, proj, bias, sinks)


def _attn_sample_kernel(q_ref, kn_ref, vn_ref, ck_ref, cv_ref, bias_ref, sink_ref,
                        o_ref, nk_ref, nv_ref):
    tb = q_ref.shape[0]
    row = lax.broadcasted_iota(I32, (tb, WINDOW, KV_W), 1)
    last = row == WINDOW - 1
    nk = jnp.where(last, kn_ref[...], pltpu.roll(ck_ref[...], WINDOW - 1, 1))
    nv = jnp.where(last, vn_ref[...], pltpu.roll(cv_ref[...], WINDOW - 1, 1))
    nk_ref[...] = nk
    nv_ref[...] = nv
    lane_grp = lax.broadcasted_iota(I32, (N_HEADS, KV_W), 1) // HEAD_DIM
    head_grp = lax.broadcasted_iota(I32, (N_HEADS, KV_W), 0) // Q_GROUP
    gmask = (lane_grp == head_grp).astype(F32)
    q = q_ref[...]
    qrow = jnp.concatenate([q] * N_KV_HEADS, axis=-1) * gmask
    s = jnp.einsum('bhc,brc->bhr', qrow.astype(BF16), nk.astype(BF16),
                   preferred_element_type=F32) * (HEAD_DIM ** -0.5)
    s = s + bias_ref[...]
    sink = sink_ref[...]
    m = jnp.maximum(jnp.max(s, axis=-1, keepdims=True), sink)
    p = jnp.exp(s - m)
    den = jnp.sum(p, axis=-1, keepdims=True) + jnp.exp(sink - m)
    o = jnp.einsum('bhr,brc->bhc', p.astype(BF16), nv.astype(BF16),
                   preferred_element_type=F32) * gmask
    o64 = o[..., 0:HEAD_DIM]
    for g in range(1, N_KV_HEADS):
        o64 = o64 + o[..., g * HEAD_DIM:(g + 1) * HEAD_DIM]
    o_ref[...] = (o64 / den).astype(BF16)


def _attn_sample(q3, kn, vn, cache_k, cache_v, bias, sinks):
    nseq = q3.shape[0]
    tb = 16
    seq3 = lambda w: pl.BlockSpec((tb, WINDOW, w), lambda i: (i, 0, 0))
    return pl.pallas_call(
        _attn_sample_kernel,
        grid=(nseq // tb,),
        in_specs=[pl.BlockSpec((tb, N_HEADS, HEAD_DIM), lambda i: (i, 0, 0)),
                  pl.BlockSpec((tb, 1, KV_W), lambda i: (i, 0, 0)),
                  pl.BlockSpec((tb, 1, KV_W), lambda i: (i, 0, 0)),
                  seq3(KV_W), seq3(KV_W),
                  pl.BlockSpec((N_HEADS, WINDOW), lambda i: (0, 0)),
                  pl.BlockSpec((N_HEADS, 1), lambda i: (0, 0))],
        out_specs=[pl.BlockSpec((tb, N_HEADS, HEAD_DIM), lambda i: (i, 0, 0)),
                   seq3(KV_W), seq3(KV_W)],
        out_shape=[jax.ShapeDtypeStruct((nseq, N_HEADS, HEAD_DIM), BF16),
                   jax.ShapeDtypeStruct((nseq, WINDOW, KV_W), F32),
                   jax.ShapeDtypeStruct((nseq, WINDOW, KV_W), F32)],
        compiler_params=_cparams(("arbitrary",)),
        name="attn_sample",
    )(q3, kn, vn, cache_k, cache_v, bias, sinks.reshape(N_HEADS, 1))


def _ssm_disc_kernel(are_ref, aim_ref, ldt_ref, bre_ref, bim_ref,
                     lbr_ref, lbi_ref, bbr_ref, bbi_ref):
    a_re = are_ref[...]
    a_im = aim_ref[...]
    dt = jnp.exp(ldt_ref[...])
    lam_re = a_re * dt
    lam_im = a_im * dt
    mag = jnp.exp(lam_re)
    lb_re = mag * jnp.cos(lam_im)
    lb_im = mag * jnp.sin(lam_im)
    den = a_re * a_re + a_im * a_im
    nr = lb_re - 1.0
    ni = lb_im
    coef_re = (nr * a_re + ni * a_im) / den
    coef_im = (ni * a_re - nr * a_im) / den
    b_re = bre_ref[...]
    b_im = bim_ref[...]
    lbr_ref[...] = lb_re
    lbi_ref[...] = lb_im
    bbr_ref[...] = coef_re * b_re - coef_im * b_im
    bbi_ref[...] = coef_re * b_im + coef_im * b_re


def _ssm_disc(a_re, a_im, log_dt, b_re, b_im):
    g, p, j = N_SSM_GROUPS, SSM_STATE, SSM_GROUP
    vec = jax.ShapeDtypeStruct((g, 1, p), F32)
    mat = jax.ShapeDtypeStruct((g, j, p), F32)
    return pl.pallas_call(
        _ssm_disc_kernel,
        out_shape=[vec, vec, mat, mat],
        name="ssm_disc",
    )(a_re.reshape(g, 1, p), a_im.reshape(g, 1, p), log_dt.reshape(g, 1, 1),
      b_re.transpose(0, 2, 1), b_im.transpose(0, 2, 1))


def _block_diag_tiles(x):
    a, b = x.shape[1], x.shape[2]
    eye = jnp.eye(SUBLANE, dtype=x.dtype)
    y = jnp.einsum('kgab,gh->kgahb', x.reshape(SSM_TILES, SUBLANE, a, b), eye)
    return y.reshape(SSM_TILES, SUBLANE * a, SUBLANE * b)


def _gelu_tanh(x):
    return 0.5 * x * (1.0 + jnp.tanh(math.sqrt(2.0 / math.pi) * (x + 0.044715 * (x * x * x))))


def _ssm_kernel(*refs, nseq, tc, seq_major):
    if seq_major:
        ua_ref, ub_ref, perm_ref, perm_t_ref = refs[:4]
        refs = refs[4:]
        u = jnp.concatenate([ua_ref[...].reshape(nseq * tc, D_SSM // 2),
                             ub_ref[...].reshape(nseq * tc, D_SSM // 2)], axis=1)
        ub = _dot(perm_ref[...], u.astype(BF16)).astype(BF16)
    else:
        u = refs[0][...]
        refs = refs[1:]
        ub = u.astype(BF16)
    (h0r_ref, h0i_ref, lbr_ref, lbi_ref, bb_ref, cc_ref, d_ref, wglu_ref, bglu_ref,
     y_ref, hTr_ref, hTi_ref, hre, him, st_r, st_i) = refs
    paired = nseq == 4

    @pl.when(pl.program_id(0) == 0)
    def _():
        if paired:
            st_r[...] = jnp.concatenate([h0r_ref[...], h0r_ref[...]], axis=0)
            st_i[...] = jnp.concatenate([h0i_ref[...], h0i_ref[...]], axis=0)
        else:
            st_r[...] = h0r_ref[...]
            st_i[...] = h0i_ref[...]

    half = SSM_W // SSM_TILES
    rows = nseq * tc
    for k in range(SSM_TILES):
        bu = _dot(ub[:, k * LANE:(k + 1) * LANE], bb_ref[k])
        br, bi = bu[:, :half], bu[:, half:]
        if paired:
            ar = lbr_ref[:, k * half:(k + 1) * half]
            ai = lbi_ref[:, k * half:(k + 1) * half]
            tiles = (rows // SUBLANE, SUBLANE, half)
            pr = pltpu.roll(br.reshape(tiles), nseq, 1).reshape(rows, half)
            pi = pltpu.roll(bi.reshape(tiles), nseq, 1).reshape(rows, half)
            first = (lax.broadcasted_iota(I32, (rows, half), 0) & nseq) == 0
            br, bi = (jnp.where(first, br, ar * pr - ai * pi + br),
                      jnp.where(first, bi, ar * pi + ai * pr + bi))
        hre[:, k * half:(k + 1) * half] = br
        him[:, k * half:(k + 1) * half] = bi

    for blk in range(SSM_W // SSM_LB):
        sl = slice(blk * SSM_LB, (blk + 1) * SSM_LB)
        ar = lbr_ref[:, sl]
        ai = lbi_ref[:, sl]
        if paired:
            lower = lax.broadcasted_iota(I32, (SUBLANE, SSM_LB), 0) < nseq
            cr = jnp.where(lower, ar, ar * ar - ai * ai)
            ci = jnp.where(lower, ai, 2.0 * ar * ai)

            def body(m, carry):
                xr, xi = carry
                r0 = pl.multiple_of(m * SUBLANE, SUBLANE)
                hr = cr * xr - ci * xi + hre[pl.ds(r0, SUBLANE), sl]
                hi = cr * xi + ci * xr + him[pl.ds(r0, SUBLANE), sl]
                hre[pl.ds(r0, SUBLANE), sl] = hr
                him[pl.ds(r0, SUBLANE), sl] = hi
                return (jnp.where(lower, pltpu.roll(hr, nseq, 0), hr),
                        jnp.where(lower, pltpu.roll(hi, nseq, 0), hi))

            sr, si = lax.fori_loop(0, tc * nseq // SUBLANE, body, (st_r[:, sl], st_i[:, sl]))
        else:
            def body(t, carry):
                sr, si = carry
                r0 = pl.multiple_of(t * nseq, SUBLANE)
                br = hre[pl.ds(r0, nseq), sl]
                bi = him[pl.ds(r0, nseq), sl]
                nr = ar * sr - ai * si + br
                ni = ar * si + ai * sr + bi
                hre[pl.ds(r0, nseq), sl] = nr
                him[pl.ds(r0, nseq), sl] = ni
                return nr, ni

            sr, si = lax.fori_loop(0, tc, body, (st_r[:, sl], st_i[:, sl]))
        st_r[:, sl] = sr
        st_i[:, sl] = si

    ys = []
    for k in range(SSM_TILES):
        hr = hre[:, k * half:(k + 1) * half].astype(BF16)
        hi = him[:, k * half:(k + 1) * half].astype(BF16)
        ys.append(_dot(hr, cc_ref[k, :half, :]) + _dot(hi, cc_ref[k, half:, :]))
    yc = jnp.concatenate(ys, axis=-1)
    if seq_major:
        yc_hi, yc_lo = _split_bf16(yc)
        yc = _dot(perm_t_ref[...], yc_hi) + _dot(perm_t_ref[...], yc_lo)
    y = _gelu_tanh(yc + d_ref[...] * u)
    z = _dot(y.astype(BF16), wglu_ref[...]) + bglu_ref[...]
    y_ref[...] = (y * _sigmoid(z)).astype(BF16).reshape(y_ref.shape)

    if paired:
        hTr_ref[...] = st_r[nseq:, :]
        hTi_ref[...] = st_i[nseq:, :]
    else:
        hTr_ref[...] = st_r[...]
        hTi_ref[...] = st_i[...]


def _ssm(u_src, h0_re, h0_im, nseq, tc, lbr, lbi, bb, cc, d, w_glu, b_glu, seq_major=False):
    r = nseq * tc
    st_rows = max(nseq, SUBLANE)
    const2 = lambda shape: pl.BlockSpec(shape, lambda c: (0, 0))
    const3 = lambda shape: pl.BlockSpec(shape, lambda c: (0, 0, 0))
    if seq_major:
        steps = u_src.shape[1] // tc
        half_w = D_SSM // 2
        col0 = (Q_W + 2 * KV_W) // half_w
        t_idx, s_idx = np.divmod(np.arange(r), nseq)
        perm = np.zeros((r, r), np.float32)
        perm[np.arange(r), s_idx * tc + t_idx] = 1.0
        u_specs = [pl.BlockSpec((nseq, tc, half_w), lambda c: (0, c, col0)),
                   pl.BlockSpec((nseq, tc, half_w), lambda c: (0, c, col0 + 1)),
                   const2((r, r)), const2((r, r))]
        u_args = [u_src, u_src, jnp.asarray(perm, BF16), jnp.asarray(perm.T, BF16)]
        y_spec = pl.BlockSpec((nseq, tc, D_SSM), lambda c: (0, c, 0))
        y_shape = jax.ShapeDtypeStruct((nseq, steps * tc, D_SSM), BF16)
    else:
        steps = u_src.shape[0] // r
        u_specs = [pl.BlockSpec((r, D_SSM), lambda c: (c, 0))]
        u_args = [u_src]
        y_spec = pl.BlockSpec((r, D_SSM), lambda c: (c, 0))
        y_shape = jax.ShapeDtypeStruct((steps * r, D_SSM), BF16)
    return pl.pallas_call(
        functools.partial(_ssm_kernel, nseq=nseq, tc=tc, seq_major=seq_major),
        grid=(steps,),
        in_specs=u_specs + [
            const2((nseq, SSM_W)), const2((nseq, SSM_W)),
            const2((1, SSM_W)), const2((1, SSM_W)),
            const3((SSM_TILES, LANE, 2 * SSM_W // SSM_TILES)),
            const3((SSM_TILES, 2 * SSM_W // SSM_TILES, LANE)),
            const2((1, D_SSM)), const2((D_SSM, D_SSM)), const2((1, D_SSM))],
        out_specs=[y_spec, const2((nseq, SSM_W)), const2((nseq, SSM_W))],
        out_shape=[y_shape,
                   jax.ShapeDtypeStruct((nseq, SSM_W), F32),
                   jax.ShapeDtypeStruct((nseq, SSM_W), F32)],
        scratch_shapes=[pltpu.VMEM((r, SSM_W), F32), pltpu.VMEM((r, SSM_W), F32),
                        pltpu.VMEM((st_rows, SSM_W), F32), pltpu.VMEM((st_rows, SSM_W), F32)],
        compiler_params=_cparams(("arbitrary",)),
        name="ssm",
    )(*u_args, h0_re, h0_im, lbr, lbi, bb, cc, d.reshape(1, D_SSM), w_glu, b_glu.reshape(1, D_SSM))


def _split_bf16(x):
    hi = x.astype(BF16)
    lo = (x - hi.astype(F32)).astype(BF16)
    return hi, lo


N_MERGE_IN = 15


def _merge_kernel(*refs):
    (o_ref, y_ref, ga_ref, gs_ref, wa_ref, ws_ref, wo_ref, x_ref, gpm_ref, g1_ref, sc2_ref, sh2_ref,
     gpf_ref, wrt_ref, br_ref) = refs[:N_MERGE_IN]
    x1_ref, h2_ref, lg_ref, mix = refs[-4:]
    j = pl.program_id(1)
    a = _dot(o_ref[...], wa_ref[...])
    s = _dot(y_ref[...], ws_ref[...])
    merged = _sigmoid(ga_ref[...]) * a + _sigmoid(gs_ref[...]) * s
    contrib = _dot(merged.astype(BF16), wo_ref[...])

    @pl.when(j == 0)
    def _():
        mix[...] = contrib

    @pl.when(j > 0)
    def _():
        mix[...] += contrib

    @pl.when(j == pl.num_programs(1) - 1)
    def _():
        x1 = x_ref[...] + g1_ref[...] * _rms(mix[...], gpm_ref[...])
        x1_ref[...] = x1
        h2 = _rms(x1, gpf_ref[...]) * (1.0 + sc2_ref[...]) + sh2_ref[...]
        h2_ref[...] = h2
        h_hi, h_lo = _split_bf16(h2)
        w_hi, w_lo = _split_bf16(wrt_ref[...])
        lg_ref[...] = (_dot_nt(w_hi, h_hi) + _dot_nt(w_hi, h_lo) + _dot_nt(w_lo, h_hi)) + br_ref[...]


def _merge(o_attn, y_ssm, proj, x, mod, per_row, rows_per_seq, n_total, row0, shared, w_br_attn, w_br_ssm,
           w_out, g_post_mix, g_pre_ffn, w_router_t, b_router):
    rows = x.shape[0]
    tm = min(MERGE_TM, rows)
    blk0 = row0 // tm
    nk = D_MODEL // MERGE_TK
    ga0 = (Q_W + 2 * KV_W + D_SSM) // MERGE_TK
    gs0 = ga0 + nk
    row2 = lambda w: pl.BlockSpec((tm, w), lambda i, j: (i, 0))
    vec = pl.BlockSpec((1, D_MODEL), lambda i, j: (0, 0))
    in_specs = [row2(Q_W), row2(D_SSM),
                pl.BlockSpec((tm, MERGE_TK), lambda i, j: (i, ga0 + j)),
                pl.BlockSpec((tm, MERGE_TK), lambda i, j: (i, gs0 + j)),
                pl.BlockSpec((Q_W, MERGE_TK), lambda i, j: (0, j)),
                pl.BlockSpec((D_SSM, MERGE_TK), lambda i, j: (0, j)),
                pl.BlockSpec((MERGE_TK, D_MODEL), lambda i, j: (j, 0)),
                row2(D_MODEL), vec,
                _mod_spec(per_row, tm, rows_per_seq, 2, 2),
                _mod_spec(per_row, tm, rows_per_seq, 4, 2),
                _mod_spec(per_row, tm, rows_per_seq, 3, 2),
                vec,
                pl.BlockSpec((N_EXPERTS, D_MODEL), lambda i, j: (0, 0)),
                pl.BlockSpec((N_EXPERTS, 1), lambda i, j: (0, 0))]
    args = [o_attn, y_ssm, proj, proj, w_br_attn, w_br_ssm, w_out, x, g_post_mix.reshape(1, D_MODEL),
            mod, mod, mod, g_pre_ffn.reshape(1, D_MODEL), w_router_t, b_router.reshape(N_EXPERTS, 1)]
    assert len(args) == N_MERGE_IN
    aliases = {}
    if shared is not None:
        aliases = {len(args): 1, len(args) + 1: 2}
        in_specs += [pl.BlockSpec(memory_space=pl.ANY), pl.BlockSpec(memory_space=pl.ANY)]
        args += list(shared)
    return pl.pallas_call(
        _merge_kernel,
        grid=(rows // tm, nk),
        in_specs=in_specs,
        out_specs=[row2(D_MODEL),
                   pl.BlockSpec((tm, D_MODEL), lambda i, j: (blk0 + i, 0)),
                   pl.BlockSpec((N_EXPERTS, tm), lambda i, j: (0, blk0 + i))],
        out_shape=[jax.ShapeDtypeStruct((rows, D_MODEL), F32),
                   jax.ShapeDtypeStruct((n_total, D_MODEL), F32),
                   jax.ShapeDtypeStruct((N_EXPERTS, n_total), F32)],
        scratch_shapes=[pltpu.VMEM((tm, D_MODEL), F32)],
        input_output_aliases=aliases,
        compiler_params=_cparams(("arbitrary", "arbitrary")),
        name="merge",
    )(*args)


def _bucket_tables():
    ql = np.arange(WINDOW)[:, None]
    kl = np.arange(2 * WINDOW)[None, :]
    dist = ql + WINDOW - kl
    prompt = np.where((dist >= 0) & (dist < WINDOW), _t5_bucket_np(dist), -1).astype(np.int32)
    first = np.where(kl >= WINDOW, prompt, -1).astype(np.int32)
    d_s = (WINDOW - 1 - np.arange(WINDOW))[None, :]
    sample = np.broadcast_to(_t5_bucket_np(d_s), (SUBLANE, WINDOW)).astype(np.int32)
    return first, prompt, sample


def _front(p):
    l = 0
    batch, seq, _ = p['x_prompt'].shape
    nseq = p['x_sample'].shape[0]
    xp = p['x_prompt'].reshape(batch * seq, D_MODEL)
    xs = p['x_sample'].reshape(nseq, D_MODEL)

    c_all = jnp.concatenate([p['c_prompt'], p['c_sample'],
                             jnp.zeros((SUBLANE - (batch + nseq) % SUBLANE, D_MODEL), F32)], axis=0)
    mod = _ada(c_all, p['w_ada'][l], p['b_ada'][l])
    mod_p = mod[:batch].reshape(batch, 1, 6 * D_MODEL)
    mod_s = mod[batch:batch + nseq]

    bucket_f, bucket_p, bucket_s = _bucket_tables()
    bias_p = jnp.stack([_bias_table(bucket_f, p['rel_bias']), _bias_table(bucket_p, p['rel_bias'])])
    bias_s = _bias_table(bucket_s, p['rel_bias'])[:, 0, :]
    sinks = p['attn_sinks'][l]

    w_in_t = p['w_in'][l].astype(BF16)
    proj_p = _proj(xp, mod_p, False, seq, p['g_pre_mix'][l], w_in_t)
    proj_s = _proj(xs, mod_s, True, 1, p['g_pre_mix'][l], w_in_t)

    o_p = _attn_prompt(proj_p, batch, seq, bias_p, sinks)
    kv_p = proj_p.reshape(batch, seq, IN_W)[:, seq - WINDOW:, Q_W:Q_W + 2 * KV_W]
    new_k_p = kv_p[..., :KV_W].reshape(1, batch, WINDOW, N_KV_HEADS, HEAD_DIM)
    new_v_p = kv_p[..., KV_W:].reshape(1, batch, WINDOW, N_KV_HEADS, HEAD_DIM)
    o_s3, new_k_s, new_v_s = _attn_sample(
        proj_s[:, :Q_W].reshape(nseq, N_HEADS, HEAD_DIM),
        proj_s[:, Q_W:Q_W + KV_W].reshape(nseq, 1, KV_W),
        proj_s[:, Q_W + KV_W:Q_W + 2 * KV_W].reshape(nseq, 1, KV_W),
        p['cache_win_k'][l].reshape(nseq, WINDOW, KV_W),
        p['cache_win_v'][l].reshape(nseq, WINDOW, KV_W), bias_s, sinks)
    o_s = o_s3.reshape(nseq, Q_W)

    lbr, lbi, bbr, bbi = _ssm_disc(p['ssm_a_re'][l], p['ssm_a_im'][l], p['ssm_log_dt'][l],
                                   p['ssm_b_re'][l], p['ssm_b_im'][l])
    lbr = lbr.reshape(1, SSM_W)
    lbi = lbi.reshape(1, SSM_W)
    bb = jnp.concatenate([_block_diag_tiles(bbr), _block_diag_tiles(bbi)], axis=-1).astype(BF16)
    c_re_t = p['ssm_c_re'][l].transpose(0, 2, 1)
    c_im_t = p['ssm_c_im'][l].transpose(0, 2, 1)
    cc = jnp.concatenate([_block_diag_tiles(c_re_t), -_block_diag_tiles(c_im_t)], axis=1).astype(BF16)
    u0 = Q_W + 2 * KV_W
    zeros = jnp.zeros((batch, SSM_W), F32)
    w_glu = p['w_glu'][l].astype(BF16)
    y_p3, hr_p, hi_p = _ssm(proj_p.reshape(batch, seq, IN_W), zeros, zeros, batch, SSM_TC, lbr, lbi, bb, cc,
                            p['ssm_d'][l], w_glu, p['b_glu'][l], seq_major=True)
    y_p = y_p3.reshape(batch * seq, D_SSM)
    y_s, hr_s, hi_s = _ssm(proj_s[:, u0:u0 + D_SSM], p['state_ssm_re'][l].reshape(nseq, SSM_W),
                           p['state_ssm_im'][l].reshape(nseq, SSM_W), nseq, 1, lbr, lbi, bb, cc,
                           p['ssm_d'][l], w_glu, p['b_glu'][l])

    wa_t = p['w_br_attn'][l].astype(BF16)
    ws_t = p['w_br_ssm'][l].astype(BF16)
    wo = p['w_out'][l].astype(BF16)
    n_total = batch * seq + nseq
    merge = functools.partial(_merge, w_br_attn=wa_t, w_br_ssm=ws_t, w_out=wo, g_post_mix=p['g_post_mix'][l],
                              g_pre_ffn=p['g_pre_ffn'][l], w_router_t=p['w_router'][l].T,
                              b_router=p['b_router'][l])
    x1_p, h2_buf, lg_buf = merge(o_p, y_p, proj_p, xp, mod_p, False, seq, n_total, 0, None)
    x1_s, h2_all, lg_all = merge(o_s, y_s, proj_s, xs, mod_s, True, 1, n_total, batch * seq, (h2_buf, lg_buf))

    st = lambda h, n: h.reshape(1, n, N_SSM_GROUPS, SSM_STATE)
    return dict(
        mod_p=mod_p, mod_s=mod_s, h2_all=h2_all, lg_all=lg_all,
        p=dict(proj=proj_p, o_attn=o_p, new_k=new_k_p, new_v=new_v_p, y_ssm=y_p, h_re=st(hr_p, batch),
               h_im=st(hi_p, batch), x1=x1_p),
        s=dict(proj=proj_s, o_attn=o_s, new_k=new_k_s.reshape(1, nseq, WINDOW, N_KV_HEADS, HEAD_DIM),
               new_v=new_v_s.reshape(1, nseq, WINDOW, N_KV_HEADS, HEAD_DIM), y_ssm=y_s,
               h_re=st(hr_s, nseq), h_im=st(hi_s, nseq), x1=x1_s))


def _count_steps(c, step, n_max):
    out = jnp.zeros_like(c)
    for q in range(-(-n_max // step)):
        out = out + jnp.where(c > float(q * step), 1.0, 0.0)
    return out


def _route_kernel(lg_ref, pos_ref, gate_ref, texp_ref, trows_ref, ntiles_ref, pstart_ref, plen_ref):
    lg = lg_ref[...]
    e, tn = lg.shape
    erow = lax.broadcasted_iota(I32, (e, tn), 0).astype(F32)
    work = lg
    vals, hits = [], []
    for _ in range(TOP_K):
        m = jnp.max(work, axis=0, keepdims=True)
        idx = jnp.min(jnp.where(work == m, erow, float(e)), axis=0, keepdims=True)
        hit = erow == idx
        vals.append(m)
        hits.append(hit)
        work = jnp.where(hit, -jnp.inf, work)
    ex = [jnp.exp(v - vals[0]) for v in vals]
    den = ex[0] + ex[1] + ex[2] + ex[3]
    gate_ref[...] = jnp.concatenate([x / den for x in ex], axis=0)

    chosen = jnp.zeros((e, tn), F32)
    for hit in hits:
        chosen = chosen + jnp.where(hit, 1.0, 0.0)
    chosen_b = chosen.astype(BF16)
    tri = (lax.broadcasted_iota(I32, (LANE, LANE), 0) <= lax.broadcasted_iota(I32, (LANE, LANE), 1))
    tri = jnp.where(tri, 1.0, 0.0).astype(BF16)
    carry = jnp.zeros((e, 1), F32)
    ranks = []
    for b in range(tn // LANE):
        blk = chosen[:, b * LANE:(b + 1) * LANE]
        inc = _dot(chosen_b[:, b * LANE:(b + 1) * LANE], tri) + carry
        ranks.append(inc - blk)
        carry = inc[:, LANE - 1:LANE]
    rank = jnp.concatenate(ranks, axis=1)
    cnt_col = carry
    cnt_row = _dot_nt(jnp.ones((SUBLANE, tn), BF16), chosen_b)[0:1, :]

    tiles_col = _count_steps(cnt_col, EXP_TM, tn)
    tiles_row = _count_steps(cnt_row, EXP_TM, tn)
    ee_r = lax.broadcasted_iota(I32, (e, e), 0)
    ee_c = lax.broadcasted_iota(I32, (e, e), 1)
    tstart_col = jnp.sum(jnp.where(ee_c < ee_r, tiles_row, 0.0), axis=1, keepdims=True)
    ntiles = jnp.sum(tiles_row, axis=1, keepdims=True)
    rstart_col = tstart_col * float(EXP_TM)
    pos = [jnp.sum(jnp.where(hit, rstart_col + rank, 0.0), axis=0, keepdims=True) for hit in hits]
    pos_ref[...] = jnp.concatenate(pos, axis=0).astype(I32)

    mm = lax.broadcasted_iota(I32, (e, LANE), 1).astype(F32)
    e_col = lax.broadcasted_iota(I32, (e, LANE), 0).astype(F32)
    own = (mm >= tstart_col) & (mm < tstart_col + tiles_col)
    texp = jnp.sum(jnp.where(own, e_col, 0.0), axis=0, keepdims=True)
    rows_here = jnp.minimum(float(EXP_TM), cnt_col - (mm - tstart_col) * float(EXP_TM))
    trows = jnp.sum(jnp.where(own, rows_here, 0.0), axis=0, keepdims=True)
    last_e = jnp.max(jnp.where(tiles_col > 0.0, e_col, 0.0), axis=0, keepdims=True)
    texp = jnp.where(mm[0:1, :] < ntiles, texp, last_e)
    texp_ref[...] = texp.astype(I32)
    trows_ref[...] = trows.astype(I32)
    ntiles_ref[...] = jnp.broadcast_to(ntiles, (1, LANE)).astype(I32)
    nsub_col = _count_steps(cnt_col, EXP_SUB, tn)
    pstart_ref[...] = jnp.broadcast_to(rstart_col + cnt_col, (e, LANE)).astype(I32)
    plen_ref[...] = jnp.broadcast_to(nsub_col * float(EXP_SUB) - cnt_col, (e, LANE)).astype(I32)


def _route(lg_t):
    e, tn = lg_t.shape
    i32 = lambda shape: jax.ShapeDtypeStruct(shape, I32)
    return pl.pallas_call(
        _route_kernel,
        out_shape=[i32((TOP_K, tn)), jax.ShapeDtypeStruct((TOP_K, tn), F32),
                   i32((1, LANE)), i32((1, LANE)), i32((1, LANE)), i32((e, LANE)), i32((e, LANE))],
        compiler_params=pltpu.CompilerParams(vmem_limit_bytes=VMEM_LIMIT),
        name="route",
    )(lg_t)


def _max_tiles(n_tok):
    return (n_tok * TOP_K) // EXP_TM + N_EXPERTS


def _dispatch_kernel(pos_ref, pstart_ref, plen_ref, h2_ref, zero_ref, xs_ref, sem):
    i = pl.program_id(0)

    def row_copy(src, s, d):
        return pltpu.make_async_copy(src.at[pl.ds(s, 1)], xs_ref.at[pl.ds(d, 1)], sem)

    for t in range(TOK_BLK):
        for k in range(TOP_K):
            row_copy(h2_ref, t, pos_ref[k, t]).start(priority=k % 2)
    for _ in range(TOP_K):
        pltpu.make_async_copy(h2_ref, xs_ref.at[pl.ds(0, TOK_BLK)], sem).wait()

    @pl.when(i == 0)
    def _():
        def per_expert(ex, c):
            n = plen_ref[ex]
            s = pstart_ref[ex]

            def zissue(r, cc):
                row_copy(zero_ref, 0, s + r).start()
                return cc

            def zdrain(r, cc):
                row_copy(zero_ref, 0, 0).wait()
                return cc

            lax.fori_loop(0, n, zissue, 0)
            lax.fori_loop(0, n, zdrain, 0)
            return c

        lax.fori_loop(0, N_EXPERTS, per_expert, 0)


def _dispatch(pos3, pstart, plen, h2_all):
    n_tok = h2_all.shape[0]
    n_rows = _max_tiles(n_tok) * EXP_TM
    smem = pl.BlockSpec(memory_space=pltpu.SMEM)
    hbm = pl.BlockSpec(memory_space=pl.ANY)
    return pl.pallas_call(
        _dispatch_kernel,
        grid=(n_tok // TOK_BLK,),
        in_specs=[pl.BlockSpec((None, TOP_K, TOK_BLK), lambda i: (i, 0, 0), memory_space=pltpu.SMEM),
                  smem, smem,
                  pl.BlockSpec((TOK_BLK, D_MODEL), lambda i: (i, 0)),
                  pl.BlockSpec((SUBLANE, D_MODEL), lambda i: (0, 0))],
        out_specs=hbm,
        out_shape=jax.ShapeDtypeStruct((n_rows, D_MODEL), F32),
        scratch_shapes=[pltpu.SemaphoreType.DMA(())],
        compiler_params=_cparams(("arbitrary",)),
        name="dispatch",
    )(pos3, pstart, plen, h2_all, jnp.zeros((SUBLANE, D_MODEL), F32))


def _expert_kernel(texp_ref, trows_ref, nt_ref, xs_hbm, wg_ref, wl_ref, bg_ref, bl_ref, wd_ref, bd_ref,
                   o_ref, x_ref, x_sem, xb_scr, act_scr, wg_scr, wl_scr, wd_scr):
    m = pl.program_id(0)
    s = pl.program_id(1)
    n_tiles = nt_ref[0]
    valid = m < n_tiles

    rows = trows_ref[m]
    nbig = lax.shift_right_logical(rows, int(math.log2(EXP_BIG)))
    big_rows = nbig * EXP_BIG
    nsmall = lax.shift_right_logical(rows - big_rows + (EXP_SUB - 1), int(math.log2(EXP_SUB)))
    nsub_done = nbig * (EXP_BIG // EXP_SUB) + nsmall

    def x_copy(tile):
        return pltpu.make_async_copy(xs_hbm.at[pl.ds(pl.multiple_of(tile * EXP_TM, EXP_TM), EXP_TM)],
                                     x_ref, x_sem)

    @pl.when((m == 0) & (s == 0))
    def _():
        x_copy(0).start()

    @pl.when(valid & (s == 0))
    def _():
        x_copy(m).wait()

        def to_bf16(r, c):
            r0 = pl.multiple_of(r * EXP_SUB, EXP_SUB)
            xb_scr[pl.ds(r0, EXP_SUB), :] = x_ref[pl.ds(r0, EXP_SUB), :].astype(BF16)
            return c

        lax.fori_loop(0, nsub_done, to_bf16, 0)

        @pl.when(m + 1 < n_tiles)
        def _():
            x_copy(m + 1).start()

    def over_rows(first, step):
        def big(r, c):
            step(pl.multiple_of(r * EXP_BIG, EXP_BIG), EXP_BIG)
            return c

        def small(r, c):
            step(pl.multiple_of(big_rows + r * EXP_SUB, EXP_SUB), EXP_SUB)
            return c

        @pl.when(nbig > 0)
        def _():
            first(EXP_BIG)
            lax.fori_loop(1, nbig, big, 0)
            lax.fori_loop(0, nsmall, small, 0)

        @pl.when(nbig == 0)
        def _():
            first(EXP_SUB)
            lax.fori_loop(1, nsmall, small, 0)

    @pl.when(valid & (s < EXP_NF))
    def _():
        def finish(r0, n, hg, hl):
            x_glu = jnp.minimum(hg, SWIGLU_LIMIT)
            x_lin = jnp.clip(hl, -SWIGLU_LIMIT, SWIGLU_LIMIT)
            act = x_glu * _sigmoid(SWIGLU_ALPHA * x_glu) * (x_lin + 1.0)
            act_scr[s, pl.ds(r0, n), :] = act.astype(BF16)

        def first(n):
            xb = xb_scr[0:n, :]
            hg = jnp.broadcast_to(bg_ref[...], (n, EXP_TF))
            hl = jnp.broadcast_to(bl_ref[...], (n, EXP_TF))
            for q in range(D_MODEL // EXP_KQ):
                ks = slice(q * EXP_KQ, (q + 1) * EXP_KQ)
                wgq = wg_ref[ks, :].astype(BF16)
                wlq = wl_ref[ks, :].astype(BF16)
                wg_scr[ks, :] = wgq
                wl_scr[ks, :] = wlq
                hg = hg + _dot(xb[:, ks], wgq)
                hl = hl + _dot(xb[:, ks], wlq)
            finish(0, n, hg, hl)

        def step(r0, n):
            xb = xb_scr[pl.ds(r0, n), :]
            finish(r0, n, _dot(xb, wg_scr[...]) + bg_ref[...], _dot(xb, wl_scr[...]) + bl_ref[...])

        over_rows(first, step)

    @pl.when(valid & (s >= EXP_NF))
    def _():
        def first(n):
            acc = jnp.broadcast_to(bd_ref[...], (n, EXP_TD))
            for f in range(EXP_NF):
                fs = slice(f * EXP_TF, (f + 1) * EXP_TF)
                wdq = wd_ref[fs, :].astype(BF16)
                wd_scr[fs, :] = wdq
                acc = acc + _dot(act_scr[f, 0:n, :], wdq)
            o_ref[0:n, :] = acc

        def step(r0, n):
            acc = jnp.broadcast_to(bd_ref[...], (n, EXP_TD))
            for f in range(EXP_NF):
                acc = acc + _dot(act_scr[f, pl.ds(r0, n), :], wd_scr[f * EXP_TF:(f + 1) * EXP_TF, :])
            o_ref[pl.ds(r0, n), :] = acc

        def zero(r, c):
            r0 = pl.multiple_of(r * EXP_SUB, EXP_SUB)
            o_ref[pl.ds(r0, EXP_SUB), :] = jnp.zeros((EXP_SUB, EXP_TD), F32)
            return c

        over_rows(first, step)
        lax.fori_loop(nsub_done, EXP_TM // EXP_SUB, zero, 0)


def _experts(texp, trows, ntiles, xs, w_gate_up, b_gate_up, w_down, b_down):
    n_tiles = xs.shape[0] // EXP_TM
    nsteps = EXP_NF + EXP_ND

    def tile(m, nt):
        return jnp.minimum(m, nt[0] - 1)

    def ea(m, s, te, nt):
        return te[jnp.where(s < EXP_NF, m, jnp.minimum(m + 1, nt[0] - 1))]

    def fa(m, s, nt):
        return jnp.where((m < nt[0]) & (s < EXP_NF), s, 0)

    def fb(m, s, nt):
        return jnp.where(m < nt[0], jnp.maximum(s - EXP_NF, 0), EXP_ND - 1)

    grid_spec = pltpu.PrefetchScalarGridSpec(
        num_scalar_prefetch=3,
        grid=(ntiles[0], nsteps),
        in_specs=[
            pl.BlockSpec(memory_space=pl.ANY),
            pl.BlockSpec((None, D_MODEL, EXP_TF),
                         lambda m, s, te, tr, nt: (ea(m, s, te, nt), 0, fa(m, s, nt))),
            pl.BlockSpec((None, D_MODEL, EXP_TF),
                         lambda m, s, te, tr, nt: (ea(m, s, te, nt), 0, EXP_NF + fa(m, s, nt))),
            pl.BlockSpec((None, 1, EXP_TF), lambda m, s, te, tr, nt: (ea(m, s, te, nt), 0, fa(m, s, nt))),
            pl.BlockSpec((None, 1, EXP_TF),
                         lambda m, s, te, tr, nt: (ea(m, s, te, nt), 0, EXP_NF + fa(m, s, nt))),
            pl.BlockSpec((None, D_FF, EXP_TD), lambda m, s, te, tr, nt: (te[m], 0, fb(m, s, nt))),
            pl.BlockSpec((None, 1, EXP_TD), lambda m, s, te, tr, nt: (te[m], 0, fb(m, s, nt))),
        ],
        out_specs=pl.BlockSpec((EXP_TM, EXP_TD), lambda m, s, te, tr, nt: (tile(m, nt), fb(m, s, nt))),
        scratch_shapes=[pltpu.VMEM((EXP_TM, D_MODEL), F32), pltpu.SemaphoreType.DMA(()),
                        pltpu.VMEM((EXP_TM, D_MODEL), BF16),
                        pltpu.VMEM((EXP_NF, EXP_TM, EXP_TF), BF16),
                        pltpu.VMEM((D_MODEL, EXP_TF), BF16), pltpu.VMEM((D_MODEL, EXP_TF), BF16),
                        pltpu.VMEM((D_FF, EXP_TD), BF16)],
    )
    return pl.pallas_call(
        _expert_kernel,
        grid_spec=grid_spec,
        out_shape=jax.ShapeDtypeStruct((xs.shape[0], D_MODEL), F32),
        compiler_params=pltpu.CompilerParams(dimension_semantics=("arbitrary", "arbitrary"),
                                             vmem_limit_bytes=EXP_VMEM_LIMIT),
        name="experts",
    )(texp, trows, ntiles, xs, w_gate_up, w_gate_up,
      b_gate_up.reshape(N_EXPERTS, 1, 2 * D_FF), b_gate_up.reshape(N_EXPERTS, 1, 2 * D_FF),
      w_down, b_down.reshape(N_EXPERTS, 1, D_MODEL))


def _combine_kernel(pos_ref, pos_next_ref, gate_ref, ys_ref, x1_ref, g2_ref, gpf_ref, o_ref, buf, sem):
    i = pl.program_id(0)
    slot = lax.rem(i, 2)

    def gather(p_ref, sl):
        for t in range(TOK_BLK):
            for k in range(TOP_K):
                pltpu.make_async_copy(ys_ref.at[pl.ds(p_ref[k, t], 1)], buf.at[sl, k, pl.ds(t, 1)],
                                      sem.at[sl]).start(priority=k % 2)

    @pl.when(i == 0)
    def _():
        gather(pos_ref, 0)

    @pl.when(i + 1 < pl.num_programs(0))
    def _():
        gather(pos_next_ref, 1 - slot)

    for k in range(TOP_K):
        pltpu.make_async_copy(ys_ref.at[pl.ds(0, TOK_BLK)], buf.at[slot, k], sem.at[slot]).wait()
    g = gate_ref[...]
    f = g[:, 0:1] * buf[slot, 0]
    for k in range(1, TOP_K):
        f = f + g[:, k:k + 1] * buf[slot, k]
    o_ref[...] = x1_ref[...] + g2_ref[...] * _rms(f, gpf_ref[...])


def _combine(pos3, gates_t, ys, x1, mod, per_row, rows_per_seq, g_post_ffn, blk0):
    rows = x1.shape[0]
    nblk = rows // TOK_BLK
    return pl.pallas_call(
        _combine_kernel,
        grid=(nblk,),
        in_specs=[pl.BlockSpec((None, TOP_K, TOK_BLK), lambda i: (blk0 + i, 0, 0), memory_space=pltpu.SMEM),
                  pl.BlockSpec((None, TOP_K, TOK_BLK), lambda i: (blk0 + jnp.minimum(i + 1, nblk - 1), 0, 0),
                               memory_space=pltpu.SMEM),
                  pl.BlockSpec((TOK_BLK, TOP_K), lambda i: (blk0 + i, 0)),
                  pl.BlockSpec(memory_space=pl.ANY),
                  pl.BlockSpec((TOK_BLK, D_MODEL), lambda i: (i, 0)),
                  _mod_spec(per_row, TOK_BLK, rows_per_seq, 5, 1),
                  pl.BlockSpec((1, D_MODEL), lambda i: (0, 0))],
        out_specs=pl.BlockSpec((TOK_BLK, D_MODEL), lambda i: (i, 0)),
        out_shape=jax.ShapeDtypeStruct((rows, D_MODEL), F32),
        scratch_shapes=[pltpu.VMEM((2, TOP_K, TOK_BLK, D_MODEL), F32), pltpu.SemaphoreType.DMA((2,))],
        compiler_params=_cparams(("arbitrary",)),
        name="combine",
    )(pos3, pos3, gates_t, ys, x1, mod, g_post_ffn.reshape(1, D_MODEL))


def _moe(fr, p, batch, seq, nseq):
    l = 0
    h2_all, lg_all = fr['h2_all'], fr['lg_all']
    n_tok = h2_all.shape[0]
    pos, gates, texp, trows, ntiles, pstart, plen = _route(lg_all)
    pos3 = pos.reshape(TOP_K, n_tok // TOK_BLK, TOK_BLK).transpose(1, 0, 2)
    xs = _dispatch(pos3, pstart[:, 0], plen[:, 0], h2_all)
    ys = _experts(texp[0], trows[0], ntiles[0, :1], xs, p['w_gate_up'][l], p['b_gate_up'][l],
                  p['w_down'][l], p['b_down'][l])
    gates_t = gates.T
    y_p = _combine(pos3, gates_t, ys, fr['p']['x1'], fr['mod_p'], False, seq, p['g_post_ffn'][l], 0)
    y_s = _combine(pos3, gates_t, ys, fr['s']['x1'], fr['mod_s'], True, 1, p['g_post_ffn'][l],
                   batch * seq // TOK_BLK)
    return y_p, y_s


def kernel(x_prompt, x_sample, c_prompt, c_sample, cache_win_k, cache_win_v, state_ssm_re, state_ssm_im, w_ada, b_ada, g_pre_mix, g_post_mix, g_pre_ffn, g_post_ffn, w_in, attn_sinks, rel_bias, ssm_a_re, ssm_a_im, ssm_log_dt, ssm_b_re, ssm_b_im, ssm_c_re, ssm_c_im, ssm_d, w_glu, b_glu, w_br_attn, w_br_ssm, w_out, w_router, b_router, w_gate_up, b_gate_up, w_down, b_down):
    p = dict(locals())
    batch, seq, _ = x_prompt.shape
    nseq = x_sample.shape[0]
    fr = _front(p)
    y_p, y_s = _moe(fr, p, batch, seq, nseq)
    fp, fs = fr['p'], fr['s']
    return (y_p.reshape(batch, seq, D_MODEL), y_s.reshape(nseq, 1, D_MODEL),
            fp['new_k'], fp['new_v'], fp['h_re'], fp['h_im'],
            fs['new_k'], fs['new_v'], fs['h_re'], fs['h_im'])
```

```python
import functools
import math

import numpy as np
import jax
import jax.numpy as jnp
from jax import lax
from jax.experimental import pallas as pl
from jax.experimental.pallas import tpu as pltpu

F32 = jnp.float32
BF16 = jnp.bfloat16
I32 = jnp.int32

D_MODEL = 2048
N_HEADS = 16
N_KV_HEADS = 4
HEAD_DIM = 64
Q_GROUP = N_HEADS // N_KV_HEADS
WINDOW = 128
N_BUCKETS = 32
MAX_EXACT = N_BUCKETS // 2
MAX_DISTANCE = 128
D_SSM = 1024
SSM_GROUP = 16
N_SSM_GROUPS = 64
SSM_STATE = 64
N_EXPERTS = 32
TOP_K = 4
D_FF = 2048
SWIGLU_LIMIT = 7.0
SWIGLU_ALPHA = 1.702
NORM_EPS = 1e-6
NEG_INF = -1e30
Q_W = N_HEADS * HEAD_DIM
KV_W = N_KV_HEADS * HEAD_DIM
IN_W = Q_W + 2 * KV_W + D_SSM + 2 * D_MODEL
SSM_W = N_SSM_GROUPS * SSM_STATE

LANE = 128
SUBLANE = 8
VMEM_LIMIT = 56 * 1024 * 1024
EXP_VMEM_LIMIT = 60 * 1024 * 1024

ADA_TN = 1024
ATTN_TB = 16
PROJ_TM = 1024
PROJ_TN = 512
MERGE_TM = 512
MERGE_TK = 512
SSM_TC = 128
SSM_LB = 512
SSM_TILES = D_SSM // LANE
TOK_BLK = 128
EXP_TM = 1152
EXP_BIG = 512
EXP_SUB = 128
EXP_TF = 512
EXP_TD = 512
EXP_KQ = 512
EXP_NF = D_FF // EXP_TF
EXP_ND = D_MODEL // EXP_TD


def _cparams(sem):
    return pltpu.CompilerParams(dimension_semantics=sem, vmem_limit_bytes=VMEM_LIMIT)


def _sigmoid(x):
    return 1.0 / (1.0 + jnp.exp(-x))


def _rms(x, g):
    return x * lax.rsqrt(jnp.mean(x * x, axis=-1, keepdims=True) + NORM_EPS) * g


def _dot(a, b):
    return jnp.dot(a, b, preferred_element_type=F32)


def _dot_nt(a, b):
    return lax.dot_general(a, b, (((1,), (1,)), ((), ())), preferred_element_type=F32)


def _ada_kernel(c_ref, w_ref, b_ref, o_ref):
    c = c_ref[...]
    s = (c * _sigmoid(c)).astype(BF16)
    o_ref[...] = _dot(s, w_ref[...].astype(BF16)) + b_ref[...]


def _ada(c_all, w_ada, b_ada):
    rows = c_all.shape[0]
    tn = ADA_TN
    n = w_ada.shape[1]
    return pl.pallas_call(
        _ada_kernel,
        grid=(n // tn,),
        in_specs=[pl.BlockSpec((rows, D_MODEL), lambda j: (0, 0)),
                  pl.BlockSpec((D_MODEL, tn), lambda j: (0, j)),
                  pl.BlockSpec((1, tn), lambda j: (0, j))],
        out_specs=pl.BlockSpec((rows, tn), lambda j: (0, j)),
        out_shape=jax.ShapeDtypeStruct((rows, n), F32),
        compiler_params=_cparams(("arbitrary",)),
        name="ada",
    )(c_all, w_ada, b_ada.reshape(1, n))


def _t5_bucket_np(dist):
    n = np.maximum(dist, 0)
    nf = np.maximum(n, 1).astype(np.float64)
    large = MAX_EXACT + (np.log(nf / MAX_EXACT) / math.log(MAX_DISTANCE / MAX_EXACT)
                         * (N_BUCKETS - MAX_EXACT)).astype(np.int32)
    large = np.minimum(large, N_BUCKETS - 1)
    return np.where(n < MAX_EXACT, n, large).astype(np.int32)


def _bias_kernel(bucket_ref, rb_ref, o_ref, *, mask_cols):
    h = pl.program_id(0)
    bucket = bucket_ref[...]
    acc = jnp.full(bucket.shape, NEG_INF, F32)
    for b in range(N_BUCKETS):
        acc = jnp.where(bucket == b, rb_ref[b, h], acc)
    o_ref[1] = acc
    col = lax.broadcasted_iota(I32, bucket.shape, 1)
    o_ref[0] = jnp.where(col >= mask_cols, acc, NEG_INF)


def _bias_table(bucket_np, rel_bias, mask_cols):
    r, c = bucket_np.shape
    return pl.pallas_call(
        functools.partial(_bias_kernel, mask_cols=mask_cols),
        grid=(N_HEADS,),
        in_specs=[pl.BlockSpec((r, c), lambda h: (0, 0)),
                  pl.BlockSpec(memory_space=pltpu.SMEM)],
        out_specs=pl.BlockSpec((2, None, r, c), lambda h: (0, h, 0, 0)),
        out_shape=jax.ShapeDtypeStruct((2, N_HEADS, r, c), F32),
        compiler_params=_cparams(("arbitrary",)),
        name="bias",
    )(jnp.asarray(bucket_np), rel_bias)


def _proj_kernel(x_ref, sc_ref, sh_ref, g_ref, w_ref, o_ref, h_scr):
    @pl.when(pl.program_id(1) == 0)
    def _():
        h = _rms(x_ref[...], g_ref[...]) * (1.0 + sc_ref[...]) + sh_ref[...]
        h_scr[...] = h.astype(BF16)

    o_ref[...] = _dot(h_scr[...], w_ref[...])


def _mod_spec(per_row, tm, rows_per_seq, col, nargs):
    if per_row:
        if nargs == 1:
            return pl.BlockSpec((tm, D_MODEL), lambda i: (i, col))
        return pl.BlockSpec((tm, D_MODEL), lambda i, j: (i, col))
    tiles_per_seq = rows_per_seq // tm
    if nargs == 1:
        return pl.BlockSpec((None, 1, D_MODEL), lambda i: (i // tiles_per_seq, 0, col))
    return pl.BlockSpec((None, 1, D_MODEL), lambda i, j: (i // tiles_per_seq, 0, col))


def _proj(x, mod, per_row, rows_per_seq, g_pre, w_in):
    rows = x.shape[0]
    tm = min(PROJ_TM, rows)
    return pl.pallas_call(
        _proj_kernel,
        grid=(rows // tm, IN_W // PROJ_TN),
        in_specs=[pl.BlockSpec((tm, D_MODEL), lambda i, j: (i, 0)),
                  _mod_spec(per_row, tm, rows_per_seq, 1, 2),
                  _mod_spec(per_row, tm, rows_per_seq, 0, 2),
                  pl.BlockSpec((1, D_MODEL), lambda i, j: (0, 0)),
                  pl.BlockSpec((D_MODEL, PROJ_TN), lambda i, j: (0, j))],
        out_specs=pl.BlockSpec((tm, PROJ_TN), lambda i, j: (i, j)),
        out_shape=jax.ShapeDtypeStruct((rows, IN_W), F32),
        scratch_shapes=[pltpu.VMEM((tm, D_MODEL), BF16)],
        compiler_params=_cparams(("arbitrary", "arbitrary")),
        name="proj",
    )(x, mod, mod, g_pre.reshape(1, D_MODEL), w_in)


def _attn_prompt_kernel(q_ref, kc_ref, kp_ref, vc_ref, vp_ref, bias_ref, sink_ref, o_ref):
    q = q_ref[...] * (HEAD_DIM ** -0.5)
    k = jnp.concatenate([kp_ref[...], kc_ref[...]], axis=0)
    v = jnp.concatenate([vp_ref[...], vc_ref[...]], axis=0)
    outs = []
    for g in range(N_KV_HEADS):
        kg = k[:, g * HEAD_DIM:(g + 1) * HEAD_DIM].astype(BF16)
        vg = v[:, g * HEAD_DIM:(g + 1) * HEAD_DIM].astype(BF16)
        for hh in range(Q_GROUP):
            h = g * Q_GROUP + hh
            qh = q[:, h * HEAD_DIM:(h + 1) * HEAD_DIM].astype(BF16)
            s = _dot_nt(qh, kg) + bias_ref[h]
            sink = sink_ref[h]
            m = jnp.maximum(jnp.max(s, axis=-1, keepdims=True), sink)
            p = jnp.exp(s - m)
            den = jnp.sum(p, axis=-1, keepdims=True) + jnp.exp(sink - m)
            outs.append(_dot(p.astype(BF16), vg) / den)
    o_ref[...] = jnp.concatenate(outs, axis=-1).astype(BF16)


def _attn_prompt(proj, batch, seq, bias, sinks):
    nb = seq // WINDOW
    kcol = Q_W // KV_W
    vcol = kcol + 1
    cur = lambda c: (lambda b, n: (b * nb + n, c))
    prev = lambda c: (lambda b, n: (b * nb + jnp.maximum(n - 1, 0), c))
    return pl.pallas_call(
        _attn_prompt_kernel,
        grid=(batch, nb),
        in_specs=[pl.BlockSpec((WINDOW, Q_W), cur(0)),
                  pl.BlockSpec((WINDOW, KV_W), cur(kcol)),
                  pl.BlockSpec((WINDOW, KV_W), prev(kcol)),
                  pl.BlockSpec((WINDOW, KV_W), cur(vcol)),
                  pl.BlockSpec((WINDOW, KV_W), prev(vcol)),
                  pl.BlockSpec((None, N_HEADS, WINDOW, 2 * WINDOW), lambda b, n: (jnp.minimum(n, 1), 0, 0, 0)),
                  pl.BlockSpec(memory_space=pltpu.SMEM)],
        out_specs=pl.BlockSpec((WINDOW, Q_W), lambda b, n: (b * nb + n, 0)),
        out_shape=jax.ShapeDtypeStruct((batch * seq, Q_W), BF16),
        compiler_params=_cparams(("arbitrary", "arbitrary")),
        name="attn_prompt",
    )(proj, proj, proj, proj, proj, bias, sinks)


def _attn_sample_kernel(q_ref, kn_ref, vn_ref, ck_ref, cv_ref, bias_ref, sink_ref,
                        o_ref, nk_ref, nv_ref):
    tb = q_ref.shape[0]
    row = lax.broadcasted_iota(I32, (tb, WINDOW, KV_W), 1)
    last = row == WINDOW - 1
    nk = jnp.where(last, kn_ref[...], pltpu.roll(ck_ref[...], WINDOW - 1, 1))
    nv = jnp.where(last, vn_ref[...], pltpu.roll(cv_ref[...], WINDOW - 1, 1))
    nk_ref[...] = nk
    nv_ref[...] = nv
    lane_grp = lax.broadcasted_iota(I32, (N_HEADS, KV_W), 1) // HEAD_DIM
    head_grp = lax.broadcasted_iota(I32, (N_HEADS, KV_W), 0) // Q_GROUP
    gmask = (lane_grp == head_grp).astype(F32)
    q = q_ref[...]
    qrow = jnp.concatenate([q] * N_KV_HEADS, axis=-1) * gmask
    s = jnp.einsum('bhc,brc->bhr', qrow.astype(BF16), nk.astype(BF16),
                   preferred_element_type=F32) * (HEAD_DIM ** -0.5)
    s = s + bias_ref[...]
    sink = sink_ref[...]
    m = jnp.maximum(jnp.max(s, axis=-1, keepdims=True), sink)
    p = jnp.exp(s - m)
    den = jnp.sum(p, axis=-1, keepdims=True) + jnp.exp(sink - m)
    o = jnp.einsum('bhr,brc->bhc', p.astype(BF16), nv.astype(BF16),
                   preferred_element_type=F32) * gmask
    o64 = o[..., 0:HEAD_DIM]
    for g in range(1, N_KV_HEADS):
        o64 = o64 + o[..., g * HEAD_DIM:(g + 1) * HEAD_DIM]
    o_ref[...] = (o64 / den).astype(BF16)


def _attn_sample(q3, kn, vn, cache_k, cache_v, bias, sinks):
    nseq = q3.shape[0]
    tb = ATTN_TB
    seq3 = lambda w: pl.BlockSpec((tb, WINDOW, w), lambda i: (i, 0, 0))
    return pl.pallas_call(
        _attn_sample_kernel,
        grid=(nseq // tb,),
        in_specs=[pl.BlockSpec((tb, N_HEADS, HEAD_DIM), lambda i: (i, 0, 0)),
                  pl.BlockSpec((tb, 1, KV_W), lambda i: (i, 0, 0)),
                  pl.BlockSpec((tb, 1, KV_W), lambda i: (i, 0, 0)),
                  seq3(KV_W), seq3(KV_W),
                  pl.BlockSpec((N_HEADS, WINDOW), lambda i: (0, 0)),
                  pl.BlockSpec((N_HEADS, 1), lambda i: (0, 0))],
        out_specs=[pl.BlockSpec((tb, N_HEADS, HEAD_DIM), lambda i: (i, 0, 0)),
                   seq3(KV_W), seq3(KV_W)],
        out_shape=[jax.ShapeDtypeStruct((nseq, N_HEADS, HEAD_DIM), BF16),
                   jax.ShapeDtypeStruct((nseq, WINDOW, KV_W), F32),
                   jax.ShapeDtypeStruct((nseq, WINDOW, KV_W), F32)],
        compiler_params=_cparams(("arbitrary",)),
        name="attn_sample",
    )(q3, kn, vn, cache_k, cache_v, bias, sinks.reshape(N_HEADS, 1))


def _ssm_disc_kernel(are_ref, aim_ref, ldt_ref, bre_ref, bim_ref,
                     lbr_ref, lbi_ref, bbr_ref, bbi_ref):
    a_re = are_ref[...]
    a_im = aim_ref[...]
    dt = jnp.exp(ldt_ref[...])
    lam_re = a_re * dt
    lam_im = a_im * dt
    mag = jnp.exp(lam_re)
    lb_re = mag * jnp.cos(lam_im)
    lb_im = mag * jnp.sin(lam_im)
    den = a_re * a_re + a_im * a_im
    nr = lb_re - 1.0
    ni = lb_im
    coef_re = (nr * a_re + ni * a_im) / den
    coef_im = (ni * a_re - nr * a_im) / den
    b_re = bre_ref[...]
    b_im = bim_ref[...]
    lbr_ref[...] = lb_re
    lbi_ref[...] = lb_im
    bbr_ref[...] = coef_re * b_re - coef_im * b_im
    bbi_ref[...] = coef_re * b_im + coef_im * b_re


def _ssm_disc(a_re, a_im, log_dt, b_re, b_im):
    g, p, j = N_SSM_GROUPS, SSM_STATE, SSM_GROUP
    vec = jax.ShapeDtypeStruct((g, 1, p), F32)
    mat = jax.ShapeDtypeStruct((g, j, p), F32)
    return pl.pallas_call(
        _ssm_disc_kernel,
        out_shape=[vec, vec, mat, mat],
        name="ssm_disc",
    )(a_re.reshape(g, 1, p), a_im.reshape(g, 1, p), log_dt.reshape(g, 1, 1),
      b_re.transpose(0, 2, 1), b_im.transpose(0, 2, 1))


def _block_diag_tiles(x):
    a, b = x.shape[1], x.shape[2]
    eye = jnp.eye(SUBLANE, dtype=x.dtype)
    y = jnp.einsum('kgab,gh->kgahb', x.reshape(SSM_TILES, SUBLANE, a, b), eye)
    return y.reshape(SSM_TILES, SUBLANE * a, SUBLANE * b)


def _gelu_tanh(x):
    return 0.5 * x * (1.0 + jnp.tanh(math.sqrt(2.0 / math.pi) * (x + 0.044715 * (x * x * x))))


def _ssm_kernel(*refs, nseq, tc, seq_major):
    if seq_major:
        ua_ref, ub_ref, perm_ref, perm_t_ref = refs[:4]
        refs = refs[4:]
        u = jnp.concatenate([ua_ref[...].reshape(nseq * tc, D_SSM // 2),
                             ub_ref[...].reshape(nseq * tc, D_SSM // 2)], axis=1)
        ub = _dot(perm_ref[...], u.astype(BF16)).astype(BF16)
    else:
        u = refs[0][...]
        refs = refs[1:]
        ub = u.astype(BF16)
    (h0r_ref, h0i_ref, lbr_ref, lbi_ref, bb_ref, cc_ref, d_ref, wglu_ref, bglu_ref,
     y_ref, hTr_ref, hTi_ref, hre, him, st_r, st_i) = refs
    paired = nseq == 4

    @pl.when(pl.program_id(0) == 0)
    def _():
        if paired:
            st_r[...] = jnp.concatenate([h0r_ref[...], h0r_ref[...]], axis=0)
            st_i[...] = jnp.concatenate([h0i_ref[...], h0i_ref[...]], axis=0)
        else:
            st_r[...] = h0r_ref[...]
            st_i[...] = h0i_ref[...]

    half = SSM_W // SSM_TILES
    rows = nseq * tc
    for k in range(SSM_TILES):
        bu = _dot(ub[:, k * LANE:(k + 1) * LANE], bb_ref[k])
        br, bi = bu[:, :half], bu[:, half:]
        if paired:
            ar = lbr_ref[:, k * half:(k + 1) * half]
            ai = lbi_ref[:, k * half:(k + 1) * half]
            tiles = (rows // SUBLANE, SUBLANE, half)
            pr = pltpu.roll(br.reshape(tiles), nseq, 1).reshape(rows, half)
            pi = pltpu.roll(bi.reshape(tiles), nseq, 1).reshape(rows, half)
            first = (lax.broadcasted_iota(I32, (rows, half), 0) & nseq) == 0
            br, bi = (jnp.where(first, br, ar * pr - ai * pi + br),
                      jnp.where(first, bi, ar * pi + ai * pr + bi))
        hre[:, k * half:(k + 1) * half] = br
        him[:, k * half:(k + 1) * half] = bi

    for blk in range(SSM_W // SSM_LB):
        sl = slice(blk * SSM_LB, (blk + 1) * SSM_LB)
        ar = lbr_ref[:, sl]
        ai = lbi_ref[:, sl]
        if paired:
            lower = lax.broadcasted_iota(I32, (SUBLANE, SSM_LB), 0) < nseq
            cr = jnp.where(lower, ar, ar * ar - ai * ai)
            ci = jnp.where(lower, ai, 2.0 * ar * ai)

            def body(m, carry):
                xr, xi = carry
                r0 = pl.multiple_of(m * SUBLANE, SUBLANE)
                hr = cr * xr - ci * xi + hre[pl.ds(r0, SUBLANE), sl]
                hi = cr * xi + ci * xr + him[pl.ds(r0, SUBLANE), sl]
                hre[pl.ds(r0, SUBLANE), sl] = hr
                him[pl.ds(r0, SUBLANE), sl] = hi
                return (jnp.where(lower, pltpu.roll(hr, nseq, 0), hr),
                        jnp.where(lower, pltpu.roll(hi, nseq, 0), hi))

            sr, si = lax.fori_loop(0, tc * nseq // SUBLANE, body, (st_r[:, sl], st_i[:, sl]))
        else:
            def body(t, carry):
                sr, si = carry
                r0 = pl.multiple_of(t * nseq, SUBLANE)
                br = hre[pl.ds(r0, nseq), sl]
                bi = him[pl.ds(r0, nseq), sl]
                nr = ar * sr - ai * si + br
                ni = ar * si + ai * sr + bi
                hre[pl.ds(r0, nseq), sl] = nr
                him[pl.ds(r0, nseq), sl] = ni
                return nr, ni

            sr, si = lax.fori_loop(0, tc, body, (st_r[:, sl], st_i[:, sl]))
        st_r[:, sl] = sr
        st_i[:, sl] = si

    ys = []
    for k in range(SSM_TILES):
        hr = hre[:, k * half:(k + 1) * half].astype(BF16)
        hi = him[:, k * half:(k + 1) * half].astype(BF16)
        ys.append(_dot(hr, cc_ref[k, :half, :]) + _dot(hi, cc_ref[k, half:, :]))
    yc = jnp.concatenate(ys, axis=-1)
    if seq_major:
        yc_hi, yc_lo = _split_bf16(yc)
        yc = _dot(perm_t_ref[...], yc_hi) + _dot(perm_t_ref[...], yc_lo)
    y = _gelu_tanh(yc + d_ref[...] * u)
    z = _dot(y.astype(BF16), wglu_ref[...]) + bglu_ref[...]
    y_ref[...] = (y * _sigmoid(z)).astype(BF16).reshape(y_ref.shape)

    if paired:
        hTr_ref[...] = st_r[nseq:, :]
        hTi_ref[...] = st_i[nseq:, :]
    else:
        hTr_ref[...] = st_r[...]
        hTi_ref[...] = st_i[...]


def _ssm(u_src, h0_re, h0_im, nseq, tc, lbr, lbi, bb, cc, d, w_glu, b_glu, seq_major=False):
    r = nseq * tc
    st_rows = max(nseq, SUBLANE)
    const2 = lambda shape: pl.BlockSpec(shape, lambda c: (0, 0))
    const3 = lambda shape: pl.BlockSpec(shape, lambda c: (0, 0, 0))
    if seq_major:
        steps = u_src.shape[1] // tc
        half_w = D_SSM // 2
        col0 = (Q_W + 2 * KV_W) // half_w
        t_idx, s_idx = np.divmod(np.arange(r), nseq)
        perm = np.zeros((r, r), np.float32)
        perm[np.arange(r), s_idx * tc + t_idx] = 1.0
        u_specs = [pl.BlockSpec((nseq, tc, half_w), lambda c: (0, c, col0)),
                   pl.BlockSpec((nseq, tc, half_w), lambda c: (0, c, col0 + 1)),
                   const2((r, r)), const2((r, r))]
        u_args = [u_src, u_src, jnp.asarray(perm, BF16), jnp.asarray(perm.T, BF16)]
        y_spec = pl.BlockSpec((nseq, tc, D_SSM), lambda c: (0, c, 0))
        y_shape = jax.ShapeDtypeStruct((nseq, steps * tc, D_SSM), BF16)
    else:
        steps = u_src.shape[0] // r
        u_specs = [pl.BlockSpec((r, D_SSM), lambda c: (c, 0))]
        u_args = [u_src]
        y_spec = pl.BlockSpec((r, D_SSM), lambda c: (c, 0))
        y_shape = jax.ShapeDtypeStruct((steps * r, D_SSM), BF16)
    return pl.pallas_call(
        functools.partial(_ssm_kernel, nseq=nseq, tc=tc, seq_major=seq_major),
        grid=(steps,),
        in_specs=u_specs + [
            const2((nseq, SSM_W)), const2((nseq, SSM_W)),
            const2((1, SSM_W)), const2((1, SSM_W)),
            const3((SSM_TILES, LANE, 2 * SSM_W // SSM_TILES)),
            const3((SSM_TILES, 2 * SSM_W // SSM_TILES, LANE)),
            const2((1, D_SSM)), const2((D_SSM, D_SSM)), const2((1, D_SSM))],
        out_specs=[y_spec, const2((nseq, SSM_W)), const2((nseq, SSM_W))],
        out_shape=[y_shape,
                   jax.ShapeDtypeStruct((nseq, SSM_W), F32),
                   jax.ShapeDtypeStruct((nseq, SSM_W), F32)],
        scratch_shapes=[pltpu.VMEM((r, SSM_W), F32), pltpu.VMEM((r, SSM_W), F32),
                        pltpu.VMEM((st_rows, SSM_W), F32), pltpu.VMEM((st_rows, SSM_W), F32)],
        compiler_params=_cparams(("arbitrary",)),
        name="ssm",
    )(*u_args, h0_re, h0_im, lbr, lbi, bb, cc, d.reshape(1, D_SSM), w_glu, b_glu.reshape(1, D_SSM))


def _split_bf16(x):
    hi = x.astype(BF16)
    lo = (x - hi.astype(F32)).astype(BF16)
    return hi, lo


N_MERGE_IN = 15


def _merge_kernel(*refs):
    (o_ref, y_ref, ga_ref, gs_ref, wa_ref, ws_ref, wo_ref, x_ref, gpm_ref, g1_ref, sc2_ref, sh2_ref,
     gpf_ref, wrt_ref, br_ref) = refs[:N_MERGE_IN]
    x1_ref, h2_ref, lg_ref, mix = refs[-4:]
    j = pl.program_id(1)
    a = _dot(o_ref[...], wa_ref[...])
    s = _dot(y_ref[...], ws_ref[...])
    merged = _sigmoid(ga_ref[...]) * a + _sigmoid(gs_ref[...]) * s
    contrib = _dot(merged.astype(BF16), wo_ref[...])

    @pl.when(j == 0)
    def _():
        mix[...] = contrib

    @pl.when(j > 0)
    def _():
        mix[...] += contrib

    @pl.when(j == pl.num_programs(1) - 1)
    def _():
        x1 = x_ref[...] + g1_ref[...] * _rms(mix[...], gpm_ref[...])
        x1_ref[...] = x1
        h2 = _rms(x1, gpf_ref[...]) * (1.0 + sc2_ref[...]) + sh2_ref[...]
        h2_ref[...] = h2
        h_hi, h_lo = _split_bf16(h2)
        w_hi, w_lo = _split_bf16(wrt_ref[...])
        lg_ref[...] = (_dot_nt(w_hi, h_hi) + _dot_nt(w_hi, h_lo) + _dot_nt(w_lo, h_hi)) + br_ref[...]


def _merge(o_attn, y_ssm, proj, x, mod, per_row, rows_per_seq, n_total, row0, shared, w_br_attn, w_br_ssm,
           w_out, g_post_mix, g_pre_ffn, w_router_t, b_router):
    rows = x.shape[0]
    tm = min(MERGE_TM, rows)
    blk0 = row0 // tm
    nk = D_MODEL // MERGE_TK
    ga0 = (Q_W + 2 * KV_W + D_SSM) // MERGE_TK
    gs0 = ga0 + nk
    row2 = lambda w: pl.BlockSpec((tm, w), lambda i, j: (i, 0))
    vec = pl.BlockSpec((1, D_MODEL), lambda i, j: (0, 0))
    in_specs = [row2(Q_W), row2(D_SSM),
                pl.BlockSpec((tm, MERGE_TK), lambda i, j: (i, ga0 + j)),
                pl.BlockSpec((tm, MERGE_TK), lambda i, j: (i, gs0 + j)),
                pl.BlockSpec((Q_W, MERGE_TK), lambda i, j: (0, j)),
                pl.BlockSpec((D_SSM, MERGE_TK), lambda i, j: (0, j)),
                pl.BlockSpec((MERGE_TK, D_MODEL), lambda i, j: (j, 0)),
                row2(D_MODEL), vec,
                _mod_spec(per_row, tm, rows_per_seq, 2, 2),
                _mod_spec(per_row, tm, rows_per_seq, 4, 2),
                _mod_spec(per_row, tm, rows_per_seq, 3, 2),
                vec,
                pl.BlockSpec((N_EXPERTS, D_MODEL), lambda i, j: (0, 0)),
                pl.BlockSpec((N_EXPERTS, 1), lambda i, j: (0, 0))]
    args = [o_attn, y_ssm, proj, proj, w_br_attn, w_br_ssm, w_out, x, g_post_mix.reshape(1, D_MODEL),
            mod, mod, mod, g_pre_ffn.reshape(1, D_MODEL), w_router_t, b_router.reshape(N_EXPERTS, 1)]
    assert len(args) == N_MERGE_IN
    aliases = {}
    if shared is not None:
        aliases = {len(args): 1, len(args) + 1: 2}
        in_specs += [pl.BlockSpec(memory_space=pl.ANY), pl.BlockSpec(memory_space=pl.ANY)]
        args += list(shared)
    return pl.pallas_call(
        _merge_kernel,
        grid=(rows // tm, nk),
        in_specs=in_specs,
        out_specs=[row2(D_MODEL),
                   pl.BlockSpec((tm, D_MODEL), lambda i, j: (blk0 + i, 0)),
                   pl.BlockSpec((N_EXPERTS, tm), lambda i, j: (0, blk0 + i))],
        out_shape=[jax.ShapeDtypeStruct((rows, D_MODEL), F32),
                   jax.ShapeDtypeStruct((n_total, D_MODEL), F32),
                   jax.ShapeDtypeStruct((N_EXPERTS, n_total), F32)],
        scratch_shapes=[pltpu.VMEM((tm, D_MODEL), F32)],
        input_output_aliases=aliases,
        compiler_params=_cparams(("arbitrary", "arbitrary")),
        name="merge",
    )(*args)


def _bucket_tables():
    ql = np.arange(WINDOW)[:, None]
    kl = np.arange(2 * WINDOW)[None, :]
    dist = ql + WINDOW - kl
    prompt = np.where((dist >= 0) & (dist < WINDOW), _t5_bucket_np(dist), -1).astype(np.int32)
    d_s = (WINDOW - 1 - np.arange(WINDOW))[None, :]
    sample = np.broadcast_to(_t5_bucket_np(d_s), (SUBLANE, WINDOW)).astype(np.int32)
    return prompt, sample


def _front(p):
    l = 0
    batch, seq, _ = p['x_prompt'].shape
    nseq = p['x_sample'].shape[0]
    xp = p['x_prompt'].reshape(batch * seq, D_MODEL)
    xs = p['x_sample'].reshape(nseq, D_MODEL)

    c_all = jnp.concatenate([p['c_prompt'], p['c_sample'],
                             jnp.zeros((SUBLANE - (batch + nseq) % SUBLANE, D_MODEL), F32)], axis=0)
    mod = _ada(c_all, p['w_ada'][l], p['b_ada'][l])
    mod_p = mod[:batch].reshape(batch, 1, 6 * D_MODEL)
    mod_s = mod[batch:batch + nseq]

    bucket_p, bucket_s = _bucket_tables()
    bias_p = _bias_table(bucket_p, p['rel_bias'], WINDOW)
    bias_s = _bias_table(bucket_s, p['rel_bias'], 0)[1, :, 0, :]
    sinks = p['attn_sinks'][l]

    w_in_t = p['w_in'][l].astype(BF16)
    proj_p = _proj(xp, mod_p, False, seq, p['g_pre_mix'][l], w_in_t)
    proj_s = _proj(xs, mod_s, True, 1, p['g_pre_mix'][l], w_in_t)

    o_p = _attn_prompt(proj_p, batch, seq, bias_p, sinks)
    kv_p = proj_p.reshape(batch, seq, IN_W)[:, seq - WINDOW:, Q_W:Q_W + 2 * KV_W]
    new_k_p = kv_p[..., :KV_W].reshape(1, batch, WINDOW, N_KV_HEADS, HEAD_DIM)
    new_v_p = kv_p[..., KV_W:].reshape(1, batch, WINDOW, N_KV_HEADS, HEAD_DIM)
    o_s3, new_k_s, new_v_s = _attn_sample(
        proj_s[:, :Q_W].reshape(nseq, N_HEADS, HEAD_DIM),
        proj_s[:, Q_W:Q_W + KV_W].reshape(nseq, 1, KV_W),
        proj_s[:, Q_W + KV_W:Q_W + 2 * KV_W].reshape(nseq, 1, KV_W),
        p['cache_win_k'][l].reshape(nseq, WINDOW, KV_W),
        p['cache_win_v'][l].reshape(nseq, WINDOW, KV_W), bias_s, sinks)
    o_s = o_s3.reshape(nseq, Q_W)

    lbr, lbi, bbr, bbi = _ssm_disc(p['ssm_a_re'][l], p['ssm_a_im'][l], p['ssm_log_dt'][l],
                                   p['ssm_b_re'][l], p['ssm_b_im'][l])
    lbr = lbr.reshape(1, SSM_W)
    lbi = lbi.reshape(1, SSM_W)
    bb = jnp.concatenate([_block_diag_tiles(bbr), _block_diag_tiles(bbi)], axis=-1).astype(BF16)
    c_re_t = p['ssm_c_re'][l].transpose(0, 2, 1)
    c_im_t = p['ssm_c_im'][l].transpose(0, 2, 1)
    cc = jnp.concatenate([_block_diag_tiles(c_re_t), -_block_diag_tiles(c_im_t)], axis=1).astype(BF16)
    u0 = Q_W + 2 * KV_W
    zeros = jnp.zeros((batch, SSM_W), F32)
    w_glu = p['w_glu'][l].astype(BF16)
    y_p3, hr_p, hi_p = _ssm(proj_p.reshape(batch, seq, IN_W), zeros, zeros, batch, SSM_TC, lbr, lbi, bb, cc,
                            p['ssm_d'][l], w_glu, p['b_glu'][l], seq_major=True)
    y_p = y_p3.reshape(batch * seq, D_SSM)
    y_s, hr_s, hi_s = _ssm(proj_s[:, u0:u0 + D_SSM], p['state_ssm_re'][l].reshape(nseq, SSM_W),
                           p['state_ssm_im'][l].reshape(nseq, SSM_W), nseq, 1, lbr, lbi, bb, cc,
                           p['ssm_d'][l], w_glu, p['b_glu'][l])

    wa_t = p['w_br_attn'][l].astype(BF16)
    ws_t = p['w_br_ssm'][l].astype(BF16)
    wo = p['w_out'][l].astype(BF16)
    n_total = batch * seq + nseq
    merge = functools.partial(_merge, w_br_attn=wa_t, w_br_ssm=ws_t, w_out=wo, g_post_mix=p['g_post_mix'][l],
                              g_pre_ffn=p['g_pre_ffn'][l], w_router_t=p['w_router'][l].T,
                              b_router=p['b_router'][l])
    x1_p, h2_buf, lg_buf = merge(o_p, y_p, proj_p, xp, mod_p, False, seq, n_total, 0, None)
    x1_s, h2_all, lg_all = merge(o_s, y_s, proj_s, xs, mod_s, True, 1, n_total, batch * seq, (h2_buf, lg_buf))

    st = lambda h, n: h.reshape(1, n, N_SSM_GROUPS, SSM_STATE)
    return dict(
        mod_p=mod_p, mod_s=mod_s, h2_all=h2_all, lg_all=lg_all,
        p=dict(proj=proj_p, o_attn=o_p, new_k=new_k_p, new_v=new_v_p, y_ssm=y_p, h_re=st(hr_p, batch),
               h_im=st(hi_p, batch), x1=x1_p),
        s=dict(proj=proj_s, o_attn=o_s, new_k=new_k_s.reshape(1, nseq, WINDOW, N_KV_HEADS, HEAD_DIM),
               new_v=new_v_s.reshape(1, nseq, WINDOW, N_KV_HEADS, HEAD_DIM), y_ssm=y_s,
               h_re=st(hr_s, nseq), h_im=st(hi_s, nseq), x1=x1_s))


def _count_steps(c, step, n_max):
    out = jnp.zeros_like(c)
    for q in range(-(-n_max // step)):
        out = out + jnp.where(c > float(q * step), 1.0, 0.0)
    return out


def _route_kernel(lg_ref, pos_ref, gate_ref, texp_ref, trows_ref, ntiles_ref, pstart_ref, plen_ref):
    lg = lg_ref[...]
    e, tn = lg.shape
    erow = lax.broadcasted_iota(I32, (e, tn), 0).astype(F32)
    work = lg
    vals, hits = [], []
    for _ in range(TOP_K):
        m = jnp.max(work, axis=0, keepdims=True)
        idx = jnp.min(jnp.where(work == m, erow, float(e)), axis=0, keepdims=True)
        hit = erow == idx
        vals.append(m)
        hits.append(hit)
        work = jnp.where(hit, -jnp.inf, work)
    ex = [jnp.exp(v - vals[0]) for v in vals]
    den = ex[0] + ex[1] + ex[2] + ex[3]
    gate_ref[...] = jnp.concatenate([x / den for x in ex], axis=0)

    chosen = jnp.zeros((e, tn), F32)
    for hit in hits:
        chosen = chosen + jnp.where(hit, 1.0, 0.0)
    chosen_b = chosen.astype(BF16)
    tri = (lax.broadcasted_iota(I32, (LANE, LANE), 0) <= lax.broadcasted_iota(I32, (LANE, LANE), 1))
    tri = jnp.where(tri, 1.0, 0.0).astype(BF16)
    carry = jnp.zeros((e, 1), F32)
    ranks = []
    for b in range(tn // LANE):
        blk = chosen[:, b * LANE:(b + 1) * LANE]
        inc = _dot(chosen_b[:, b * LANE:(b + 1) * LANE], tri) + carry
        ranks.append(inc - blk)
        carry = inc[:, LANE - 1:LANE]
    rank = jnp.concatenate(ranks, axis=1)
    cnt_col = carry
    cnt_row = _dot_nt(jnp.ones((SUBLANE, tn), BF16), chosen_b)[0:1, :]

    tiles_col = _count_steps(cnt_col, EXP_TM, tn)
    tiles_row = _count_steps(cnt_row, EXP_TM, tn)
    ee_r = lax.broadcasted_iota(I32, (e, e), 0)
    ee_c = lax.broadcasted_iota(I32, (e, e), 1)
    tstart_col = jnp.sum(jnp.where(ee_c < ee_r, tiles_row, 0.0), axis=1, keepdims=True)
    ntiles = jnp.sum(tiles_row, axis=1, keepdims=True)
    rstart_col = tstart_col * float(EXP_TM)
    pos = [jnp.sum(jnp.where(hit, rstart_col + rank, 0.0), axis=0, keepdims=True) for hit in hits]
    pos_ref[...] = jnp.concatenate(pos, axis=0).astype(I32)

    mm = lax.broadcasted_iota(I32, (e, LANE), 1).astype(F32)
    e_col = lax.broadcasted_iota(I32, (e, LANE), 0).astype(F32)
    own = (mm >= tstart_col) & (mm < tstart_col + tiles_col)
    texp = jnp.sum(jnp.where(own, e_col, 0.0), axis=0, keepdims=True)
    rows_here = jnp.minimum(float(EXP_TM), cnt_col - (mm - tstart_col) * float(EXP_TM))
    trows = jnp.sum(jnp.where(own, rows_here, 0.0), axis=0, keepdims=True)
    last_e = jnp.max(jnp.where(tiles_col > 0.0, e_col, 0.0), axis=0, keepdims=True)
    texp = jnp.where(mm[0:1, :] < ntiles, texp, last_e)
    texp_ref[...] = texp.astype(I32)
    trows_ref[...] = trows.astype(I32)
    ntiles_ref[...] = jnp.broadcast_to(ntiles, (1, LANE)).astype(I32)
    nsub_col = _count_steps(cnt_col, EXP_SUB, tn)
    pstart_ref[...] = jnp.broadcast_to(rstart_col + cnt_col, (e, LANE)).astype(I32)
    plen_ref[...] = jnp.broadcast_to(nsub_col * float(EXP_SUB) - cnt_col, (e, LANE)).astype(I32)


def _route(lg_t):
    e, tn = lg_t.shape
    i32 = lambda shape: jax.ShapeDtypeStruct(shape, I32)
    return pl.pallas_call(
        _route_kernel,
        out_shape=[i32((TOP_K, tn)), jax.ShapeDtypeStruct((TOP_K, tn), F32),
                   i32((1, LANE)), i32((1, LANE)), i32((1, LANE)), i32((e, LANE)), i32((e, LANE))],
        compiler_params=pltpu.CompilerParams(vmem_limit_bytes=VMEM_LIMIT),
        name="route",
    )(lg_t)


def _max_tiles(n_tok):
    return (n_tok * TOP_K) // EXP_TM + N_EXPERTS


def _dispatch_kernel(pos_ref, pstart_ref, plen_ref, h2_ref, zero_ref, xs_ref, sem):
    i = pl.program_id(0)

    def row_copy(src, s, d):
        return pltpu.make_async_copy(src.at[pl.ds(s, 1)], xs_ref.at[pl.ds(d, 1)], sem)

    for t in range(TOK_BLK):
        for k in range(TOP_K):
            row_copy(h2_ref, t, pos_ref[k, t]).start(priority=k % 2)
    for _ in range(TOP_K):
        pltpu.make_async_copy(h2_ref, xs_ref.at[pl.ds(0, TOK_BLK)], sem).wait()

    @pl.when(i == 0)
    def _():
        def per_expert(ex, c):
            n = plen_ref[ex]
            s = pstart_ref[ex]

            def zissue(r, cc):
                row_copy(zero_ref, 0, s + r).start()
                return cc

            def zdrain(r, cc):
                row_copy(zero_ref, 0, 0).wait()
                return cc

            lax.fori_loop(0, n, zissue, 0)
            lax.fori_loop(0, n, zdrain, 0)
            return c

        lax.fori_loop(0, N_EXPERTS, per_expert, 0)


def _dispatch(pos3, pstart, plen, h2_all):
    n_tok = h2_all.shape[0]
    n_rows = _max_tiles(n_tok) * EXP_TM
    smem = pl.BlockSpec(memory_space=pltpu.SMEM)
    hbm = pl.BlockSpec(memory_space=pl.ANY)
    return pl.pallas_call(
        _dispatch_kernel,
        grid=(n_tok // TOK_BLK,),
        in_specs=[pl.BlockSpec((None, TOP_K, TOK_BLK), lambda i: (i, 0, 0), memory_space=pltpu.SMEM),
                  smem, smem,
                  pl.BlockSpec((TOK_BLK, D_MODEL), lambda i: (i, 0)),
                  pl.BlockSpec((SUBLANE, D_MODEL), lambda i: (0, 0))],
        out_specs=hbm,
        out_shape=jax.ShapeDtypeStruct((n_rows, D_MODEL), F32),
        scratch_shapes=[pltpu.SemaphoreType.DMA(())],
        compiler_params=_cparams(("arbitrary",)),
        name="dispatch",
    )(pos3, pstart, plen, h2_all, jnp.zeros((SUBLANE, D_MODEL), F32))


def _expert_kernel(texp_ref, trows_ref, nt_ref, xs_hbm, wg_ref, wl_ref, bg_ref, bl_ref, wd_ref, bd_ref,
                   o_ref, x_ref, x_sem, xb_scr, act_scr, wg_scr, wl_scr, wd_scr):
    m = pl.program_id(0)
    s = pl.program_id(1)
    n_tiles = nt_ref[0]
    valid = m < n_tiles

    rows = trows_ref[m]
    nbig = lax.shift_right_logical(rows, int(math.log2(EXP_BIG)))
    big_rows = nbig * EXP_BIG
    nsmall = lax.shift_right_logical(rows - big_rows + (EXP_SUB - 1), int(math.log2(EXP_SUB)))
    nsub_done = nbig * (EXP_BIG // EXP_SUB) + nsmall

    def x_copy(tile):
        return pltpu.make_async_copy(xs_hbm.at[pl.ds(pl.multiple_of(tile * EXP_TM, EXP_TM), EXP_TM)],
                                     x_ref, x_sem)

    @pl.when((m == 0) & (s == 0))
    def _():
        x_copy(0).start()

    @pl.when(valid & (s == 0))
    def _():
        x_copy(m).wait()

        def to_bf16(r, c):
            r0 = pl.multiple_of(r * EXP_SUB, EXP_SUB)
            xb_scr[pl.ds(r0, EXP_SUB), :] = x_ref[pl.ds(r0, EXP_SUB), :].astype(BF16)
            return c

        lax.fori_loop(0, nsub_done, to_bf16, 0)

        @pl.when(m + 1 < n_tiles)
        def _():
            x_copy(m + 1).start()

    def over_rows(first, step):
        def big(r, c):
            step(pl.multiple_of(r * EXP_BIG, EXP_BIG), EXP_BIG)
            return c

        def small(r, c):
            step(pl.multiple_of(big_rows + r * EXP_SUB, EXP_SUB), EXP_SUB)
            return c

        @pl.when(nbig > 0)
        def _():
            first(EXP_BIG)
            lax.fori_loop(1, nbig, big, 0)
            lax.fori_loop(0, nsmall, small, 0)

        @pl.when(nbig == 0)
        def _():
            first(EXP_SUB)
            lax.fori_loop(1, nsmall, small, 0)

    @pl.when(valid & (s < EXP_NF))
    def _():
        def finish(r0, n, hg, hl):
            x_glu = jnp.minimum(hg, SWIGLU_LIMIT)
            x_lin = jnp.clip(hl, -SWIGLU_LIMIT, SWIGLU_LIMIT)
            act = x_glu * _sigmoid(SWIGLU_ALPHA * x_glu) * (x_lin + 1.0)
            act_scr[s, pl.ds(r0, n), :] = act.astype(BF16)

        def first(n):
            xb = xb_scr[0:n, :]
            hg = jnp.broadcast_to(bg_ref[...], (n, EXP_TF))
            hl = jnp.broadcast_to(bl_ref[...], (n, EXP_TF))
            for q in range(D_MODEL // EXP_KQ):
                ks = slice(q * EXP_KQ, (q + 1) * EXP_KQ)
                wgq = wg_ref[ks, :].astype(BF16)
                wlq = wl_ref[ks, :].astype(BF16)
                wg_scr[ks, :] = wgq
                wl_scr[ks, :] = wlq
                hg = hg + _dot(xb[:, ks], wgq)
                hl = hl + _dot(xb[:, ks], wlq)
            finish(0, n, hg, hl)

        def step(r0, n):
            xb = xb_scr[pl.ds(r0, n), :]
            finish(r0, n, _dot(xb, wg_scr[...]) + bg_ref[...], _dot(xb, wl_scr[...]) + bl_ref[...])

        over_rows(first, step)

    @pl.when(valid & (s >= EXP_NF))
    def _():
        def first(n):
            acc = jnp.broadcast_to(bd_ref[...], (n, EXP_TD))
            for f in range(EXP_NF):
                fs = slice(f * EXP_TF, (f + 1) * EXP_TF)
                wdq = wd_ref[fs, :].astype(BF16)
                wd_scr[fs, :] = wdq
                acc = acc + _dot(act_scr[f, 0:n, :], wdq)
            o_ref[0:n, :] = acc

        def step(r0, n):
            acc = jnp.broadcast_to(bd_ref[...], (n, EXP_TD))
            for f in range(EXP_NF):
                acc = acc + _dot(act_scr[f, pl.ds(r0, n), :], wd_scr[f * EXP_TF:(f + 1) * EXP_TF, :])
            o_ref[pl.ds(r0, n), :] = acc

        def zero(r, c):
            r0 = pl.multiple_of(r * EXP_SUB, EXP_SUB)
            o_ref[pl.ds(r0, EXP_SUB), :] = jnp.zeros((EXP_SUB, EXP_TD), F32)
            return c

        over_rows(first, step)
        lax.fori_loop(nsub_done, EXP_TM // EXP_SUB, zero, 0)


def _experts(texp, trows, ntiles, xs, w_gate_up, b_gate_up, w_down, b_down):
    n_tiles = xs.shape[0] // EXP_TM
    nsteps = EXP_NF + EXP_ND

    def tile(m, nt):
        return jnp.minimum(m, nt[0] - 1)

    def ea(m, s, te, nt):
        return te[jnp.where(s < EXP_NF, m, jnp.minimum(m + 1, nt[0] - 1))]

    def fa(m, s, nt):
        return jnp.where((m < nt[0]) & (s < EXP_NF), s, 0)

    def fb(m, s, nt):
        return jnp.where(m < nt[0], jnp.maximum(s - EXP_NF, 0), EXP_ND - 1)

    grid_spec = pltpu.PrefetchScalarGridSpec(
        num_scalar_prefetch=3,
        grid=(ntiles[0], nsteps),
        in_specs=[
            pl.BlockSpec(memory_space=pl.ANY),
            pl.BlockSpec((None, D_MODEL, EXP_TF),
                         lambda m, s, te, tr, nt: (ea(m, s, te, nt), 0, fa(m, s, nt))),
            pl.BlockSpec((None, D_MODEL, EXP_TF),
                         lambda m, s, te, tr, nt: (ea(m, s, te, nt), 0, EXP_NF + fa(m, s, nt))),
            pl.BlockSpec((None, 1, EXP_TF), lambda m, s, te, tr, nt: (ea(m, s, te, nt), 0, fa(m, s, nt))),
            pl.BlockSpec((None, 1, EXP_TF),
                         lambda m, s, te, tr, nt: (ea(m, s, te, nt), 0, EXP_NF + fa(m, s, nt))),
            pl.BlockSpec((None, D_FF, EXP_TD), lambda m, s, te, tr, nt: (te[m], 0, fb(m, s, nt))),
            pl.BlockSpec((None, 1, EXP_TD), lambda m, s, te, tr, nt: (te[m], 0, fb(m, s, nt))),
        ],
        out_specs=pl.BlockSpec((EXP_TM, EXP_TD), lambda m, s, te, tr, nt: (tile(m, nt), fb(m, s, nt))),
        scratch_shapes=[pltpu.VMEM((EXP_TM, D_MODEL), F32), pltpu.SemaphoreType.DMA(()),
                        pltpu.VMEM((EXP_TM, D_MODEL), BF16),
                        pltpu.VMEM((EXP_NF, EXP_TM, EXP_TF), BF16),
                        pltpu.VMEM((D_MODEL, EXP_TF), BF16), pltpu.VMEM((D_MODEL, EXP_TF), BF16),
                        pltpu.VMEM((D_FF, EXP_TD), BF16)],
    )
    return pl.pallas_call(
        _expert_kernel,
        grid_spec=grid_spec,
        out_shape=jax.ShapeDtypeStruct((xs.shape[0], D_MODEL), F32),
        compiler_params=pltpu.CompilerParams(dimension_semantics=("arbitrary", "arbitrary"),
                                             vmem_limit_bytes=EXP_VMEM_LIMIT),
        name="experts",
    )(texp, trows, ntiles, xs, w_gate_up, w_gate_up,
      b_gate_up.reshape(N_EXPERTS, 1, 2 * D_FF), b_gate_up.reshape(N_EXPERTS, 1, 2 * D_FF),
      w_down, b_down.reshape(N_EXPERTS, 1, D_MODEL))


def _combine_kernel(pos_ref, pos_next_ref, gate_ref, ys_ref, x1_ref, g2_ref, gpf_ref, o_ref, buf, sem):
    i = pl.program_id(0)
    slot = lax.rem(i, 2)

    def gather(p_ref, sl):
        for t in range(TOK_BLK):
            for k in range(TOP_K):
                pltpu.make_async_copy(ys_ref.at[pl.ds(p_ref[k, t], 1)], buf.at[sl, k, pl.ds(t, 1)],
                                      sem.at[sl]).start(priority=k % 2)

    @pl.when(i == 0)
    def _():
        gather(pos_ref, 0)

    @pl.when(i + 1 < pl.num_programs(0))
    def _():
        gather(pos_next_ref, 1 - slot)

    for k in range(TOP_K):
        pltpu.make_async_copy(ys_ref.at[pl.ds(0, TOK_BLK)], buf.at[slot, k], sem.at[slot]).wait()
    g = gate_ref[...]
    f = g[:, 0:1] * buf[slot, 0]
    for k in range(1, TOP_K):
        f = f + g[:, k:k + 1] * buf[slot, k]
    o_ref[...] = x1_ref[...] + g2_ref[...] * _rms(f, gpf_ref[...])


def _combine(pos3, gates_t, ys, x1, mod, per_row, rows_per_seq, g_post_ffn, blk0):
    rows = x1.shape[0]
    nblk = rows // TOK_BLK
    return pl.pallas_call(
        _combine_kernel,
        grid=(nblk,),
        in_specs=[pl.BlockSpec((None, TOP_K, TOK_BLK), lambda i: (blk0 + i, 0, 0), memory_space=pltpu.SMEM),
                  pl.BlockSpec((None, TOP_K, TOK_BLK), lambda i: (blk0 + jnp.minimum(i + 1, nblk - 1), 0, 0),
                               memory_space=pltpu.SMEM),
                  pl.BlockSpec((TOK_BLK, TOP_K), lambda i: (blk0 + i, 0)),
                  pl.BlockSpec(memory_space=pl.ANY),
                  pl.BlockSpec((TOK_BLK, D_MODEL), lambda i: (i, 0)),
                  _mod_spec(per_row, TOK_BLK, rows_per_seq, 5, 1),
                  pl.BlockSpec((1, D_MODEL), lambda i: (0, 0))],
        out_specs=pl.BlockSpec((TOK_BLK, D_MODEL), lambda i: (i, 0)),
        out_shape=jax.ShapeDtypeStruct((rows, D_MODEL), F32),
        scratch_shapes=[pltpu.VMEM((2, TOP_K, TOK_BLK, D_MODEL), F32), pltpu.SemaphoreType.DMA((2,))],
        compiler_params=_cparams(("arbitrary",)),
        name="combine",
    )(pos3, pos3, gates_t, ys, x1, mod, g_post_ffn.reshape(1, D_MODEL))


def _moe(fr, p, batch, seq, nseq):
    l = 0
    h2_all, lg_all = fr['h2_all'], fr['lg_all']
    n_tok = h2_all.shape[0]
    pos, gates, texp, trows, ntiles, pstart, plen = _route(lg_all)
    pos3 = pos.reshape(TOP_K, n_tok // TOK_BLK, TOK_BLK).transpose(1, 0, 2)
    xs = _dispatch(pos3, pstart[:, 0], plen[:, 0], h2_all)
    ys = _experts(texp[0], trows[0], ntiles[0, :1], xs, p['w_gate_up'][l], p['b_gate_up'][l],
                  p['w_down'][l], p['b_down'][l])
    gates_t = gates.T
    y_p = _combine(pos3, gates_t, ys, fr['p']['x1'], fr['mod_p'], False, seq, p['g_post_ffn'][l], 0)
    y_s = _combine(pos3, gates_t, ys, fr['s']['x1'], fr['mod_s'], True, 1, p['g_post_ffn'][l],
                   batch * seq // TOK_BLK)
    return y_p, y_s


def kernel(x_prompt, x_sample, c_prompt, c_sample, cache_win_k, cache_win_v, state_ssm_re, state_ssm_im, w_ada, b_ada, g_pre_mix, g_post_mix, g_pre_ffn, g_post_ffn, w_in, attn_sinks, rel_bias, ssm_a_re, ssm_a_im, ssm_log_dt, ssm_b_re, ssm_b_im, ssm_c_re, ssm_c_im, ssm_d, w_glu, b_glu, w_br_attn, w_br_ssm, w_out, w_router, b_router, w_gate_up, b_gate_up, w_down, b_down):
    p = dict(locals())
    batch, seq, _ = x_prompt.shape
    nseq = x_sample.shape[0]
    fr = _front(p)
    y_p, y_s = _moe(fr, p, batch, seq, nseq)
    fp, fs = fr['p'], fr['s']
    return (y_p.reshape(batch, seq, D_MODEL), y_s.reshape(nseq, 1, D_MODEL),
            fp['new_k'], fp['new_v'], fp['h_re'], fp['h_im'],
            fs['new_k'], fs['new_v'], fs['h_re'], fs['h_im'])
```

```python
import functools
import math

import numpy as np
import jax
import jax.numpy as jnp
from jax import lax
from jax.experimental import pallas as pl
from jax.experimental.pallas import tpu as pltpu

F32 = jnp.float32
BF16 = jnp.bfloat16
I32 = jnp.int32

D_MODEL = 2048
N_HEADS = 16
N_KV_HEADS = 4
HEAD_DIM = 64
Q_GROUP = N_HEADS // N_KV_HEADS
WINDOW = 128
N_BUCKETS = 32
MAX_EXACT = N_BUCKETS // 2
MAX_DISTANCE = 128
D_SSM = 1024
SSM_GROUP = 16
N_SSM_GROUPS = 64
SSM_STATE = 64
N_EXPERTS = 32
TOP_K = 4
D_FF = 2048
SWIGLU_LIMIT = 7.0
SWIGLU_ALPHA = 1.702
NORM_EPS = 1e-6
NEG_INF = -1e30
Q_W = N_HEADS * HEAD_DIM
KV_W = N_KV_HEADS * HEAD_DIM
IN_W = Q_W + 2 * KV_W + D_SSM + 2 * D_MODEL
SSM_W = N_SSM_GROUPS * SSM_STATE

LANE = 128
SUBLANE = 8
VMEM_LIMIT = 56 * 1024 * 1024
EXP_VMEM_LIMIT = 60 * 1024 * 1024

ADA_TN = 1024
ATTN_TB = 16
PROJ_TM = 1024
PROJ_TN = 512
MERGE_TM = 512
MERGE_TK = 512
SSM_TC = 128
SSM_LB = 512
SSM_TILES = D_SSM // LANE
TOK_BLK = 128
EXP_TM = 1152
EXP_BIG = 512
EXP_SUB = 128
EXP_TF = 512
EXP_TD = 512
EXP_KQ = 512
EXP_NF = D_FF // EXP_TF
EXP_ND = D_MODEL // EXP_TD


def _cparams(sem):
    return pltpu.CompilerParams(dimension_semantics=sem, vmem_limit_bytes=VMEM_LIMIT)


def _sigmoid(x):
    return 1.0 / (1.0 + jnp.exp(-x))


def _rms(x, g):
    return x * lax.rsqrt(jnp.mean(x * x, axis=-1, keepdims=True) + NORM_EPS) * g


def _dot(a, b):
    return jnp.dot(a, b, preferred_element_type=F32)


def _dot_nt(a, b):
    return lax.dot_general(a, b, (((1,), (1,)), ((), ())), preferred_element_type=F32)


def _ada_kernel(c_ref, w_ref, b_ref, o_ref):
    c = c_ref[...]
    s = (c * _sigmoid(c)).astype(BF16)
    o_ref[...] = _dot(s, w_ref[...].astype(BF16)) + b_ref[...]


def _ada(c_all, w_ada, b_ada):
    rows = c_all.shape[0]
    tn = ADA_TN
    n = w_ada.shape[1]
    return pl.pallas_call(
        _ada_kernel,
        grid=(n // tn,),
        in_specs=[pl.BlockSpec((rows, D_MODEL), lambda j: (0, 0)),
                  pl.BlockSpec((D_MODEL, tn), lambda j: (0, j)),
                  pl.BlockSpec((1, tn), lambda j: (0, j))],
        out_specs=pl.BlockSpec((rows, tn), lambda j: (0, j)),
        out_shape=jax.ShapeDtypeStruct((rows, n), F32),
        compiler_params=_cparams(("arbitrary",)),
        name="ada",
    )(c_all, w_ada, b_ada.reshape(1, n))


def _t5_bucket_np(dist):
    n = np.maximum(dist, 0)
    nf = np.maximum(n, 1).astype(np.float64)
    large = MAX_EXACT + (np.log(nf / MAX_EXACT) / math.log(MAX_DISTANCE / MAX_EXACT)
                         * (N_BUCKETS - MAX_EXACT)).astype(np.int32)
    large = np.minimum(large, N_BUCKETS - 1)
    return np.where(n < MAX_EXACT, n, large).astype(np.int32)


def _bias_kernel(bucket_ref, rb_ref, o_ref, *, mask_cols):
    h = pl.program_id(0)
    bucket = bucket_ref[...]
    acc = jnp.full(bucket.shape, NEG_INF, F32)
    for b in range(N_BUCKETS):
        acc = jnp.where(bucket == b, rb_ref[b, h], acc)
    o_ref[1] = acc
    col = lax.broadcasted_iota(I32, bucket.shape, 1)
    o_ref[0] = jnp.where(col >= mask_cols, acc, NEG_INF)


def _bias_table(bucket_np, rel_bias, mask_cols):
    r, c = bucket_np.shape
    return pl.pallas_call(
        functools.partial(_bias_kernel, mask_cols=mask_cols),
        grid=(N_HEADS,),
        in_specs=[pl.BlockSpec((r, c), lambda h: (0, 0)),
                  pl.BlockSpec(memory_space=pltpu.SMEM)],
        out_specs=pl.BlockSpec((2, None, r, c), lambda h: (0, h, 0, 0)),
        out_shape=jax.ShapeDtypeStruct((2, N_HEADS, r, c), F32),
        compiler_params=_cparams(("arbitrary",)),
        name="bias",
    )(jnp.asarray(bucket_np), rel_bias)


def _proj_kernel(x_ref, sc_ref, sh_ref, g_ref, w_ref, o_ref, h_scr):
    j = pl.program_id(1)
    tm = x_ref.shape[0]
    part = tm // 2 if tm % (2 * SUBLANE) == 0 else tm

    @pl.when(j == 0)
    def _():
        for lo in range(0, tm, part):
            rows = slice(lo, lo + part)
            sc = sc_ref[...] if sc_ref.shape[0] == 1 else sc_ref[rows, :]
            sh = sh_ref[...] if sh_ref.shape[0] == 1 else sh_ref[rows, :]
            h = (_rms(x_ref[rows, :], g_ref[...]) * (1.0 + sc) + sh).astype(BF16)
            h_scr[rows, :] = h
            o_ref[rows, :] = _dot(h, w_ref[...])

    @pl.when(j > 0)
    def _():
        o_ref[...] = _dot(h_scr[...], w_ref[...])


def _mod_spec(per_row, tm, rows_per_seq, col, nargs):
    if per_row:
        if nargs == 1:
            return pl.BlockSpec((tm, D_MODEL), lambda i: (i, col))
        return pl.BlockSpec((tm, D_MODEL), lambda i, j: (i, col))
    tiles_per_seq = rows_per_seq // tm
    if nargs == 1:
        return pl.BlockSpec((None, 1, D_MODEL), lambda i: (i // tiles_per_seq, 0, col))
    return pl.BlockSpec((None, 1, D_MODEL), lambda i, j: (i // tiles_per_seq, 0, col))


def _proj(x, mod, per_row, rows_per_seq, g_pre, w_in):
    rows = x.shape[0]
    tm = min(PROJ_TM, rows)
    return pl.pallas_call(
        _proj_kernel,
        grid=(rows // tm, IN_W // PROJ_TN),
        in_specs=[pl.BlockSpec((tm, D_MODEL), lambda i, j: (i, 0)),
                  _mod_spec(per_row, tm, rows_per_seq, 1, 2),
                  _mod_spec(per_row, tm, rows_per_seq, 0, 2),
                  pl.BlockSpec((1, D_MODEL), lambda i, j: (0, 0)),
                  pl.BlockSpec((D_MODEL, PROJ_TN), lambda i, j: (0, j))],
        out_specs=pl.BlockSpec((tm, PROJ_TN), lambda i, j: (i, j)),
        out_shape=jax.ShapeDtypeStruct((rows, IN_W), F32),
        scratch_shapes=[pltpu.VMEM((tm, D_MODEL), BF16)],
        compiler_params=_cparams(("arbitrary", "arbitrary")),
        name="proj",
    )(x, mod, mod, g_pre.reshape(1, D_MODEL), w_in)


def _attn_prompt_kernel(q_ref, kc_ref, kp_ref, vc_ref, vp_ref, bias_ref, sink_ref, o_ref):
    q = q_ref[...] * (HEAD_DIM ** -0.5)
    k = jnp.concatenate([kp_ref[...], kc_ref[...]], axis=0)
    v = jnp.concatenate([vp_ref[...], vc_ref[...]], axis=0)
    outs = []
    for g in range(N_KV_HEADS):
        kg = k[:, g * HEAD_DIM:(g + 1) * HEAD_DIM].astype(BF16)
        vg = v[:, g * HEAD_DIM:(g + 1) * HEAD_DIM].astype(BF16)
        for hh in range(Q_GROUP):
            h = g * Q_GROUP + hh
            qh = q[:, h * HEAD_DIM:(h + 1) * HEAD_DIM].astype(BF16)
            s = _dot_nt(qh, kg) + bias_ref[h]
            sink = sink_ref[h]
            m = jnp.maximum(jnp.max(s, axis=-1, keepdims=True), sink)
            p = jnp.exp(s - m)
            den = jnp.sum(p, axis=-1, keepdims=True) + jnp.exp(sink - m)
            outs.append(_dot(p.astype(BF16), vg) / den)
    o_ref[...] = jnp.concatenate(outs, axis=-1).astype(BF16)


def _attn_prompt(proj, batch, seq, bias, sinks):
    nb = seq // WINDOW
    kcol = Q_W // KV_W
    vcol = kcol + 1
    cur = lambda c: (lambda b, n: (b * nb + n, c))
    prev = lambda c: (lambda b, n: (b * nb + jnp.maximum(n - 1, 0), c))
    return pl.pallas_call(
        _attn_prompt_kernel,
        grid=(batch, nb),
        in_specs=[pl.BlockSpec((WINDOW, Q_W), cur(0)),
                  pl.BlockSpec((WINDOW, KV_W), cur(kcol)),
                  pl.BlockSpec((WINDOW, KV_W), prev(kcol)),
                  pl.BlockSpec((WINDOW, KV_W), cur(vcol)),
                  pl.BlockSpec((WINDOW, KV_W), prev(vcol)),
                  pl.BlockSpec((None, N_HEADS, WINDOW, 2 * WINDOW), lambda b, n: (jnp.minimum(n, 1), 0, 0, 0)),
                  pl.BlockSpec(memory_space=pltpu.SMEM)],
        out_specs=pl.BlockSpec((WINDOW, Q_W), lambda b, n: (b * nb + n, 0)),
        out_shape=jax.ShapeDtypeStruct((batch * seq, Q_W), BF16),
        compiler_params=_cparams(("arbitrary", "arbitrary")),
        name="attn_prompt",
    )(proj, proj, proj, proj, proj, bias, sinks)


def _attn_sample_kernel(q_ref, kn_ref, vn_ref, ck_ref, cv_ref, bias_ref, sink_ref,
                        o_ref, nk_ref, nv_ref):
    tb = q_ref.shape[0]
    row = lax.broadcasted_iota(I32, (tb, WINDOW, KV_W), 1)
    last = row == WINDOW - 1
    nk = jnp.where(last, kn_ref[...], pltpu.roll(ck_ref[...], WINDOW - 1, 1))
    nv = jnp.where(last, vn_ref[...], pltpu.roll(cv_ref[...], WINDOW - 1, 1))
    nk_ref[...] = nk
    nv_ref[...] = nv
    lane_grp = lax.broadcasted_iota(I32, (N_HEADS, KV_W), 1) // HEAD_DIM
    head_grp = lax.broadcasted_iota(I32, (N_HEADS, KV_W), 0) // Q_GROUP
    gmask = (lane_grp == head_grp).astype(F32)
    q = q_ref[...]
    qrow = jnp.concatenate([q] * N_KV_HEADS, axis=-1) * gmask
    s = jnp.einsum('bhc,brc->bhr', qrow.astype(BF16), nk.astype(BF16),
                   preferred_element_type=F32) * (HEAD_DIM ** -0.5)
    s = s + bias_ref[...]
    sink = sink_ref[...]
    m = jnp.maximum(jnp.max(s, axis=-1, keepdims=True), sink)
    p = jnp.exp(s - m)
    den = jnp.sum(p, axis=-1, keepdims=True) + jnp.exp(sink - m)
    o = jnp.einsum('bhr,brc->bhc', p.astype(BF16), nv.astype(BF16),
                   preferred_element_type=F32) * gmask
    o64 = o[..., 0:HEAD_DIM]
    for g in range(1, N_KV_HEADS):
        o64 = o64 + o[..., g * HEAD_DIM:(g + 1) * HEAD_DIM]
    o_ref[...] = (o64 / den).astype(BF16)


def _attn_sample(q3, kn, vn, cache_k, cache_v, bias, sinks):
    nseq = q3.shape[0]
    tb = ATTN_TB
    seq3 = lambda w: pl.BlockSpec((tb, WINDOW, w), lambda i: (i, 0, 0))
    return pl.pallas_call(
        _attn_sample_kernel,
        grid=(nseq // tb,),
        in_specs=[pl.BlockSpec((tb, N_HEADS, HEAD_DIM), lambda i: (i, 0, 0)),
                  pl.BlockSpec((tb, 1, KV_W), lambda i: (i, 0, 0)),
                  pl.BlockSpec((tb, 1, KV_W), lambda i: (i, 0, 0)),
                  seq3(KV_W), seq3(KV_W),
                  pl.BlockSpec((N_HEADS, WINDOW), lambda i: (0, 0)),
                  pl.BlockSpec((N_HEADS, 1), lambda i: (0, 0))],
        out_specs=[pl.BlockSpec((tb, N_HEADS, HEAD_DIM), lambda i: (i, 0, 0)),
                   seq3(KV_W), seq3(KV_W)],
        out_shape=[jax.ShapeDtypeStruct((nseq, N_HEADS, HEAD_DIM), BF16),
                   jax.ShapeDtypeStruct((nseq, WINDOW, KV_W), F32),
                   jax.ShapeDtypeStruct((nseq, WINDOW, KV_W), F32)],
        compiler_params=_cparams(("arbitrary",)),
        name="attn_sample",
    )(q3, kn, vn, cache_k, cache_v, bias, sinks.reshape(N_HEADS, 1))


def _ssm_disc_kernel(are_ref, aim_ref, ldt_ref, bre_ref, bim_ref,
                     lbr_ref, lbi_ref, bbr_ref, bbi_ref):
    a_re = are_ref[...]
    a_im = aim_ref[...]
    dt = jnp.exp(ldt_ref[...])
    lam_re = a_re * dt
    lam_im = a_im * dt
    mag = jnp.exp(lam_re)
    lb_re = mag * jnp.cos(lam_im)
    lb_im = mag * jnp.sin(lam_im)
    den = a_re * a_re + a_im * a_im
    nr = lb_re - 1.0
    ni = lb_im
    coef_re = (nr * a_re + ni * a_im) / den
    coef_im = (ni * a_re - nr * a_im) / den
    b_re = bre_ref[...]
    b_im = bim_ref[...]
    lbr_ref[...] = lb_re
    lbi_ref[...] = lb_im
    bbr_ref[...] = coef_re * b_re - coef_im * b_im
    bbi_ref[...] = coef_re * b_im + coef_im * b_re


def _ssm_disc(a_re, a_im, log_dt, b_re, b_im):
    g, p, j = N_SSM_GROUPS, SSM_STATE, SSM_GROUP
    vec = jax.ShapeDtypeStruct((g, 1, p), F32)
    mat = jax.ShapeDtypeStruct((g, j, p), F32)
    return pl.pallas_call(
        _ssm_disc_kernel,
        out_shape=[vec, vec, mat, mat],
        name="ssm_disc",
    )(a_re.reshape(g, 1, p), a_im.reshape(g, 1, p), log_dt.reshape(g, 1, 1),
      b_re.transpose(0, 2, 1), b_im.transpose(0, 2, 1))


def _block_diag_tiles(x):
    a, b = x.shape[1], x.shape[2]
    eye = jnp.eye(SUBLANE, dtype=x.dtype)
    y = jnp.einsum('kgab,gh->kgahb', x.reshape(SSM_TILES, SUBLANE, a, b), eye)
    return y.reshape(SSM_TILES, SUBLANE * a, SUBLANE * b)


def _gelu_tanh(x):
    return 0.5 * x * (1.0 + jnp.tanh(math.sqrt(2.0 / math.pi) * (x + 0.044715 * (x * x * x))))


def _ssm_kernel(*refs, nseq, tc, seq_major):
    if seq_major:
        ua_ref, ub_ref, perm_ref, perm_t_ref = refs[:4]
        refs = refs[4:]
        u = jnp.concatenate([ua_ref[...].reshape(nseq * tc, D_SSM // 2),
                             ub_ref[...].reshape(nseq * tc, D_SSM // 2)], axis=1)
        ub = _dot(perm_ref[...], u.astype(BF16)).astype(BF16)
    else:
        u = refs[0][...]
        refs = refs[1:]
        ub = u.astype(BF16)
    (h0r_ref, h0i_ref, lbr_ref, lbi_ref, bb_ref, cc_ref, d_ref, wglu_ref, bglu_ref,
     y_ref, hTr_ref, hTi_ref, hre, him, st_r, st_i) = refs
    paired = nseq == 4

    @pl.when(pl.program_id(0) == 0)
    def _():
        if paired:
            st_r[...] = jnp.concatenate([h0r_ref[...], h0r_ref[...]], axis=0)
            st_i[...] = jnp.concatenate([h0i_ref[...], h0i_ref[...]], axis=0)
        else:
            st_r[...] = h0r_ref[...]
            st_i[...] = h0i_ref[...]

    half = SSM_W // SSM_TILES
    rows = nseq * tc
    for k in range(SSM_TILES):
        bu = _dot(ub[:, k * LANE:(k + 1) * LANE], bb_ref[k])
        br, bi = bu[:, :half], bu[:, half:]
        if paired:
            ar = lbr_ref[:, k * half:(k + 1) * half]
            ai = lbi_ref[:, k * half:(k + 1) * half]
            tiles = (rows // SUBLANE, SUBLANE, half)
            pr = pltpu.roll(br.reshape(tiles), nseq, 1).reshape(rows, half)
            pi = pltpu.roll(bi.reshape(tiles), nseq, 1).reshape(rows, half)
            first = (lax.broadcasted_iota(I32, (rows, half), 0) & nseq) == 0
            br, bi = (jnp.where(first, br, ar * pr - ai * pi + br),
                      jnp.where(first, bi, ar * pi + ai * pr + bi))
        hre[:, k * half:(k + 1) * half] = br
        him[:, k * half:(k + 1) * half] = bi

    for blk in range(SSM_W // SSM_LB):
        sl = slice(blk * SSM_LB, (blk + 1) * SSM_LB)
        ar = lbr_ref[:, sl]
        ai = lbi_ref[:, sl]
        if paired:
            lower = lax.broadcasted_iota(I32, (SUBLANE, SSM_LB), 0) < nseq
            cr = jnp.where(lower, ar, ar * ar - ai * ai)
            ci = jnp.where(lower, ai, 2.0 * ar * ai)

            def body(m, carry):
                xr, xi = carry
                r0 = pl.multiple_of(m * SUBLANE, SUBLANE)
                hr = cr * xr - ci * xi + hre[pl.ds(r0, SUBLANE), sl]
                hi = cr * xi + ci * xr + him[pl.ds(r0, SUBLANE), sl]
                hre[pl.ds(r0, SUBLANE), sl] = hr
                him[pl.ds(r0, SUBLANE), sl] = hi
                return (jnp.where(lower, pltpu.roll(hr, nseq, 0), hr),
                        jnp.where(lower, pltpu.roll(hi, nseq, 0), hi))

            sr, si = lax.fori_loop(0, tc * nseq // SUBLANE, body, (st_r[:, sl], st_i[:, sl]))
        else:
            def body(t, carry):
                sr, si = carry
                r0 = pl.multiple_of(t * nseq, SUBLANE)
                br = hre[pl.ds(r0, nseq), sl]
                bi = him[pl.ds(r0, nseq), sl]
                nr = ar * sr - ai * si + br
                ni = ar * si + ai * sr + bi
                hre[pl.ds(r0, nseq), sl] = nr
                him[pl.ds(r0, nseq), sl] = ni
                return nr, ni

            sr, si = lax.fori_loop(0, tc, body, (st_r[:, sl], st_i[:, sl]))
        st_r[:, sl] = sr
        st_i[:, sl] = si

    ys = []
    for k in range(SSM_TILES):
        hr = hre[:, k * half:(k + 1) * half].astype(BF16)
        hi = him[:, k * half:(k + 1) * half].astype(BF16)
        ys.append(_dot(hr, cc_ref[k, :half, :]) + _dot(hi, cc_ref[k, half:, :]))
    yc = jnp.concatenate(ys, axis=-1)
    if seq_major:
        yc_hi, yc_lo = _split_bf16(yc)
        yc = _dot(perm_t_ref[...], yc_hi) + _dot(perm_t_ref[...], yc_lo)
    y = _gelu_tanh(yc + d_ref[...] * u)
    z = _dot(y.astype(BF16), wglu_ref[...]) + bglu_ref[...]
    y_ref[...] = (y * _sigmoid(z)).astype(BF16).reshape(y_ref.shape)

    if paired:
        hTr_ref[...] = st_r[nseq:, :]
        hTi_ref[...] = st_i[nseq:, :]
    else:
        hTr_ref[...] = st_r[...]
        hTi_ref[...] = st_i[...]


def _ssm(u_src, h0_re, h0_im, nseq, tc, lbr, lbi, bb, cc, d, w_glu, b_glu, seq_major=False):
    r = nseq * tc
    st_rows = max(nseq, SUBLANE)
    const2 = lambda shape: pl.BlockSpec(shape, lambda c: (0, 0))
    const3 = lambda shape: pl.BlockSpec(shape, lambda c: (0, 0, 0))
    if seq_major:
        steps = u_src.shape[1] // tc
        half_w = D_SSM // 2
        col0 = (Q_W + 2 * KV_W) // half_w
        t_idx, s_idx = np.divmod(np.arange(r), nseq)
        perm = np.zeros((r, r), np.float32)
        perm[np.arange(r), s_idx * tc + t_idx] = 1.0
        u_specs = [pl.BlockSpec((nseq, tc, half_w), lambda c: (0, c, col0)),
                   pl.BlockSpec((nseq, tc, half_w), lambda c: (0, c, col0 + 1)),
                   const2((r, r)), const2((r, r))]
        u_args = [u_src, u_src, jnp.asarray(perm, BF16), jnp.asarray(perm.T, BF16)]
        y_spec = pl.BlockSpec((nseq, tc, D_SSM), lambda c: (0, c, 0))
        y_shape = jax.ShapeDtypeStruct((nseq, steps * tc, D_SSM), BF16)
    else:
        steps = u_src.shape[0] // r
        u_specs = [pl.BlockSpec((r, D_SSM), lambda c: (c, 0))]
        u_args = [u_src]
        y_spec = pl.BlockSpec((r, D_SSM), lambda c: (c, 0))
        y_shape = jax.ShapeDtypeStruct((steps * r, D_SSM), BF16)
    return pl.pallas_call(
        functools.partial(_ssm_kernel, nseq=nseq, tc=tc, seq_major=seq_major),
        grid=(steps,),
        in_specs=u_specs + [
            const2((nseq, SSM_W)), const2((nseq, SSM_W)),
            const2((1, SSM_W)), const2((1, SSM_W)),
            const3((SSM_TILES, LANE, 2 * SSM_W // SSM_TILES)),
            const3((SSM_TILES, 2 * SSM_W // SSM_TILES, LANE)),
            const2((1, D_SSM)), const2((D_SSM, D_SSM)), const2((1, D_SSM))],
        out_specs=[y_spec, const2((nseq, SSM_W)), const2((nseq, SSM_W))],
        out_shape=[y_shape,
                   jax.ShapeDtypeStruct((nseq, SSM_W), F32),
                   jax.ShapeDtypeStruct((nseq, SSM_W), F32)],
        scratch_shapes=[pltpu.VMEM((r, SSM_W), F32), pltpu.VMEM((r, SSM_W), F32),
                        pltpu.VMEM((st_rows, SSM_W), F32), pltpu.VMEM((st_rows, SSM_W), F32)],
        compiler_params=_cparams(("arbitrary",)),
        name="ssm",
    )(*u_args, h0_re, h0_im, lbr, lbi, bb, cc, d.reshape(1, D_SSM), w_glu, b_glu.reshape(1, D_SSM))


def _split_bf16(x):
    hi = x.astype(BF16)
    lo = (x - hi.astype(F32)).astype(BF16)
    return hi, lo


N_MERGE_IN = 15


def _merge_kernel(*refs):
    (o_ref, y_ref, ga_ref, gs_ref, wa_ref, ws_ref, wo_ref, x_ref, gpm_ref, g1_ref, sc2_ref, sh2_ref,
     gpf_ref, wrt_ref, br_ref) = refs[:N_MERGE_IN]
    x1_ref, h2_ref, lg_ref, mix = refs[-4:]
    j = pl.program_id(1)
    a = _dot(o_ref[...], wa_ref[...])
    s = _dot(y_ref[...], ws_ref[...])
    merged = _sigmoid(ga_ref[...]) * a + _sigmoid(gs_ref[...]) * s
    contrib = _dot(merged.astype(BF16), wo_ref[...])

    @pl.when(j == 0)
    def _():
        mix[...] = contrib

    @pl.when(j > 0)
    def _():
        mix[...] += contrib

    @pl.when(j == pl.num_programs(1) - 1)
    def _():
        x1 = x_ref[...] + g1_ref[...] * _rms(mix[...], gpm_ref[...])
        x1_ref[...] = x1
        h2 = _rms(x1, gpf_ref[...]) * (1.0 + sc2_ref[...]) + sh2_ref[...]
        h2_ref[...] = h2
        h_hi, h_lo = _split_bf16(h2)
        w_hi, w_lo = _split_bf16(wrt_ref[...])
        lg_ref[...] = (_dot_nt(w_hi, h_hi) + _dot_nt(w_hi, h_lo) + _dot_nt(w_lo, h_hi)) + br_ref[...]


def _merge(o_attn, y_ssm, proj, x, mod, per_row, rows_per_seq, n_total, row0, shared, w_br_attn, w_br_ssm,
           w_out, g_post_mix, g_pre_ffn, w_router_t, b_router):
    rows = x.shape[0]
    tm = min(MERGE_TM, rows)
    blk0 = row0 // tm
    nk = D_MODEL // MERGE_TK
    ga0 = (Q_W + 2 * KV_W + D_SSM) // MERGE_TK
    gs0 = ga0 + nk
    row2 = lambda w: pl.BlockSpec((tm, w), lambda i, j: (i, 0))
    vec = pl.BlockSpec((1, D_MODEL), lambda i, j: (0, 0))
    in_specs = [row2(Q_W), row2(D_SSM),
                pl.BlockSpec((tm, MERGE_TK), lambda i, j: (i, ga0 + j)),
                pl.BlockSpec((tm, MERGE_TK), lambda i, j: (i, gs0 + j)),
                pl.BlockSpec((Q_W, MERGE_TK), lambda i, j: (0, j)),
                pl.BlockSpec((D_SSM, MERGE_TK), lambda i, j: (0, j)),
                pl.BlockSpec((MERGE_TK, D_MODEL), lambda i, j: (j, 0)),
                row2(D_MODEL), vec,
                _mod_spec(per_row, tm, rows_per_seq, 2, 2),
                _mod_spec(per_row, tm, rows_per_seq, 4, 2),
                _mod_spec(per_row, tm, rows_per_seq, 3, 2),
                vec,
                pl.BlockSpec((N_EXPERTS, D_MODEL), lambda i, j: (0, 0)),
                pl.BlockSpec((N_EXPERTS, 1), lambda i, j: (0, 0))]
    args = [o_attn, y_ssm, proj, proj, w_br_attn, w_br_ssm, w_out, x, g_post_mix.reshape(1, D_MODEL),
            mod, mod, mod, g_pre_ffn.reshape(1, D_MODEL), w_router_t, b_router.reshape(N_EXPERTS, 1)]
    assert len(args) == N_MERGE_IN
    aliases = {}
    if shared is not None:
        aliases = {len(args): 1, len(args) + 1: 2}
        in_specs += [pl.BlockSpec(memory_space=pl.ANY), pl.BlockSpec(memory_space=pl.ANY)]
        args += list(shared)
    return pl.pallas_call(
        _merge_kernel,
        grid=(rows // tm, nk),
        in_specs=in_specs,
        out_specs=[row2(D_MODEL),
                   pl.BlockSpec((tm, D_MODEL), lambda i, j: (blk0 + i, 0)),
                   pl.BlockSpec((N_EXPERTS, tm), lambda i, j: (0, blk0 + i))],
        out_shape=[jax.ShapeDtypeStruct((rows, D_MODEL), F32),
                   jax.ShapeDtypeStruct((n_total, D_MODEL), F32),
                   jax.ShapeDtypeStruct((N_EXPERTS, n_total), F32)],
        scratch_shapes=[pltpu.VMEM((tm, D_MODEL), F32)],
        input_output_aliases=aliases,
        compiler_params=_cparams(("arbitrary", "arbitrary")),
        name="merge",
    )(*args)


def _bucket_tables():
    ql = np.arange(WINDOW)[:, None]
    kl = np.arange(2 * WINDOW)[None, :]
    dist = ql + WINDOW - kl
    prompt = np.where((dist >= 0) & (dist < WINDOW), _t5_bucket_np(dist), -1).astype(np.int32)
    d_s = (WINDOW - 1 - np.arange(WINDOW))[None, :]
    sample = np.broadcast_to(_t5_bucket_np(d_s), (SUBLANE, WINDOW)).astype(np.int32)
    return prompt, sample


def _front(p):
    l = 0
    batch, seq, _ = p['x_prompt'].shape
    nseq = p['x_sample'].shape[0]
    xp = p['x_prompt'].reshape(batch * seq, D_MODEL)
    xs = p['x_sample'].reshape(nseq, D_MODEL)

    c_all = jnp.concatenate([p['c_prompt'], p['c_sample'],
                             jnp.zeros((SUBLANE - (batch + nseq) % SUBLANE, D_MODEL), F32)], axis=0)
    mod = _ada(c_all, p['w_ada'][l], p['b_ada'][l])
    mod_p = mod[:batch].reshape(batch, 1, 6 * D_MODEL)
    mod_s = mod[batch:batch + nseq]

    bucket_p, bucket_s = _bucket_tables()
    bias_p = _bias_table(bucket_p, p['rel_bias'], WINDOW)
    bias_s = _bias_table(bucket_s, p['rel_bias'], 0)[1, :, 0, :]
    sinks = p['attn_sinks'][l]

    w_in_t = p['w_in'][l].astype(BF16)
    proj_p = _proj(xp, mod_p, False, seq, p['g_pre_mix'][l], w_in_t)
    proj_s = _proj(xs, mod_s, True, 1, p['g_pre_mix'][l], w_in_t)

    o_p = _attn_prompt(proj_p, batch, seq, bias_p, sinks)
    kv_p = proj_p.reshape(batch, seq, IN_W)[:, seq - WINDOW:, Q_W:Q_W + 2 * KV_W]
    new_k_p = kv_p[..., :KV_W].reshape(1, batch, WINDOW, N_KV_HEADS, HEAD_DIM)
    new_v_p = kv_p[..., KV_W:].reshape(1, batch, WINDOW, N_KV_HEADS, HEAD_DIM)
    o_s3, new_k_s, new_v_s = _attn_sample(
        proj_s[:, :Q_W].reshape(nseq, N_HEADS, HEAD_DIM),
        proj_s[:, Q_W:Q_W + KV_W].reshape(nseq, 1, KV_W),
        proj_s[:, Q_W + KV_W:Q_W + 2 * KV_W].reshape(nseq, 1, KV_W),
        p['cache_win_k'][l].reshape(nseq, WINDOW, KV_W),
        p['cache_win_v'][l].reshape(nseq, WINDOW, KV_W), bias_s, sinks)
    o_s = o_s3.reshape(nseq, Q_W)

    lbr, lbi, bbr, bbi = _ssm_disc(p['ssm_a_re'][l], p['ssm_a_im'][l], p['ssm_log_dt'][l],
                                   p['ssm_b_re'][l], p['ssm_b_im'][l])
    lbr = lbr.reshape(1, SSM_W)
    lbi = lbi.reshape(1, SSM_W)
    bb = jnp.concatenate([_block_diag_tiles(bbr), _block_diag_tiles(bbi)], axis=-1).astype(BF16)
    c_re_t = p['ssm_c_re'][l].transpose(0, 2, 1)
    c_im_t = p['ssm_c_im'][l].transpose(0, 2, 1)
    cc = jnp.concatenate([_block_diag_tiles(c_re_t), -_block_diag_tiles(c_im_t)], axis=1).astype(BF16)
    u0 = Q_W + 2 * KV_W
    zeros = jnp.zeros((batch, SSM_W), F32)
    w_glu = p['w_glu'][l].astype(BF16)
    y_p3, hr_p, hi_p = _ssm(proj_p.reshape(batch, seq, IN_W), zeros, zeros, batch, SSM_TC, lbr, lbi, bb, cc,
                            p['ssm_d'][l], w_glu, p['b_glu'][l], seq_major=True)
    y_p = y_p3.reshape(batch * seq, D_SSM)
    y_s, hr_s, hi_s = _ssm(proj_s[:, u0:u0 + D_SSM], p['state_ssm_re'][l].reshape(nseq, SSM_W),
                           p['state_ssm_im'][l].reshape(nseq, SSM_W), nseq, 1, lbr, lbi, bb, cc,
                           p['ssm_d'][l], w_glu, p['b_glu'][l])

    wa_t = p['w_br_attn'][l].astype(BF16)
    ws_t = p['w_br_ssm'][l].astype(BF16)
    wo = p['w_out'][l].astype(BF16)
    n_total = batch * seq + nseq
    merge = functools.partial(_merge, w_br_attn=wa_t, w_br_ssm=ws_t, w_out=wo, g_post_mix=p['g_post_mix'][l],
                              g_pre_ffn=p['g_pre_ffn'][l], w_router_t=p['w_router'][l].T,
                              b_router=p['b_router'][l])
    x1_p, h2_buf, lg_buf = merge(o_p, y_p, proj_p, xp, mod_p, False, seq, n_total, 0, None)
    x1_s, h2_all, lg_all = merge(o_s, y_s, proj_s, xs, mod_s, True, 1, n_total, batch * seq, (h2_buf, lg_buf))

    st = lambda h, n: h.reshape(1, n, N_SSM_GROUPS, SSM_STATE)
    return dict(
        mod_p=mod_p, mod_s=mod_s, h2_all=h2_all, lg_all=lg_all,
        p=dict(proj=proj_p, o_attn=o_p, new_k=new_k_p, new_v=new_v_p, y_ssm=y_p, h_re=st(hr_p, batch),
               h_im=st(hi_p, batch), x1=x1_p),
        s=dict(proj=proj_s, o_attn=o_s, new_k=new_k_s.reshape(1, nseq, WINDOW, N_KV_HEADS, HEAD_DIM),
               new_v=new_v_s.reshape(1, nseq, WINDOW, N_KV_HEADS, HEAD_DIM), y_ssm=y_s,
               h_re=st(hr_s, nseq), h_im=st(hi_s, nseq), x1=x1_s))


def _count_steps(c, step, n_max):
    out = jnp.zeros_like(c)
    for q in range(-(-n_max // step)):
        out = out + jnp.where(c > float(q * step), 1.0, 0.0)
    return out


def _route_kernel(lg_ref, pos_ref, gate_ref, texp_ref, trows_ref, ntiles_ref, pstart_ref, plen_ref):
    lg = lg_ref[...]
    e, tn = lg.shape
    erow = lax.broadcasted_iota(I32, (e, tn), 0).astype(F32)
    work = lg
    vals, hits = [], []
    for _ in range(TOP_K):
        m = jnp.max(work, axis=0, keepdims=True)
        idx = jnp.min(jnp.where(work == m, erow, float(e)), axis=0, keepdims=True)
        hit = erow == idx
        vals.append(m)
        hits.append(hit)
        work = jnp.where(hit, -jnp.inf, work)
    ex = [jnp.exp(v - vals[0]) for v in vals]
    den = ex[0] + ex[1] + ex[2] + ex[3]
    gate_ref[...] = jnp.concatenate([x / den for x in ex], axis=0)

    chosen = jnp.zeros((e, tn), F32)
    for hit in hits:
        chosen = chosen + jnp.where(hit, 1.0, 0.0)
    chosen_b = chosen.astype(BF16)
    tri = (lax.broadcasted_iota(I32, (LANE, LANE), 0) <= lax.broadcasted_iota(I32, (LANE, LANE), 1))
    tri = jnp.where(tri, 1.0, 0.0).astype(BF16)
    carry = jnp.zeros((e, 1), F32)
    ranks = []
    for b in range(tn // LANE):
        blk = chosen[:, b * LANE:(b + 1) * LANE]
        inc = _dot(chosen_b[:, b * LANE:(b + 1) * LANE], tri) + carry
        ranks.append(inc - blk)
        carry = inc[:, LANE - 1:LANE]
    rank = jnp.concatenate(ranks, axis=1)
    cnt_col = carry
    cnt_row = _dot_nt(jnp.ones((SUBLANE, tn), BF16), chosen_b)[0:1, :]

    tiles_col = _count_steps(cnt_col, EXP_TM, tn)
    tiles_row = _count_steps(cnt_row, EXP_TM, tn)
    ee_r = lax.broadcasted_iota(I32, (e, e), 0)
    ee_c = lax.broadcasted_iota(I32, (e, e), 1)
    tstart_col = jnp.sum(jnp.where(ee_c < ee_r, tiles_row, 0.0), axis=1, keepdims=True)
    ntiles = jnp.sum(tiles_row, axis=1, keepdims=True)
    rstart_col = tstart_col * float(EXP_TM)
    pos = [jnp.sum(jnp.where(hit, rstart_col + rank, 0.0), axis=0, keepdims=True) for hit in hits]
    pos_ref[...] = jnp.concatenate(pos, axis=0).astype(I32)

    mm = lax.broadcasted_iota(I32, (e, LANE), 1).astype(F32)
    e_col = lax.broadcasted_iota(I32, (e, LANE), 0).astype(F32)
    own = (mm >= tstart_col) & (mm < tstart_col + tiles_col)
    texp = jnp.sum(jnp.where(own, e_col, 0.0), axis=0, keepdims=True)
    rows_here = jnp.minimum(float(EXP_TM), cnt_col - (mm - tstart_col) * float(EXP_TM))
    trows = jnp.sum(jnp.where(own, rows_here, 0.0), axis=0, keepdims=True)
    last_e = jnp.max(jnp.where(tiles_col > 0.0, e_col, 0.0), axis=0, keepdims=True)
    texp = jnp.where(mm[0:1, :] < ntiles, texp, last_e)
    texp_ref[...] = texp.astype(I32)
    trows_ref[...] = trows.astype(I32)
    ntiles_ref[...] = jnp.broadcast_to(ntiles, (1, LANE)).astype(I32)
    nsub_col = _count_steps(cnt_col, EXP_SUB, tn)
    pstart_ref[...] = jnp.broadcast_to(rstart_col + cnt_col, (e, LANE)).astype(I32)
    plen_ref[...] = jnp.broadcast_to(nsub_col * float(EXP_SUB) - cnt_col, (e, LANE)).astype(I32)


def _route(lg_t):
    e, tn = lg_t.shape
    i32 = lambda shape: jax.ShapeDtypeStruct(shape, I32)
    return pl.pallas_call(
        _route_kernel,
        out_shape=[i32((TOP_K, tn)), jax.ShapeDtypeStruct((TOP_K, tn), F32),
                   i32((1, LANE)), i32((1, LANE)), i32((1, LANE)), i32((e, LANE)), i32((e, LANE))],
        compiler_params=pltpu.CompilerParams(vmem_limit_bytes=VMEM_LIMIT),
        name="route",
    )(lg_t)


def _max_tiles(n_tok):
    return (n_tok * TOP_K) // EXP_TM + N_EXPERTS


def _dispatch_kernel(pos_ref, pstart_ref, plen_ref, h2_ref, zero_ref, xs_ref, sem):
    i = pl.program_id(0)

    def row_copy(src, s, d):
        return pltpu.make_async_copy(src.at[pl.ds(s, 1)], xs_ref.at[pl.ds(d, 1)], sem)

    for t in range(TOK_BLK):
        for k in range(TOP_K):
            row_copy(h2_ref, t, pos_ref[k, t]).start(priority=k % 2)
    for _ in range(TOP_K):
        pltpu.make_async_copy(h2_ref, xs_ref.at[pl.ds(0, TOK_BLK)], sem).wait()

    @pl.when(i == 0)
    def _():
        def per_expert(ex, c):
            n = plen_ref[ex]
            s = pstart_ref[ex]

            def zissue(r, cc):
                row_copy(zero_ref, 0, s + r).start()
                return cc

            def zdrain(r, cc):
                row_copy(zero_ref, 0, 0).wait()
                return cc

            lax.fori_loop(0, n, zissue, 0)
            lax.fori_loop(0, n, zdrain, 0)
            return c

        lax.fori_loop(0, N_EXPERTS, per_expert, 0)


def _dispatch(pos3, pstart, plen, h2_all):
    n_tok = h2_all.shape[0]
    n_rows = _max_tiles(n_tok) * EXP_TM
    smem = pl.BlockSpec(memory_space=pltpu.SMEM)
    hbm = pl.BlockSpec(memory_space=pl.ANY)
    return pl.pallas_call(
        _dispatch_kernel,
        grid=(n_tok // TOK_BLK,),
        in_specs=[pl.BlockSpec((None, TOP_K, TOK_BLK), lambda i: (i, 0, 0), memory_space=pltpu.SMEM),
                  smem, smem,
                  pl.BlockSpec((TOK_BLK, D_MODEL), lambda i: (i, 0)),
                  pl.BlockSpec((SUBLANE, D_MODEL), lambda i: (0, 0))],
        out_specs=hbm,
        out_shape=jax.ShapeDtypeStruct((n_rows, D_MODEL), F32),
        scratch_shapes=[pltpu.SemaphoreType.DMA(())],
        compiler_params=_cparams(("arbitrary",)),
        name="dispatch",
    )(pos3, pstart, plen, h2_all, jnp.zeros((SUBLANE, D_MODEL), F32))


def _expert_kernel(texp_ref, trows_ref, nt_ref, xs_hbm, wg_ref, wl_ref, bg_ref, bl_ref, wd_ref, bd_ref,
                   o_ref, x_ref, x_sem, xb_scr, act_scr, wg_scr, wl_scr, wd_scr):
    m = pl.program_id(0)
    s = pl.program_id(1)
    n_tiles = nt_ref[0]
    valid = m < n_tiles

    rows = trows_ref[m]
    nbig = lax.shift_right_logical(rows, int(math.log2(EXP_BIG)))
    big_rows = nbig * EXP_BIG
    nsmall = lax.shift_right_logical(rows - big_rows + (EXP_SUB - 1), int(math.log2(EXP_SUB)))
    nsub_done = nbig * (EXP_BIG // EXP_SUB) + nsmall

    def x_copy(tile):
        return pltpu.make_async_copy(xs_hbm.at[pl.ds(pl.multiple_of(tile * EXP_TM, EXP_TM), EXP_TM)],
                                     x_ref, x_sem)

    @pl.when((m == 0) & (s == 0))
    def _():
        x_copy(0).start()

    @pl.when(valid & (s == 0))
    def _():
        x_copy(m).wait()

        def to_bf16(r, c):
            r0 = pl.multiple_of(r * EXP_SUB, EXP_SUB)
            xb_scr[pl.ds(r0, EXP_SUB), :] = x_ref[pl.ds(r0, EXP_SUB), :].astype(BF16)
            return c

        lax.fori_loop(0, nsub_done, to_bf16, 0)

        @pl.when(m + 1 < n_tiles)
        def _():
            x_copy(m + 1).start()

    def over_rows(first, step):
        def big(r, c):
            step(pl.multiple_of(r * EXP_BIG, EXP_BIG), EXP_BIG)
            return c

        def small(r, c):
            step(pl.multiple_of(big_rows + r * EXP_SUB, EXP_SUB), EXP_SUB)
            return c

        @pl.when(nbig > 0)
        def _():
            first(EXP_BIG)
            lax.fori_loop(1, nbig, big, 0)
            lax.fori_loop(0, nsmall, small, 0)

        @pl.when(nbig == 0)
        def _():
            first(EXP_SUB)
            lax.fori_loop(1, nsmall, small, 0)

    @pl.when(valid & (s < EXP_NF))
    def _():
        def finish(r0, n, hg, hl):
            x_glu = jnp.minimum(hg, SWIGLU_LIMIT)
            x_lin = jnp.clip(hl, -SWIGLU_LIMIT, SWIGLU_LIMIT)
            act = x_glu * _sigmoid(SWIGLU_ALPHA * x_glu) * (x_lin + 1.0)
            act_scr[s, pl.ds(r0, n), :] = act.astype(BF16)

        def first(n):
            xb = xb_scr[0:n, :]
            hg = jnp.broadcast_to(bg_ref[...], (n, EXP_TF))
            hl = jnp.broadcast_to(bl_ref[...], (n, EXP_TF))
            for q in range(D_MODEL // EXP_KQ):
                ks = slice(q * EXP_KQ, (q + 1) * EXP_KQ)
                wgq = wg_ref[ks, :].astype(BF16)
                wlq = wl_ref[ks, :].astype(BF16)
                wg_scr[ks, :] = wgq
                wl_scr[ks, :] = wlq
                hg = hg + _dot(xb[:, ks], wgq)
                hl = hl + _dot(xb[:, ks], wlq)
            finish(0, n, hg, hl)

        def step(r0, n):
            xb = xb_scr[pl.ds(r0, n), :]
            finish(r0, n, _dot(xb, wg_scr[...]) + bg_ref[...], _dot(xb, wl_scr[...]) + bl_ref[...])

        over_rows(first, step)

    @pl.when(valid & (s >= EXP_NF))
    def _():
        def first(n):
            acc = jnp.broadcast_to(bd_ref[...], (n, EXP_TD))
            for f in range(EXP_NF):
                fs = slice(f * EXP_TF, (f + 1) * EXP_TF)
                wdq = wd_ref[fs, :].astype(BF16)
                wd_scr[fs, :] = wdq
                acc = acc + _dot(act_scr[f, 0:n, :], wdq)
            o_ref[0:n, :] = acc

        def step(r0, n):
            acc = jnp.broadcast_to(bd_ref[...], (n, EXP_TD))
            for f in range(EXP_NF):
                acc = acc + _dot(act_scr[f, pl.ds(r0, n), :], wd_scr[f * EXP_TF:(f + 1) * EXP_TF, :])
            o_ref[pl.ds(r0, n), :] = acc

        def zero(r, c):
            r0 = pl.multiple_of(r * EXP_SUB, EXP_SUB)
            o_ref[pl.ds(r0, EXP_SUB), :] = jnp.zeros((EXP_SUB, EXP_TD), F32)
            return c

        over_rows(first, step)
        lax.fori_loop(nsub_done, EXP_TM // EXP_SUB, zero, 0)


def _experts(texp, trows, ntiles, xs, w_gate_up, b_gate_up, w_down, b_down):
    n_tiles = xs.shape[0] // EXP_TM
    nsteps = EXP_NF + EXP_ND

    def tile(m, nt):
        return jnp.minimum(m, nt[0] - 1)

    def ea(m, s, te, nt):
        return te[jnp.where(s < EXP_NF, m, jnp.minimum(m + 1, nt[0] - 1))]

    def fa(m, s, nt):
        return jnp.where((m < nt[0]) & (s < EXP_NF), s, 0)

    def fb(m, s, nt):
        return jnp.where(m < nt[0], jnp.maximum(s - EXP_NF, 0), EXP_ND - 1)

    grid_spec = pltpu.PrefetchScalarGridSpec(
        num_scalar_prefetch=3,
        grid=(ntiles[0], nsteps),
        in_specs=[
            pl.BlockSpec(memory_space=pl.ANY),
            pl.BlockSpec((None, D_MODEL, EXP_TF),
                         lambda m, s, te, tr, nt: (ea(m, s, te, nt), 0, fa(m, s, nt))),
            pl.BlockSpec((None, D_MODEL, EXP_TF),
                         lambda m, s, te, tr, nt: (ea(m, s, te, nt), 0, EXP_NF + fa(m, s, nt))),
            pl.BlockSpec((None, 1, EXP_TF), lambda m, s, te, tr, nt: (ea(m, s, te, nt), 0, fa(m, s, nt))),
            pl.BlockSpec((None, 1, EXP_TF),
                         lambda m, s, te, tr, nt: (ea(m, s, te, nt), 0, EXP_NF + fa(m, s, nt))),
            pl.BlockSpec((None, D_FF, EXP_TD), lambda m, s, te, tr, nt: (te[m], 0, fb(m, s, nt))),
            pl.BlockSpec((None, 1, EXP_TD), lambda m, s, te, tr, nt: (te[m], 0, fb(m, s, nt))),
        ],
        out_specs=pl.BlockSpec((EXP_TM, EXP_TD), lambda m, s, te, tr, nt: (tile(m, nt), fb(m, s, nt))),
        scratch_shapes=[pltpu.VMEM((EXP_TM, D_MODEL), F32), pltpu.SemaphoreType.DMA(()),
                        pltpu.VMEM((EXP_TM, D_MODEL), BF16),
                        pltpu.VMEM((EXP_NF, EXP_TM, EXP_TF), BF16),
                        pltpu.VMEM((D_MODEL, EXP_TF), BF16), pltpu.VMEM((D_MODEL, EXP_TF), BF16),
                        pltpu.VMEM((D_FF, EXP_TD), BF16)],
    )
    return pl.pallas_call(
        _expert_kernel,
        grid_spec=grid_spec,
        out_shape=jax.ShapeDtypeStruct((xs.shape[0], D_MODEL), F32),
        compiler_params=pltpu.CompilerParams(dimension_semantics=("arbitrary", "arbitrary"),
                                             vmem_limit_bytes=EXP_VMEM_LIMIT),
        name="experts",
    )(texp, trows, ntiles, xs, w_gate_up, w_gate_up,
      b_gate_up.reshape(N_EXPERTS, 1, 2 * D_FF), b_gate_up.reshape(N_EXPERTS, 1, 2 * D_FF),
      w_down, b_down.reshape(N_EXPERTS, 1, D_MODEL))


def _combine_kernel(pos_ref, pos_next_ref, gate_ref, ys_ref, x1_ref, g2_ref, gpf_ref, o_ref, buf, sem):
    i = pl.program_id(0)
    slot = lax.rem(i, 2)

    def gather(p_ref, sl):
        for t in range(TOK_BLK):
            for k in range(TOP_K):
                pltpu.make_async_copy(ys_ref.at[pl.ds(p_ref[k, t], 1)], buf.at[sl, k, pl.ds(t, 1)],
                                      sem.at[sl]).start(priority=k % 2)

    @pl.when(i == 0)
    def _():
        gather(pos_ref, 0)

    @pl.when(i + 1 < pl.num_programs(0))
    def _():
        gather(pos_next_ref, 1 - slot)

    for k in range(TOP_K):
        pltpu.make_async_copy(ys_ref.at[pl.ds(0, TOK_BLK)], buf.at[slot, k], sem.at[slot]).wait()
    g = gate_ref[...]
    f = g[:, 0:1] * buf[slot, 0]
    for k in range(1, TOP_K):
        f = f + g[:, k:k + 1] * buf[slot, k]
    o_ref[...] = x1_ref[...] + g2_ref[...] * _rms(f, gpf_ref[...])


def _combine(pos3, gates_t, ys, x1, mod, per_row, rows_per_seq, g_post_ffn, blk0):
    rows = x1.shape[0]
    nblk = rows // TOK_BLK
    return pl.pallas_call(
        _combine_kernel,
        grid=(nblk,),
        in_specs=[pl.BlockSpec((None, TOP_K, TOK_BLK), lambda i: (blk0 + i, 0, 0), memory_space=pltpu.SMEM),
                  pl.BlockSpec((None, TOP_K, TOK_BLK), lambda i: (blk0 + jnp.minimum(i + 1, nblk - 1), 0, 0),
                               memory_space=pltpu.SMEM),
                  pl.BlockSpec((TOK_BLK, TOP_K), lambda i: (blk0 + i, 0)),
                  pl.BlockSpec(memory_space=pl.ANY),
                  pl.BlockSpec((TOK_BLK, D_MODEL), lambda i: (i, 0)),
                  _mod_spec(per_row, TOK_BLK, rows_per_seq, 5, 1),
                  pl.BlockSpec((1, D_MODEL), lambda i: (0, 0))],
        out_specs=pl.BlockSpec((TOK_BLK, D_MODEL), lambda i: (i, 0)),
        out_shape=jax.ShapeDtypeStruct((rows, D_MODEL), F32),
        scratch_shapes=[pltpu.VMEM((2, TOP_K, TOK_BLK, D_MODEL), F32), pltpu.SemaphoreType.DMA((2,))],
        compiler_params=_cparams(("arbitrary",)),
        name="combine",
    )(pos3, pos3, gates_t, ys, x1, mod, g_post_ffn.reshape(1, D_MODEL))


def _moe(fr, p, batch, seq, nseq):
    l = 0
    h2_all, lg_all = fr['h2_all'], fr['lg_all']
    n_tok = h2_all.shape[0]
    pos, gates, texp, trows, ntiles, pstart, plen = _route(lg_all)
    pos3 = pos.reshape(TOP_K, n_tok // TOK_BLK, TOK_BLK).transpose(1, 0, 2)
    xs = _dispatch(pos3, pstart[:, 0], plen[:, 0], h2_all)
    ys = _experts(texp[0], trows[0], ntiles[0, :1], xs, p['w_gate_up'][l], p['b_gate_up'][l],
                  p['w_down'][l], p['b_down'][l])
    gates_t = gates.T
    y_p = _combine(pos3, gates_t, ys, fr['p']['x1'], fr['mod_p'], False, seq, p['g_post_ffn'][l], 0)
    y_s = _combine(pos3, gates_t, ys, fr['s']['x1'], fr['mod_s'], True, 1, p['g_post_ffn'][l],
                   batch * seq // TOK_BLK)
    return y_p, y_s


def kernel(x_prompt, x_sample, c_prompt, c_sample, cache_win_k, cache_win_v, state_ssm_re, state_ssm_im, w_ada, b_ada, g_pre_mix, g_post_mix, g_pre_ffn, g_post_ffn, w_in, attn_sinks, rel_bias, ssm_a_re, ssm_a_im, ssm_log_dt, ssm_b_re, ssm_b_im, ssm_c_re, ssm_c_im, ssm_d, w_glu, b_glu, w_br_attn, w_br_ssm, w_out, w_router, b_router, w_gate_up, b_gate_up, w_down, b_down):
    p = dict(locals())
    batch, seq, _ = x_prompt.shape
    nseq = x_sample.shape[0]
    fr = _front(p)
    y_p, y_s = _moe(fr, p, batch, seq, nseq)
    fp, fs = fr['p'], fr['s']
    return (y_p.reshape(batch, seq, D_MODEL), y_s.reshape(nseq, 1, D_MODEL),
            fp['new_k'], fp['new_v'], fp['h_re'], fp['h_im'],
            fs['new_k'], fs['new_v'], fs['h_re'], fs['h_im'])
```

```python
import functools
import math

import numpy as np
import jax
import jax.numpy as jnp
from jax import lax
from jax.experimental import pallas as pl
from jax.experimental.pallas import tpu as pltpu

F32 = jnp.float32
BF16 = jnp.bfloat16
I32 = jnp.int32

D_MODEL = 2048
N_HEADS = 16
N_KV_HEADS = 4
HEAD_DIM = 64
Q_GROUP = N_HEADS // N_KV_HEADS
WINDOW = 128
N_BUCKETS = 32
MAX_EXACT = N_BUCKETS // 2
MAX_DISTANCE = 128
D_SSM = 1024
SSM_GROUP = 16
N_SSM_GROUPS = 64
SSM_STATE = 64
N_EXPERTS = 32
TOP_K = 4
D_FF = 2048
SWIGLU_LIMIT = 7.0
SWIGLU_ALPHA = 1.702
NORM_EPS = 1e-6
NEG_INF = -1e30
Q_W = N_HEADS * HEAD_DIM
KV_W = N_KV_HEADS * HEAD_DIM
IN_W = Q_W + 2 * KV_W + D_SSM + 2 * D_MODEL
SSM_W = N_SSM_GROUPS * SSM_STATE

LANE = 128
SUBLANE = 8
VMEM_LIMIT = 56 * 1024 * 1024
EXP_VMEM_LIMIT = 60 * 1024 * 1024

ADA_TN = 1024
ATTN_TB = 16
PROJ_TM = 1024
PROJ_TN = 512
MERGE_TM = 512
MERGE_TK = 512
SSM_TC = 128
SSM_LB = 512
SSM_TILES = D_SSM // LANE
TOK_BLK = 128
EXP_TM = 1152
EXP_BIG = 512
EXP_SUB = 128
EXP_TF = 512
EXP_TD = 512
EXP_KQ = 512
EXP_NF = D_FF // EXP_TF
EXP_ND = D_MODEL // EXP_TD


def _cparams(sem):
    return pltpu.CompilerParams(dimension_semantics=sem, vmem_limit_bytes=VMEM_LIMIT)


def _sigmoid(x):
    return 1.0 / (1.0 + jnp.exp(-x))


def _rms(x, g):
    return x * lax.rsqrt(jnp.mean(x * x, axis=-1, keepdims=True) + NORM_EPS) * g


def _dot(a, b):
    return jnp.dot(a, b, preferred_element_type=F32)


def _dot_nt(a, b):
    return lax.dot_general(a, b, (((1,), (1,)), ((), ())), preferred_element_type=F32)


def _ada_kernel(c_ref, w_ref, b_ref, o_ref):
    c = c_ref[...]
    s = (c * _sigmoid(c)).astype(BF16)
    o_ref[...] = _dot(s, w_ref[...].astype(BF16)) + b_ref[...]


def _ada(c_all, w_ada, b_ada):
    rows = c_all.shape[0]
    tn = ADA_TN
    n = w_ada.shape[1]
    return pl.pallas_call(
        _ada_kernel,
        grid=(n // tn,),
        in_specs=[pl.BlockSpec((rows, D_MODEL), lambda j: (0, 0)),
                  pl.BlockSpec((D_MODEL, tn), lambda j: (0, j)),
                  pl.BlockSpec((1, tn), lambda j: (0, j))],
        out_specs=pl.BlockSpec((rows, tn), lambda j: (0, j)),
        out_shape=jax.ShapeDtypeStruct((rows, n), F32),
        compiler_params=_cparams(("arbitrary",)),
        name="ada",
    )(c_all, w_ada, b_ada.reshape(1, n))


def _t5_bucket_np(dist):
    n = np.maximum(dist, 0)
    nf = np.maximum(n, 1).astype(np.float64)
    large = MAX_EXACT + (np.log(nf / MAX_EXACT) / math.log(MAX_DISTANCE / MAX_EXACT)
                         * (N_BUCKETS - MAX_EXACT)).astype(np.int32)
    large = np.minimum(large, N_BUCKETS - 1)
    return np.where(n < MAX_EXACT, n, large).astype(np.int32)


def _bias_kernel(bucket_ref, rb_ref, o_ref, *, mask_cols):
    h = pl.program_id(0)
    bucket = bucket_ref[...]
    acc = jnp.full(bucket.shape, NEG_INF, F32)
    for b in range(N_BUCKETS):
        acc = jnp.where(bucket == b, rb_ref[b, h], acc)
    o_ref[1] = acc
    col = lax.broadcasted_iota(I32, bucket.shape, 1)
    o_ref[0] = jnp.where(col >= mask_cols, acc, NEG_INF)


def _bias_table(bucket_np, rel_bias, mask_cols):
    r, c = bucket_np.shape
    return pl.pallas_call(
        functools.partial(_bias_kernel, mask_cols=mask_cols),
        grid=(N_HEADS,),
        in_specs=[pl.BlockSpec((r, c), lambda h: (0, 0)),
                  pl.BlockSpec(memory_space=pltpu.SMEM)],
        out_specs=pl.BlockSpec((2, None, r, c), lambda h: (0, h, 0, 0)),
        out_shape=jax.ShapeDtypeStruct((2, N_HEADS, r, c), F32),
        compiler_params=_cparams(("arbitrary",)),
        name="bias",
    )(jnp.asarray(bucket_np), rel_bias)


def _proj_kernel(x_ref, sc_ref, sh_ref, g_ref, w_ref, o_ref, h_scr):
    j = pl.program_id(1)
    tm = x_ref.shape[0]
    part = tm // 2 if tm % (2 * SUBLANE) == 0 else tm

    @pl.when(j == 0)
    def _():
        for lo in range(0, tm, part):
            rows = slice(lo, lo + part)
            sc = sc_ref[...] if sc_ref.shape[0] == 1 else sc_ref[rows, :]
            sh = sh_ref[...] if sh_ref.shape[0] == 1 else sh_ref[rows, :]
            h = (_rms(x_ref[rows, :], g_ref[...]) * (1.0 + sc) + sh).astype(BF16)
            h_scr[rows, :] = h
            o_ref[rows, :] = _dot(h, w_ref[...])

    @pl.when(j > 0)
    def _():
        o_ref[...] = _dot(h_scr[...], w_ref[...])


def _mod_spec(per_row, tm, rows_per_seq, col, nargs):
    if per_row:
        if nargs == 1:
            return pl.BlockSpec((tm, D_MODEL), lambda i: (i, col))
        return pl.BlockSpec((tm, D_MODEL), lambda i, j: (i, col))
    tiles_per_seq = rows_per_seq // tm
    if nargs == 1:
        return pl.BlockSpec((None, 1, D_MODEL), lambda i: (i // tiles_per_seq, 0, col))
    return pl.BlockSpec((None, 1, D_MODEL), lambda i, j: (i // tiles_per_seq, 0, col))


def _proj(x, mod, per_row, rows_per_seq, g_pre, w_in):
    rows = x.shape[0]
    tm = min(PROJ_TM, rows)
    return pl.pallas_call(
        _proj_kernel,
        grid=(rows // tm, IN_W // PROJ_TN),
        in_specs=[pl.BlockSpec((tm, D_MODEL), lambda i, j: (i, 0)),
                  _mod_spec(per_row, tm, rows_per_seq, 1, 2),
                  _mod_spec(per_row, tm, rows_per_seq, 0, 2),
                  pl.BlockSpec((1, D_MODEL), lambda i, j: (0, 0)),
                  pl.BlockSpec((D_MODEL, PROJ_TN), lambda i, j: (0, j))],
        out_specs=pl.BlockSpec((tm, PROJ_TN), lambda i, j: (i, j)),
        out_shape=jax.ShapeDtypeStruct((rows, IN_W), F32),
        scratch_shapes=[pltpu.VMEM((tm, D_MODEL), BF16)],
        compiler_params=_cparams(("arbitrary", "arbitrary")),
        name="proj",
    )(x, mod, mod, g_pre.reshape(1, D_MODEL), w_in)


def _attn_prompt_kernel(q_ref, kc_ref, kp_ref, vc_ref, vp_ref, bias_ref, sink_ref, o_ref):
    q = q_ref[...] * (HEAD_DIM ** -0.5)
    k = jnp.concatenate([kp_ref[...], kc_ref[...]], axis=0)
    v = jnp.concatenate([vp_ref[...], vc_ref[...]], axis=0)
    outs = []
    for g in range(N_KV_HEADS):
        kg = k[:, g * HEAD_DIM:(g + 1) * HEAD_DIM].astype(BF16)
        vg = v[:, g * HEAD_DIM:(g + 1) * HEAD_DIM].astype(BF16)
        for hh in range(Q_GROUP):
            h = g * Q_GROUP + hh
            qh = q[:, h * HEAD_DIM:(h + 1) * HEAD_DIM].astype(BF16)
            s = _dot_nt(qh, kg) + bias_ref[h]
            sink = sink_ref[h]
            m = jnp.maximum(jnp.max(s, axis=-1, keepdims=True), sink)
            p = jnp.exp(s - m)
            den = jnp.sum(p, axis=-1, keepdims=True) + jnp.exp(sink - m)
            outs.append(_dot(p.astype(BF16), vg) / den)
    o_ref[...] = jnp.concatenate(outs, axis=-1).astype(BF16)


def _attn_prompt(proj, batch, seq, bias, sinks):
    nb = seq // WINDOW
    kcol = Q_W // KV_W
    vcol = kcol + 1
    cur = lambda c: (lambda b, n: (b * nb + n, c))
    prev = lambda c: (lambda b, n: (b * nb + jnp.maximum(n - 1, 0), c))
    return pl.pallas_call(
        _attn_prompt_kernel,
        grid=(batch, nb),
        in_specs=[pl.BlockSpec((WINDOW, Q_W), cur(0)),
                  pl.BlockSpec((WINDOW, KV_W), cur(kcol)),
                  pl.BlockSpec((WINDOW, KV_W), prev(kcol)),
                  pl.BlockSpec((WINDOW, KV_W), cur(vcol)),
                  pl.BlockSpec((WINDOW, KV_W), prev(vcol)),
                  pl.BlockSpec((None, N_HEADS, WINDOW, 2 * WINDOW), lambda b, n: (jnp.minimum(n, 1), 0, 0, 0)),
                  pl.BlockSpec(memory_space=pltpu.SMEM)],
        out_specs=pl.BlockSpec((WINDOW, Q_W), lambda b, n: (b * nb + n, 0)),
        out_shape=jax.ShapeDtypeStruct((batch * seq, Q_W), BF16),
        compiler_params=_cparams(("arbitrary", "arbitrary")),
        name="attn_prompt",
    )(proj, proj, proj, proj, proj, bias, sinks)


def _attn_sample_kernel(q_ref, kn_ref, vn_ref, ck_ref, cv_ref, bias_ref, sink_ref,
                        o_ref, nk_ref, nv_ref):
    tb = q_ref.shape[0]
    row = lax.broadcasted_iota(I32, (tb, WINDOW, KV_W), 1)
    last = row == WINDOW - 1
    nk = jnp.where(last, kn_ref[...], pltpu.roll(ck_ref[...], WINDOW - 1, 1))
    nv = jnp.where(last, vn_ref[...], pltpu.roll(cv_ref[...], WINDOW - 1, 1))
    nk_ref[...] = nk
    nv_ref[...] = nv
    lane_grp = lax.broadcasted_iota(I32, (N_HEADS, KV_W), 1) // HEAD_DIM
    head_grp = lax.broadcasted_iota(I32, (N_HEADS, KV_W), 0) // Q_GROUP
    gmask = (lane_grp == head_grp).astype(F32)
    q = q_ref[...]
    qrow = jnp.concatenate([q] * N_KV_HEADS, axis=-1) * gmask
    s = jnp.einsum('bhc,brc->bhr', qrow.astype(BF16), nk.astype(BF16),
                   preferred_element_type=F32) * (HEAD_DIM ** -0.5)
    s = s + bias_ref[...]
    sink = sink_ref[...]
    m = jnp.maximum(jnp.max(s, axis=-1, keepdims=True), sink)
    p = jnp.exp(s - m)
    den = jnp.sum(p, axis=-1, keepdims=True) + jnp.exp(sink - m)
    o = jnp.einsum('bhr,brc->bhc', p.astype(BF16), nv.astype(BF16),
                   preferred_element_type=F32) * gmask
    o64 = o[..., 0:HEAD_DIM]
    for g in range(1, N_KV_HEADS):
        o64 = o64 + o[..., g * HEAD_DIM:(g + 1) * HEAD_DIM]
    o_ref[...] = (o64 / den).astype(BF16)


def _attn_sample(q3, kn, vn, cache_k, cache_v, bias, sinks):
    nseq = q3.shape[0]
    tb = ATTN_TB
    seq3 = lambda w: pl.BlockSpec((tb, WINDOW, w), lambda i: (i, 0, 0))
    return pl.pallas_call(
        _attn_sample_kernel,
        grid=(nseq // tb,),
        in_specs=[pl.BlockSpec((tb, N_HEADS, HEAD_DIM), lambda i: (i, 0, 0)),
                  pl.BlockSpec((tb, 1, KV_W), lambda i: (i, 0, 0)),
                  pl.BlockSpec((tb, 1, KV_W), lambda i: (i, 0, 0)),
                  seq3(KV_W), seq3(KV_W),
                  pl.BlockSpec((N_HEADS, WINDOW), lambda i: (0, 0)),
                  pl.BlockSpec((N_HEADS, 1), lambda i: (0, 0))],
        out_specs=[pl.BlockSpec((tb, N_HEADS, HEAD_DIM), lambda i: (i, 0, 0)),
                   seq3(KV_W), seq3(KV_W)],
        out_shape=[jax.ShapeDtypeStruct((nseq, N_HEADS, HEAD_DIM), BF16),
                   jax.ShapeDtypeStruct((nseq, WINDOW, KV_W), F32),
                   jax.ShapeDtypeStruct((nseq, WINDOW, KV_W), F32)],
        compiler_params=_cparams(("arbitrary",)),
        name="attn_sample",
    )(q3, kn, vn, cache_k, cache_v, bias, sinks.reshape(N_HEADS, 1))


def _ssm_disc_kernel(are_ref, aim_ref, ldt_ref, bre_ref, bim_ref,
                     lbr_ref, lbi_ref, bbr_ref, bbi_ref):
    a_re = are_ref[...]
    a_im = aim_ref[...]
    dt = jnp.exp(ldt_ref[...])
    lam_re = a_re * dt
    lam_im = a_im * dt
    mag = jnp.exp(lam_re)
    lb_re = mag * jnp.cos(lam_im)
    lb_im = mag * jnp.sin(lam_im)
    den = a_re * a_re + a_im * a_im
    nr = lb_re - 1.0
    ni = lb_im
    coef_re = (nr * a_re + ni * a_im) / den
    coef_im = (ni * a_re - nr * a_im) / den
    b_re = bre_ref[...]
    b_im = bim_ref[...]
    lbr_ref[...] = lb_re
    lbi_ref[...] = lb_im
    bbr_ref[...] = coef_re * b_re - coef_im * b_im
    bbi_ref[...] = coef_re * b_im + coef_im * b_re


def _ssm_disc(a_re, a_im, log_dt, b_re, b_im):
    g, p, j = N_SSM_GROUPS, SSM_STATE, SSM_GROUP
    vec = jax.ShapeDtypeStruct((g, 1, p), F32)
    mat = jax.ShapeDtypeStruct((g, j, p), F32)
    return pl.pallas_call(
        _ssm_disc_kernel,
        out_shape=[vec, vec, mat, mat],
        name="ssm_disc",
    )(a_re.reshape(g, 1, p), a_im.reshape(g, 1, p), log_dt.reshape(g, 1, 1),
      b_re.transpose(0, 2, 1), b_im.transpose(0, 2, 1))


def _block_diag_tiles(x):
    a, b = x.shape[1], x.shape[2]
    eye = jnp.eye(SUBLANE, dtype=x.dtype)
    y = jnp.einsum('kgab,gh->kgahb', x.reshape(SSM_TILES, SUBLANE, a, b), eye)
    return y.reshape(SSM_TILES, SUBLANE * a, SUBLANE * b)


def _gelu_tanh(x):
    return 0.5 * x * (1.0 + jnp.tanh(math.sqrt(2.0 / math.pi) * (x + 0.044715 * (x * x * x))))


def _ssm_kernel(*refs, nseq, tc, seq_major):
    if seq_major:
        ua_ref, ub_ref, perm_ref, perm_t_ref = refs[:4]
        refs = refs[4:]
        u = jnp.concatenate([ua_ref[...].reshape(nseq * tc, D_SSM // 2),
                             ub_ref[...].reshape(nseq * tc, D_SSM // 2)], axis=1)
        ub = _dot(perm_ref[...], u.astype(BF16)).astype(BF16)
    else:
        u = refs[0][...]
        refs = refs[1:]
        ub = u.astype(BF16)
    (h0r_ref, h0i_ref, lbr_ref, lbi_ref, bb_ref, cc_ref, d_ref, wglu_ref, bglu_ref,
     y_ref, hTr_ref, hTi_ref, hre, him, st_r, st_i) = refs
    paired = nseq == 4

    @pl.when(pl.program_id(0) == 0)
    def _():
        if paired:
            st_r[...] = jnp.concatenate([h0r_ref[...], h0r_ref[...]], axis=0)
            st_i[...] = jnp.concatenate([h0i_ref[...], h0i_ref[...]], axis=0)
        else:
            st_r[...] = h0r_ref[...]
            st_i[...] = h0i_ref[...]

    half = SSM_W // SSM_TILES
    rows = nseq * tc
    for k in range(SSM_TILES):
        bu = _dot(ub[:, k * LANE:(k + 1) * LANE], bb_ref[k])
        br, bi = bu[:, :half], bu[:, half:]
        if paired:
            ar = lbr_ref[:, k * half:(k + 1) * half]
            ai = lbi_ref[:, k * half:(k + 1) * half]
            tiles = (rows // SUBLANE, SUBLANE, half)
            pr = pltpu.roll(br.reshape(tiles), nseq, 1).reshape(rows, half)
            pi = pltpu.roll(bi.reshape(tiles), nseq, 1).reshape(rows, half)
            first = (lax.broadcasted_iota(I32, (rows, half), 0) & nseq) == 0
            br, bi = (jnp.where(first, br, ar * pr - ai * pi + br),
                      jnp.where(first, bi, ar * pi + ai * pr + bi))
        hre[:, k * half:(k + 1) * half] = br
        him[:, k * half:(k + 1) * half] = bi

    for blk in range(SSM_W // SSM_LB):
        sl = slice(blk * SSM_LB, (blk + 1) * SSM_LB)
        ar = lbr_ref[:, sl]
        ai = lbi_ref[:, sl]
        if paired:
            lower = lax.broadcasted_iota(I32, (SUBLANE, SSM_LB), 0) < nseq
            cr = jnp.where(lower, ar, ar * ar - ai * ai)
            ci = jnp.where(lower, ai, 2.0 * ar * ai)

            def body(m, carry):
                xr, xi = carry
                r0 = pl.multiple_of(m * SUBLANE, SUBLANE)
                hr = cr * xr - ci * xi + hre[pl.ds(r0, SUBLANE), sl]
                hi = cr * xi + ci * xr + him[pl.ds(r0, SUBLANE), sl]
                hre[pl.ds(r0, SUBLANE), sl] = hr
                him[pl.ds(r0, SUBLANE), sl] = hi
                return (jnp.where(lower, pltpu.roll(hr, nseq, 0), hr),
                        jnp.where(lower, pltpu.roll(hi, nseq, 0), hi))

            sr, si = lax.fori_loop(0, tc * nseq // SUBLANE, body, (st_r[:, sl], st_i[:, sl]))
        else:
            def body(t, carry):
                sr, si = carry
                r0 = pl.multiple_of(t * nseq, SUBLANE)
                br = hre[pl.ds(r0, nseq), sl]
                bi = him[pl.ds(r0, nseq), sl]
                nr = ar * sr - ai * si + br
                ni = ar * si + ai * sr + bi
                hre[pl.ds(r0, nseq), sl] = nr
                him[pl.ds(r0, nseq), sl] = ni
                return nr, ni

            sr, si = lax.fori_loop(0, tc, body, (st_r[:, sl], st_i[:, sl]))
        st_r[:, sl] = sr
        st_i[:, sl] = si

    ys = []
    for k in range(SSM_TILES):
        hr = hre[:, k * half:(k + 1) * half].astype(BF16)
        hi = him[:, k * half:(k + 1) * half].astype(BF16)
        ys.append(_dot(hr, cc_ref[k, :half, :]) + _dot(hi, cc_ref[k, half:, :]))
    yc = jnp.concatenate(ys, axis=-1)
    if seq_major:
        yc_hi, yc_lo = _split_bf16(yc)
        yc = _dot(perm_t_ref[...], yc_hi) + _dot(perm_t_ref[...], yc_lo)
    y = _gelu_tanh(yc + d_ref[...] * u)
    z = _dot(y.astype(BF16), wglu_ref[...]) + bglu_ref[...]
    y_ref[...] = (y * _sigmoid(z)).astype(BF16).reshape(y_ref.shape)

    if paired:
        hTr_ref[...] = st_r[nseq:, :]
        hTi_ref[...] = st_i[nseq:, :]
    else:
        hTr_ref[...] = st_r[...]
        hTi_ref[...] = st_i[...]


def _ssm(u_src, h0_re, h0_im, nseq, tc, lbr, lbi, bb, cc, d, w_glu, b_glu, seq_major=False):
    r = nseq * tc
    st_rows = max(nseq, SUBLANE)
    const2 = lambda shape: pl.BlockSpec(shape, lambda c: (0, 0))
    const3 = lambda shape: pl.BlockSpec(shape, lambda c: (0, 0, 0))
    if seq_major:
        steps = u_src.shape[1] // tc
        half_w = D_SSM // 2
        col0 = (Q_W + 2 * KV_W) // half_w
        t_idx, s_idx = np.divmod(np.arange(r), nseq)
        perm = np.zeros((r, r), np.float32)
        perm[np.arange(r), s_idx * tc + t_idx] = 1.0
        u_specs = [pl.BlockSpec((nseq, tc, half_w), lambda c: (0, c, col0)),
                   pl.BlockSpec((nseq, tc, half_w), lambda c: (0, c, col0 + 1)),
                   const2((r, r)), const2((r, r))]
        u_args = [u_src, u_src, jnp.asarray(perm, BF16), jnp.asarray(perm.T, BF16)]
        y_spec = pl.BlockSpec((nseq, tc, D_SSM), lambda c: (0, c, 0))
        y_shape = jax.ShapeDtypeStruct((nseq, steps * tc, D_SSM), BF16)
    else:
        steps = u_src.shape[0] // r
        u_specs = [pl.BlockSpec((r, D_SSM), lambda c: (c, 0))]
        u_args = [u_src]
        y_spec = pl.BlockSpec((r, D_SSM), lambda c: (c, 0))
        y_shape = jax.ShapeDtypeStruct((steps * r, D_SSM), BF16)
    return pl.pallas_call(
        functools.partial(_ssm_kernel, nseq=nseq, tc=tc, seq_major=seq_major),
        grid=(steps,),
        in_specs=u_specs + [
            const2((nseq, SSM_W)), const2((nseq, SSM_W)),
            const2((1, SSM_W)), const2((1, SSM_W)),
            const3((SSM_TILES, LANE, 2 * SSM_W // SSM_TILES)),
            const3((SSM_TILES, 2 * SSM_W // SSM_TILES, LANE)),
            const2((1, D_SSM)), const2((D_SSM, D_SSM)), const2((1, D_SSM))],
        out_specs=[y_spec, const2((nseq, SSM_W)), const2((nseq, SSM_W))],
        out_shape=[y_shape,
                   jax.ShapeDtypeStruct((nseq, SSM_W), F32),
                   jax.ShapeDtypeStruct((nseq, SSM_W), F32)],
        scratch_shapes=[pltpu.VMEM((r, SSM_W), F32), pltpu.VMEM((r, SSM_W), F32),
                        pltpu.VMEM((st_rows, SSM_W), F32), pltpu.VMEM((st_rows, SSM_W), F32)],
        compiler_params=_cparams(("arbitrary",)),
        name="ssm",
    )(*u_args, h0_re, h0_im, lbr, lbi, bb, cc, d.reshape(1, D_SSM), w_glu, b_glu.reshape(1, D_SSM))


def _split_bf16(x):
    hi = x.astype(BF16)
    lo = (x - hi.astype(F32)).astype(BF16)
    return hi, lo


N_MERGE_IN = 15


def _merge_kernel(*refs):
    (o_ref, y_ref, ga_ref, gs_ref, wa_ref, ws_ref, wo_ref, x_ref, gpm_ref, g1_ref, sc2_ref, sh2_ref,
     gpf_ref, wrt_ref, br_ref) = refs[:N_MERGE_IN]
    x1_ref, h2_ref, lg_ref, mix = refs[-4:]
    j = pl.program_id(1)
    last = pl.num_programs(1) - 1

    def contrib():
        a = _dot(o_ref[...], wa_ref[...])
        s = _dot(y_ref[...], ws_ref[...])
        merged = _sigmoid(ga_ref[...]) * a + _sigmoid(gs_ref[...]) * s
        return _dot(merged.astype(BF16), wo_ref[...])

    @pl.when(j == 0)
    def _():
        mix[...] = contrib()

    @pl.when((j > 0) & (j < last))
    def _():
        mix[...] += contrib()

    @pl.when(j == last)
    def _():
        x1 = x_ref[...] + g1_ref[...] * _rms(mix[...] + contrib(), gpm_ref[...])
        x1_ref[...] = x1
        h2 = _rms(x1, gpf_ref[...]) * (1.0 + sc2_ref[...]) + sh2_ref[...]
        h2_ref[...] = h2
        h_hi, h_lo = _split_bf16(h2)
        w_hi, w_lo = _split_bf16(wrt_ref[...])
        lg_ref[...] = (_dot_nt(w_hi, h_hi) + _dot_nt(w_hi, h_lo) + _dot_nt(w_lo, h_hi)) + br_ref[...]


def _merge(o_attn, y_ssm, proj, x, mod, per_row, rows_per_seq, n_total, row0, shared, w_br_attn, w_br_ssm,
           w_out, g_post_mix, g_pre_ffn, w_router_t, b_router):
    rows = x.shape[0]
    tm = min(MERGE_TM, rows)
    blk0 = row0 // tm
    nk = D_MODEL // MERGE_TK
    ga0 = (Q_W + 2 * KV_W + D_SSM) // MERGE_TK
    gs0 = ga0 + nk
    row2 = lambda w: pl.BlockSpec((tm, w), lambda i, j: (i, 0))
    vec = pl.BlockSpec((1, D_MODEL), lambda i, j: (0, 0))
    in_specs = [row2(Q_W), row2(D_SSM),
                pl.BlockSpec((tm, MERGE_TK), lambda i, j: (i, ga0 + j)),
                pl.BlockSpec((tm, MERGE_TK), lambda i, j: (i, gs0 + j)),
                pl.BlockSpec((Q_W, MERGE_TK), lambda i, j: (0, j)),
                pl.BlockSpec((D_SSM, MERGE_TK), lambda i, j: (0, j)),
                pl.BlockSpec((MERGE_TK, D_MODEL), lambda i, j: (j, 0)),
                row2(D_MODEL), vec,
                _mod_spec(per_row, tm, rows_per_seq, 2, 2),
                _mod_spec(per_row, tm, rows_per_seq, 4, 2),
                _mod_spec(per_row, tm, rows_per_seq, 3, 2),
                vec,
                pl.BlockSpec((N_EXPERTS, D_MODEL), lambda i, j: (0, 0)),
                pl.BlockSpec((N_EXPERTS, 1), lambda i, j: (0, 0))]
    args = [o_attn, y_ssm, proj, proj, w_br_attn, w_br_ssm, w_out, x, g_post_mix.reshape(1, D_MODEL),
            mod, mod, mod, g_pre_ffn.reshape(1, D_MODEL), w_router_t, b_router.reshape(N_EXPERTS, 1)]
    assert len(args) == N_MERGE_IN
    aliases = {}
    if shared is not None:
        aliases = {len(args): 1, len(args) + 1: 2}
        in_specs += [pl.BlockSpec(memory_space=pl.ANY), pl.BlockSpec(memory_space=pl.ANY)]
        args += list(shared)
    return pl.pallas_call(
        _merge_kernel,
        grid=(rows // tm, nk),
        in_specs=in_specs,
        out_specs=[row2(D_MODEL),
                   pl.BlockSpec((tm, D_MODEL), lambda i, j: (blk0 + i, 0)),
                   pl.BlockSpec((N_EXPERTS, tm), lambda i, j: (0, blk0 + i))],
        out_shape=[jax.ShapeDtypeStruct((rows, D_MODEL), F32),
                   jax.ShapeDtypeStruct((n_total, D_MODEL), F32),
                   jax.ShapeDtypeStruct((N_EXPERTS, n_total), F32)],
        scratch_shapes=[pltpu.VMEM((tm, D_MODEL), F32)],
        input_output_aliases=aliases,
        compiler_params=_cparams(("arbitrary", "arbitrary")),
        name="merge",
    )(*args)


def _bucket_tables():
    ql = np.arange(WINDOW)[:, None]
    kl = np.arange(2 * WINDOW)[None, :]
    dist = ql + WINDOW - kl
    prompt = np.where((dist >= 0) & (dist < WINDOW), _t5_bucket_np(dist), -1).astype(np.int32)
    d_s = (WINDOW - 1 - np.arange(WINDOW))[None, :]
    sample = np.broadcast_to(_t5_bucket_np(d_s), (SUBLANE, WINDOW)).astype(np.int32)
    return prompt, sample


def _front(p):
    l = 0
    batch, seq, _ = p['x_prompt'].shape
    nseq = p['x_sample'].shape[0]
    xp = p['x_prompt'].reshape(batch * seq, D_MODEL)
    xs = p['x_sample'].reshape(nseq, D_MODEL)

    c_all = jnp.concatenate([p['c_prompt'], p['c_sample'],
                             jnp.zeros((SUBLANE - (batch + nseq) % SUBLANE, D_MODEL), F32)], axis=0)
    mod = _ada(c_all, p['w_ada'][l], p['b_ada'][l])
    mod_p = mod[:batch].reshape(batch, 1, 6 * D_MODEL)
    mod_s = mod[batch:batch + nseq]

    bucket_p, bucket_s = _bucket_tables()
    bias_p = _bias_table(bucket_p, p['rel_bias'], WINDOW)
    bias_s = _bias_table(bucket_s, p['rel_bias'], 0)[1, :, 0, :]
    sinks = p['attn_sinks'][l]

    w_in_t = p['w_in'][l].astype(BF16)
    proj_p = _proj(xp, mod_p, False, seq, p['g_pre_mix'][l], w_in_t)
    proj_s = _proj(xs, mod_s, True, 1, p['g_pre_mix'][l], w_in_t)

    o_p = _attn_prompt(proj_p, batch, seq, bias_p, sinks)
    kv_p = proj_p.reshape(batch, seq, IN_W)[:, seq - WINDOW:, Q_W:Q_W + 2 * KV_W]
    new_k_p = kv_p[..., :KV_W].reshape(1, batch, WINDOW, N_KV_HEADS, HEAD_DIM)
    new_v_p = kv_p[..., KV_W:].reshape(1, batch, WINDOW, N_KV_HEADS, HEAD_DIM)
    o_s3, new_k_s, new_v_s = _attn_sample(
        proj_s[:, :Q_W].reshape(nseq, N_HEADS, HEAD_DIM),
        proj_s[:, Q_W:Q_W + KV_W].reshape(nseq, 1, KV_W),
        proj_s[:, Q_W + KV_W:Q_W + 2 * KV_W].reshape(nseq, 1, KV_W),
        p['cache_win_k'][l].reshape(nseq, WINDOW, KV_W),
        p['cache_win_v'][l].reshape(nseq, WINDOW, KV_W), bias_s, sinks)
    o_s = o_s3.reshape(nseq, Q_W)

    lbr, lbi, bbr, bbi = _ssm_disc(p['ssm_a_re'][l], p['ssm_a_im'][l], p['ssm_log_dt'][l],
                                   p['ssm_b_re'][l], p['ssm_b_im'][l])
    lbr = lbr.reshape(1, SSM_W)
    lbi = lbi.reshape(1, SSM_W)
    bb = jnp.concatenate([_block_diag_tiles(bbr), _block_diag_tiles(bbi)], axis=-1).astype(BF16)
    c_re_t = p['ssm_c_re'][l].transpose(0, 2, 1)
    c_im_t = p['ssm_c_im'][l].transpose(0, 2, 1)
    cc = jnp.concatenate([_block_diag_tiles(c_re_t), -_block_diag_tiles(c_im_t)], axis=1).astype(BF16)
    u0 = Q_W + 2 * KV_W
    zeros = jnp.zeros((batch, SSM_W), F32)
    w_glu = p['w_glu'][l].astype(BF16)
    y_p3, hr_p, hi_p = _ssm(proj_p.reshape(batch, seq, IN_W), zeros, zeros, batch, SSM_TC, lbr, lbi, bb, cc,
                            p['ssm_d'][l], w_glu, p['b_glu'][l], seq_major=True)
    y_p = y_p3.reshape(batch * seq, D_SSM)
    y_s, hr_s, hi_s = _ssm(proj_s[:, u0:u0 + D_SSM], p['state_ssm_re'][l].reshape(nseq, SSM_W),
                           p['state_ssm_im'][l].reshape(nseq, SSM_W), nseq, 1, lbr, lbi, bb, cc,
                           p['ssm_d'][l], w_glu, p['b_glu'][l])

    wa_t = p['w_br_attn'][l].astype(BF16)
    ws_t = p['w_br_ssm'][l].astype(BF16)
    wo = p['w_out'][l].astype(BF16)
    n_total = batch * seq + nseq
    merge = functools.partial(_merge, w_br_attn=wa_t, w_br_ssm=ws_t, w_out=wo, g_post_mix=p['g_post_mix'][l],
                              g_pre_ffn=p['g_pre_ffn'][l], w_router_t=p['w_router'][l].T,
                              b_router=p['b_router'][l])
    x1_p, h2_buf, lg_buf = merge(o_p, y_p, proj_p, xp, mod_p, False, seq, n_total, 0, None)
    x1_s, h2_all, lg_all = merge(o_s, y_s, proj_s, xs, mod_s, True, 1, n_total, batch * seq, (h2_buf, lg_buf))

    st = lambda h, n: h.reshape(1, n, N_SSM_GROUPS, SSM_STATE)
    return dict(
        mod_p=mod_p, mod_s=mod_s, h2_all=h2_all, lg_all=lg_all,
        p=dict(proj=proj_p, o_attn=o_p, new_k=new_k_p, new_v=new_v_p, y_ssm=y_p, h_re=st(hr_p, batch),
               h_im=st(hi_p, batch), x1=x1_p),
        s=dict(proj=proj_s, o_attn=o_s, new_k=new_k_s.reshape(1, nseq, WINDOW, N_KV_HEADS, HEAD_DIM),
               new_v=new_v_s.reshape(1, nseq, WINDOW, N_KV_HEADS, HEAD_DIM), y_ssm=y_s,
               h_re=st(hr_s, nseq), h_im=st(hi_s, nseq), x1=x1_s))


def _count_steps(c, step, n_max):
    out = jnp.zeros_like(c)
    for q in range(-(-n_max // step)):
        out = out + jnp.where(c > float(q * step), 1.0, 0.0)
    return out


def _route_kernel(lg_ref, pos_ref, gate_ref, texp_ref, trows_ref, ntiles_ref, pstart_ref, plen_ref):
    lg = lg_ref[...]
    e, tn = lg.shape
    erow = lax.broadcasted_iota(I32, (e, tn), 0).astype(F32)
    work = lg
    vals, hits = [], []
    for _ in range(TOP_K):
        m = jnp.max(work, axis=0, keepdims=True)
        idx = jnp.min(jnp.where(work == m, erow, float(e)), axis=0, keepdims=True)
        hit = erow == idx
        vals.append(m)
        hits.append(hit)
        work = jnp.where(hit, -jnp.inf, work)
    ex = [jnp.exp(v - vals[0]) for v in vals]
    den = ex[0] + ex[1] + ex[2] + ex[3]
    gate_ref[...] = jnp.concatenate([x / den for x in ex], axis=0)

    chosen = jnp.zeros((e, tn), F32)
    for hit in hits:
        chosen = chosen + jnp.where(hit, 1.0, 0.0)
    chosen_b = chosen.astype(BF16)
    tri = (lax.broadcasted_iota(I32, (LANE, LANE), 0) <= lax.broadcasted_iota(I32, (LANE, LANE), 1))
    tri = jnp.where(tri, 1.0, 0.0).astype(BF16)
    carry = jnp.zeros((e, 1), F32)
    ranks = []
    for b in range(tn // LANE):
        blk = chosen[:, b * LANE:(b + 1) * LANE]
        inc = _dot(chosen_b[:, b * LANE:(b + 1) * LANE], tri) + carry
        ranks.append(inc - blk)
        carry = inc[:, LANE - 1:LANE]
    rank = jnp.concatenate(ranks, axis=1)
    cnt_col = carry
    cnt_row = _dot_nt(jnp.ones((SUBLANE, tn), BF16), chosen_b)[0:1, :]

    tiles_col = _count_steps(cnt_col, EXP_TM, tn)
    tiles_row = _count_steps(cnt_row, EXP_TM, tn)
    ee_r = lax.broadcasted_iota(I32, (e, e), 0)
    ee_c = lax.broadcasted_iota(I32, (e, e), 1)
    tstart_col = jnp.sum(jnp.where(ee_c < ee_r, tiles_row, 0.0), axis=1, keepdims=True)
    ntiles = jnp.sum(tiles_row, axis=1, keepdims=True)
    rstart_col = tstart_col * float(EXP_TM)
    pos = [jnp.sum(jnp.where(hit, rstart_col + rank, 0.0), axis=0, keepdims=True) for hit in hits]
    pos_ref[...] = jnp.concatenate(pos, axis=0).astype(I32)

    mm = lax.broadcasted_iota(I32, (e, LANE), 1).astype(F32)
    e_col = lax.broadcasted_iota(I32, (e, LANE), 0).astype(F32)
    own = (mm >= tstart_col) & (mm < tstart_col + tiles_col)
    texp = jnp.sum(jnp.where(own, e_col, 0.0), axis=0, keepdims=True)
    rows_here = jnp.minimum(float(EXP_TM), cnt_col - (mm - tstart_col) * float(EXP_TM))
    trows = jnp.sum(jnp.where(own, rows_here, 0.0), axis=0, keepdims=True)
    last_e = jnp.max(jnp.where(tiles_col > 0.0, e_col, 0.0), axis=0, keepdims=True)
    texp = jnp.where(mm[0:1, :] < ntiles, texp, last_e)
    texp_ref[...] = texp.astype(I32)
    trows_ref[...] = trows.astype(I32)
    ntiles_ref[...] = jnp.broadcast_to(ntiles, (1, LANE)).astype(I32)
    nsub_col = _count_steps(cnt_col, EXP_SUB, tn)
    pstart_ref[...] = jnp.broadcast_to(rstart_col + cnt_col, (e, LANE)).astype(I32)
    plen_ref[...] = jnp.broadcast_to(nsub_col * float(EXP_SUB) - cnt_col, (e, LANE)).astype(I32)


def _route(lg_t):
    e, tn = lg_t.shape
    i32 = lambda shape: jax.ShapeDtypeStruct(shape, I32)
    return pl.pallas_call(
        _route_kernel,
        out_shape=[i32((TOP_K, tn)), jax.ShapeDtypeStruct((TOP_K, tn), F32),
                   i32((1, LANE)), i32((1, LANE)), i32((1, LANE)), i32((e, LANE)), i32((e, LANE))],
        compiler_params=pltpu.CompilerParams(vmem_limit_bytes=VMEM_LIMIT),
        name="route",
    )(lg_t)


def _max_tiles(n_tok):
    return (n_tok * TOP_K) // EXP_TM + N_EXPERTS


def _dispatch_kernel(pos_ref, pstart_ref, plen_ref, h2_ref, zero_ref, xs_ref, sem):
    i = pl.program_id(0)

    def row_copy(src, s, d):
        return pltpu.make_async_copy(src.at[pl.ds(s, 1)], xs_ref.at[pl.ds(d, 1)], sem)

    for t in range(TOK_BLK):
        for k in range(TOP_K):
            row_copy(h2_ref, t, pos_ref[k, t]).start(priority=k % 2)
    for _ in range(TOP_K):
        pltpu.make_async_copy(h2_ref, xs_ref.at[pl.ds(0, TOK_BLK)], sem).wait()

    @pl.when(i == 0)
    def _():
        def per_expert(ex, c):
            n = plen_ref[ex]
            s = pstart_ref[ex]

            def zissue(r, cc):
                row_copy(zero_ref, 0, s + r).start()
                return cc

            def zdrain(r, cc):
                row_copy(zero_ref, 0, 0).wait()
                return cc

            lax.fori_loop(0, n, zissue, 0)
            lax.fori_loop(0, n, zdrain, 0)
            return c

        lax.fori_loop(0, N_EXPERTS, per_expert, 0)


def _dispatch(pos3, pstart, plen, h2_all):
    n_tok = h2_all.shape[0]
    n_rows = _max_tiles(n_tok) * EXP_TM
    smem = pl.BlockSpec(memory_space=pltpu.SMEM)
    hbm = pl.BlockSpec(memory_space=pl.ANY)
    return pl.pallas_call(
        _dispatch_kernel,
        grid=(n_tok // TOK_BLK,),
        in_specs=[pl.BlockSpec((None, TOP_K, TOK_BLK), lambda i: (i, 0, 0), memory_space=pltpu.SMEM),
                  smem, smem,
                  pl.BlockSpec((TOK_BLK, D_MODEL), lambda i: (i, 0)),
                  pl.BlockSpec((SUBLANE, D_MODEL), lambda i: (0, 0))],
        out_specs=hbm,
        out_shape=jax.ShapeDtypeStruct((n_rows, D_MODEL), F32),
        scratch_shapes=[pltpu.SemaphoreType.DMA(())],
        compiler_params=_cparams(("arbitrary",)),
        name="dispatch",
    )(pos3, pstart, plen, h2_all, jnp.zeros((SUBLANE, D_MODEL), F32))


def _expert_kernel(texp_ref, trows_ref, nt_ref, xs_hbm, wg_ref, wl_ref, bg_ref, bl_ref, wd_ref, bd_ref,
                   o_ref, x_ref, x_sem, xb_scr, act_scr, wg_scr, wl_scr, wd_scr):
    m = pl.program_id(0)
    s = pl.program_id(1)
    n_tiles = nt_ref[0]
    valid = m < n_tiles

    rows = trows_ref[m]
    nbig = lax.shift_right_logical(rows, int(math.log2(EXP_BIG)))
    big_rows = nbig * EXP_BIG
    nsmall = lax.shift_right_logical(rows - big_rows + (EXP_SUB - 1), int(math.log2(EXP_SUB)))
    nsub_done = nbig * (EXP_BIG // EXP_SUB) + nsmall

    def x_copy(tile):
        return pltpu.make_async_copy(xs_hbm.at[pl.ds(pl.multiple_of(tile * EXP_TM, EXP_TM), EXP_TM)],
                                     x_ref, x_sem)

    @pl.when((m == 0) & (s == 0))
    def _():
        x_copy(0).start()

    @pl.when(valid & (s == 0))
    def _():
        x_copy(m).wait()

        def to_bf16(r, c):
            r0 = pl.multiple_of(r * EXP_SUB, EXP_SUB)
            xb_scr[pl.ds(r0, EXP_SUB), :] = x_ref[pl.ds(r0, EXP_SUB), :].astype(BF16)
            return c

        lax.fori_loop(0, nsub_done, to_bf16, 0)

        @pl.when(m + 1 < n_tiles)
        def _():
            x_copy(m + 1).start()

    def over_rows(first, step):
        def big(r, c):
            step(pl.multiple_of(r * EXP_BIG, EXP_BIG), EXP_BIG)
            return c

        def small(r, c):
            step(pl.multiple_of(big_rows + r * EXP_SUB, EXP_SUB), EXP_SUB)
            return c

        @pl.when(nbig > 0)
        def _():
            first(EXP_BIG)
            lax.fori_loop(1, nbig, big, 0)
            lax.fori_loop(0, nsmall, small, 0)

        @pl.when(nbig == 0)
        def _():
            first(EXP_SUB)
            lax.fori_loop(1, nsmall, small, 0)

    @pl.when(valid & (s < EXP_NF))
    def _():
        def finish(r0, n, hg, hl):
            x_glu = jnp.minimum(hg, SWIGLU_LIMIT)
            x_lin = jnp.clip(hl, -SWIGLU_LIMIT, SWIGLU_LIMIT)
            act = x_glu * _sigmoid(SWIGLU_ALPHA * x_glu) * (x_lin + 1.0)
            act_scr[s, pl.ds(r0, n), :] = act.astype(BF16)

        def first(n):
            xb = xb_scr[0:n, :]
            hg = jnp.broadcast_to(bg_ref[...], (n, EXP_TF))
            hl = jnp.broadcast_to(bl_ref[...], (n, EXP_TF))
            for q in range(D_MODEL // EXP_KQ):
                ks = slice(q * EXP_KQ, (q + 1) * EXP_KQ)
                wgq = wg_ref[ks, :].astype(BF16)
                wlq = wl_ref[ks, :].astype(BF16)
                wg_scr[ks, :] = wgq
                wl_scr[ks, :] = wlq
                hg = hg + _dot(xb[:, ks], wgq)
                hl = hl + _dot(xb[:, ks], wlq)
            finish(0, n, hg, hl)

        def step(r0, n):
            xb = xb_scr[pl.ds(r0, n), :]
            finish(r0, n, _dot(xb, wg_scr[...]) + bg_ref[...], _dot(xb, wl_scr[...]) + bl_ref[...])

        over_rows(first, step)

    @pl.when(valid & (s >= EXP_NF))
    def _():
        def first(n):
            acc = jnp.broadcast_to(bd_ref[...], (n, EXP_TD))
            for f in range(EXP_NF):
                fs = slice(f * EXP_TF, (f + 1) * EXP_TF)
                wdq = wd_ref[fs, :].astype(BF16)
                wd_scr[fs, :] = wdq
                acc = acc + _dot(act_scr[f, 0:n, :], wdq)
            o_ref[0:n, :] = acc

        def step(r0, n):
            acc = jnp.broadcast_to(bd_ref[...], (n, EXP_TD))
            for f in range(EXP_NF):
                acc = acc + _dot(act_scr[f, pl.ds(r0, n), :], wd_scr[f * EXP_TF:(f + 1) * EXP_TF, :])
            o_ref[pl.ds(r0, n), :] = acc

        def zero(r, c):
            r0 = pl.multiple_of(r * EXP_SUB, EXP_SUB)
            o_ref[pl.ds(r0, EXP_SUB), :] = jnp.zeros((EXP_SUB, EXP_TD), F32)
            return c

        over_rows(first, step)
        lax.fori_loop(nsub_done, EXP_TM // EXP_SUB, zero, 0)


def _experts(texp, trows, ntiles, xs, w_gate_up, b_gate_up, w_down, b_down):
    n_tiles = xs.shape[0] // EXP_TM
    nsteps = EXP_NF + EXP_ND

    def tile(m, nt):
        return jnp.minimum(m, nt[0] - 1)

    def ea(m, s, te, nt):
        return te[jnp.where(s < EXP_NF, m, jnp.minimum(m + 1, nt[0] - 1))]

    def fa(m, s, nt):
        return jnp.where((m < nt[0]) & (s < EXP_NF), s, 0)

    def fb(m, s, nt):
        return jnp.where(m < nt[0], jnp.maximum(s - EXP_NF, 0), EXP_ND - 1)

    grid_spec = pltpu.PrefetchScalarGridSpec(
        num_scalar_prefetch=3,
        grid=(ntiles[0], nsteps),
        in_specs=[
            pl.BlockSpec(memory_space=pl.ANY),
            pl.BlockSpec((None, D_MODEL, EXP_TF),
                         lambda m, s, te, tr, nt: (ea(m, s, te, nt), 0, fa(m, s, nt))),
            pl.BlockSpec((None, D_MODEL, EXP_TF),
                         lambda m, s, te, tr, nt: (ea(m, s, te, nt), 0, EXP_NF + fa(m, s, nt))),
            pl.BlockSpec((None, 1, EXP_TF), lambda m, s, te, tr, nt: (ea(m, s, te, nt), 0, fa(m, s, nt))),
            pl.BlockSpec((None, 1, EXP_TF),
                         lambda m, s, te, tr, nt: (ea(m, s, te, nt), 0, EXP_NF + fa(m, s, nt))),
            pl.BlockSpec((None, D_FF, EXP_TD), lambda m, s, te, tr, nt: (te[m], 0, fb(m, s, nt))),
            pl.BlockSpec((None, 1, EXP_TD), lambda m, s, te, tr, nt: (te[m], 0, fb(m, s, nt))),
        ],
        out_specs=pl.BlockSpec((EXP_TM, EXP_TD), lambda m, s, te, tr, nt: (tile(m, nt), fb(m, s, nt))),
        scratch_shapes=[pltpu.VMEM((EXP_TM, D_MODEL), F32), pltpu.SemaphoreType.DMA(()),
                        pltpu.VMEM((EXP_TM, D_MODEL), BF16),
                        pltpu.VMEM((EXP_NF, EXP_TM, EXP_TF), BF16),
                        pltpu.VMEM((D_MODEL, EXP_TF), BF16), pltpu.VMEM((D_MODEL, EXP_TF), BF16),
                        pltpu.VMEM((D_FF, EXP_TD), BF16)],
    )
    return pl.pallas_call(
        _expert_kernel,
        grid_spec=grid_spec,
        out_shape=jax.ShapeDtypeStruct((xs.shape[0], D_MODEL), F32),
        compiler_params=pltpu.CompilerParams(dimension_semantics=("arbitrary", "arbitrary"),
                                             vmem_limit_bytes=EXP_VMEM_LIMIT),
        name="experts",
    )(texp, trows, ntiles, xs, w_gate_up, w_gate_up,
      b_gate_up.reshape(N_EXPERTS, 1, 2 * D_FF), b_gate_up.reshape(N_EXPERTS, 1, 2 * D_FF),
      w_down, b_down.reshape(N_EXPERTS, 1, D_MODEL))


def _combine_kernel(pos_ref, pos_next_ref, gate_ref, ys_ref, x1_ref, g2_ref, gpf_ref, o_ref, buf, sem):
    i = pl.program_id(0)
    slot = lax.rem(i, 2)

    def gather(p_ref, sl):
        for t in range(TOK_BLK):
            for k in range(TOP_K):
                pltpu.make_async_copy(ys_ref.at[pl.ds(p_ref[k, t], 1)], buf.at[sl, k, pl.ds(t, 1)],
                                      sem.at[sl]).start(priority=k % 2)

    @pl.when(i == 0)
    def _():
        gather(pos_ref, 0)

    @pl.when(i + 1 < pl.num_programs(0))
    def _():
        gather(pos_next_ref, 1 - slot)

    for k in range(TOP_K):
        pltpu.make_async_copy(ys_ref.at[pl.ds(0, TOK_BLK)], buf.at[slot, k], sem.at[slot]).wait()
    g = gate_ref[...]
    f = g[:, 0:1] * buf[slot, 0]
    for k in range(1, TOP_K):
        f = f + g[:, k:k + 1] * buf[slot, k]
    o_ref[...] = x1_ref[...] + g2_ref[...] * _rms(f, gpf_ref[...])


def _combine(pos3, gates_t, ys, x1, mod, per_row, rows_per_seq, g_post_ffn, blk0):
    rows = x1.shape[0]
    nblk = rows // TOK_BLK
    return pl.pallas_call(
        _combine_kernel,
        grid=(nblk,),
        in_specs=[pl.BlockSpec((None, TOP_K, TOK_BLK), lambda i: (blk0 + i, 0, 0), memory_space=pltpu.SMEM),
                  pl.BlockSpec((None, TOP_K, TOK_BLK), lambda i: (blk0 + jnp.minimum(i + 1, nblk - 1), 0, 0),
                               memory_space=pltpu.SMEM),
                  pl.BlockSpec((TOK_BLK, TOP_K), lambda i: (blk0 + i, 0)),
                  pl.BlockSpec(memory_space=pl.ANY),
                  pl.BlockSpec((TOK_BLK, D_MODEL), lambda i: (i, 0)),
                  _mod_spec(per_row, TOK_BLK, rows_per_seq, 5, 1),
                  pl.BlockSpec((1, D_MODEL), lambda i: (0, 0))],
        out_specs=pl.BlockSpec((TOK_BLK, D_MODEL), lambda i: (i, 0)),
        out_shape=jax.ShapeDtypeStruct((rows, D_MODEL), F32),
        scratch_shapes=[pltpu.VMEM((2, TOP_K, TOK_BLK, D_MODEL), F32), pltpu.SemaphoreType.DMA((2,))],
        compiler_params=_cparams(("arbitrary",)),
        name="combine",
    )(pos3, pos3, gates_t, ys, x1, mod, g_post_ffn.reshape(1, D_MODEL))


def _moe(fr, p, batch, seq, nseq):
    l = 0
    h2_all, lg_all = fr['h2_all'], fr['lg_all']
    n_tok = h2_all.shape[0]
    pos, gates, texp, trows, ntiles, pstart, plen = _route(lg_all)
    pos3 = pos.reshape(TOP_K, n_tok // TOK_BLK, TOK_BLK).transpose(1, 0, 2)
    xs = _dispatch(pos3, pstart[:, 0], plen[:, 0], h2_all)
    ys = _experts(texp[0], trows[0], ntiles[0, :1], xs, p['w_gate_up'][l], p['b_gate_up'][l],
                  p['w_down'][l], p['b_down'][l])
    gates_t = gates.T
    y_p = _combine(pos3, gates_t, ys, fr['p']['x1'], fr['mod_p'], False, seq, p['g_post_ffn'][l], 0)
    y_s = _combine(pos3, gates_t, ys, fr['s']['x1'], fr['mod_s'], True, 1, p['g_post_ffn'][l],
                   batch * seq // TOK_BLK)
    return y_p, y_s


def kernel(x_prompt, x_sample, c_prompt, c_sample, cache_win_k, cache_win_v, state_ssm_re, state_ssm_im, w_ada, b_ada, g_pre_mix, g_post_mix, g_pre_ffn, g_post_ffn, w_in, attn_sinks, rel_bias, ssm_a_re, ssm_a_im, ssm_log_dt, ssm_b_re, ssm_b_im, ssm_c_re, ssm_c_im, ssm_d, w_glu, b_glu, w_br_attn, w_br_ssm, w_out, w_router, b_router, w_gate_up, b_gate_up, w_down, b_down):
    p = dict(locals())
    batch, seq, _ = x_prompt.shape
    nseq = x_sample.shape[0]
    fr = _front(p)
    y_p, y_s = _moe(fr, p, batch, seq, nseq)
    fp, fs = fr['p'], fr['s']
    return (y_p.reshape(batch, seq, D_MODEL), y_s.reshape(nseq, 1, D_MODEL),
            fp['new_k'], fp['new_v'], fp['h_re'], fp['h_im'],
            fs['new_k'], fs['new_v'], fs['h_re'], fs['h_im'])
```

```python
import functools
import math

import numpy as np
import jax
import jax.numpy as jnp
from jax import lax
from jax.experimental import pallas as pl
from jax.experimental.pallas import tpu as pltpu

F32 = jnp.float32
BF16 = jnp.bfloat16
I32 = jnp.int32

D_MODEL = 2048
N_HEADS = 16
N_KV_HEADS = 4
HEAD_DIM = 64
Q_GROUP = N_HEADS // N_KV_HEADS
WINDOW = 128
N_BUCKETS = 32
MAX_EXACT = N_BUCKETS // 2
MAX_DISTANCE = 128
D_SSM = 1024
SSM_GROUP = 16
N_SSM_GROUPS = 64
SSM_STATE = 64
N_EXPERTS = 32
TOP_K = 4
D_FF = 2048
SWIGLU_LIMIT = 7.0
SWIGLU_ALPHA = 1.702
NORM_EPS = 1e-6
NEG_INF = -1e30
Q_W = N_HEADS * HEAD_DIM
KV_W = N_KV_HEADS * HEAD_DIM
IN_W = Q_W + 2 * KV_W + D_SSM + 2 * D_MODEL
SSM_W = N_SSM_GROUPS * SSM_STATE

LANE = 128
SUBLANE = 8
VMEM_LIMIT = 56 * 1024 * 1024
EXP_VMEM_LIMIT = 60 * 1024 * 1024

ADA_TN = 1024
ATTN_TB = 16
PROJ_TM = 1024
PROJ_TN = 512
MERGE_TM = 512
MERGE_TK = 512
SSM_TC = 128
SSM_LB = 512
SSM_TILES = D_SSM // LANE
TOK_BLK = 128
EXP_TM = 1152
EXP_BIG = 512
EXP_SUB = 128
EXP_TF = 512
EXP_TD = 512
EXP_KQ = 1024
EXP_NF = D_FF // EXP_TF
EXP_ND = D_MODEL // EXP_TD


def _cparams(sem):
    return pltpu.CompilerParams(dimension_semantics=sem, vmem_limit_bytes=VMEM_LIMIT)


def _sigmoid(x):
    return 1.0 / (1.0 + jnp.exp(-x))


def _rms(x, g):
    return x * lax.rsqrt(jnp.mean(x * x, axis=-1, keepdims=True) + NORM_EPS) * g


def _dot(a, b):
    return jnp.dot(a, b, preferred_element_type=F32)


def _dot_nt(a, b):
    return lax.dot_general(a, b, (((1,), (1,)), ((), ())), preferred_element_type=F32)


def _ada_kernel(c_ref, w_ref, b_ref, o_ref):
    c = c_ref[...]
    s = (c * _sigmoid(c)).astype(BF16)
    o_ref[...] = _dot(s, w_ref[...].astype(BF16)) + b_ref[...]


def _ada(c_all, w_ada, b_ada):
    rows = c_all.shape[0]
    tn = ADA_TN
    n = w_ada.shape[1]
    return pl.pallas_call(
        _ada_kernel,
        grid=(n // tn,),
        in_specs=[pl.BlockSpec((rows, D_MODEL), lambda j: (0, 0)),
                  pl.BlockSpec((D_MODEL, tn), lambda j: (0, j)),
                  pl.BlockSpec((1, tn), lambda j: (0, j))],
        out_specs=pl.BlockSpec((rows, tn), lambda j: (0, j)),
        out_shape=jax.ShapeDtypeStruct((rows, n), F32),
        compiler_params=_cparams(("arbitrary",)),
        name="ada",
    )(c_all, w_ada, b_ada.reshape(1, n))


def _t5_bucket_np(dist):
    n = np.maximum(dist, 0)
    nf = np.maximum(n, 1).astype(np.float64)
    large = MAX_EXACT + (np.log(nf / MAX_EXACT) / math.log(MAX_DISTANCE / MAX_EXACT)
                         * (N_BUCKETS - MAX_EXACT)).astype(np.int32)
    large = np.minimum(large, N_BUCKETS - 1)
    return np.where(n < MAX_EXACT, n, large).astype(np.int32)


def _bias_kernel(bucket_ref, rb_ref, o_ref, *, mask_cols):
    h = pl.program_id(0)
    bucket = bucket_ref[...]
    acc = jnp.full(bucket.shape, NEG_INF, F32)
    for b in range(N_BUCKETS):
        acc = jnp.where(bucket == b, rb_ref[b, h], acc)
    o_ref[1] = acc
    col = lax.broadcasted_iota(I32, bucket.shape, 1)
    o_ref[0] = jnp.where(col >= mask_cols, acc, NEG_INF)


def _bias_table(bucket_np, rel_bias, mask_cols):
    r, c = bucket_np.shape
    return pl.pallas_call(
        functools.partial(_bias_kernel, mask_cols=mask_cols),
        grid=(N_HEADS,),
        in_specs=[pl.BlockSpec((r, c), lambda h: (0, 0)),
                  pl.BlockSpec(memory_space=pltpu.SMEM)],
        out_specs=pl.BlockSpec((2, None, r, c), lambda h: (0, h, 0, 0)),
        out_shape=jax.ShapeDtypeStruct((2, N_HEADS, r, c), F32),
        compiler_params=_cparams(("arbitrary",)),
        name="bias",
    )(jnp.asarray(bucket_np), rel_bias)


def _proj_kernel(x_ref, sc_ref, sh_ref, g_ref, w_ref, o_ref, h_scr):
    j = pl.program_id(1)
    tm = x_ref.shape[0]
    part = tm // 2 if tm % (2 * SUBLANE) == 0 else tm

    @pl.when(j == 0)
    def _():
        for lo in range(0, tm, part):
            rows = slice(lo, lo + part)
            sc = sc_ref[...] if sc_ref.shape[0] == 1 else sc_ref[rows, :]
            sh = sh_ref[...] if sh_ref.shape[0] == 1 else sh_ref[rows, :]
            h = (_rms(x_ref[rows, :], g_ref[...]) * (1.0 + sc) + sh).astype(BF16)
            h_scr[rows, :] = h
            o_ref[rows, :] = _dot(h, w_ref[...])

    @pl.when(j > 0)
    def _():
        o_ref[...] = _dot(h_scr[...], w_ref[...])


def _mod_spec(per_row, tm, rows_per_seq, col, nargs):
    if per_row:
        if nargs == 1:
            return pl.BlockSpec((tm, D_MODEL), lambda i: (i, col))
        return pl.BlockSpec((tm, D_MODEL), lambda i, j: (i, col))
    tiles_per_seq = rows_per_seq // tm
    if nargs == 1:
        return pl.BlockSpec((None, 1, D_MODEL), lambda i: (i // tiles_per_seq, 0, col))
    return pl.BlockSpec((None, 1, D_MODEL), lambda i, j: (i // tiles_per_seq, 0, col))


def _proj(x, mod, per_row, rows_per_seq, g_pre, w_in):
    rows = x.shape[0]
    tm = min(PROJ_TM, rows)
    return pl.pallas_call(
        _proj_kernel,
        grid=(rows // tm, IN_W // PROJ_TN),
        in_specs=[pl.BlockSpec((tm, D_MODEL), lambda i, j: (i, 0)),
                  _mod_spec(per_row, tm, rows_per_seq, 1, 2),
                  _mod_spec(per_row, tm, rows_per_seq, 0, 2),
                  pl.BlockSpec((1, D_MODEL), lambda i, j: (0, 0)),
                  pl.BlockSpec((D_MODEL, PROJ_TN), lambda i, j: (0, j))],
        out_specs=pl.BlockSpec((tm, PROJ_TN), lambda i, j: (i, j)),
        out_shape=jax.ShapeDtypeStruct((rows, IN_W), F32),
        scratch_shapes=[pltpu.VMEM((tm, D_MODEL), BF16)],
        compiler_params=_cparams(("arbitrary", "arbitrary")),
        name="proj",
    )(x, mod, mod, g_pre.reshape(1, D_MODEL), w_in)


def _attn_prompt_kernel(q_ref, kc_ref, kp_ref, vc_ref, vp_ref, bias_ref, sink_ref, o_ref):
    q = q_ref[...] * (HEAD_DIM ** -0.5)
    k = jnp.concatenate([kp_ref[...], kc_ref[...]], axis=0)
    v = jnp.concatenate([vp_ref[...], vc_ref[...]], axis=0)
    outs = []
    for g in range(N_KV_HEADS):
        kg = k[:, g * HEAD_DIM:(g + 1) * HEAD_DIM].astype(BF16)
        vg = v[:, g * HEAD_DIM:(g + 1) * HEAD_DIM].astype(BF16)
        for hh in range(Q_GROUP):
            h = g * Q_GROUP + hh
            qh = q[:, h * HEAD_DIM:(h + 1) * HEAD_DIM].astype(BF16)
            s = _dot_nt(qh, kg) + bias_ref[h]
            sink = sink_ref[h]
            m = jnp.maximum(jnp.max(s, axis=-1, keepdims=True), sink)
            p = jnp.exp(s - m)
            den = jnp.sum(p, axis=-1, keepdims=True) + jnp.exp(sink - m)
            outs.append(_dot(p.astype(BF16), vg) / den)
    o_ref[...] = jnp.concatenate(outs, axis=-1).astype(BF16)


def _attn_prompt(proj, batch, seq, bias, sinks):
    nb = seq // WINDOW
    kcol = Q_W // KV_W
    vcol = kcol + 1
    cur = lambda c: (lambda b, n: (b * nb + n, c))
    prev = lambda c: (lambda b, n: (b * nb + jnp.maximum(n - 1, 0), c))
    return pl.pallas_call(
        _attn_prompt_kernel,
        grid=(batch, nb),
        in_specs=[pl.BlockSpec((WINDOW, Q_W), cur(0)),
                  pl.BlockSpec((WINDOW, KV_W), cur(kcol)),
                  pl.BlockSpec((WINDOW, KV_W), prev(kcol)),
                  pl.BlockSpec((WINDOW, KV_W), cur(vcol)),
                  pl.BlockSpec((WINDOW, KV_W), prev(vcol)),
                  pl.BlockSpec((None, N_HEADS, WINDOW, 2 * WINDOW), lambda b, n: (jnp.minimum(n, 1), 0, 0, 0)),
                  pl.BlockSpec(memory_space=pltpu.SMEM)],
        out_specs=pl.BlockSpec((WINDOW, Q_W), lambda b, n: (b * nb + n, 0)),
        out_shape=jax.ShapeDtypeStruct((batch * seq, Q_W), BF16),
        compiler_params=_cparams(("arbitrary", "arbitrary")),
        name="attn_prompt",
    )(proj, proj, proj, proj, proj, bias, sinks)


def _attn_sample_kernel(q_ref, kn_ref, vn_ref, ck_ref, cv_ref, bias_ref, sink_ref,
                        o_ref, nk_ref, nv_ref):
    tb = q_ref.shape[0]
    row = lax.broadcasted_iota(I32, (tb, WINDOW, KV_W), 1)
    last = row == WINDOW - 1
    nk = jnp.where(last, kn_ref[...], pltpu.roll(ck_ref[...], WINDOW - 1, 1))
    nv = jnp.where(last, vn_ref[...], pltpu.roll(cv_ref[...], WINDOW - 1, 1))
    nk_ref[...] = nk
    nv_ref[...] = nv
    lane_grp = lax.broadcasted_iota(I32, (N_HEADS, KV_W), 1) // HEAD_DIM
    head_grp = lax.broadcasted_iota(I32, (N_HEADS, KV_W), 0) // Q_GROUP
    gmask = (lane_grp == head_grp).astype(F32)
    q = q_ref[...]
    qrow = jnp.concatenate([q] * N_KV_HEADS, axis=-1) * gmask
    s = jnp.einsum('bhc,brc->bhr', qrow.astype(BF16), nk.astype(BF16),
                   preferred_element_type=F32) * (HEAD_DIM ** -0.5)
    s = s + bias_ref[...]
    sink = sink_ref[...]
    m = jnp.maximum(jnp.max(s, axis=-1, keepdims=True), sink)
    p = jnp.exp(s - m)
    den = jnp.sum(p, axis=-1, keepdims=True) + jnp.exp(sink - m)
    o = jnp.einsum('bhr,brc->bhc', p.astype(BF16), nv.astype(BF16),
                   preferred_element_type=F32) * gmask
    o64 = o[..., 0:HEAD_DIM]
    for g in range(1, N_KV_HEADS):
        o64 = o64 + o[..., g * HEAD_DIM:(g + 1) * HEAD_DIM]
    o_ref[...] = (o64 / den).astype(BF16)


def _attn_sample(q3, kn, vn, cache_k, cache_v, bias, sinks):
    nseq = q3.shape[0]
    tb = ATTN_TB
    seq3 = lambda w: pl.BlockSpec((tb, WINDOW, w), lambda i: (i, 0, 0))
    return pl.pallas_call(
        _attn_sample_kernel,
        grid=(nseq // tb,),
        in_specs=[pl.BlockSpec((tb, N_HEADS, HEAD_DIM), lambda i: (i, 0, 0)),
                  pl.BlockSpec((tb, 1, KV_W), lambda i: (i, 0, 0)),
                  pl.BlockSpec((tb, 1, KV_W), lambda i: (i, 0, 0)),
                  seq3(KV_W), seq3(KV_W),
                  pl.BlockSpec((N_HEADS, WINDOW), lambda i: (0, 0)),
                  pl.BlockSpec((N_HEADS, 1), lambda i: (0, 0))],
        out_specs=[pl.BlockSpec((tb, N_HEADS, HEAD_DIM), lambda i: (i, 0, 0)),
                   seq3(KV_W), seq3(KV_W)],
        out_shape=[jax.ShapeDtypeStruct((nseq, N_HEADS, HEAD_DIM), BF16),
                   jax.ShapeDtypeStruct((nseq, WINDOW, KV_W), F32),
                   jax.ShapeDtypeStruct((nseq, WINDOW, KV_W), F32)],
        compiler_params=_cparams(("arbitrary",)),
        name="attn_sample",
    )(q3, kn, vn, cache_k, cache_v, bias, sinks.reshape(N_HEADS, 1))


def _ssm_disc_kernel(are_ref, aim_ref, ldt_ref, bre_ref, bim_ref,
                     lbr_ref, lbi_ref, bbr_ref, bbi_ref):
    a_re = are_ref[...]
    a_im = aim_ref[...]
    dt = jnp.exp(ldt_ref[...])
    lam_re = a_re * dt
    lam_im = a_im * dt
    mag = jnp.exp(lam_re)
    lb_re = mag * jnp.cos(lam_im)
    lb_im = mag * jnp.sin(lam_im)
    den = a_re * a_re + a_im * a_im
    nr = lb_re - 1.0
    ni = lb_im
    coef_re = (nr * a_re + ni * a_im) / den
    coef_im = (ni * a_re - nr * a_im) / den
    b_re = bre_ref[...]
    b_im = bim_ref[...]
    lbr_ref[...] = lb_re
    lbi_ref[...] = lb_im
    bbr_ref[...] = coef_re * b_re - coef_im * b_im
    bbi_ref[...] = coef_re * b_im + coef_im * b_re


def _ssm_disc(a_re, a_im, log_dt, b_re, b_im):
    g, p, j = N_SSM_GROUPS, SSM_STATE, SSM_GROUP
    vec = jax.ShapeDtypeStruct((g, 1, p), F32)
    mat = jax.ShapeDtypeStruct((g, j, p), F32)
    return pl.pallas_call(
        _ssm_disc_kernel,
        out_shape=[vec, vec, mat, mat],
        name="ssm_disc",
    )(a_re.reshape(g, 1, p), a_im.reshape(g, 1, p), log_dt.reshape(g, 1, 1),
      b_re.transpose(0, 2, 1), b_im.transpose(0, 2, 1))


def _block_diag_tiles(x):
    a, b = x.shape[1], x.shape[2]
    eye = jnp.eye(SUBLANE, dtype=x.dtype)
    y = jnp.einsum('kgab,gh->kgahb', x.reshape(SSM_TILES, SUBLANE, a, b), eye)
    return y.reshape(SSM_TILES, SUBLANE * a, SUBLANE * b)


def _gelu_tanh(x):
    return 0.5 * x * (1.0 + jnp.tanh(math.sqrt(2.0 / math.pi) * (x + 0.044715 * (x * x * x))))


def _ssm_kernel(*refs, nseq, tc, seq_major):
    if seq_major:
        ua_ref, ub_ref, perm_ref, perm_t_ref = refs[:4]
        refs = refs[4:]
        u = jnp.concatenate([ua_ref[...].reshape(nseq * tc, D_SSM // 2),
                             ub_ref[...].reshape(nseq * tc, D_SSM // 2)], axis=1)
        ub = _dot(perm_ref[...], u.astype(BF16)).astype(BF16)
    else:
        u = refs[0][...]
        refs = refs[1:]
        ub = u.astype(BF16)
    (h0r_ref, h0i_ref, lbr_ref, lbi_ref, bb_ref, cc_ref, d_ref, wglu_ref, bglu_ref,
     y_ref, hTr_ref, hTi_ref, hre, him, st_r, st_i) = refs
    paired = nseq == 4

    @pl.when(pl.program_id(0) == 0)
    def _():
        if paired:
            st_r[...] = jnp.concatenate([h0r_ref[...], h0r_ref[...]], axis=0)
            st_i[...] = jnp.concatenate([h0i_ref[...], h0i_ref[...]], axis=0)
        else:
            st_r[...] = h0r_ref[...]
            st_i[...] = h0i_ref[...]

    half = SSM_W // SSM_TILES
    rows = nseq * tc
    for k in range(SSM_TILES):
        bu = _dot(ub[:, k * LANE:(k + 1) * LANE], bb_ref[k])
        br, bi = bu[:, :half], bu[:, half:]
        if paired:
            ar = lbr_ref[:, k * half:(k + 1) * half]
            ai = lbi_ref[:, k * half:(k + 1) * half]
            tiles = (rows // SUBLANE, SUBLANE, half)
            pr = pltpu.roll(br.reshape(tiles), nseq, 1).reshape(rows, half)
            pi = pltpu.roll(bi.reshape(tiles), nseq, 1).reshape(rows, half)
            first = (lax.broadcasted_iota(I32, (rows, half), 0) & nseq) == 0
            br, bi = (jnp.where(first, br, ar * pr - ai * pi + br),
                      jnp.where(first, bi, ar * pi + ai * pr + bi))
        hre[:, k * half:(k + 1) * half] = br
        him[:, k * half:(k + 1) * half] = bi

    for blk in range(SSM_W // SSM_LB):
        sl = slice(blk * SSM_LB, (blk + 1) * SSM_LB)
        ar = lbr_ref[:, sl]
        ai = lbi_ref[:, sl]
        if paired:
            lower = lax.broadcasted_iota(I32, (SUBLANE, SSM_LB), 0) < nseq
            cr = jnp.where(lower, ar, ar * ar - ai * ai)
            ci = jnp.where(lower, ai, 2.0 * ar * ai)

            def body(m, carry):
                xr, xi = carry
                r0 = pl.multiple_of(m * SUBLANE, SUBLANE)
                hr = cr * xr - ci * xi + hre[pl.ds(r0, SUBLANE), sl]
                hi = cr * xi + ci * xr + him[pl.ds(r0, SUBLANE), sl]
                hre[pl.ds(r0, SUBLANE), sl] = hr
                him[pl.ds(r0, SUBLANE), sl] = hi
                return (jnp.where(lower, pltpu.roll(hr, nseq, 0), hr),
                        jnp.where(lower, pltpu.roll(hi, nseq, 0), hi))

            sr, si = lax.fori_loop(0, tc * nseq // SUBLANE, body, (st_r[:, sl], st_i[:, sl]))
        else:
            def body(t, carry):
                sr, si = carry
                r0 = pl.multiple_of(t * nseq, SUBLANE)
                br = hre[pl.ds(r0, nseq), sl]
                bi = him[pl.ds(r0, nseq), sl]
                nr = ar * sr - ai * si + br
                ni = ar * si + ai * sr + bi
                hre[pl.ds(r0, nseq), sl] = nr
                him[pl.ds(r0, nseq), sl] = ni
                return nr, ni

            sr, si = lax.fori_loop(0, tc, body, (st_r[:, sl], st_i[:, sl]))
        st_r[:, sl] = sr
        st_i[:, sl] = si

    ys = []
    for k in range(SSM_TILES):
        hr = hre[:, k * half:(k + 1) * half].astype(BF16)
        hi = him[:, k * half:(k + 1) * half].astype(BF16)
        ys.append(_dot(hr, cc_ref[k, :half, :]) + _dot(hi, cc_ref[k, half:, :]))
    yc = jnp.concatenate(ys, axis=-1)
    if seq_major:
        yc_hi, yc_lo = _split_bf16(yc)
        yc = _dot(perm_t_ref[...], yc_hi) + _dot(perm_t_ref[...], yc_lo)
    y = _gelu_tanh(yc + d_ref[...] * u)
    z = _dot(y.astype(BF16), wglu_ref[...]) + bglu_ref[...]
    y_ref[...] = (y * _sigmoid(z)).astype(BF16).reshape(y_ref.shape)

    if paired:
        hTr_ref[...] = st_r[nseq:, :]
        hTi_ref[...] = st_i[nseq:, :]
    else:
        hTr_ref[...] = st_r[...]
        hTi_ref[...] = st_i[...]


def _ssm(u_src, h0_re, h0_im, nseq, tc, lbr, lbi, bb, cc, d, w_glu, b_glu, seq_major=False):
    r = nseq * tc
    st_rows = max(nseq, SUBLANE)
    const2 = lambda shape: pl.BlockSpec(shape, lambda c: (0, 0))
    const3 = lambda shape: pl.BlockSpec(shape, lambda c: (0, 0, 0))
    if seq_major:
        steps = u_src.shape[1] // tc
        half_w = D_SSM // 2
        col0 = (Q_W + 2 * KV_W) // half_w
        t_idx, s_idx = np.divmod(np.arange(r), nseq)
        perm = np.zeros((r, r), np.float32)
        perm[np.arange(r), s_idx * tc + t_idx] = 1.0
        u_specs = [pl.BlockSpec((nseq, tc, half_w), lambda c: (0, c, col0)),
                   pl.BlockSpec((nseq, tc, half_w), lambda c: (0, c, col0 + 1)),
                   const2((r, r)), const2((r, r))]
        u_args = [u_src, u_src, jnp.asarray(perm, BF16), jnp.asarray(perm.T, BF16)]
        y_spec = pl.BlockSpec((nseq, tc, D_SSM), lambda c: (0, c, 0))
        y_shape = jax.ShapeDtypeStruct((nseq, steps * tc, D_SSM), BF16)
    else:
        steps = u_src.shape[0] // r
        u_specs = [pl.BlockSpec((r, D_SSM), lambda c: (c, 0))]
        u_args = [u_src]
        y_spec = pl.BlockSpec((r, D_SSM), lambda c: (c, 0))
        y_shape = jax.ShapeDtypeStruct((steps * r, D_SSM), BF16)
    return pl.pallas_call(
        functools.partial(_ssm_kernel, nseq=nseq, tc=tc, seq_major=seq_major),
        grid=(steps,),
        in_specs=u_specs + [
            const2((nseq, SSM_W)), const2((nseq, SSM_W)),
            const2((1, SSM_W)), const2((1, SSM_W)),
            const3((SSM_TILES, LANE, 2 * SSM_W // SSM_TILES)),
            const3((SSM_TILES, 2 * SSM_W // SSM_TILES, LANE)),
            const2((1, D_SSM)), const2((D_SSM, D_SSM)), const2((1, D_SSM))],
        out_specs=[y_spec, const2((nseq, SSM_W)), const2((nseq, SSM_W))],
        out_shape=[y_shape,
                   jax.ShapeDtypeStruct((nseq, SSM_W), F32),
                   jax.ShapeDtypeStruct((nseq, SSM_W), F32)],
        scratch_shapes=[pltpu.VMEM((r, SSM_W), F32), pltpu.VMEM((r, SSM_W), F32),
                        pltpu.VMEM((st_rows, SSM_W), F32), pltpu.VMEM((st_rows, SSM_W), F32)],
        compiler_params=_cparams(("arbitrary",)),
        name="ssm",
    )(*u_args, h0_re, h0_im, lbr, lbi, bb, cc, d.reshape(1, D_SSM), w_glu, b_glu.reshape(1, D_SSM))


def _split_bf16(x):
    hi = x.astype(BF16)
    lo = (x - hi.astype(F32)).astype(BF16)
    return hi, lo


N_MERGE_IN = 15


def _merge_kernel(*refs):
    (o_ref, y_ref, ga_ref, gs_ref, wa_ref, ws_ref, wo_ref, x_ref, gpm_ref, g1_ref, sc2_ref, sh2_ref,
     gpf_ref, wrt_ref, br_ref) = refs[:N_MERGE_IN]
    x1_ref, h2_ref, lg_ref, mix = refs[-4:]
    j = pl.program_id(1)
    last = pl.num_programs(1) - 1

    def contrib():
        a = _dot(o_ref[...], wa_ref[...])
        s = _dot(y_ref[...], ws_ref[...])
        merged = _sigmoid(ga_ref[...]) * a + _sigmoid(gs_ref[...]) * s
        return _dot(merged.astype(BF16), wo_ref[...])

    @pl.when(j == 0)
    def _():
        mix[...] = contrib()

    @pl.when((j > 0) & (j < last))
    def _():
        mix[...] += contrib()

    @pl.when(j == last)
    def _():
        x1 = x_ref[...] + g1_ref[...] * _rms(mix[...] + contrib(), gpm_ref[...])
        x1_ref[...] = x1
        h2 = _rms(x1, gpf_ref[...]) * (1.0 + sc2_ref[...]) + sh2_ref[...]
        h2_ref[...] = h2
        h_hi, h_lo = _split_bf16(h2)
        w_hi, w_lo = _split_bf16(wrt_ref[...])
        lg_ref[...] = (_dot_nt(w_hi, h_hi) + _dot_nt(w_hi, h_lo) + _dot_nt(w_lo, h_hi)) + br_ref[...]


def _merge(o_attn, y_ssm, proj, x, mod, per_row, rows_per_seq, n_total, row0, shared, w_br_attn, w_br_ssm,
           w_out, g_post_mix, g_pre_ffn, w_router_t, b_router):
    rows = x.shape[0]
    tm = min(MERGE_TM, rows)
    blk0 = row0 // tm
    nk = D_MODEL // MERGE_TK
    ga0 = (Q_W + 2 * KV_W + D_SSM) // MERGE_TK
    gs0 = ga0 + nk
    row2 = lambda w: pl.BlockSpec((tm, w), lambda i, j: (i, 0))
    vec = pl.BlockSpec((1, D_MODEL), lambda i, j: (0, 0))
    in_specs = [row2(Q_W), row2(D_SSM),
                pl.BlockSpec((tm, MERGE_TK), lambda i, j: (i, ga0 + j)),
                pl.BlockSpec((tm, MERGE_TK), lambda i, j: (i, gs0 + j)),
                pl.BlockSpec((Q_W, MERGE_TK), lambda i, j: (0, j)),
                pl.BlockSpec((D_SSM, MERGE_TK), lambda i, j: (0, j)),
                pl.BlockSpec((MERGE_TK, D_MODEL), lambda i, j: (j, 0)),
                row2(D_MODEL), vec,
                _mod_spec(per_row, tm, rows_per_seq, 2, 2),
                _mod_spec(per_row, tm, rows_per_seq, 4, 2),
                _mod_spec(per_row, tm, rows_per_seq, 3, 2),
                vec,
                pl.BlockSpec((N_EXPERTS, D_MODEL), lambda i, j: (0, 0)),
                pl.BlockSpec((N_EXPERTS, 1), lambda i, j: (0, 0))]
    args = [o_attn, y_ssm, proj, proj, w_br_attn, w_br_ssm, w_out, x, g_post_mix.reshape(1, D_MODEL),
            mod, mod, mod, g_pre_ffn.reshape(1, D_MODEL), w_router_t, b_router.reshape(N_EXPERTS, 1)]
    assert len(args) == N_MERGE_IN
    aliases = {}
    if shared is not None:
        aliases = {len(args): 1, len(args) + 1: 2}
        in_specs += [pl.BlockSpec(memory_space=pl.ANY), pl.BlockSpec(memory_space=pl.ANY)]
        args += list(shared)
    return pl.pallas_call(
        _merge_kernel,
        grid=(rows // tm, nk),
        in_specs=in_specs,
        out_specs=[row2(D_MODEL),
                   pl.BlockSpec((tm, D_MODEL), lambda i, j: (blk0 + i, 0)),
                   pl.BlockSpec((N_EXPERTS, tm), lambda i, j: (0, blk0 + i))],
        out_shape=[jax.ShapeDtypeStruct((rows, D_MODEL), F32),
                   jax.ShapeDtypeStruct((n_total, D_MODEL), F32),
                   jax.ShapeDtypeStruct((N_EXPERTS, n_total), F32)],
        scratch_shapes=[pltpu.VMEM((tm, D_MODEL), F32)],
        input_output_aliases=aliases,
        compiler_params=_cparams(("arbitrary", "arbitrary")),
        name="merge",
    )(*args)


def _bucket_tables():
    ql = np.arange(WINDOW)[:, None]
    kl = np.arange(2 * WINDOW)[None, :]
    dist = ql + WINDOW - kl
    prompt = np.where((dist >= 0) & (dist < WINDOW), _t5_bucket_np(dist), -1).astype(np.int32)
    d_s = (WINDOW - 1 - np.arange(WINDOW))[None, :]
    sample = np.broadcast_to(_t5_bucket_np(d_s), (SUBLANE, WINDOW)).astype(np.int32)
    return prompt, sample


def _front(p):
    l = 0
    batch, seq, _ = p['x_prompt'].shape
    nseq = p['x_sample'].shape[0]
    xp = p['x_prompt'].reshape(batch * seq, D_MODEL)
    xs = p['x_sample'].reshape(nseq, D_MODEL)

    c_all = jnp.concatenate([p['c_prompt'], p['c_sample'],
                             jnp.zeros((SUBLANE - (batch + nseq) % SUBLANE, D_MODEL), F32)], axis=0)
    mod = _ada(c_all, p['w_ada'][l], p['b_ada'][l])
    mod_p = mod[:batch].reshape(batch, 1, 6 * D_MODEL)
    mod_s = mod[batch:batch + nseq]

    bucket_p, bucket_s = _bucket_tables()
    bias_p = _bias_table(bucket_p, p['rel_bias'], WINDOW)
    bias_s = _bias_table(bucket_s, p['rel_bias'], 0)[1, :, 0, :]
    sinks = p['attn_sinks'][l]

    w_in_t = p['w_in'][l].astype(BF16)
    proj_p = _proj(xp, mod_p, False, seq, p['g_pre_mix'][l], w_in_t)
    proj_s = _proj(xs, mod_s, True, 1, p['g_pre_mix'][l], w_in_t)

    o_p = _attn_prompt(proj_p, batch, seq, bias_p, sinks)
    kv_p = proj_p.reshape(batch, seq, IN_W)[:, seq - WINDOW:, Q_W:Q_W + 2 * KV_W]
    new_k_p = kv_p[..., :KV_W].reshape(1, batch, WINDOW, N_KV_HEADS, HEAD_DIM)
    new_v_p = kv_p[..., KV_W:].reshape(1, batch, WINDOW, N_KV_HEADS, HEAD_DIM)
    o_s3, new_k_s, new_v_s = _attn_sample(
        proj_s[:, :Q_W].reshape(nseq, N_HEADS, HEAD_DIM),
        proj_s[:, Q_W:Q_W + KV_W].reshape(nseq, 1, KV_W),
        proj_s[:, Q_W + KV_W:Q_W + 2 * KV_W].reshape(nseq, 1, KV_W),
        p['cache_win_k'][l].reshape(nseq, WINDOW, KV_W),
        p['cache_win_v'][l].reshape(nseq, WINDOW, KV_W), bias_s, sinks)
    o_s = o_s3.reshape(nseq, Q_W)

    lbr, lbi, bbr, bbi = _ssm_disc(p['ssm_a_re'][l], p['ssm_a_im'][l], p['ssm_log_dt'][l],
                                   p['ssm_b_re'][l], p['ssm_b_im'][l])
    lbr = lbr.reshape(1, SSM_W)
    lbi = lbi.reshape(1, SSM_W)
    bb = jnp.concatenate([_block_diag_tiles(bbr), _block_diag_tiles(bbi)], axis=-1).astype(BF16)
    c_re_t = p['ssm_c_re'][l].transpose(0, 2, 1)
    c_im_t = p['ssm_c_im'][l].transpose(0, 2, 1)
    cc = jnp.concatenate([_block_diag_tiles(c_re_t), -_block_diag_tiles(c_im_t)], axis=1).astype(BF16)
    u0 = Q_W + 2 * KV_W
    zeros = jnp.zeros((batch, SSM_W), F32)
    w_glu = p['w_glu'][l].astype(BF16)
    y_p3, hr_p, hi_p = _ssm(proj_p.reshape(batch, seq, IN_W), zeros, zeros, batch, SSM_TC, lbr, lbi, bb, cc,
                            p['ssm_d'][l], w_glu, p['b_glu'][l], seq_major=True)
    y_p = y_p3.reshape(batch * seq, D_SSM)
    y_s, hr_s, hi_s = _ssm(proj_s[:, u0:u0 + D_SSM], p['state_ssm_re'][l].reshape(nseq, SSM_W),
                           p['state_ssm_im'][l].reshape(nseq, SSM_W), nseq, 1, lbr, lbi, bb, cc,
                           p['ssm_d'][l], w_glu, p['b_glu'][l])

    wa_t = p['w_br_attn'][l].astype(BF16)
    ws_t = p['w_br_ssm'][l].astype(BF16)
    wo = p['w_out'][l].astype(BF16)
    n_total = batch * seq + nseq
    merge = functools.partial(_merge, w_br_attn=wa_t, w_br_ssm=ws_t, w_out=wo, g_post_mix=p['g_post_mix'][l],
                              g_pre_ffn=p['g_pre_ffn'][l], w_router_t=p['w_router'][l].T,
                              b_router=p['b_router'][l])
    x1_p, h2_buf, lg_buf = merge(o_p, y_p, proj_p, xp, mod_p, False, seq, n_total, 0, None)
    x1_s, h2_all, lg_all = merge(o_s, y_s, proj_s, xs, mod_s, True, 1, n_total, batch * seq, (h2_buf, lg_buf))

    st = lambda h, n: h.reshape(1, n, N_SSM_GROUPS, SSM_STATE)
    return dict(
        mod_p=mod_p, mod_s=mod_s, h2_all=h2_all, lg_all=lg_all,
        p=dict(proj=proj_p, o_attn=o_p, new_k=new_k_p, new_v=new_v_p, y_ssm=y_p, h_re=st(hr_p, batch),
               h_im=st(hi_p, batch), x1=x1_p),
        s=dict(proj=proj_s, o_attn=o_s, new_k=new_k_s.reshape(1, nseq, WINDOW, N_KV_HEADS, HEAD_DIM),
               new_v=new_v_s.reshape(1, nseq, WINDOW, N_KV_HEADS, HEAD_DIM), y_ssm=y_s,
               h_re=st(hr_s, nseq), h_im=st(hi_s, nseq), x1=x1_s))


def _count_steps(c, step, n_max):
    out = jnp.zeros_like(c)
    for q in range(-(-n_max // step)):
        out = out + jnp.where(c > float(q * step), 1.0, 0.0)
    return out


def _route_kernel(lg_ref, pos_ref, gate_ref, texp_ref, trows_ref, ntiles_ref, pstart_ref, plen_ref):
    lg = lg_ref[...]
    e, tn = lg.shape
    erow = lax.broadcasted_iota(I32, (e, tn), 0).astype(F32)
    work = lg
    vals, hits = [], []
    for _ in range(TOP_K):
        m = jnp.max(work, axis=0, keepdims=True)
        idx = jnp.min(jnp.where(work == m, erow, float(e)), axis=0, keepdims=True)
        hit = erow == idx
        vals.append(m)
        hits.append(hit)
        work = jnp.where(hit, -jnp.inf, work)
    ex = [jnp.exp(v - vals[0]) for v in vals]
    den = ex[0] + ex[1] + ex[2] + ex[3]
    gate_ref[...] = jnp.concatenate([x / den for x in ex], axis=0)

    chosen = jnp.zeros((e, tn), F32)
    for hit in hits:
        chosen = chosen + jnp.where(hit, 1.0, 0.0)
    chosen_b = chosen.astype(BF16)
    tri = (lax.broadcasted_iota(I32, (LANE, LANE), 0) <= lax.broadcasted_iota(I32, (LANE, LANE), 1))
    tri = jnp.where(tri, 1.0, 0.0).astype(BF16)
    carry = jnp.zeros((e, 1), F32)
    ranks = []
    for b in range(tn // LANE):
        blk = chosen[:, b * LANE:(b + 1) * LANE]
        inc = _dot(chosen_b[:, b * LANE:(b + 1) * LANE], tri) + carry
        ranks.append(inc - blk)
        carry = inc[:, LANE - 1:LANE]
    rank = jnp.concatenate(ranks, axis=1)
    cnt_col = carry
    cnt_row = _dot_nt(jnp.ones((SUBLANE, tn), BF16), chosen_b)[0:1, :]

    tiles_col = _count_steps(cnt_col, EXP_TM, tn)
    tiles_row = _count_steps(cnt_row, EXP_TM, tn)
    ee_r = lax.broadcasted_iota(I32, (e, e), 0)
    ee_c = lax.broadcasted_iota(I32, (e, e), 1)
    tstart_col = jnp.sum(jnp.where(ee_c < ee_r, tiles_row, 0.0), axis=1, keepdims=True)
    ntiles = jnp.sum(tiles_row, axis=1, keepdims=True)
    rstart_col = tstart_col * float(EXP_TM)
    pos = [jnp.sum(jnp.where(hit, rstart_col + rank, 0.0), axis=0, keepdims=True) for hit in hits]
    pos_ref[...] = jnp.concatenate(pos, axis=0).astype(I32)

    mm = lax.broadcasted_iota(I32, (e, LANE), 1).astype(F32)
    e_col = lax.broadcasted_iota(I32, (e, LANE), 0).astype(F32)
    own = (mm >= tstart_col) & (mm < tstart_col + tiles_col)
    texp = jnp.sum(jnp.where(own, e_col, 0.0), axis=0, keepdims=True)
    rows_here = jnp.minimum(float(EXP_TM), cnt_col - (mm - tstart_col) * float(EXP_TM))
    trows = jnp.sum(jnp.where(own, rows_here, 0.0), axis=0, keepdims=True)
    last_e = jnp.max(jnp.where(tiles_col > 0.0, e_col, 0.0), axis=0, keepdims=True)
    texp = jnp.where(mm[0:1, :] < ntiles, texp, last_e)
    texp_ref[...] = texp.astype(I32)
    trows_ref[...] = trows.astype(I32)
    ntiles_ref[...] = jnp.broadcast_to(ntiles, (1, LANE)).astype(I32)
    nsub_col = _count_steps(cnt_col, EXP_SUB, tn)
    pstart_ref[...] = jnp.broadcast_to(rstart_col + cnt_col, (e, LANE)).astype(I32)
    plen_ref[...] = jnp.broadcast_to(nsub_col * float(EXP_SUB) - cnt_col, (e, LANE)).astype(I32)


def _route(lg_t):
    e, tn = lg_t.shape
    i32 = lambda shape: jax.ShapeDtypeStruct(shape, I32)
    return pl.pallas_call(
        _route_kernel,
        out_shape=[i32((TOP_K, tn)), jax.ShapeDtypeStruct((TOP_K, tn), F32),
                   i32((1, LANE)), i32((1, LANE)), i32((1, LANE)), i32((e, LANE)), i32((e, LANE))],
        compiler_params=pltpu.CompilerParams(vmem_limit_bytes=VMEM_LIMIT),
        name="route",
    )(lg_t)


def _max_tiles(n_tok):
    return (n_tok * TOP_K) // EXP_TM + N_EXPERTS


def _dispatch_kernel(pos_ref, pstart_ref, plen_ref, h2_ref, zero_ref, xs_ref, sem):
    i = pl.program_id(0)

    def row_copy(src, s, d):
        return pltpu.make_async_copy(src.at[pl.ds(s, 1)], xs_ref.at[pl.ds(d, 1)], sem)

    for t in range(TOK_BLK):
        for k in range(TOP_K):
            row_copy(h2_ref, t, pos_ref[k, t]).start(priority=k % 2)
    for _ in range(TOP_K):
        pltpu.make_async_copy(h2_ref, xs_ref.at[pl.ds(0, TOK_BLK)], sem).wait()

    @pl.when(i == 0)
    def _():
        def per_expert(ex, c):
            n = plen_ref[ex]
            s = pstart_ref[ex]

            def zissue(r, cc):
                row_copy(zero_ref, 0, s + r).start()
                return cc

            def zdrain(r, cc):
                row_copy(zero_ref, 0, 0).wait()
                return cc

            lax.fori_loop(0, n, zissue, 0)
            lax.fori_loop(0, n, zdrain, 0)
            return c

        lax.fori_loop(0, N_EXPERTS, per_expert, 0)


def _dispatch(pos3, pstart, plen, h2_all):
    n_tok = h2_all.shape[0]
    n_rows = _max_tiles(n_tok) * EXP_TM
    smem = pl.BlockSpec(memory_space=pltpu.SMEM)
    hbm = pl.BlockSpec(memory_space=pl.ANY)
    return pl.pallas_call(
        _dispatch_kernel,
        grid=(n_tok // TOK_BLK,),
        in_specs=[pl.BlockSpec((None, TOP_K, TOK_BLK), lambda i: (i, 0, 0), memory_space=pltpu.SMEM),
                  smem, smem,
                  pl.BlockSpec((TOK_BLK, D_MODEL), lambda i: (i, 0)),
                  pl.BlockSpec((SUBLANE, D_MODEL), lambda i: (0, 0))],
        out_specs=hbm,
        out_shape=jax.ShapeDtypeStruct((n_rows, D_MODEL), F32),
        scratch_shapes=[pltpu.SemaphoreType.DMA(())],
        compiler_params=_cparams(("arbitrary",)),
        name="dispatch",
    )(pos3, pstart, plen, h2_all, jnp.zeros((SUBLANE, D_MODEL), F32))


def _expert_kernel(texp_ref, trows_ref, nt_ref, xs_hbm, wg_ref, wl_ref, bg_ref, bl_ref, wd_ref, bd_ref,
                   o_ref, x_ref, x_sem, xb_scr, act_scr, wg_scr, wl_scr, wd_scr):
    m = pl.program_id(0)
    s = pl.program_id(1)
    n_tiles = nt_ref[0]
    valid = m < n_tiles

    rows = trows_ref[m]
    nbig = lax.shift_right_logical(rows, int(math.log2(EXP_BIG)))
    big_rows = nbig * EXP_BIG
    nsmall = lax.shift_right_logical(rows - big_rows + (EXP_SUB - 1), int(math.log2(EXP_SUB)))
    nsub_done = nbig * (EXP_BIG // EXP_SUB) + nsmall

    def x_copy(tile):
        return pltpu.make_async_copy(xs_hbm.at[pl.ds(pl.multiple_of(tile * EXP_TM, EXP_TM), EXP_TM)],
                                     x_ref, x_sem)

    @pl.when((m == 0) & (s == 0))
    def _():
        x_copy(0).start()

    @pl.when(valid & (s == 0))
    def _():
        x_copy(m).wait()

        def to_bf16(r, c):
            r0 = pl.multiple_of(r * EXP_SUB, EXP_SUB)
            xb_scr[pl.ds(r0, EXP_SUB), :] = x_ref[pl.ds(r0, EXP_SUB), :].astype(BF16)
            return c

        lax.fori_loop(0, nsub_done, to_bf16, 0)

        @pl.when(m + 1 < n_tiles)
        def _():
            x_copy(m + 1).start()

    def over_rows(first, step):
        def big(r, c):
            step(pl.multiple_of(r * EXP_BIG, EXP_BIG), EXP_BIG)
            return c

        def small(r, c):
            step(pl.multiple_of(big_rows + r * EXP_SUB, EXP_SUB), EXP_SUB)
            return c

        @pl.when(nbig > 0)
        def _():
            first(EXP_BIG)
            lax.fori_loop(1, nbig, big, 0)
            lax.fori_loop(0, nsmall, small, 0)

        @pl.when(nbig == 0)
        def _():
            first(EXP_SUB)
            lax.fori_loop(1, nsmall, small, 0)

    @pl.when(valid & (s < EXP_NF))
    def _():
        def finish(r0, n, hg, hl):
            x_glu = jnp.minimum(hg, SWIGLU_LIMIT)
            x_lin = jnp.clip(hl, -SWIGLU_LIMIT, SWIGLU_LIMIT)
            act = x_glu * _sigmoid(SWIGLU_ALPHA * x_glu) * (x_lin + 1.0)
            act_scr[s, pl.ds(r0, n), :] = act.astype(BF16)

        def first(n):
            xb = xb_scr[0:n, :]
            hg = jnp.broadcast_to(bg_ref[...], (n, EXP_TF))
            hl = jnp.broadcast_to(bl_ref[...], (n, EXP_TF))
            for q in range(D_MODEL // EXP_KQ):
                ks = slice(q * EXP_KQ, (q + 1) * EXP_KQ)
                wgq = wg_ref[ks, :].astype(BF16)
                wlq = wl_ref[ks, :].astype(BF16)
                wg_scr[ks, :] = wgq
                wl_scr[ks, :] = wlq
                hg = hg + _dot(xb[:, ks], wgq)
                hl = hl + _dot(xb[:, ks], wlq)
            finish(0, n, hg, hl)

        def step(r0, n):
            xb = xb_scr[pl.ds(r0, n), :]
            finish(r0, n, _dot(xb, wg_scr[...]) + bg_ref[...], _dot(xb, wl_scr[...]) + bl_ref[...])

        over_rows(first, step)

    @pl.when(valid & (s >= EXP_NF))
    def _():
        def first(n):
            acc = jnp.broadcast_to(bd_ref[...], (n, EXP_TD))
            for f in range(EXP_NF):
                fs = slice(f * EXP_TF, (f + 1) * EXP_TF)
                wdq = wd_ref[fs, :].astype(BF16)
                wd_scr[fs, :] = wdq
                acc = acc + _dot(act_scr[f, 0:n, :], wdq)
            o_ref[0:n, :] = acc

        def step(r0, n):
            acc = jnp.broadcast_to(bd_ref[...], (n, EXP_TD))
            for f in range(EXP_NF):
                acc = acc + _dot(act_scr[f, pl.ds(r0, n), :], wd_scr[f * EXP_TF:(f + 1) * EXP_TF, :])
            o_ref[pl.ds(r0, n), :] = acc

        def zero(r, c):
            r0 = pl.multiple_of(r * EXP_SUB, EXP_SUB)
            o_ref[pl.ds(r0, EXP_SUB), :] = jnp.zeros((EXP_SUB, EXP_TD), F32)
            return c

        over_rows(first, step)
        lax.fori_loop(nsub_done, EXP_TM // EXP_SUB, zero, 0)


def _experts(texp, trows, ntiles, xs, w_gate_up, b_gate_up, w_down, b_down):
    n_tiles = xs.shape[0] // EXP_TM
    nsteps = EXP_NF + EXP_ND

    def tile(m, nt):
        return jnp.minimum(m, nt[0] - 1)

    def ea(m, s, te, nt):
        return te[jnp.where(s < EXP_NF, m, jnp.minimum(m + 1, nt[0] - 1))]

    def fa(m, s, nt):
        return jnp.where((m < nt[0]) & (s < EXP_NF), s, 0)

    def fb(m, s, nt):
        return jnp.where(m < nt[0], jnp.maximum(s - EXP_NF, 0), EXP_ND - 1)

    grid_spec = pltpu.PrefetchScalarGridSpec(
        num_scalar_prefetch=3,
        grid=(ntiles[0], nsteps),
        in_specs=[
            pl.BlockSpec(memory_space=pl.ANY),
            pl.BlockSpec((None, D_MODEL, EXP_TF),
                         lambda m, s, te, tr, nt: (ea(m, s, te, nt), 0, fa(m, s, nt))),
            pl.BlockSpec((None, D_MODEL, EXP_TF),
                         lambda m, s, te, tr, nt: (ea(m, s, te, nt), 0, EXP_NF + fa(m, s, nt))),
            pl.BlockSpec((None, 1, EXP_TF), lambda m, s, te, tr, nt: (ea(m, s, te, nt), 0, fa(m, s, nt))),
            pl.BlockSpec((None, 1, EXP_TF),
                         lambda m, s, te, tr, nt: (ea(m, s, te, nt), 0, EXP_NF + fa(m, s, nt))),
            pl.BlockSpec((None, D_FF, EXP_TD), lambda m, s, te, tr, nt: (te[m], 0, fb(m, s, nt))),
            pl.BlockSpec((None, 1, EXP_TD), lambda m, s, te, tr, nt: (te[m], 0, fb(m, s, nt))),
        ],
        out_specs=pl.BlockSpec((EXP_TM, EXP_TD), lambda m, s, te, tr, nt: (tile(m, nt), fb(m, s, nt))),
        scratch_shapes=[pltpu.VMEM((EXP_TM, D_MODEL), F32), pltpu.SemaphoreType.DMA(()),
                        pltpu.VMEM((EXP_TM, D_MODEL), BF16),
                        pltpu.VMEM((EXP_NF, EXP_TM, EXP_TF), BF16),
                        pltpu.VMEM((D_MODEL, EXP_TF), BF16), pltpu.VMEM((D_MODEL, EXP_TF), BF16),
                        pltpu.VMEM((D_FF, EXP_TD), BF16)],
    )
    return pl.pallas_call(
        _expert_kernel,
        grid_spec=grid_spec,
        out_shape=jax.ShapeDtypeStruct((xs.shape[0], D_MODEL), F32),
        compiler_params=pltpu.CompilerParams(dimension_semantics=("arbitrary", "arbitrary"),
                                             vmem_limit_bytes=EXP_VMEM_LIMIT),
        name="experts",
    )(texp, trows, ntiles, xs, w_gate_up, w_gate_up,
      b_gate_up.reshape(N_EXPERTS, 1, 2 * D_FF), b_gate_up.reshape(N_EXPERTS, 1, 2 * D_FF),
      w_down, b_down.reshape(N_EXPERTS, 1, D_MODEL))


def _combine_kernel(pos_ref, pos_next_ref, gate_ref, ys_ref, x1_ref, g2_ref, gpf_ref, o_ref, buf, sem):
    i = pl.program_id(0)
    slot = lax.rem(i, 2)

    def gather(p_ref, sl):
        for t in range(TOK_BLK):
            for k in range(TOP_K):
                pltpu.make_async_copy(ys_ref.at[pl.ds(p_ref[k, t], 1)], buf.at[sl, k, pl.ds(t, 1)],
                                      sem.at[sl]).start(priority=k % 2)

    @pl.when(i == 0)
    def _():
        gather(pos_ref, 0)

    @pl.when(i + 1 < pl.num_programs(0))
    def _():
        gather(pos_next_ref, 1 - slot)

    for k in range(TOP_K):
        pltpu.make_async_copy(ys_ref.at[pl.ds(0, TOK_BLK)], buf.at[slot, k], sem.at[slot]).wait()
    g = gate_ref[...]
    f = g[:, 0:1] * buf[slot, 0]
    for k in range(1, TOP_K):
        f = f + g[:, k:k + 1] * buf[slot, k]
    o_ref[...] = x1_ref[...] + g2_ref[...] * _rms(f, gpf_ref[...])


def _combine(pos3, gates_t, ys, x1, mod, per_row, rows_per_seq, g_post_ffn, blk0):
    rows = x1.shape[0]
    nblk = rows // TOK_BLK
    return pl.pallas_call(
        _combine_kernel,
        grid=(nblk,),
        in_specs=[pl.BlockSpec((None, TOP_K, TOK_BLK), lambda i: (blk0 + i, 0, 0), memory_space=pltpu.SMEM),
                  pl.BlockSpec((None, TOP_K, TOK_BLK), lambda i: (blk0 + jnp.minimum(i + 1, nblk - 1), 0, 0),
                               memory_space=pltpu.SMEM),
                  pl.BlockSpec((TOK_BLK, TOP_K), lambda i: (blk0 + i, 0)),
                  pl.BlockSpec(memory_space=pl.ANY),
                  pl.BlockSpec((TOK_BLK, D_MODEL), lambda i: (i, 0)),
                  _mod_spec(per_row, TOK_BLK, rows_per_seq, 5, 1),
                  pl.BlockSpec((1, D_MODEL), lambda i: (0, 0))],
        out_specs=pl.BlockSpec((TOK_BLK, D_MODEL), lambda i: (i, 0)),
        out_shape=jax.ShapeDtypeStruct((rows, D_MODEL), F32),
        scratch_shapes=[pltpu.VMEM((2, TOP_K, TOK_BLK, D_MODEL), F32), pltpu.SemaphoreType.DMA((2,))],
        compiler_params=_cparams(("arbitrary",)),
        name="combine",
    )(pos3, pos3, gates_t, ys, x1, mod, g_post_ffn.reshape(1, D_MODEL))


def _moe(fr, p, batch, seq, nseq):
    l = 0
    h2_all, lg_all = fr['h2_all'], fr['lg_all']
    n_tok = h2_all.shape[0]
    pos, gates, texp, trows, ntiles, pstart, plen = _route(lg_all)
    pos3 = pos.reshape(TOP_K, n_tok // TOK_BLK, TOK_BLK).transpose(1, 0, 2)
    xs = _dispatch(pos3, pstart[:, 0], plen[:, 0], h2_all)
    ys = _experts(texp[0], trows[0], ntiles[0, :1], xs, p['w_gate_up'][l], p['b_gate_up'][l],
                  p['w_down'][l], p['b_down'][l])
    gates_t = gates.T
    y_p = _combine(pos3, gates_t, ys, fr['p']['x1'], fr['mod_p'], False, seq, p['g_post_ffn'][l], 0)
    y_s = _combine(pos3, gates_t, ys, fr['s']['x1'], fr['mod_s'], True, 1, p['g_post_ffn'][l],
                   batch * seq // TOK_BLK)
    return y_p, y_s


def kernel(x_prompt, x_sample, c_prompt, c_sample, cache_win_k, cache_win_v, state_ssm_re, state_ssm_im, w_ada, b_ada, g_pre_mix, g_post_mix, g_pre_ffn, g_post_ffn, w_in, attn_sinks, rel_bias, ssm_a_re, ssm_a_im, ssm_log_dt, ssm_b_re, ssm_b_im, ssm_c_re, ssm_c_im, ssm_d, w_glu, b_glu, w_br_attn, w_br_ssm, w_out, w_router, b_router, w_gate_up, b_gate_up, w_down, b_down):
    p = dict(locals())
    batch, seq, _ = x_prompt.shape
    nseq = x_sample.shape[0]
    fr = _front(p)
    y_p, y_s = _moe(fr, p, batch, seq, nseq)
    fp, fs = fr['p'], fr['s']
    return (y_p.reshape(batch, seq, D_MODEL), y_s.reshape(nseq, 1, D_MODEL),
            fp['new_k'], fp['new_v'], fp['h_re'], fp['h_im'],
            fs['new_k'], fs['new_v'], fs['h_re'], fs['h_im'])
```
